```python
import jax, jax.numpy as jnp
from jax import lax
import numpy as np

D_MODEL = 1024
BATCH = 8
SEQ = 2048
DEPTH = 4

N_MOD = 6
EPS = 1e-6
CONV_WIDTH = D_MODEL
CONV_KERNEL = 31
RET_HEADS = 4
RET_QK_DIM = D_MODEL // RET_HEADS
RET_V_DIM = 2 * RET_QK_DIM
RET_QK = RET_HEADS * RET_QK_DIM
RET_V = RET_HEADS * RET_V_DIM
RET_CHUNK = 128
ROPE_BASE = 10000.0
IN_SIZES = (CONV_WIDTH, CONV_WIDTH, RET_QK, RET_QK, RET_V, RET_V, D_MODEL, D_MODEL)
IN_OFFSETS = tuple(int(o) for o in np.cumsum(IN_SIZES)[:-1])
IN_COLS = int(sum(IN_SIZES))
N_EXPERTS = 16
N_GROUPS = 4
EXPERTS_PER_GROUP = N_EXPERTS // N_GROUPS
TOP_K = 2
D_FF_EXPERT = D_MODEL
MOE_BLOCK = 256

kernel_name = 'hybrid_conv_retention_grouped_moe_adaln'


def rmsnorm(x, w):
    xf = x.astype(jnp.float32)
    y = xf * lax.rsqrt(jnp.mean(xf * xf, axis=-1, keepdims=True) + EPS)
    return (y * w.astype(jnp.float32)).astype(x.dtype)


def layernorm(x, w, b):
    xf = x.astype(jnp.float32)
    mu = jnp.mean(xf, axis=-1, keepdims=True)
    var = jnp.mean(jnp.square(xf - mu), axis=-1, keepdims=True)
    y = (xf - mu) * lax.rsqrt(var + EPS)
    return (y * w.astype(jnp.float32) + b.astype(jnp.float32)).astype(x.dtype)


def rope(t, positions):
    half = t.shape[-1] // 2
    inv = ROPE_BASE ** (-jnp.arange(half, dtype=jnp.float32) / half)
    ang = positions.astype(jnp.float32)[..., None] * inv
    cos = jnp.cos(ang)[:, :, None, :]
    sin = jnp.sin(ang)[:, :, None, :]
    t1 = t[..., :half].astype(jnp.float32)
    t2 = t[..., half:].astype(jnp.float32)
    return jnp.concatenate([t1 * cos - t2 * sin, t1 * sin + t2 * cos], axis=-1)


def retention(q, k, v, positions):
    B, S, _ = q.shape
    H, dk, dv, C = RET_HEADS, RET_QK_DIM, RET_V_DIM, RET_CHUNK
    nc = S // C
    qh = rope(q.reshape(B, S, H, dk), positions)
    kh = rope(k.reshape(B, S, H, dk), positions) * (dk ** -0.5)
    vh = v.reshape(B, S, H, dv).astype(jnp.float32)

    def to_chunks(t):
        return t.reshape(B, nc, C, H, t.shape[-1]).transpose(1, 0, 3, 2, 4)

    log_g = jnp.log1p(-jnp.exp2(-5.0 - jnp.arange(H, dtype=jnp.float32)))
    idx = jnp.arange(C, dtype=jnp.float32)
    rel = idx[:, None] - idx[None, :]
    dmask = jnp.where(rel[None] >= 0, jnp.exp(jnp.maximum(rel, 0.0)[None] * log_g[:, None, None]), 0.0)
    xi = jnp.exp((idx + 1.0)[None, :] * log_g[:, None])[..., None]
    zeta = jnp.exp((C - 1.0 - idx)[None, :] * log_g[:, None])[..., None]
    g_chunk = jnp.exp(C * log_g)[:, None, None]

    def step(state, qkv):
        qc, kc, vc = qkv
        scores = jnp.einsum('bhid,bhjd->bhij', qc, kc) * dmask
        inner = jnp.einsum('bhij,bhjv->bhiv', scores, vc)
        cross = jnp.einsum('bhid,bhdv->bhiv', qc * xi, state)
        state = state * g_chunk + jnp.einsum('bhjd,bhjv->bhdv', kc * zeta, vc)
        return state, inner + cross

    state0 = jnp.zeros((B, H, dk, dv), jnp.float32)
    _, out = lax.scan(step, state0, (to_chunks(qh), to_chunks(kh), to_chunks(vh)))
    out = out.transpose(1, 0, 3, 2, 4).reshape(B, S, H, dv)
    mu = jnp.mean(out, axis=-1, keepdims=True)
    var = jnp.mean(jnp.square(out - mu), axis=-1, keepdims=True)
    out = (out - mu) * lax.rsqrt(var + EPS)
    return out.reshape(B, S, RET_V).astype(v.dtype)


def mixer(h, positions, w_in, conv_w, conv_b, conv_ln_w, conv_ln_b, w_conv_out, w_ret_out, w_out):
    proj = h @ w_in
    a, b, q, k, v, g, gate_a, gate_b = jnp.split(proj, IN_OFFSETS, axis=-1)
    u = a * jax.nn.sigmoid(b)
    u = lax.conv_general_dilated(u, conv_w[:, None, :], window_strides=(1,), padding=[(CONV_KERNEL - 1, 0)],
                                 dimension_numbers=('NWC', 'WIO', 'NWC'), feature_group_count=CONV_WIDTH) + conv_b
    u = layernorm(u, conv_ln_w, conv_ln_b)
    y_a = jax.nn.silu(u) @ w_conv_out
    r = retention(q, k, v, positions)
    y_b = (jax.nn.silu(g) * r) @ w_ret_out
    y = jax.nn.sigmoid(gate_a) * y_a + jax.nn.sigmoid(gate_b) * y_b
    return y @ w_out


def route(h2, w_router, router_bias):
    n = h2.shape[0]
    s = jax.nn.sigmoid(h2.astype(jnp.float32) @ w_router.astype(jnp.float32))
    sb = s + router_bias.astype(jnp.float32)
    grp_score = jnp.sum(lax.top_k(sb.reshape(n, N_GROUPS, EXPERTS_PER_GROUP), 2)[0], axis=-1)
    top_g = jnp.argmax(grp_score, axis=-1)
    in_grp = (jnp.arange(N_EXPERTS) // EXPERTS_PER_GROUP)[None, :] == top_g[:, None]
    _, idx = lax.top_k(jnp.where(in_grp, sb, -jnp.inf), TOP_K)
    w = jnp.take_along_axis(s, idx, axis=-1)
    w = w / jnp.sum(w, axis=-1, keepdims=True)
    return idx.astype(jnp.int32), w


def moe(h2, w_router, router_bias, w_gate, w_up, w_down):
    n, _ = h2.shape
    idx, w = route(h2, w_router, router_bias)
    nk = n * TOP_K
    flat_e = idx.reshape(-1)
    flat_tok = jnp.repeat(jnp.arange(n, dtype=jnp.int32), TOP_K)
    flat_w = w.reshape(-1).astype(h2.dtype)
    order = jnp.argsort(flat_e)
    se, st, sw = flat_e[order], flat_tok[order], flat_w[order]
    counts = jnp.zeros((N_EXPERTS,), jnp.int32).at[flat_e].add(1)
    padded = (counts + MOE_BLOCK - 1) // MOE_BLOCK * MOE_BLOCK
    pad_end = jnp.cumsum(padded)
    pad_start = pad_end - padded
    start = jnp.cumsum(counts) - counts
    dest = pad_start[se] + (jnp.arange(nk, dtype=jnp.int32) - start[se])
    n_blocks = -(-nk // MOE_BLOCK) + N_EXPERTS
    p = n_blocks * MOE_BLOCK
    slot_tok = jnp.zeros((p,), jnp.int32).at[dest].set(st)
    slot_w = jnp.zeros((p,), h2.dtype).at[dest].set(sw)
    block_e = jnp.minimum(jnp.searchsorted(pad_end, jnp.arange(n_blocks, dtype=jnp.int32) * MOE_BLOCK, side='right'),
                          N_EXPERTS - 1).astype(jnp.int32)
    xb = h2[slot_tok].reshape(n_blocks, MOE_BLOCK, -1)

    def run_block(args):
        xblk, e = args
        return (jax.nn.silu(xblk @ w_gate[e]) * (xblk @ w_up[e])) @ w_down[e]

    yb = lax.map(run_block, (xb, block_e)).reshape(p, -1)
    return jax.ops.segment_sum(yb * slot_w[:, None], slot_tok, num_segments=n)


def setup_inputs(seed: int = 0) -> dict:
    key = jax.random.key(seed)
    ks = jax.random.split(key, 24)
    f32 = jnp.float32
    nrm = lambda k, shape, scale: jax.random.normal(k, shape, f32) * scale
    D, L = D_MODEL, DEPTH
    return {
        'x': nrm(ks[0], (BATCH, SEQ, D), 1.0),
        'c': nrm(ks[1], (BATCH, D), 1.0),
        'positions': jnp.broadcast_to(jnp.arange(SEQ, dtype=jnp.int32), (BATCH, SEQ)),
        'w_ada': nrm(ks[2], (L, D, N_MOD * D), 0.5 * D ** -0.5),
        'b_ada': nrm(ks[3], (L, N_MOD * D), 0.01),
        'norm1_w': 1.0 + nrm(ks[4], (L, D), 0.01),
        'w_in': nrm(ks[5], (L, D, IN_COLS), D ** -0.5),
        'conv_w': nrm(ks[6], (L, CONV_KERNEL, CONV_WIDTH), CONV_KERNEL ** -0.5),
        'conv_b': nrm(ks[7], (L, CONV_WIDTH), 0.01),
        'conv_ln_w': 1.0 + nrm(ks[8], (L, CONV_WIDTH), 0.01),
        'conv_ln_b': nrm(ks[9], (L, CONV_WIDTH), 0.01),
        'w_conv_out': nrm(ks[10], (L, CONV_WIDTH, D), CONV_WIDTH ** -0.5),
        'w_ret_out': nrm(ks[11], (L, RET_V, D), RET_V ** -0.5),
        'w_out': nrm(ks[12], (L, D, D), D ** -0.5),
        'norm2_w': 1.0 + nrm(ks[13], (L, D), 0.01),
        'w_router': nrm(ks[14], (D, N_EXPERTS), D ** -0.5),
        'router_bias': nrm(ks[15], (N_EXPERTS,), 0.01),
        'w_exp_gate': nrm(ks[16], (L, N_EXPERTS, D, D_FF_EXPERT), D ** -0.5),
        'w_exp_up': nrm(ks[17], (L, N_EXPERTS, D, D_FF_EXPERT), D ** -0.5),
        'w_exp_down': nrm(ks[18], (L, N_EXPERTS, D_FF_EXPERT, D), D_FF_EXPERT ** -0.5),
        'final_norm_w': 1.0 + nrm(ks[19], (D,), 0.01),
    }


def reference(x, c, positions, w_ada, b_ada, norm1_w, w_in, conv_w, conv_b, conv_ln_w, conv_ln_b,
              w_conv_out, w_ret_out, w_out, norm2_w, w_router, router_bias, w_exp_gate, w_exp_up,
              w_exp_down, final_norm_w):
    B, S, D = x.shape
    silu_c = jax.nn.silu(c)
    for l in range(DEPTH):
        mod = (silu_c @ w_ada[l] + b_ada[l])[:, None, :]
        sh1, sc1, gt1, sh2, sc2, gt2 = jnp.split(mod, N_MOD, axis=-1)
        h = rmsnorm(x, norm1_w[l]) * (1.0 + sc1) + sh1
        x = x + gt1 * mixer(h, positions, w_in[l], conv_w[l], conv_b[l], conv_ln_w[l], conv_ln_b[l],
                            w_conv_out[l], w_ret_out[l], w_out[l])
        h = rmsnorm(x, norm2_w[l]) * (1.0 + sc2) + sh2
        y = moe(h.reshape(B * S, D), w_router, router_bias, w_exp_gate[l], w_exp_up[l], w_exp_down[l])
        x = x + gt2 * y.reshape(B, S, D)
    return rmsnorm(x, final_norm_w)
```

```python
import functools

import numpy as np
import jax
import jax.numpy as jnp
from jax import lax
from jax.experimental import pallas as pl
from jax.experimental.pallas import tpu as pltpu

F32 = jnp.float32
BF16 = jnp.bfloat16

D = 1024
BATCH = 8
SEQ = 2048
DEPTH = 4
NTOK = BATCH * SEQ
N_MOD = 6
EPS = 1e-6

CONV_K = 31
HEADS = 4
DK = 256
DV = 512
CHUNK = 128
ROPE_BASE = 10000.0
HALF = DK // 2

N_EXPERTS = 16
N_GROUPS = 4
GROUP_SIZE = 4
PAIRS = 6
N_BUCKETS = N_GROUPS * PAIRS
BUCKET_ROWS = 32

P_COLS = 9 * D
PCOL_U, PCOL_Q, PCOL_K, PCOL_V, PCOL_G, PCOL_GA, PCOL_GB = 0, 1, 2, 3, 5, 7, 8

MOE_BLK = 256
N_BLOCKS = NTOK // MOE_BLK + N_BUCKETS
SLOT_ROWS = N_BLOCKS * MOE_BLK
TAIL = 128
XROW = D + TAIL

VMEM_LIMIT = 56 * 1024 * 1024

NT_DIMS = (((1,), (1,)), ((), ()))


def _cparams(*sem):
    return pltpu.CompilerParams(dimension_semantics=sem, vmem_limit_bytes=VMEM_LIMIT)


def _sigmoid(x):
    return jax.nn.sigmoid(x)


def _rms(x):
    return x * lax.rsqrt(jnp.mean(x * x, axis=-1, keepdims=True) + EPS)


ADA_TN = 1536


def _ada_kernel(c_ref, w_ref, b_ref, o_ref):
    c = c_ref[...]
    sc = c * _sigmoid(c)
    o_ref[0] = jnp.dot(sc, w_ref[0], precision=lax.Precision.HIGHEST,
                       preferred_element_type=F32) + b_ref[0]


def _ada(c, w_ada, b_ada):
    nj = N_MOD * D // ADA_TN
    return pl.pallas_call(
        _ada_kernel,
        out_shape=jax.ShapeDtypeStruct((DEPTH, BATCH, N_MOD * D), F32),
        grid=(DEPTH, nj),
        in_specs=[pl.BlockSpec((BATCH, D), lambda l, j: (0, 0)),
                  pl.BlockSpec((1, D, ADA_TN), lambda l, j: (l, 0, j)),
                  pl.BlockSpec((1, 1, ADA_TN), lambda l, j: (l, 0, j))],
        out_specs=pl.BlockSpec((1, BATCH, ADA_TN), lambda l, j: (l, 0, j)),
        compiler_params=_cparams("parallel", "parallel"),
        name="ada",
    )(c, w_ada, b_ada.reshape(DEPTH, 1, N_MOD * D))


ROPE_TN = 2048


def _rope_kernel(pos_ref, inv_ref, cos_ref, sin_ref):
    ang = pos_ref[...] * inv_ref[...]
    cos_ref[...] = jnp.cos(ang)
    sin_ref[...] = jnp.sin(ang)


def _rope_tables(positions):
    inv = ROPE_BASE ** (-jnp.arange(HALF, dtype=F32) / HALF)
    pos = positions.astype(F32).reshape(NTOK, 1)
    return pl.pallas_call(
        _rope_kernel,
        out_shape=(jax.ShapeDtypeStruct((NTOK, HALF), F32),) * 2,
        grid=(NTOK // ROPE_TN,),
        in_specs=[pl.BlockSpec((ROPE_TN, 1), lambda i: (i, 0)),
                  pl.BlockSpec((1, HALF), lambda i: (0, 0))],
        out_specs=(pl.BlockSpec((ROPE_TN, HALF), lambda i: (i, 0)),) * 2,
        compiler_params=_cparams("parallel"),
        name="rope_tables",
    )(pos, inv.reshape(1, HALF))


NORM_TS = 512


def _modnorm_kernel(x_ref, w_ref, sc_ref, sh_ref, o_ref):
    y = _rms(x_ref[0]) * w_ref[...]
    o_ref[0] = (y * (1.0 + sc_ref[0]) + sh_ref[0]).astype(o_ref.dtype)


def _modnorm(x, w, sc, sh):
    return pl.pallas_call(
        _modnorm_kernel,
        out_shape=jax.ShapeDtypeStruct((BATCH, SEQ, D), BF16),
        grid=(BATCH, SEQ // NORM_TS),
        in_specs=[pl.BlockSpec((1, NORM_TS, D), lambda b, i: (b, i, 0)),
                  pl.BlockSpec((1, D), lambda b, i: (0, 0)),
                  pl.BlockSpec((1, 1, D), lambda b, i: (b, 0, 0)),
                  pl.BlockSpec((1, 1, D), lambda b, i: (b, 0, 0))],
        out_specs=pl.BlockSpec((1, NORM_TS, D), lambda b, i: (b, i, 0)),
        compiler_params=_cparams("parallel", "parallel"),
        name="modnorm",
    )(x, w.reshape(1, D), sc, sh)


INP_TM = 1024
INP_GROUPS = 9


def _inproj_kernel(h_ref, w1_ref, w2_ref, cos_ref, sin_ref, o_ref):
    j = pl.program_id(0)
    h = h_ref[...]

    def proj():
        return jnp.dot(h, w1_ref[0], preferred_element_type=F32)

    @pl.when(j == 0)
    def _():
        a = proj()
        b = jnp.dot(h, w2_ref[0], preferred_element_type=F32)
        o_ref[...] = (a * _sigmoid(b)).astype(BF16)

    @pl.when((j == 1) | (j == 2))
    def _():
        t = proj()
        scale = jnp.where(j == 2, DK ** -0.5, 1.0).astype(F32)
        cos = cos_ref[...] * scale
        sin = sin_ref[...] * scale
        for hd in range(HEADS):
            c0 = hd * DK
            t1 = t[:, c0:c0 + HALF]
            t2 = t[:, c0 + HALF:c0 + DK]
            o_ref[:, c0:c0 + HALF] = (t1 * cos - t2 * sin).astype(BF16)
            o_ref[:, c0 + HALF:c0 + DK] = (t1 * sin + t2 * cos).astype(BF16)

    @pl.when((j == 3) | (j == 4))
    def _():
        o_ref[...] = proj().astype(BF16)

    @pl.when((j == 5) | (j == 6))
    def _():
        g = proj()
        o_ref[...] = (g * _sigmoid(g)).astype(BF16)

    @pl.when(j >= 7)
    def _():
        o_ref[...] = _sigmoid(proj()).astype(BF16)


def _inproj(h, w_in_bf, l, cos, sin):
    def rope_idx(j, m):
        return (jnp.where((j == 1) | (j == 2), m, 0), 0)

    return pl.pallas_call(
        _inproj_kernel,
        out_shape=jax.ShapeDtypeStruct((NTOK, P_COLS), BF16),
        grid=(INP_GROUPS, NTOK // INP_TM),
        in_specs=[pl.BlockSpec((INP_TM, D), lambda j, m: (m, 0)),
                  pl.BlockSpec((1, D, D), lambda j, m: (l, 0, jnp.where(j == 0, 0, j + 1))),
                  pl.BlockSpec((1, D, D), lambda j, m: (l, 0, 1)),
                  pl.BlockSpec((INP_TM, HALF), rope_idx),
                  pl.BlockSpec((INP_TM, HALF), rope_idx)],
        out_specs=pl.BlockSpec((INP_TM, D), lambda j, m: (m, j)),
        compiler_params=_cparams("parallel", "parallel"),
        name="inproj",
    )(h, w_in_bf, w_in_bf, cos, sin)


CONV_TS = 512
CONV_HALO = 32
CONV_RC = 32
CONV_CW = 256
CONV_SH = CONV_TS + 24


def _conv_kernel(u_ref, halo_ref, sga_ref, cw_ref, cb_ref, lnw_ref, lnb_ref, wo_ref, o_ref,
                 buf_ref, sh_ref, acc_ref):
    i = pl.program_id(1)
    halo = halo_ref[0].astype(F32)
    buf_ref[0:CONV_HALO, :] = jnp.where(i > 0, halo, 0.0)
    buf_ref[CONV_HALO:CONV_HALO + CONV_TS, :] = u_ref[0].astype(F32)
    for r in range(1, 8):
        sh_ref[r - 1] = buf_ref[r:r + CONV_SH, :]

    def body(ci, carry):
        r0 = pl.multiple_of(ci * CONV_RC, CONV_RC)
        for cc in range(D // CONV_CW):
            cols = slice(cc * CONV_CW, (cc + 1) * CONV_CW)
            acc = jnp.zeros((CONV_RC, CONV_CW), F32)
            for off in range(2, CONV_K + 2):
                q, r = divmod(off, 8)
                if r == 0:
                    win = buf_ref[pl.ds(r0 + 8 * q, CONV_RC), cols]
                else:
                    win = sh_ref[r - 1, pl.ds(r0 + 8 * q, CONV_RC), cols]
                acc = acc + win * cw_ref[off - 2:off - 1, cols]
            acc_ref[pl.ds(r0, CONV_RC), cols] = acc + cb_ref[:, cols]
        return carry

    lax.fori_loop(0, CONV_TS // CONV_RC, body, 0)

    c = acc_ref[...]
    mu = jnp.mean(c, axis=-1, keepdims=True)
    cen = c - mu
    var = jnp.mean(cen * cen, axis=-1, keepdims=True)
    y = cen * lax.rsqrt(var + EPS) * lnw_ref[...] + lnb_ref[...]
    y = y * _sigmoid(y)
    out = jnp.dot(y.astype(BF16), wo_ref[0], preferred_element_type=F32)
    o_ref[0] = (out * sga_ref[0].astype(F32)).astype(BF16)


def _conv_branch(p3, l, conv_w, conv_b, ln_w, ln_b, w_conv_out_bf):
    hb = CONV_TS // CONV_HALO
    return pl.pallas_call(
        _conv_kernel,
        out_shape=jax.ShapeDtypeStruct((BATCH, SEQ, D), BF16),
        grid=(BATCH, SEQ // CONV_TS),
        in_specs=[pl.BlockSpec((1, CONV_TS, D), lambda b, i: (b, i, PCOL_U)),
                  pl.BlockSpec((1, CONV_HALO, D), lambda b, i: (b, jnp.maximum(i * hb - 1, 0), PCOL_U)),
                  pl.BlockSpec((1, CONV_TS, D), lambda b, i: (b, i, PCOL_GA)),
                  pl.BlockSpec((CONV_K, D), lambda b, i: (0, 0)),
                  pl.BlockSpec((1, D), lambda b, i: (0, 0)),
                  pl.BlockSpec((1, D), lambda b, i: (0, 0)),
                  pl.BlockSpec((1, D), lambda b, i: (0, 0)),
                  pl.BlockSpec((1, D, D), lambda b, i: (l, 0, 0))],
        out_specs=pl.BlockSpec((1, CONV_TS, D), lambda b, i: (b, i, 0)),
        scratch_shapes=[pltpu.VMEM((CONV_HALO + CONV_TS, D), F32),
                        pltpu.VMEM((7, CONV_SH, D), F32),
                        pltpu.VMEM((CONV_TS, D), F32)],
        compiler_params=_cparams("parallel", "parallel"),
        name="conv_branch",
    )(p3, p3, p3, conv_w, conv_b.reshape(1, D), ln_w.reshape(1, D), ln_b.reshape(1, D), w_conv_out_bf)


def _ret_tables():
    hh = jnp.arange(HEADS, dtype=F32)
    log_g = jnp.log1p(-jnp.exp2(-5.0 - hh))
    idx = jnp.arange(CHUNK, dtype=F32)
    rel = idx[:, None] - idx[None, :]
    dmask = jnp.where(rel[None] >= 0, jnp.exp(jnp.maximum(rel, 0.0)[None] * log_g[:, None, None]), 0.0)
    xi = jnp.exp((idx + 1.0)[None, :] * log_g[:, None])[..., None]
    zeta = jnp.exp((CHUNK - 1.0 - idx)[None, :] * log_g[:, None])[..., None]
    g_chunk = jnp.exp(CHUNK * log_g)[:, None, None]
    return (dmask,
            jnp.broadcast_to(xi, (HEADS, CHUNK, DK)),
            jnp.broadcast_to(zeta, (HEADS, CHUNK, DK)),
            jnp.broadcast_to(g_chunk, (HEADS, 1, DV)))


def _ret_kernel(q_ref, k_ref, v_ref, sg_ref, dm_ref, xi_ref, zt_ref, gc_ref, o_ref, st_ref):
    st_ref[...] = jnp.zeros_like(st_ref)
    dm = dm_ref[0]
    xi = xi_ref[0]
    zt = zt_ref[0]
    gc = gc_ref[0]

    def body(c, carry):
        r0 = pl.multiple_of(c * CHUNK, CHUNK)
        rows = pl.ds(r0, CHUNK)
        qc = q_ref[0, rows, :]
        kc = k_ref[0, rows, :]
        vc = v_ref[0, rows, :]
        scores = lax.dot_general(qc, kc, NT_DIMS, preferred_element_type=F32) * dm
        inner = jnp.dot(scores.astype(BF16), vc, preferred_element_type=F32)
        st = st_ref[...]
        qx = (qc.astype(F32) * xi).astype(BF16)
        cross = jnp.dot(qx, st.astype(BF16), preferred_element_type=F32)
        kzt = (kc.astype(F32) * zt).T.astype(BF16)
        st_ref[...] = st * gc + jnp.dot(kzt, vc, preferred_element_type=F32)
        o = inner + cross
        mu = jnp.mean(o, axis=-1, keepdims=True)
        cen = o - mu
        var = jnp.mean(cen * cen, axis=-1, keepdims=True)
        r = cen * lax.rsqrt(var + EPS)
        o_ref[0, rows, :] = (r * sg_ref[0, rows, :].astype(F32)).astype(BF16)
        return carry

    lax.fori_loop(0, SEQ // CHUNK, body, 0)


def _retention(p3, tables):
    dmask, xi, zeta, gch = tables
    qb, kb, vb, gb = PCOL_Q * D // DK, PCOL_K * D // DK, PCOL_V * D // DV, PCOL_G * D // DV
    return pl.pallas_call(
        _ret_kernel,
        out_shape=jax.ShapeDtypeStruct((BATCH, SEQ, HEADS * DV), BF16),
        grid=(BATCH, HEADS),
        in_specs=[pl.BlockSpec((1, SEQ, DK), lambda b, h: (b, 0, qb + h)),
                  pl.BlockSpec((1, SEQ, DK), lambda b, h: (b, 0, kb + h)),
                  pl.BlockSpec((1, SEQ, DV), lambda b, h: (b, 0, vb + h)),
                  pl.BlockSpec((1, SEQ, DV), lambda b, h: (b, 0, gb + h)),
                  pl.BlockSpec((1, CHUNK, CHUNK), lambda b, h: (h, 0, 0)),
                  pl.BlockSpec((1, CHUNK, DK), lambda b, h: (h, 0, 0)),
                  pl.BlockSpec((1, CHUNK, DK), lambda b, h: (h, 0, 0)),
                  pl.BlockSpec((1, 1, DV), lambda b, h: (h, 0, 0))],
        out_specs=pl.BlockSpec((1, SEQ, DV), lambda b, h: (b, 0, h)),
        scratch_shapes=[pltpu.VMEM((DK, DV), F32)],
        compiler_params=_cparams("parallel", "parallel"),
        name="retention",
    )(p3, p3, p3, p3, dmask, xi, zeta, gch)


MIX_TM = 512


def _route_rows(s, sb):
    row = lambda a, e: a[e:e + 1, :]
    best = None
    gidx = None
    for g in range(N_GROUPS):
        v = [row(sb, GROUP_SIZE * g + i) for i in range(GROUP_SIZE)]
        pair_sums = [v[a] + v[b] for a in range(GROUP_SIZE) for b in range(a + 1, GROUP_SIZE)]
        gs = functools.reduce(jnp.maximum, pair_sums)
        if g == 0:
            best, gidx = gs, jnp.zeros(gs.shape, jnp.int32)
        else:
            upd = gs > best
            gidx = jnp.where(upd, g, gidx)
            best = jnp.where(upd, gs, best)

    def pick(a, i):
        out = row(a, i)
        for g in range(1, N_GROUPS):
            out = jnp.where(gidx == g, row(a, GROUP_SIZE * g + i), out)
        return out

    vb = [pick(sb, i) for i in range(GROUP_SIZE)]
    vs = [pick(s, i) for i in range(GROUP_SIZE)]
    m1, i1, s1 = vb[0], jnp.zeros(gidx.shape, jnp.int32), vs[0]
    for i in range(1, GROUP_SIZE):
        upd = vb[i] > m1
        m1 = jnp.where(upd, vb[i], m1)
        i1 = jnp.where(upd, i, i1)
        s1 = jnp.where(upd, vs[i], s1)
    m2 = i2 = s2 = None
    for i in range(GROUP_SIZE):
        cand = jnp.where(i1 == i, -jnp.inf, vb[i])
        if m2 is None:
            m2, i2, s2 = cand, jnp.zeros(gidx.shape, jnp.int32), vs[0]
        else:
            upd = cand > m2
            m2 = jnp.where(upd, cand, m2)
            i2 = jnp.where(upd, i, i2)
            s2 = jnp.where(upd, vs[i], s2)
    den = s1 + s2
    w1 = s1 / den
    w2 = s2 / den
    first_low = i1 < i2
    lo = jnp.minimum(i1, i2)
    hi = jnp.maximum(i1, i2)
    pair = jnp.where(lo == 0, hi - 1, jnp.where(lo == 1, hi + 1, 5))
    bucket = gidx * PAIRS + pair
    return bucket, jnp.where(first_low, w1, w2), jnp.where(first_low, w2, w1)


def _mix_kernel(rg_ref, ya_ref, sgb_ref, x_ref, gt_ref, sc_ref, sh_ref, wr_ref, wo_ref, n2_ref,
                rhi_ref, rlo_ref, rb_ref, tri_ref,
                x1_ref, hx_ref, ri_ref, cnt_ref, carry_ref):
    m = pl.program_id(0)

    @pl.when(m == 0)
    def _():
        carry_ref[...] = jnp.zeros_like(carry_ref)

    yb = jnp.dot(rg_ref[...], wr_ref[0], preferred_element_type=F32)
    y = ya_ref[...].astype(F32) + sgb_ref[...].astype(F32) * yb
    o = jnp.dot(y.astype(BF16), wo_ref[0], preferred_element_type=F32)
    x1 = x_ref[...] + gt_ref[0] * o
    x1_ref[...] = x1
    h2 = _rms(x1) * n2_ref[...] * (1.0 + sc_ref[0]) + sh_ref[0]
    hx_ref[:, 0:D] = h2

    hi = h2.astype(BF16)
    lo = (h2 - hi.astype(F32)).astype(BF16)
    rhi = rhi_ref[...]
    logits = (lax.dot_general(rhi, hi, NT_DIMS, preferred_element_type=F32)
              + lax.dot_general(rhi, lo, NT_DIMS, preferred_element_type=F32)
              + lax.dot_general(rlo_ref[...], hi, NT_DIMS, preferred_element_type=F32))
    s = _sigmoid(logits)
    bucket, w_lo, w_hi = _route_rows(s, s + rb_ref[...])

    onehot = (lax.broadcasted_iota(jnp.int32, (BUCKET_ROWS, MIX_TM), 0) == bucket).astype(F32)
    prefix = jnp.dot(onehot.astype(BF16), tri_ref[...], preferred_element_type=F32)
    carry = carry_ref[:, 0:1]
    rank = jnp.sum(onehot * (prefix + carry), axis=0, keepdims=True)
    new_carry = carry + jnp.sum(onehot, axis=1, keepdims=True)
    carry_ref[...] = jnp.broadcast_to(new_carry, carry_ref.shape)
    cnt_ref[...] = jnp.broadcast_to(new_carry, cnt_ref.shape)

    rid = lax.broadcasted_iota(jnp.int32, (8, MIX_TM), 0)
    ri_ref[...] = jnp.where(rid == 0, bucket, jnp.where(rid == 1, rank.astype(jnp.int32), 0))
    wid = lax.broadcasted_iota(jnp.int32, (TAIL, MIX_TM), 0)
    wrows = jnp.where(wid == 0, w_lo, jnp.where(wid == 1, w_hi, 0.0))
    hx_ref[:, D:XROW] = wrows.T


def _mix(rg, ya, p, x, gt1, sc2, sh2, w_ret_out_bf, w_out_bf, l, norm2_w, rhi, rlo, rb, tri):
    tpb = SEQ // MIX_TM
    bidx = lambda m: (m // tpb, 0, 0)
    return pl.pallas_call(
        _mix_kernel,
        out_shape=(jax.ShapeDtypeStruct((NTOK, D), F32),
                   jax.ShapeDtypeStruct((NTOK, XROW), F32),
                   jax.ShapeDtypeStruct((8, NTOK), jnp.int32),
                   jax.ShapeDtypeStruct((BUCKET_ROWS, 128), F32)),
        grid=(NTOK // MIX_TM,),
        in_specs=[pl.BlockSpec((MIX_TM, HEADS * DV), lambda m: (m, 0)),
                  pl.BlockSpec((MIX_TM, D), lambda m: (m, 0)),
                  pl.BlockSpec((MIX_TM, D), lambda m: (m, PCOL_GB)),
                  pl.BlockSpec((MIX_TM, D), lambda m: (m, 0)),
                  pl.BlockSpec((1, 1, D), bidx),
                  pl.BlockSpec((1, 1, D), bidx),
                  pl.BlockSpec((1, 1, D), bidx),
                  pl.BlockSpec((1, HEADS * DV, D), lambda m: (l, 0, 0)),
                  pl.BlockSpec((1, D, D), lambda m: (l, 0, 0)),
                  pl.BlockSpec((1, D), lambda m: (0, 0)),
                  pl.BlockSpec((N_EXPERTS, D), lambda m: (0, 0)),
                  pl.BlockSpec((N_EXPERTS, D), lambda m: (0, 0)),
                  pl.BlockSpec((N_EXPERTS, MIX_TM), lambda m: (0, 0)),
                  pl.BlockSpec((MIX_TM, MIX_TM), lambda m: (0, 0))],
        out_specs=(pl.BlockSpec((MIX_TM, D), lambda m: (m, 0)),
                   pl.BlockSpec((MIX_TM, XROW), lambda m: (m, 0)),
                   pl.BlockSpec((8, MIX_TM), lambda m: (0, m)),
                   pl.BlockSpec((BUCKET_ROWS, 128), lambda m: (0, 0))),
        scratch_shapes=[pltpu.VMEM((BUCKET_ROWS, 128), F32)],
        compiler_params=_cparams("arbitrary"),
        name="mix_route",
    )(rg, ya, p, x, gt1, sc2, sh2, w_ret_out_bf, w_out_bf, norm2_w.reshape(1, D), rhi, rlo, rb, tri)


DISP_TG = 512


def _dispatch_kernel(dest_ref, nval_ref, src_ref, xs_ref, zero_ref, sem, zsem):
    step = pl.program_id(0)
    base = step * DISP_TG

    @pl.when(step == 0)
    def _():
        zero_ref[...] = jnp.zeros_like(zero_ref)

        def zero_copy(b):
            r0 = pl.multiple_of(b * MOE_BLK, MOE_BLK)
            return pltpu.make_async_copy(zero_ref, xs_ref.at[pl.ds(r0, MOE_BLK), :], zsem)

        def zissue(b, carry):
            @pl.when(nval_ref[b] < MOE_BLK)
            def _():
                zero_copy(b).start()
            return carry

        lax.fori_loop(0, N_BLOCKS, zissue, 0)

        def zdrain(b, carry):
            @pl.when(nval_ref[b] < MOE_BLK)
            def _():
                zero_copy(b).wait()
            return carry

        lax.fori_loop(0, N_BLOCKS, zdrain, 0)

    def row_copy(t, d):
        return pltpu.make_async_copy(src_ref.at[pl.ds(t, 1), :], xs_ref.at[pl.ds(d, 1), :], sem)

    def issue(t, carry):
        row_copy(t, dest_ref[base + t]).start()
        return carry

    lax.fori_loop(0, DISP_TG, issue, 0)

    def drain(t, carry):
        row_copy(t, dest_ref[base + t]).wait()
        return carry

    lax.fori_loop(0, DISP_TG, drain, 0)


def _dispatch(dest, nval, hx):
    return pl.pallas_call(
        _dispatch_kernel,
        out_shape=jax.ShapeDtypeStruct((SLOT_ROWS, XROW), F32),
        grid_spec=pltpu.PrefetchScalarGridSpec(
            num_scalar_prefetch=2,
            grid=(NTOK // DISP_TG,),
            in_specs=[pl.BlockSpec((DISP_TG, XROW), lambda i, d, nv: (i, 0))],
            out_specs=pl.BlockSpec(memory_space=pl.ANY),
            scratch_shapes=[pltpu.VMEM((MOE_BLK, XROW), F32),
                            pltpu.SemaphoreType.DMA(()), pltpu.SemaphoreType.DMA(())]),
        compiler_params=_cparams("arbitrary"),
        name="dispatch",
    )(dest, nval, hx)


def _moe_kernel(elo_ref, ehi_ref, nval_ref, xs_ref, wg1, wu1, wd1, wg2, wu2, wd2, o_ref):
    i = pl.program_id(0)
    nval = nval_ref[i]

    @pl.when(nval > 0)
    def _():
        x = xs_ref[:, 0:D].astype(BF16)
        w_lo = xs_ref[:, D:D + 1]
        w_hi = xs_ref[:, D + 1:D + 2]

        def expert(wg, wu, wd):
            g = jnp.dot(x, wg[0, 0], preferred_element_type=F32)
            u = jnp.dot(x, wu[0, 0], preferred_element_type=F32)
            a = (g * _sigmoid(g) * u).astype(BF16)
            return jnp.dot(a, wd[0, 0], preferred_element_type=F32)

        o_ref[...] = w_lo * expert(wg1, wu1, wd1) + w_hi * expert(wg2, wu2, wd2)

    @pl.when(nval == 0)
    def _():
        o_ref[...] = jnp.zeros_like(o_ref)


def _moe(elo, ehi, nval, xs, wg, wu, wd, l):
    wspec = lambda tab: pl.BlockSpec((1, 1, D, D), lambda i, elo, ehi, nv: (l, (elo, ehi)[tab][i], 0, 0))
    return pl.pallas_call(
        _moe_kernel,
        out_shape=jax.ShapeDtypeStruct((SLOT_ROWS, D), F32),
        grid_spec=pltpu.PrefetchScalarGridSpec(
            num_scalar_prefetch=3,
            grid=(N_BLOCKS,),
            in_specs=[pl.BlockSpec((MOE_BLK, XROW), lambda i, elo, ehi, nv: (i, 0)),
                      wspec(0), wspec(0), wspec(0), wspec(1), wspec(1), wspec(1)],
            out_specs=pl.BlockSpec((MOE_BLK, D), lambda i, elo, ehi, nv: (i, 0))),
        compiler_params=_cparams("arbitrary"),
        name="experts",
    )(elo, ehi, nval, xs, wg, wu, wd, wg, wu, wd)


COMB_TG = 512


def _combine_kernel(dest_ref, ys_ref, x1_ref, gt_ref, nw_ref, sc_ref, sh_ref, *rest, last):
    if last:
        hn_ref, ybuf, sem = rest
    else:
        x2_ref, hn_ref, ybuf, sem = rest
    base = pl.program_id(0) * COMB_TG

    def row_copy(t, d):
        return pltpu.make_async_copy(ys_ref.at[pl.ds(d, 1), :], ybuf.at[pl.ds(t, 1), :], sem)

    def issue(t, carry):
        row_copy(t, dest_ref[base + t]).start()
        return carry

    lax.fori_loop(0, COMB_TG, issue, 0)

    def drain(t, carry):
        row_copy(t, dest_ref[base + t]).wait()
        return carry

    lax.fori_loop(0, COMB_TG, drain, 0)

    x2 = x1_ref[...] + gt_ref[0] * ybuf[...]
    y = _rms(x2) * nw_ref[...]
    if last:
        hn_ref[...] = y
    else:
        x2_ref[...] = x2
        hn_ref[...] = (y * (1.0 + sc_ref[0]) + sh_ref[0]).astype(hn_ref.dtype)


def _combine(dest, ys, x1, gt2, nw, sc, sh, last):
    tpb = SEQ // COMB_TG
    bidx = lambda i, d: (i // tpb, 0, 0)
    tile = pl.BlockSpec((COMB_TG, D), lambda i, d: (i, 0))
    if last:
        out_shape = jax.ShapeDtypeStruct((NTOK, D), F32)
        out_specs = tile
    else:
        out_shape = (jax.ShapeDtypeStruct((NTOK, D), F32), jax.ShapeDtypeStruct((NTOK, D), BF16))
        out_specs = (tile, tile)
    return pl.pallas_call(
        functools.partial(_combine_kernel, last=last),
        out_shape=out_shape,
        grid_spec=pltpu.PrefetchScalarGridSpec(
            num_scalar_prefetch=1,
            grid=(NTOK // COMB_TG,),
            in_specs=[pl.BlockSpec(memory_space=pl.ANY),
                      tile,
                      pl.BlockSpec((1, 1, D), bidx),
                      pl.BlockSpec((1, D), lambda i, d: (0, 0)),
                      pl.BlockSpec((1, 1, D), bidx),
                      pl.BlockSpec((1, 1, D), bidx)],
            out_specs=out_specs,
            scratch_shapes=[pltpu.VMEM((COMB_TG, D), F32), pltpu.SemaphoreType.DMA(())]),
        compiler_params=_cparams("arbitrary"),
        name="combine",
    )(dest, ys, x1, gt2, nw.reshape(1, D), sc, sh)


def _pair_tables():
    lo, hi = [], []
    for g in range(N_GROUPS):
        for a in range(GROUP_SIZE):
            for b in range(a + 1, GROUP_SIZE):
                lo.append(GROUP_SIZE * g + a)
                hi.append(GROUP_SIZE * g + b)
    return np.asarray(lo, np.int32), np.asarray(hi, np.int32)


def _slot_plan(ri, cnt):
    counts = cnt[:N_BUCKETS, 0].astype(jnp.int32)
    padded = (counts + MOE_BLK - 1) // MOE_BLK * MOE_BLK
    pad_end = jnp.cumsum(padded)
    pad_start = pad_end - padded
    dest = pad_start[ri[0]] + ri[1]
    blk_start = jnp.arange(N_BLOCKS, dtype=jnp.int32) * MOE_BLK
    bb = jnp.minimum(jnp.searchsorted(pad_end, blk_start, side='right'), N_BUCKETS - 1).astype(jnp.int32)
    nval = jnp.clip(pad_start[bb] + counts[bb] - blk_start, 0, MOE_BLK).astype(jnp.int32)
    lo_tab, hi_tab = _pair_tables()
    return dest.astype(jnp.int32), jnp.asarray(lo_tab)[bb], jnp.asarray(hi_tab)[bb], nval


def kernel(x, c, positions, w_ada, b_ada, norm1_w, w_in, conv_w, conv_b, conv_ln_w, conv_ln_b, w_conv_out,
           w_ret_out, w_out, norm2_w, w_router, router_bias, w_exp_gate, w_exp_up, w_exp_down, final_norm_w):
    mod = _ada(c, w_ada, b_ada).reshape(DEPTH, BATCH, N_MOD, 1, D)
    sh1, sc1, gt1, sh2, sc2, gt2 = (mod[:, :, i] for i in range(N_MOD))
    cos, sin = _rope_tables(positions)
    ret_tables = _ret_tables()

    w_in_bf = w_in.astype(BF16)
    w_conv_out_bf = w_conv_out.astype(BF16)
    w_ret_out_bf = w_ret_out.astype(BF16)
    w_out_bf = w_out.astype(BF16)
    wg_bf = w_exp_gate.astype(BF16)
    wu_bf = w_exp_up.astype(BF16)
    wd_bf = w_exp_down.astype(BF16)

    wr_t = w_router.T
    rhi = wr_t.astype(BF16)
    rlo = (wr_t - rhi.astype(F32)).astype(BF16)
    rb = jnp.broadcast_to(router_bias.astype(F32)[:, None], (N_EXPERTS, MIX_TM))
    tri = (jnp.arange(MIX_TM)[:, None] < jnp.arange(MIX_TM)[None, :]).astype(BF16)

    xf = x.reshape(NTOK, D)
    h = _modnorm(x, norm1_w[0], sc1[0], sh1[0]).reshape(NTOK, D)
    out = None
    for l in range(DEPTH):
        p = _inproj(h, w_in_bf, l, cos, sin)
        p3 = p.reshape(BATCH, SEQ, P_COLS)
        ya = _conv_branch(p3, l, conv_w[l], conv_b[l], conv_ln_w[l], conv_ln_b[l], w_conv_out_bf)
        rg = _retention(p3, ret_tables)
        x1, hx, ri, cnt = _mix(rg.reshape(NTOK, HEADS * DV), ya.reshape(NTOK, D), p, xf, gt1[l], sc2[l], sh2[l],
                               w_ret_out_bf, w_out_bf, l, norm2_w[l], rhi, rlo, rb, tri)
        dest, elo, ehi, nval = _slot_plan(ri, cnt)
        xs = _dispatch(dest, nval, hx)
        ys = _moe(elo, ehi, nval, xs, wg_bf, wu_bf, wd_bf, l)
        if l + 1 < DEPTH:
            xf, h = _combine(dest, ys, x1, gt2[l], norm1_w[l + 1], sc1[l + 1], sh1[l + 1], last=False)
        else:
            out = _combine(dest, ys, x1, gt2[l], final_norm_w, sc1[l], sh1[l], last=True)
    return out.reshape(BATCH, SEQ, D)
```

```python
import functools

import numpy as np
import jax
import jax.numpy as jnp
from jax import lax
from jax.experimental import pallas as pl
from jax.experimental.pallas import tpu as pltpu

F32 = jnp.float32
BF16 = jnp.bfloat16

D = 1024
BATCH = 8
SEQ = 2048
DEPTH = 4
NTOK = BATCH * SEQ
N_MOD = 6
EPS = 1e-6

CONV_K = 31
HEADS = 4
DK = 256
DV = 512
CHUNK = 128
ROPE_BASE = 10000.0
HALF = DK // 2

N_EXPERTS = 16
N_GROUPS = 4
GROUP_SIZE = 4
PAIRS = 6
N_BUCKETS = N_GROUPS * PAIRS
BUCKET_ROWS = 32

P_COLS = 9 * D
PCOL_U, PCOL_Q, PCOL_K, PCOL_V, PCOL_G, PCOL_GA, PCOL_GB = 0, 1, 2, 3, 5, 7, 8

MOE_BLK = 256
N_BLOCKS = NTOK // MOE_BLK + N_BUCKETS
SLOT_ROWS = N_BLOCKS * MOE_BLK
TAIL = 128
XROW = D + TAIL

VMEM_LIMIT = 56 * 1024 * 1024

NT_DIMS = (((1,), (1,)), ((), ()))


ROW_DMA_UNROLL = 8


def _cparams(*sem, row_dma=False):
    return pltpu.CompilerParams(dimension_semantics=sem, vmem_limit_bytes=VMEM_LIMIT,
                                disable_bounds_checks=row_dma)


def _sigmoid(x):
    return jax.nn.sigmoid(x)


def _rms(x):
    return x * lax.rsqrt(jnp.mean(x * x, axis=-1, keepdims=True) + EPS)


ADA_TN = 1536


def _ada_kernel(c_ref, w_ref, b_ref, o_ref):
    c = c_ref[...]
    sc = c * _sigmoid(c)
    o_ref[0] = jnp.dot(sc, w_ref[0], precision=lax.Precision.HIGHEST,
                       preferred_element_type=F32) + b_ref[0]


def _ada(c, w_ada, b_ada):
    nj = N_MOD * D // ADA_TN
    return pl.pallas_call(
        _ada_kernel,
        out_shape=jax.ShapeDtypeStruct((DEPTH, BATCH, N_MOD * D), F32),
        grid=(DEPTH, nj),
        in_specs=[pl.BlockSpec((BATCH, D), lambda l, j: (0, 0)),
                  pl.BlockSpec((1, D, ADA_TN), lambda l, j: (l, 0, j)),
                  pl.BlockSpec((1, 1, ADA_TN), lambda l, j: (l, 0, j))],
        out_specs=pl.BlockSpec((1, BATCH, ADA_TN), lambda l, j: (l, 0, j)),
        compiler_params=_cparams("parallel", "parallel"),
        name="ada",
    )(c, w_ada, b_ada.reshape(DEPTH, 1, N_MOD * D))


ROPE_TN = 2048


def _rope_kernel(pos_ref, inv_ref, cos_ref, sin_ref):
    ang = pos_ref[...] * inv_ref[...]
    cos_ref[...] = jnp.cos(ang)
    sin_ref[...] = jnp.sin(ang)


def _rope_tables(positions):
    inv = ROPE_BASE ** (-jnp.arange(HALF, dtype=F32) / HALF)
    pos = positions.astype(F32).reshape(NTOK, 1)
    return pl.pallas_call(
        _rope_kernel,
        out_shape=(jax.ShapeDtypeStruct((NTOK, HALF), F32),) * 2,
        grid=(NTOK // ROPE_TN,),
        in_specs=[pl.BlockSpec((ROPE_TN, 1), lambda i: (i, 0)),
                  pl.BlockSpec((1, HALF), lambda i: (0, 0))],
        out_specs=(pl.BlockSpec((ROPE_TN, HALF), lambda i: (i, 0)),) * 2,
        compiler_params=_cparams("parallel"),
        name="rope_tables",
    )(pos, inv.reshape(1, HALF))


NORM_TS = 512


def _modnorm_kernel(x_ref, w_ref, sc_ref, sh_ref, o_ref):
    y = _rms(x_ref[0]) * w_ref[...]
    o_ref[0] = (y * (1.0 + sc_ref[0]) + sh_ref[0]).astype(o_ref.dtype)


def _modnorm(x, w, sc, sh):
    return pl.pallas_call(
        _modnorm_kernel,
        out_shape=jax.ShapeDtypeStruct((BATCH, SEQ, D), BF16),
        grid=(BATCH, SEQ // NORM_TS),
        in_specs=[pl.BlockSpec((1, NORM_TS, D), lambda b, i: (b, i, 0)),
                  pl.BlockSpec((1, D), lambda b, i: (0, 0)),
                  pl.BlockSpec((1, 1, D), lambda b, i: (b, 0, 0)),
                  pl.BlockSpec((1, 1, D), lambda b, i: (b, 0, 0))],
        out_specs=pl.BlockSpec((1, NORM_TS, D), lambda b, i: (b, i, 0)),
        compiler_params=_cparams("parallel", "parallel"),
        name="modnorm",
    )(x, w.reshape(1, D), sc, sh)


INP_TM = 1024
INP_GROUPS = 9


def _inproj_kernel(h_ref, w1_ref, w2_ref, cos_ref, sin_ref, o_ref):
    j = pl.program_id(0)
    h = h_ref[...]

    def proj():
        return jnp.dot(h, w1_ref[0], preferred_element_type=F32)

    @pl.when(j == 0)
    def _():
        a = proj()
        b = jnp.dot(h, w2_ref[0], preferred_element_type=F32)
        o_ref[...] = (a * _sigmoid(b)).astype(BF16)

    @pl.when((j == 1) | (j == 2))
    def _():
        t = proj()
        scale = jnp.where(j == 2, DK ** -0.5, 1.0).astype(F32)
        cos = cos_ref[...] * scale
        sin = sin_ref[...] * scale
        for hd in range(HEADS):
            c0 = hd * DK
            t1 = t[:, c0:c0 + HALF]
            t2 = t[:, c0 + HALF:c0 + DK]
            o_ref[:, c0:c0 + HALF] = (t1 * cos - t2 * sin).astype(BF16)
            o_ref[:, c0 + HALF:c0 + DK] = (t1 * sin + t2 * cos).astype(BF16)

    @pl.when((j == 3) | (j == 4))
    def _():
        o_ref[...] = proj().astype(BF16)

    @pl.when((j == 5) | (j == 6))
    def _():
        g = proj()
        o_ref[...] = (g * _sigmoid(g)).astype(BF16)

    @pl.when(j >= 7)
    def _():
        o_ref[...] = _sigmoid(proj()).astype(BF16)


def _inproj(h, w_in_bf, l, cos, sin):
    def rope_idx(j, m):
        return (jnp.where((j == 1) | (j == 2), m, 0), 0)

    return pl.pallas_call(
        _inproj_kernel,
        out_shape=jax.ShapeDtypeStruct((NTOK, P_COLS), BF16),
        grid=(INP_GROUPS, NTOK // INP_TM),
        in_specs=[pl.BlockSpec((INP_TM, D), lambda j, m: (m, 0)),
                  pl.BlockSpec((1, D, D), lambda j, m: (l, 0, jnp.where(j == 0, 0, j + 1))),
                  pl.BlockSpec((1, D, D), lambda j, m: (l, 0, 1)),
                  pl.BlockSpec((INP_TM, HALF), rope_idx),
                  pl.BlockSpec((INP_TM, HALF), rope_idx)],
        out_specs=pl.BlockSpec((INP_TM, D), lambda j, m: (m, j)),
        compiler_params=_cparams("parallel", "parallel"),
        name="inproj",
    )(h, w_in_bf, w_in_bf, cos, sin)


CONV_TS = 512
CONV_HALO = 32
CONV_RC = 32
CONV_CW = 256
CONV_SH = CONV_TS + 24


def _conv_kernel(u_ref, halo_ref, sga_ref, cw_ref, cb_ref, lnw_ref, lnb_ref, wo_ref, o_ref,
                 buf_ref, sh_ref, acc_ref):
    i = pl.program_id(1)
    halo = halo_ref[0].astype(F32)
    buf_ref[0:CONV_HALO, :] = jnp.where(i > 0, halo, 0.0)
    buf_ref[CONV_HALO:CONV_HALO + CONV_TS, :] = u_ref[0].astype(F32)
    for r in range(1, 8):
        sh_ref[r - 1] = buf_ref[r:r + CONV_SH, :]

    def body(ci, carry):
        r0 = pl.multiple_of(ci * CONV_RC, CONV_RC)
        for cc in range(D // CONV_CW):
            cols = slice(cc * CONV_CW, (cc + 1) * CONV_CW)
            acc = jnp.zeros((CONV_RC, CONV_CW), F32)
            for off in range(2, CONV_K + 2):
                q, r = divmod(off, 8)
                if r == 0:
                    win = buf_ref[pl.ds(r0 + 8 * q, CONV_RC), cols]
                else:
                    win = sh_ref[r - 1, pl.ds(r0 + 8 * q, CONV_RC), cols]
                acc = acc + win * cw_ref[off - 2:off - 1, cols]
            acc_ref[pl.ds(r0, CONV_RC), cols] = acc + cb_ref[:, cols]
        return carry

    lax.fori_loop(0, CONV_TS // CONV_RC, body, 0)

    c = acc_ref[...]
    mu = jnp.mean(c, axis=-1, keepdims=True)
    cen = c - mu
    var = jnp.mean(cen * cen, axis=-1, keepdims=True)
    y = cen * lax.rsqrt(var + EPS) * lnw_ref[...] + lnb_ref[...]
    y = y * _sigmoid(y)
    out = jnp.dot(y.astype(BF16), wo_ref[0], preferred_element_type=F32)
    o_ref[0] = (out * sga_ref[0].astype(F32)).astype(BF16)


def _conv_branch(p3, l, conv_w, conv_b, ln_w, ln_b, w_conv_out_bf):
    hb = CONV_TS // CONV_HALO
    return pl.pallas_call(
        _conv_kernel,
        out_shape=jax.ShapeDtypeStruct((BATCH, SEQ, D), BF16),
        grid=(BATCH, SEQ // CONV_TS),
        in_specs=[pl.BlockSpec((1, CONV_TS, D), lambda b, i: (b, i, PCOL_U)),
                  pl.BlockSpec((1, CONV_HALO, D), lambda b, i: (b, jnp.maximum(i * hb - 1, 0), PCOL_U)),
                  pl.BlockSpec((1, CONV_TS, D), lambda b, i: (b, i, PCOL_GA)),
                  pl.BlockSpec((CONV_K, D), lambda b, i: (0, 0)),
                  pl.BlockSpec((1, D), lambda b, i: (0, 0)),
                  pl.BlockSpec((1, D), lambda b, i: (0, 0)),
                  pl.BlockSpec((1, D), lambda b, i: (0, 0)),
                  pl.BlockSpec((1, D, D), lambda b, i: (l, 0, 0))],
        out_specs=pl.BlockSpec((1, CONV_TS, D), lambda b, i: (b, i, 0)),
        scratch_shapes=[pltpu.VMEM((CONV_HALO + CONV_TS, D), F32),
                        pltpu.VMEM((7, CONV_SH, D), F32),
                        pltpu.VMEM((CONV_TS, D), F32)],
        compiler_params=_cparams("parallel", "parallel"),
        name="conv_branch",
    )(p3, p3, p3, conv_w, conv_b.reshape(1, D), ln_w.reshape(1, D), ln_b.reshape(1, D), w_conv_out_bf)


def _ret_tables():
    hh = jnp.arange(HEADS, dtype=F32)
    log_g = jnp.log1p(-jnp.exp2(-5.0 - hh))
    idx = jnp.arange(CHUNK, dtype=F32)
    rel = idx[:, None] - idx[None, :]
    dmask = jnp.where(rel[None] >= 0, jnp.exp(jnp.maximum(rel, 0.0)[None] * log_g[:, None, None]), 0.0)
    xi = jnp.exp((idx + 1.0)[None, :] * log_g[:, None])[..., None]
    zeta = jnp.exp((CHUNK - 1.0 - idx)[None, :] * log_g[:, None])[..., None]
    g_chunk = jnp.exp(CHUNK * log_g)[:, None, None]
    return (dmask,
            jnp.broadcast_to(xi, (HEADS, CHUNK, DK)),
            jnp.broadcast_to(zeta, (HEADS, CHUNK, DK)),
            jnp.broadcast_to(g_chunk, (HEADS, 1, DV)))


RET_TS = 512


def _ret_kernel(q_ref, k_ref, v01_ref, v23_ref, g01_ref, g23_ref, dm_ref, xi_ref, zt_ref, gc_ref, o_ref, st_ref):
    @pl.when(pl.program_id(1) == 0)
    def _():
        st_ref[...] = jnp.zeros_like(st_ref)

    def body(c, carry):
        r0 = pl.multiple_of(c * CHUNK, CHUNK)
        rows = pl.ds(r0, CHUNK)
        for hd in range(HEADS):
            v_ref, g_ref = (v01_ref, g01_ref) if hd < 2 else (v23_ref, g23_ref)
            vcols = slice((hd % 2) * DV, (hd % 2 + 1) * DV)
            qc = q_ref[0, rows, hd * DK:(hd + 1) * DK]
            kc = k_ref[0, rows, hd * DK:(hd + 1) * DK]
            vc = v_ref[0, rows, vcols]
            scores = lax.dot_general(qc, kc, NT_DIMS, preferred_element_type=F32) * dm_ref[hd]
            inner = jnp.dot(scores.astype(BF16), vc, preferred_element_type=F32)
            st = st_ref[hd]
            qx = (qc.astype(F32) * xi_ref[hd]).astype(BF16)
            cross = jnp.dot(qx, st.astype(BF16), preferred_element_type=F32)
            kzt = (kc.astype(F32) * zt_ref[hd]).T.astype(BF16)
            st_ref[hd] = st * gc_ref[hd] + jnp.dot(kzt, vc, preferred_element_type=F32)
            o = inner + cross
            mu = jnp.mean(o, axis=-1, keepdims=True)
            cen = o - mu
            var = jnp.mean(cen * cen, axis=-1, keepdims=True)
            r = cen * lax.rsqrt(var + EPS)
            o_ref[0, rows, hd * DV:(hd + 1) * DV] = (r * g_ref[0, rows, vcols].astype(F32)).astype(BF16)
        return carry

    lax.fori_loop(0, RET_TS // CHUNK, body, 0)


def _retention(p3, tables):
    dmask, xi, zeta, gch = tables
    tile = lambda col: pl.BlockSpec((1, RET_TS, D), lambda b, i: (b, i, col))
    full = lambda a: pl.BlockSpec(a.shape, lambda b, i: (0, 0, 0))
    return pl.pallas_call(
        _ret_kernel,
        out_shape=jax.ShapeDtypeStruct((BATCH, SEQ, HEADS * DV), BF16),
        grid=(BATCH, SEQ // RET_TS),
        in_specs=[tile(PCOL_Q), tile(PCOL_K), tile(PCOL_V), tile(PCOL_V + 1), tile(PCOL_G), tile(PCOL_G + 1),
                  full(dmask), full(xi), full(zeta), full(gch)],
        out_specs=pl.BlockSpec((1, RET_TS, HEADS * DV), lambda b, i: (b, i, 0)),
        scratch_shapes=[pltpu.VMEM((HEADS, DK, DV), F32)],
        compiler_params=_cparams("parallel", "arbitrary"),
        name="retention",
    )(p3, p3, p3, p3, p3, p3, dmask, xi, zeta, gch)


MIX_TM = 512


def _route_rows(s, sb):
    row = lambda a, e: a[e:e + 1, :]
    best = None
    gidx = None
    for g in range(N_GROUPS):
        v = [row(sb, GROUP_SIZE * g + i) for i in range(GROUP_SIZE)]
        pair_sums = [v[a] + v[b] for a in range(GROUP_SIZE) for b in range(a + 1, GROUP_SIZE)]
        gs = functools.reduce(jnp.maximum, pair_sums)
        if g == 0:
            best, gidx = gs, jnp.zeros(gs.shape, jnp.int32)
        else:
            upd = gs > best
            gidx = jnp.where(upd, g, gidx)
            best = jnp.where(upd, gs, best)

    def pick(a, i):
        out = row(a, i)
        for g in range(1, N_GROUPS):
            out = jnp.where(gidx == g, row(a, GROUP_SIZE * g + i), out)
        return out

    vb = [pick(sb, i) for i in range(GROUP_SIZE)]
    vs = [pick(s, i) for i in range(GROUP_SIZE)]
    m1, i1, s1 = vb[0], jnp.zeros(gidx.shape, jnp.int32), vs[0]
    for i in range(1, GROUP_SIZE):
        upd = vb[i] > m1
        m1 = jnp.where(upd, vb[i], m1)
        i1 = jnp.where(upd, i, i1)
        s1 = jnp.where(upd, vs[i], s1)
    m2 = i2 = s2 = None
    for i in range(GROUP_SIZE):
        cand = jnp.where(i1 == i, -jnp.inf, vb[i])
        if m2 is None:
            m2, i2, s2 = cand, jnp.zeros(gidx.shape, jnp.int32), vs[0]
        else:
            upd = cand > m2
            m2 = jnp.where(upd, cand, m2)
            i2 = jnp.where(upd, i, i2)
            s2 = jnp.where(upd, vs[i], s2)
    den = s1 + s2
    w1 = s1 / den
    w2 = s2 / den
    first_low = i1 < i2
    lo = jnp.minimum(i1, i2)
    hi = jnp.maximum(i1, i2)
    pair = jnp.where(lo == 0, hi - 1, jnp.where(lo == 1, hi + 1, 5))
    bucket = gidx * PAIRS + pair
    return bucket, jnp.where(first_low, w1, w2), jnp.where(first_low, w2, w1)


def _mix_kernel(rg_ref, ya_ref, sgb_ref, x_ref, gt_ref, sc_ref, sh_ref, wr_ref, wo_ref, n2_ref,
                rhi_ref, rlo_ref, rb_ref, tri_ref,
                x1_ref, hx_ref, ri_ref, cnt_ref, carry_ref):
    m = pl.program_id(0)

    @pl.when(m == 0)
    def _():
        carry_ref[...] = jnp.zeros_like(carry_ref)

    yb = jnp.dot(rg_ref[...], wr_ref[0], preferred_element_type=F32)
    y = ya_ref[...].astype(F32) + sgb_ref[...].astype(F32) * yb
    o = jnp.dot(y.astype(BF16), wo_ref[0], preferred_element_type=F32)
    x1 = x_ref[...] + gt_ref[0] * o
    x1_ref[...] = x1
    h2 = _rms(x1) * n2_ref[...] * (1.0 + sc_ref[0]) + sh_ref[0]
    hx_ref[:, 0:D] = h2

    hi = h2.astype(BF16)
    lo = (h2 - hi.astype(F32)).astype(BF16)
    rhi = rhi_ref[...]
    logits = (lax.dot_general(rhi, hi, NT_DIMS, preferred_element_type=F32)
              + lax.dot_general(rhi, lo, NT_DIMS, preferred_element_type=F32)
              + lax.dot_general(rlo_ref[...], hi, NT_DIMS, preferred_element_type=F32))
    s = _sigmoid(logits)
    bucket, w_lo, w_hi = _route_rows(s, s + rb_ref[...])

    onehot = (lax.broadcasted_iota(jnp.int32, (BUCKET_ROWS, MIX_TM), 0) == bucket).astype(F32)
    prefix = jnp.dot(onehot.astype(BF16), tri_ref[...], preferred_element_type=F32)
    carry = carry_ref[:, 0:1]
    rank = jnp.sum(onehot * (prefix + carry), axis=0, keepdims=True)
    new_carry = carry + jnp.sum(onehot, axis=1, keepdims=True)
    carry_ref[...] = jnp.broadcast_to(new_carry, carry_ref.shape)
    cnt_ref[...] = jnp.broadcast_to(new_carry, cnt_ref.shape)

    rid = lax.broadcasted_iota(jnp.int32, (8, MIX_TM), 0)
    ri_ref[...] = jnp.where(rid == 0, bucket, jnp.where(rid == 1, rank.astype(jnp.int32), 0))
    wid = lax.broadcasted_iota(jnp.int32, (TAIL, MIX_TM), 0)
    wrows = jnp.where(wid == 0, w_lo, jnp.where(wid == 1, w_hi, 0.0))
    hx_ref[:, D:XROW] = wrows.T


def _mix(rg, ya, p, x, gt1, sc2, sh2, w_ret_out_bf, w_out_bf, l, norm2_w, rhi, rlo, rb, tri):
    tpb = SEQ // MIX_TM
    bidx = lambda m: (m // tpb, 0, 0)
    return pl.pallas_call(
        _mix_kernel,
        out_shape=(jax.ShapeDtypeStruct((NTOK, D), F32),
                   jax.ShapeDtypeStruct((NTOK, XROW), F32),
                   jax.ShapeDtypeStruct((8, NTOK), jnp.int32),
                   jax.ShapeDtypeStruct((BUCKET_ROWS, 128), F32)),
        grid=(NTOK // MIX_TM,),
        in_specs=[pl.BlockSpec((MIX_TM, HEADS * DV), lambda m: (m, 0)),
                  pl.BlockSpec((MIX_TM, D), lambda m: (m, 0)),
                  pl.BlockSpec((MIX_TM, D), lambda m: (m, PCOL_GB)),
                  pl.BlockSpec((MIX_TM, D), lambda m: (m, 0)),
                  pl.BlockSpec((1, 1, D), bidx),
                  pl.BlockSpec((1, 1, D), bidx),
                  pl.BlockSpec((1, 1, D), bidx),
                  pl.BlockSpec((1, HEADS * DV, D), lambda m: (l, 0, 0)),
                  pl.BlockSpec((1, D, D), lambda m: (l, 0, 0)),
                  pl.BlockSpec((1, D), lambda m: (0, 0)),
                  pl.BlockSpec((N_EXPERTS, D), lambda m: (0, 0)),
                  pl.BlockSpec((N_EXPERTS, D), lambda m: (0, 0)),
                  pl.BlockSpec((N_EXPERTS, MIX_TM), lambda m: (0, 0)),
                  pl.BlockSpec((MIX_TM, MIX_TM), lambda m: (0, 0))],
        out_specs=(pl.BlockSpec((MIX_TM, D), lambda m: (m, 0)),
                   pl.BlockSpec((MIX_TM, XROW), lambda m: (m, 0)),
                   pl.BlockSpec((8, MIX_TM), lambda m: (0, m)),
                   pl.BlockSpec((BUCKET_ROWS, 128), lambda m: (0, 0))),
        scratch_shapes=[pltpu.VMEM((BUCKET_ROWS, 128), F32)],
        compiler_params=_cparams("arbitrary"),
        name="mix_route",
    )(rg, ya, p, x, gt1, sc2, sh2, w_ret_out_bf, w_out_bf, norm2_w.reshape(1, D), rhi, rlo, rb, tri)


DISP_TG = 512


def _dispatch_kernel(dest_ref, nval_ref, src_ref, xs_ref, zero_ref, sem, zsem):
    step = pl.program_id(0)
    base = step * DISP_TG

    @pl.when(step == 0)
    def _():
        zero_ref[...] = jnp.zeros_like(zero_ref)

        def zero_copy(b):
            r0 = pl.multiple_of(b * MOE_BLK, MOE_BLK)
            return pltpu.make_async_copy(zero_ref, xs_ref.at[pl.ds(r0, MOE_BLK), :], zsem)

        def zissue(b, carry):
            @pl.when(nval_ref[b] < MOE_BLK)
            def _():
                zero_copy(b).start()
            return carry

        lax.fori_loop(0, N_BLOCKS, zissue, 0)

        def zdrain(b, carry):
            @pl.when(nval_ref[b] < MOE_BLK)
            def _():
                zero_copy(b).wait()
            return carry

        lax.fori_loop(0, N_BLOCKS, zdrain, 0)

    def row_copy(t, d):
        return pltpu.make_async_copy(src_ref.at[pl.ds(t, 1), :], xs_ref.at[pl.ds(d, 1), :], sem)

    def issue(t, carry):
        row_copy(t, dest_ref[base + t]).start()
        return carry

    lax.fori_loop(0, DISP_TG, issue, 0, unroll=ROW_DMA_UNROLL)
    pltpu.make_async_copy(src_ref, xs_ref.at[pl.ds(0, DISP_TG), :], sem).wait()


def _dispatch(dest, nval, hx):
    return pl.pallas_call(
        _dispatch_kernel,
        out_shape=jax.ShapeDtypeStruct((SLOT_ROWS, XROW), F32),
        grid_spec=pltpu.PrefetchScalarGridSpec(
            num_scalar_prefetch=2,
            grid=(NTOK // DISP_TG,),
            in_specs=[pl.BlockSpec((DISP_TG, XROW), lambda i, d, nv: (i, 0))],
            out_specs=pl.BlockSpec(memory_space=pl.ANY),
            scratch_shapes=[pltpu.VMEM((MOE_BLK, XROW), F32),
                            pltpu.SemaphoreType.DMA(()), pltpu.SemaphoreType.DMA(())]),
        compiler_params=_cparams("arbitrary", row_dma=True),
        name="dispatch",
    )(dest, nval, hx)


def _moe_kernel(elo_ref, ehi_ref, nval_ref, xs_ref, wg1, wu1, wd1, wg2, wu2, wd2, o_ref):
    i = pl.program_id(0)
    nval = nval_ref[i]

    @pl.when(nval > 0)
    def _():
        x = xs_ref[:, 0:D].astype(BF16)
        w_lo = xs_ref[:, D:D + 1]
        w_hi = xs_ref[:, D + 1:D + 2]

        def expert(wg, wu, wd):
            g = jnp.dot(x, wg[0, 0], preferred_element_type=F32)
            u = jnp.dot(x, wu[0, 0], preferred_element_type=F32)
            a = (g * _sigmoid(g) * u).astype(BF16)
            return jnp.dot(a, wd[0, 0], preferred_element_type=F32)

        o_ref[...] = w_lo * expert(wg1, wu1, wd1) + w_hi * expert(wg2, wu2, wd2)

    @pl.when(nval == 0)
    def _():
        o_ref[...] = jnp.zeros_like(o_ref)


def _moe(elo, ehi, nval, xs, wg, wu, wd, l):
    wspec = lambda tab: pl.BlockSpec((1, 1, D, D), lambda i, elo, ehi, nv: (l, (elo, ehi)[tab][i], 0, 0))
    return pl.pallas_call(
        _moe_kernel,
        out_shape=jax.ShapeDtypeStruct((SLOT_ROWS, D), F32),
        grid_spec=pltpu.PrefetchScalarGridSpec(
            num_scalar_prefetch=3,
            grid=(N_BLOCKS,),
            in_specs=[pl.BlockSpec((MOE_BLK, XROW), lambda i, elo, ehi, nv: (i, 0)),
                      wspec(0), wspec(0), wspec(0), wspec(1), wspec(1), wspec(1)],
            out_specs=pl.BlockSpec((MOE_BLK, D), lambda i, elo, ehi, nv: (i, 0))),
        compiler_params=_cparams("arbitrary"),
        name="experts",
    )(elo, ehi, nval, xs, wg, wu, wd, wg, wu, wd)


COMB_TG = 512


def _combine_kernel(dest_ref, ys_ref, x1_ref, gt_ref, nw_ref, sc_ref, sh_ref, *rest, last):
    if last:
        hn_ref, ybuf, sem = rest
    else:
        x2_ref, hn_ref, ybuf, sem = rest
    step = pl.program_id(0)
    slot = step % 2

    def gather_tile(tile, into):
        base = tile * COMB_TG

        def issue(t, carry):
            src = ys_ref.at[pl.ds(dest_ref[base + t], 1), :]
            pltpu.make_async_copy(src, ybuf.at[into, pl.ds(t, 1), :], sem.at[into]).start()
            return carry

        lax.fori_loop(0, COMB_TG, issue, 0, unroll=ROW_DMA_UNROLL)

    @pl.when(step == 0)
    def _():
        gather_tile(0, 0)

    @pl.when(step + 1 < pl.num_programs(0))
    def _():
        gather_tile(step + 1, 1 - slot)

    pltpu.make_async_copy(ys_ref.at[pl.ds(0, COMB_TG), :], ybuf.at[slot], sem.at[slot]).wait()

    x2 = x1_ref[...] + gt_ref[0] * ybuf[slot]
    y = _rms(x2) * nw_ref[...]
    if last:
        hn_ref[...] = y
    else:
        x2_ref[...] = x2
        hn_ref[...] = (y * (1.0 + sc_ref[0]) + sh_ref[0]).astype(hn_ref.dtype)


def _combine(dest, ys, x1, gt2, nw, sc, sh, last):
    tpb = SEQ // COMB_TG
    bidx = lambda i, d: (i // tpb, 0, 0)
    tile = pl.BlockSpec((COMB_TG, D), lambda i, d: (i, 0))
    if last:
        out_shape = jax.ShapeDtypeStruct((NTOK, D), F32)
        out_specs = tile
    else:
        out_shape = (jax.ShapeDtypeStruct((NTOK, D), F32), jax.ShapeDtypeStruct((NTOK, D), BF16))
        out_specs = (tile, tile)
    return pl.pallas_call(
        functools.partial(_combine_kernel, last=last),
        out_shape=out_shape,
        grid_spec=pltpu.PrefetchScalarGridSpec(
            num_scalar_prefetch=1,
            grid=(NTOK // COMB_TG,),
            in_specs=[pl.BlockSpec(memory_space=pl.ANY),
                      tile,
                      pl.BlockSpec((1, 1, D), bidx),
                      pl.BlockSpec((1, D), lambda i, d: (0, 0)),
                      pl.BlockSpec((1, 1, D), bidx),
                      pl.BlockSpec((1, 1, D), bidx)],
            out_specs=out_specs,
            scratch_shapes=[pltpu.VMEM((2, COMB_TG, D), F32), pltpu.SemaphoreType.DMA((2,))]),
        compiler_params=_cparams("arbitrary", row_dma=True),
        name="combine",
    )(dest, ys, x1, gt2, nw.reshape(1, D), sc, sh)


def _pair_tables():
    lo, hi = [], []
    for g in range(N_GROUPS):
        for a in range(GROUP_SIZE):
            for b in range(a + 1, GROUP_SIZE):
                lo.append(GROUP_SIZE * g + a)
                hi.append(GROUP_SIZE * g + b)
    return np.asarray(lo, np.int32), np.asarray(hi, np.int32)


def _slot_plan(ri, cnt):
    counts = cnt[:N_BUCKETS, 0].astype(jnp.int32)
    padded = (counts + MOE_BLK - 1) // MOE_BLK * MOE_BLK
    pad_end = jnp.cumsum(padded)
    pad_start = pad_end - padded
    dest = pad_start[ri[0]] + ri[1]
    blk_start = jnp.arange(N_BLOCKS, dtype=jnp.int32) * MOE_BLK
    bb = jnp.minimum(jnp.searchsorted(pad_end, blk_start, side='right'), N_BUCKETS - 1).astype(jnp.int32)
    nval = jnp.clip(pad_start[bb] + counts[bb] - blk_start, 0, MOE_BLK).astype(jnp.int32)
    lo_tab, hi_tab = _pair_tables()
    return dest.astype(jnp.int32), jnp.asarray(lo_tab)[bb], jnp.asarray(hi_tab)[bb], nval


def kernel(x, c, positions, w_ada, b_ada, norm1_w, w_in, conv_w, conv_b, conv_ln_w, conv_ln_b, w_conv_out,
           w_ret_out, w_out, norm2_w, w_router, router_bias, w_exp_gate, w_exp_up, w_exp_down, final_norm_w):
    mod = _ada(c, w_ada, b_ada).reshape(DEPTH, BATCH, N_MOD, 1, D)
    sh1, sc1, gt1, sh2, sc2, gt2 = (mod[:, :, i] for i in range(N_MOD))
    cos, sin = _rope_tables(positions)
    ret_tables = _ret_tables()

    w_in_bf = w_in.astype(BF16)
    w_conv_out_bf = w_conv_out.astype(BF16)
    w_ret_out_bf = w_ret_out.astype(BF16)
    w_out_bf = w_out.astype(BF16)
    wg_bf = w_exp_gate.astype(BF16)
    wu_bf = w_exp_up.astype(BF16)
    wd_bf = w_exp_down.astype(BF16)

    wr_t = w_router.T
    rhi = wr_t.astype(BF16)
    rlo = (wr_t - rhi.astype(F32)).astype(BF16)
    rb = jnp.broadcast_to(router_bias.astype(F32)[:, None], (N_EXPERTS, MIX_TM))
    tri = (jnp.arange(MIX_TM)[:, None] < jnp.arange(MIX_TM)[None, :]).astype(BF16)

    xf = x.reshape(NTOK, D)
    h = _modnorm(x, norm1_w[0], sc1[0], sh1[0]).reshape(NTOK, D)
    out = None
    for l in range(DEPTH):
        p = _inproj(h, w_in_bf, l, cos, sin)
        p3 = p.reshape(BATCH, SEQ, P_COLS)
        ya = _conv_branch(p3, l, conv_w[l], conv_b[l], conv_ln_w[l], conv_ln_b[l], w_conv_out_bf)
        rg = _retention(p3, ret_tables)
        x1, hx, ri, cnt = _mix(rg.reshape(NTOK, HEADS * DV), ya.reshape(NTOK, D), p, xf, gt1[l], sc2[l], sh2[l],
                               w_ret_out_bf, w_out_bf, l, norm2_w[l], rhi, rlo, rb, tri)
        dest, elo, ehi, nval = _slot_plan(ri, cnt)
        xs = _dispatch(dest, nval, hx)
        ys = _moe(elo, ehi, nval, xs, wg_bf, wu_bf, wd_bf, l)
        if l + 1 < DEPTH:
            xf, h = _combine(dest, ys, x1, gt2[l], norm1_w[l + 1], sc1[l + 1], sh1[l + 1], last=False)
        else:
            out = _combine(dest, ys, x1, gt2[l], final_norm_w, sc1[l], sh1[l], last=True)
    return out.reshape(BATCH, SEQ, D)
```

```python
import functools

import numpy as np
import jax
import jax.numpy as jnp
from jax import lax
from jax.experimental import pallas as pl
from jax.experimental.pallas import tpu as pltpu

F32 = jnp.float32
BF16 = jnp.bfloat16

D = 1024
BATCH = 8
SEQ = 2048
DEPTH = 4
NTOK = BATCH * SEQ
N_MOD = 6
EPS = 1e-6

CONV_K = 31
HEADS = 4
DK = 256
DV = 512
CHUNK = 128
ROPE_BASE = 10000.0
HALF = DK // 2

N_EXPERTS = 16
N_GROUPS = 4
GROUP_SIZE = 4
PAIRS = 6
N_BUCKETS = N_GROUPS * PAIRS
BUCKET_ROWS = 32

P_COLS = 9 * D
PCOL_U, PCOL_Q, PCOL_K, PCOL_V, PCOL_G, PCOL_GA, PCOL_GB = 0, 1, 2, 3, 5, 7, 8

MOE_BLK = 256
N_BLOCKS = NTOK // MOE_BLK + N_BUCKETS
SLOT_ROWS = N_BLOCKS * MOE_BLK
TAIL = 128
XROW = D + TAIL

VMEM_LIMIT = 56 * 1024 * 1024

NT_DIMS = (((1,), (1,)), ((), ()))


ROW_DMA_UNROLL = 8


def _cparams(*sem, row_dma=False):
    return pltpu.CompilerParams(dimension_semantics=sem, vmem_limit_bytes=VMEM_LIMIT,
                                disable_bounds_checks=row_dma)


def _sigmoid(x):
    return jax.nn.sigmoid(x)


def _rms(x):
    return x * lax.rsqrt(jnp.mean(x * x, axis=-1, keepdims=True) + EPS)


ADA_TN = 1536


def _ada_kernel(c_ref, w_ref, b_ref, o_ref):
    c = c_ref[...]
    sc = c * _sigmoid(c)
    o_ref[0] = jnp.dot(sc, w_ref[0], precision=lax.Precision.HIGHEST,
                       preferred_element_type=F32) + b_ref[0]


def _ada(c, w_ada, b_ada):
    nj = N_MOD * D // ADA_TN
    return pl.pallas_call(
        _ada_kernel,
        out_shape=jax.ShapeDtypeStruct((DEPTH, BATCH, N_MOD * D), F32),
        grid=(DEPTH, nj),
        in_specs=[pl.BlockSpec((BATCH, D), lambda l, j: (0, 0)),
                  pl.BlockSpec((1, D, ADA_TN), lambda l, j: (l, 0, j)),
                  pl.BlockSpec((1, 1, ADA_TN), lambda l, j: (l, 0, j))],
        out_specs=pl.BlockSpec((1, BATCH, ADA_TN), lambda l, j: (l, 0, j)),
        compiler_params=_cparams("parallel", "parallel"),
        name="ada",
    )(c, w_ada, b_ada.reshape(DEPTH, 1, N_MOD * D))


ROPE_TN = 2048


def _rope_kernel(pos_ref, inv_ref, cos_ref, sin_ref):
    ang = pos_ref[...] * inv_ref[...]
    cos_ref[...] = jnp.cos(ang)
    sin_ref[...] = jnp.sin(ang)


def _rope_tables(positions):
    inv = ROPE_BASE ** (-jnp.arange(HALF, dtype=F32) / HALF)
    pos = positions.astype(F32).reshape(NTOK, 1)
    return pl.pallas_call(
        _rope_kernel,
        out_shape=(jax.ShapeDtypeStruct((NTOK, HALF), F32),) * 2,
        grid=(NTOK // ROPE_TN,),
        in_specs=[pl.BlockSpec((ROPE_TN, 1), lambda i: (i, 0)),
                  pl.BlockSpec((1, HALF), lambda i: (0, 0))],
        out_specs=(pl.BlockSpec((ROPE_TN, HALF), lambda i: (i, 0)),) * 2,
        compiler_params=_cparams("parallel"),
        name="rope_tables",
    )(pos, inv.reshape(1, HALF))


NORM_TS = 512


def _modnorm_kernel(x_ref, w_ref, sc_ref, sh_ref, o_ref):
    y = _rms(x_ref[0]) * w_ref[...]
    o_ref[0] = (y * (1.0 + sc_ref[0]) + sh_ref[0]).astype(o_ref.dtype)


def _modnorm(x, w, sc, sh):
    return pl.pallas_call(
        _modnorm_kernel,
        out_shape=jax.ShapeDtypeStruct((BATCH, SEQ, D), BF16),
        grid=(BATCH, SEQ // NORM_TS),
        in_specs=[pl.BlockSpec((1, NORM_TS, D), lambda b, i: (b, i, 0)),
                  pl.BlockSpec((1, D), lambda b, i: (0, 0)),
                  pl.BlockSpec((1, 1, D), lambda b, i: (b, 0, 0)),
                  pl.BlockSpec((1, 1, D), lambda b, i: (b, 0, 0))],
        out_specs=pl.BlockSpec((1, NORM_TS, D), lambda b, i: (b, i, 0)),
        compiler_params=_cparams("parallel", "parallel"),
        name="modnorm",
    )(x, w.reshape(1, D), sc, sh)


INP_TM = 1024
INP_GROUPS = 9
EXP_SLAB_ROWS = 512
EXP_SLABS_PER_MAT = D // EXP_SLAB_ROWS
EXP_SLABS = N_EXPERTS * EXP_SLABS_PER_MAT


def _inproj_kernel(h_ref, w1_ref, w2_ref, cos_ref, sin_ref, eg_ref, eu_ref, ed_ref,
                   o_ref, og_ref, ou_ref, od_ref, wb1_ref, wb2_ref):
    j = pl.program_id(0)
    m = pl.program_id(1)
    h = h_ref[...]

    step = j * pl.num_programs(1) + m
    for t, (src, dst) in enumerate(((eg_ref, og_ref), (eu_ref, ou_ref), (ed_ref, od_ref))):
        @pl.when((step >= EXP_SLABS * t) & (step < EXP_SLABS * (t + 1)))
        def _():
            dst[0] = src[0, 0].astype(BF16)

    @pl.when(m == 0)
    def _():
        wb1_ref[...] = w1_ref[0].astype(BF16)

    @pl.when((m == 0) & (j == 0))
    def _():
        wb2_ref[...] = w2_ref[0].astype(BF16)

    def proj():
        return jnp.dot(h, wb1_ref[...], preferred_element_type=F32)

    @pl.when(j == 0)
    def _():
        a = proj()
        b = jnp.dot(h, wb2_ref[...], preferred_element_type=F32)
        o_ref[...] = (a * _sigmoid(b)).astype(BF16)

    @pl.when((j == 1) | (j == 2))
    def _():
        t = proj()
        scale = jnp.where(j == 2, DK ** -0.5, 1.0).astype(F32)
        cos = cos_ref[...] * scale
        sin = sin_ref[...] * scale
        for hd in range(HEADS):
            c0 = hd * DK
            t1 = t[:, c0:c0 + HALF]
            t2 = t[:, c0 + HALF:c0 + DK]
            o_ref[:, c0:c0 + HALF] = (t1 * cos - t2 * sin).astype(BF16)
            o_ref[:, c0 + HALF:c0 + DK] = (t1 * sin + t2 * cos).astype(BF16)

    @pl.when((j == 3) | (j == 4))
    def _():
        o_ref[...] = proj().astype(BF16)

    @pl.when((j == 5) | (j == 6))
    def _():
        g = proj()
        o_ref[...] = (g * _sigmoid(g)).astype(BF16)

    @pl.when(j >= 7)
    def _():
        o_ref[...] = _sigmoid(proj()).astype(BF16)


def _inproj(h, w_in, l, cos, sin, w_exp):
    n_m = NTOK // INP_TM
    assert 3 * EXP_SLABS <= INP_GROUPS * n_m

    def rope_idx(j, m):
        return (jnp.where((j == 1) | (j == 2), m, 0), 0)

    def slab(t, j, m):
        s = jnp.clip(j * n_m + m - EXP_SLABS * t, 0, EXP_SLABS - 1)
        return s // EXP_SLABS_PER_MAT, s % EXP_SLABS_PER_MAT

    def exp_in(t):
        return pl.BlockSpec((1, 1, EXP_SLAB_ROWS, D), lambda j, m: (l, *slab(t, j, m), 0))

    def exp_out(t):
        return pl.BlockSpec((1, EXP_SLAB_ROWS, D), lambda j, m: (*slab(t, j, m), 0))

    exp_shape = jax.ShapeDtypeStruct((N_EXPERTS, D, D), BF16)
    return pl.pallas_call(
        _inproj_kernel,
        out_shape=(jax.ShapeDtypeStruct((NTOK, P_COLS), BF16), exp_shape, exp_shape, exp_shape),
        grid=(INP_GROUPS, n_m),
        in_specs=[pl.BlockSpec((INP_TM, D), lambda j, m: (m, 0)),
                  pl.BlockSpec((1, D, D), lambda j, m: (l, 0, jnp.where(j == 0, 0, j + 1))),
                  pl.BlockSpec((1, D, D), lambda j, m: (l, 0, 1), pipeline_mode=pl.Buffered(1)),
                  pl.BlockSpec((INP_TM, HALF), rope_idx),
                  pl.BlockSpec((INP_TM, HALF), rope_idx),
                  exp_in(0), exp_in(1), exp_in(2)],
        out_specs=(pl.BlockSpec((INP_TM, D), lambda j, m: (m, j)), exp_out(0), exp_out(1), exp_out(2)),
        scratch_shapes=[pltpu.VMEM((D, D), BF16), pltpu.VMEM((D, D), BF16)],
        compiler_params=_cparams("arbitrary", "arbitrary"),
        name="inproj",
    )(h, w_in, w_in, cos, sin, *w_exp)


CONV_TS = 512
CONV_HALO = 32
CONV_RC = 64
CONV_CW = 128
CONV_SH = CONV_TS + 24
SUBLANES = 8


def _conv_kernel(u_ref, halo_ref, sga_ref, cw_ref, cb_ref, lnw_ref, lnb_ref, wo_ref, o_ref,
                 buf_ref, sh_ref, acc_ref, wbf_ref):
    i = pl.program_id(1)

    @pl.when((pl.program_id(0) == 0) & (i == 0))
    def _():
        wbf_ref[...] = wo_ref[0].astype(BF16)

    halo = halo_ref[0].astype(F32)
    buf_ref[0:CONV_HALO, :] = jnp.where(i > 0, halo, 0.0)
    buf_ref[CONV_HALO:CONV_HALO + CONV_TS, :] = u_ref[0].astype(F32)
    for r in range(1, SUBLANES):
        sh_ref[r - 1] = buf_ref[r:r + CONV_SH, :]

    groups = CONV_RC // SUBLANES

    def body(ci, carry):
        r0 = pl.multiple_of(ci * CONV_RC, CONV_RC)
        for cc in range(D // CONV_CW):
            cols = slice(cc * CONV_CW, (cc + 1) * CONV_CW)
            accs = [cb_ref[:, cols]] * groups
            for off in range(2, CONV_K + 2):
                q, r = divmod(off, SUBLANES)
                w8 = cw_ref[off - 2, :, cols]
                for g in range(groups):
                    rows = pl.ds(r0 + SUBLANES * (q + g), SUBLANES)
                    win = buf_ref[rows, cols] if r == 0 else sh_ref[r - 1, rows, cols]
                    accs[g] = accs[g] + win * w8
            for g in range(groups):
                acc_ref[pl.ds(r0 + SUBLANES * g, SUBLANES), cols] = accs[g]
        return carry

    lax.fori_loop(0, CONV_TS // CONV_RC, body, 0)

    c = acc_ref[...]
    mu = jnp.mean(c, axis=-1, keepdims=True)
    cen = c - mu
    var = jnp.mean(cen * cen, axis=-1, keepdims=True)
    y = cen * lax.rsqrt(var + EPS) * lnw_ref[...] + lnb_ref[...]
    y = y * _sigmoid(y)
    out = jnp.dot(y.astype(BF16), wbf_ref[...], preferred_element_type=F32)
    o_ref[0] = (out * sga_ref[0].astype(F32)).astype(BF16)


def _conv_branch(p3, l, conv_w, conv_b, ln_w, ln_b, w_conv_out):
    hb = CONV_TS // CONV_HALO
    cw8 = jnp.broadcast_to(conv_w[:, None, :], (CONV_K, SUBLANES, D))
    cb8 = jnp.broadcast_to(conv_b[None, :], (SUBLANES, D))
    return pl.pallas_call(
        _conv_kernel,
        out_shape=jax.ShapeDtypeStruct((BATCH, SEQ, D), BF16),
        grid=(BATCH, SEQ // CONV_TS),
        in_specs=[pl.BlockSpec((1, CONV_TS, D), lambda b, i: (b, i, PCOL_U)),
                  pl.BlockSpec((1, CONV_HALO, D), lambda b, i: (b, jnp.maximum(i * hb - 1, 0), PCOL_U)),
                  pl.BlockSpec((1, CONV_TS, D), lambda b, i: (b, i, PCOL_GA)),
                  pl.BlockSpec((CONV_K, SUBLANES, D), lambda b, i: (0, 0, 0)),
                  pl.BlockSpec((SUBLANES, D), lambda b, i: (0, 0)),
                  pl.BlockSpec((1, D), lambda b, i: (0, 0)),
                  pl.BlockSpec((1, D), lambda b, i: (0, 0)),
                  pl.BlockSpec((1, D, D), lambda b, i: (l, 0, 0), pipeline_mode=pl.Buffered(1))],
        out_specs=pl.BlockSpec((1, CONV_TS, D), lambda b, i: (b, i, 0)),
        scratch_shapes=[pltpu.VMEM((CONV_HALO + CONV_TS, D), F32),
                        pltpu.VMEM((SUBLANES - 1, CONV_SH, D), F32),
                        pltpu.VMEM((CONV_TS, D), F32),
                        pltpu.VMEM((D, D), BF16)],
        compiler_params=_cparams("arbitrary", "arbitrary"),
        name="conv_branch",
    )(p3, p3, p3, cw8, cb8, ln_w.reshape(1, D), ln_b.reshape(1, D), w_conv_out)


def _ret_tables():
    hh = jnp.arange(HEADS, dtype=F32)
    log_g = jnp.log1p(-jnp.exp2(-5.0 - hh))
    idx = jnp.arange(CHUNK, dtype=F32)
    rel = idx[:, None] - idx[None, :]
    dmask = jnp.where(rel[None] >= 0, jnp.exp(jnp.maximum(rel, 0.0)[None] * log_g[:, None, None]), 0.0)
    xi = jnp.exp((idx + 1.0)[None, :] * log_g[:, None])[..., None]
    zeta = jnp.exp((CHUNK - 1.0 - idx)[None, :] * log_g[:, None])[..., None]
    g_chunk = jnp.exp(CHUNK * log_g)[:, None, None]
    return (dmask,
            jnp.broadcast_to(xi, (HEADS, CHUNK, DK)),
            jnp.broadcast_to(zeta, (HEADS, CHUNK, DK)),
            jnp.broadcast_to(g_chunk, (HEADS, 1, DV)))


RET_TS = 512


def _ret_kernel(q_ref, k_ref, v01_ref, v23_ref, g01_ref, g23_ref, dm_ref, xi_ref, zt_ref, gc_ref, o_ref, st_ref):
    @pl.when(pl.program_id(1) == 0)
    def _():
        st_ref[...] = jnp.zeros_like(st_ref)

    def body(c, carry):
        r0 = pl.multiple_of(c * CHUNK, CHUNK)
        rows = pl.ds(r0, CHUNK)
        for hd in range(HEADS):
            v_ref, g_ref = (v01_ref, g01_ref) if hd < 2 else (v23_ref, g23_ref)
            vcols = slice((hd % 2) * DV, (hd % 2 + 1) * DV)
            qc = q_ref[0, rows, hd * DK:(hd + 1) * DK]
            kc = k_ref[0, rows, hd * DK:(hd + 1) * DK]
            vc = v_ref[0, rows, vcols]
            scores = lax.dot_general(qc, kc, NT_DIMS, preferred_element_type=F32) * dm_ref[hd]
            inner = jnp.dot(scores.astype(BF16), vc, preferred_element_type=F32)
            st = st_ref[hd]
            qx = (qc.astype(F32) * xi_ref[hd]).astype(BF16)
            cross = jnp.dot(qx, st.astype(BF16), preferred_element_type=F32)
            kzt = (kc.astype(F32) * zt_ref[hd]).T.astype(BF16)
            st_ref[hd] = st * gc_ref[hd] + jnp.dot(kzt, vc, preferred_element_type=F32)
            o = inner + cross
            mu = jnp.mean(o, axis=-1, keepdims=True)
            cen = o - mu
            var = jnp.mean(cen * cen, axis=-1, keepdims=True)
            r = cen * lax.rsqrt(var + EPS)
            o_ref[0, rows, hd * DV:(hd + 1) * DV] = (r * g_ref[0, rows, vcols].astype(F32)).astype(BF16)
        return carry

    lax.fori_loop(0, RET_TS // CHUNK, body, 0)


def _retention(p3, tables):
    dmask, xi, zeta, gch = tables
    tile = lambda col: pl.BlockSpec((1, RET_TS, D), lambda b, i: (b, i, col))
    full = lambda a: pl.BlockSpec(a.shape, lambda b, i: (0, 0, 0))
    return pl.pallas_call(
        _ret_kernel,
        out_shape=jax.ShapeDtypeStruct((BATCH, SEQ, HEADS * DV), BF16),
        grid=(BATCH, SEQ // RET_TS),
        in_specs=[tile(PCOL_Q), tile(PCOL_K), tile(PCOL_V), tile(PCOL_V + 1), tile(PCOL_G), tile(PCOL_G + 1),
                  full(dmask), full(xi), full(zeta), full(gch)],
        out_specs=pl.BlockSpec((1, RET_TS, HEADS * DV), lambda b, i: (b, i, 0)),
        scratch_shapes=[pltpu.VMEM((HEADS, DK, DV), F32)],
        compiler_params=_cparams("parallel", "arbitrary"),
        name="retention",
    )(p3, p3, p3, p3, p3, p3, dmask, xi, zeta, gch)


MIX_TM = 512


def _route_rows(s, sb):
    row = lambda a, e: a[e:e + 1, :]
    best = None
    gidx = None
    for g in range(N_GROUPS):
        v = [row(sb, GROUP_SIZE * g + i) for i in range(GROUP_SIZE)]
        pair_sums = [v[a] + v[b] for a in range(GROUP_SIZE) for b in range(a + 1, GROUP_SIZE)]
        gs = functools.reduce(jnp.maximum, pair_sums)
        if g == 0:
            best, gidx = gs, jnp.zeros(gs.shape, jnp.int32)
        else:
            upd = gs > best
            gidx = jnp.where(upd, g, gidx)
            best = jnp.where(upd, gs, best)

    def pick(a, i):
        out = row(a, i)
        for g in range(1, N_GROUPS):
            out = jnp.where(gidx == g, row(a, GROUP_SIZE * g + i), out)
        return out

    vb = [pick(sb, i) for i in range(GROUP_SIZE)]
    vs = [pick(s, i) for i in range(GROUP_SIZE)]
    m1, i1, s1 = vb[0], jnp.zeros(gidx.shape, jnp.int32), vs[0]
    for i in range(1, GROUP_SIZE):
        upd = vb[i] > m1
        m1 = jnp.where(upd, vb[i], m1)
        i1 = jnp.where(upd, i, i1)
        s1 = jnp.where(upd, vs[i], s1)
    m2 = i2 = s2 = None
    for i in range(GROUP_SIZE):
        cand = jnp.where(i1 == i, -jnp.inf, vb[i])
        if m2 is None:
            m2, i2, s2 = cand, jnp.zeros(gidx.shape, jnp.int32), vs[0]
        else:
            upd = cand > m2
            m2 = jnp.where(upd, cand, m2)
            i2 = jnp.where(upd, i, i2)
            s2 = jnp.where(upd, vs[i], s2)
    den = s1 + s2
    w1 = s1 / den
    w2 = s2 / den
    first_low = i1 < i2
    lo = jnp.minimum(i1, i2)
    hi = jnp.maximum(i1, i2)
    pair = jnp.where(lo == 0, hi - 1, jnp.where(lo == 1, hi + 1, 5))
    bucket = gidx * PAIRS + pair
    return bucket, jnp.where(first_low, w1, w2), jnp.where(first_low, w2, w1)


def _mix_kernel(rg_ref, ya_ref, sgb_ref, x_ref, gt_ref, sc_ref, sh_ref, wr_ref, wo_ref, n2_ref,
                rhi_ref, rlo_ref, rb_ref, tri_ref,
                x1_ref, hx_ref, ri_ref, cnt_ref, carry_ref, wrb_ref, wob_ref):
    m = pl.program_id(0)

    @pl.when(m == 0)
    def _():
        carry_ref[...] = jnp.zeros_like(carry_ref)
        wrb_ref[...] = wr_ref[0].astype(BF16)
        wob_ref[...] = wo_ref[0].astype(BF16)

    yb = jnp.dot(rg_ref[...], wrb_ref[...], preferred_element_type=F32)
    y = ya_ref[...].astype(F32) + sgb_ref[...].astype(F32) * yb
    o = jnp.dot(y.astype(BF16), wob_ref[...], preferred_element_type=F32)
    x1 = x_ref[...] + gt_ref[0] * o
    x1_ref[...] = x1
    h2 = _rms(x1) * n2_ref[...] * (1.0 + sc_ref[0]) + sh_ref[0]
    hx_ref[:, 0:D] = h2

    hi = h2.astype(BF16)
    lo = (h2 - hi.astype(F32)).astype(BF16)
    rhi = rhi_ref[...]
    logits = (lax.dot_general(rhi, hi, NT_DIMS, preferred_element_type=F32)
              + lax.dot_general(rhi, lo, NT_DIMS, preferred_element_type=F32)
              + lax.dot_general(rlo_ref[...], hi, NT_DIMS, preferred_element_type=F32))
    s = _sigmoid(logits)
    bucket, w_lo, w_hi = _route_rows(s, s + rb_ref[...])

    onehot = (lax.broadcasted_iota(jnp.int32, (BUCKET_ROWS, MIX_TM), 0) == bucket).astype(F32)
    prefix = jnp.dot(onehot.astype(BF16), tri_ref[...], preferred_element_type=F32)
    carry = carry_ref[:, 0:1]
    rank = jnp.sum(onehot * (prefix + carry), axis=0, keepdims=True)
    new_carry = carry + jnp.sum(onehot, axis=1, keepdims=True)
    carry_ref[...] = jnp.broadcast_to(new_carry, carry_ref.shape)
    cnt_ref[...] = jnp.broadcast_to(new_carry, cnt_ref.shape)

    rid = lax.broadcasted_iota(jnp.int32, (8, MIX_TM), 0)
    ri_ref[...] = jnp.where(rid == 0, bucket, jnp.where(rid == 1, rank.astype(jnp.int32), 0))
    wid = lax.broadcasted_iota(jnp.int32, (TAIL, MIX_TM), 0)
    wrows = jnp.where(wid == 0, w_lo, jnp.where(wid == 1, w_hi, 0.0))
    hx_ref[:, D:XROW] = wrows.T


def _mix(rg, ya, p, x, gt1, sc2, sh2, w_ret_out, w_out, l, norm2_w, rhi, rlo, rb, tri):
    tpb = SEQ // MIX_TM
    bidx = lambda m: (m // tpb, 0, 0)
    return pl.pallas_call(
        _mix_kernel,
        out_shape=(jax.ShapeDtypeStruct((NTOK, D), F32),
                   jax.ShapeDtypeStruct((NTOK, XROW), F32),
                   jax.ShapeDtypeStruct((8, NTOK), jnp.int32),
                   jax.ShapeDtypeStruct((BUCKET_ROWS, 128), F32)),
        grid=(NTOK // MIX_TM,),
        in_specs=[pl.BlockSpec((MIX_TM, HEADS * DV), lambda m: (m, 0)),
                  pl.BlockSpec((MIX_TM, D), lambda m: (m, 0)),
                  pl.BlockSpec((MIX_TM, D), lambda m: (m, PCOL_GB)),
                  pl.BlockSpec((MIX_TM, D), lambda m: (m, 0)),
                  pl.BlockSpec((1, 1, D), bidx),
                  pl.BlockSpec((1, 1, D), bidx),
                  pl.BlockSpec((1, 1, D), bidx),
                  pl.BlockSpec((1, HEADS * DV, D), lambda m: (l, 0, 0), pipeline_mode=pl.Buffered(1)),
                  pl.BlockSpec((1, D, D), lambda m: (l, 0, 0), pipeline_mode=pl.Buffered(1)),
                  pl.BlockSpec((1, D), lambda m: (0, 0)),
                  pl.BlockSpec((N_EXPERTS, D), lambda m: (0, 0)),
                  pl.BlockSpec((N_EXPERTS, D), lambda m: (0, 0)),
                  pl.BlockSpec((N_EXPERTS, MIX_TM), lambda m: (0, 0)),
                  pl.BlockSpec((MIX_TM, MIX_TM), lambda m: (0, 0))],
        out_specs=(pl.BlockSpec((MIX_TM, D), lambda m: (m, 0)),
                   pl.BlockSpec((MIX_TM, XROW), lambda m: (m, 0)),
                   pl.BlockSpec((8, MIX_TM), lambda m: (0, m)),
                   pl.BlockSpec((BUCKET_ROWS, 128), lambda m: (0, 0))),
        scratch_shapes=[pltpu.VMEM((BUCKET_ROWS, 128), F32),
                        pltpu.VMEM((HEADS * DV, D), BF16), pltpu.VMEM((D, D), BF16)],
        compiler_params=_cparams("arbitrary"),
        name="mix_route",
    )(rg, ya, p, x, gt1, sc2, sh2, w_ret_out, w_out, norm2_w.reshape(1, D), rhi, rlo, rb, tri)


DISP_TG = 512


def _dispatch_kernel(dest_ref, nval_ref, src_ref, xs_ref, zero_ref, sem, zsem):
    step = pl.program_id(0)
    base = step * DISP_TG

    @pl.when(step == 0)
    def _():
        zero_ref[...] = jnp.zeros_like(zero_ref)

        def zero_copy(b):
            r0 = pl.multiple_of(b * MOE_BLK, MOE_BLK)
            return pltpu.make_async_copy(zero_ref, xs_ref.at[pl.ds(r0, MOE_BLK), :], zsem)

        def zissue(b, carry):
            @pl.when(nval_ref[b] < MOE_BLK)
            def _():
                zero_copy(b).start()
            return carry

        lax.fori_loop(0, N_BLOCKS, zissue, 0)

        def zdrain(b, carry):
            @pl.when(nval_ref[b] < MOE_BLK)
            def _():
                zero_copy(b).wait()
            return carry

        lax.fori_loop(0, N_BLOCKS, zdrain, 0)

    def row_copy(t, d):
        return pltpu.make_async_copy(src_ref.at[pl.ds(t, 1), :], xs_ref.at[pl.ds(d, 1), :], sem)

    def issue(t, carry):
        row_copy(t, dest_ref[base + t]).start()
        return carry

    lax.fori_loop(0, DISP_TG, issue, 0, unroll=ROW_DMA_UNROLL)
    pltpu.make_async_copy(src_ref, xs_ref.at[pl.ds(0, DISP_TG), :], sem).wait()


def _dispatch(dest, nval, hx):
    return pl.pallas_call(
        _dispatch_kernel,
        out_shape=jax.ShapeDtypeStruct((SLOT_ROWS, XROW), F32),
        grid_spec=pltpu.PrefetchScalarGridSpec(
            num_scalar_prefetch=2,
            grid=(NTOK // DISP_TG,),
            in_specs=[pl.BlockSpec((DISP_TG, XROW), lambda i, d, nv: (i, 0))],
            out_specs=pl.BlockSpec(memory_space=pl.ANY),
            scratch_shapes=[pltpu.VMEM((MOE_BLK, XROW), F32),
                            pltpu.SemaphoreType.DMA(()), pltpu.SemaphoreType.DMA(())]),
        compiler_params=_cparams("arbitrary", row_dma=True),
        name="dispatch",
    )(dest, nval, hx)


def _moe_kernel(elo_ref, ehi_ref, nval_ref, xs_ref, wg1, wu1, wd1, wg2, wu2, wd2, o_ref):
    i = pl.program_id(0)
    nval = nval_ref[i]

    @pl.when(nval > 0)
    def _():
        x = xs_ref[:, 0:D].astype(BF16)
        w_lo = xs_ref[:, D:D + 1]
        w_hi = xs_ref[:, D + 1:D + 2]

        def expert(wg, wu, wd):
            g = jnp.dot(x, wg[0], preferred_element_type=F32)
            u = jnp.dot(x, wu[0], preferred_element_type=F32)
            a = (g * _sigmoid(g) * u).astype(BF16)
            return jnp.dot(a, wd[0], preferred_element_type=F32)

        o_ref[...] = w_lo * expert(wg1, wu1, wd1) + w_hi * expert(wg2, wu2, wd2)

    @pl.when(nval == 0)
    def _():
        o_ref[...] = jnp.zeros_like(o_ref)


def _moe(elo, ehi, nval, xs, wg, wu, wd):
    wspec = lambda tab: pl.BlockSpec((1, D, D), lambda i, elo, ehi, nv: ((elo, ehi)[tab][i], 0, 0))
    return pl.pallas_call(
        _moe_kernel,
        out_shape=jax.ShapeDtypeStruct((SLOT_ROWS, D), F32),
        grid_spec=pltpu.PrefetchScalarGridSpec(
            num_scalar_prefetch=3,
            grid=(N_BLOCKS,),
            in_specs=[pl.BlockSpec((MOE_BLK, XROW), lambda i, elo, ehi, nv: (i, 0)),
                      wspec(0), wspec(0), wspec(0), wspec(1), wspec(1), wspec(1)],
            out_specs=pl.BlockSpec((MOE_BLK, D), lambda i, elo, ehi, nv: (i, 0))),
        compiler_params=_cparams("arbitrary"),
        name="experts",
    )(elo, ehi, nval, xs, wg, wu, wd, wg, wu, wd)


COMB_TG = 512


def _combine_kernel(dest_ref, ys_ref, x1_ref, gt_ref, nw_ref, sc_ref, sh_ref, *rest, last):
    if last:
        hn_ref, ybuf, sem = rest
    else:
        x2_ref, hn_ref, ybuf, sem = rest
    step = pl.program_id(0)
    slot = step % 2

    def gather_tile(tile, into):
        base = tile * COMB_TG

        def issue(t, carry):
            src = ys_ref.at[pl.ds(dest_ref[base + t], 1), :]
            pltpu.make_async_copy(src, ybuf.at[into, pl.ds(t, 1), :], sem.at[into]).start()
            return carry

        lax.fori_loop(0, COMB_TG, issue, 0, unroll=ROW_DMA_UNROLL)

    @pl.when(step == 0)
    def _():
        gather_tile(0, 0)

    @pl.when(step + 1 < pl.num_programs(0))
    def _():
        gather_tile(step + 1, 1 - slot)

    pltpu.make_async_copy(ys_ref.at[pl.ds(0, COMB_TG), :], ybuf.at[slot], sem.at[slot]).wait()

    x2 = x1_ref[...] + gt_ref[0] * ybuf[slot]
    y = _rms(x2) * nw_ref[...]
    if last:
        hn_ref[...] = y
    else:
        x2_ref[...] = x2
        hn_ref[...] = (y * (1.0 + sc_ref[0]) + sh_ref[0]).astype(hn_ref.dtype)


def _combine(dest, ys, x1, gt2, nw, sc, sh, last):
    tpb = SEQ // COMB_TG
    bidx = lambda i, d: (i // tpb, 0, 0)
    tile = pl.BlockSpec((COMB_TG, D), lambda i, d: (i, 0))
    if last:
        out_shape = jax.ShapeDtypeStruct((NTOK, D), F32)
        out_specs = tile
    else:
        out_shape = (jax.ShapeDtypeStruct((NTOK, D), F32), jax.ShapeDtypeStruct((NTOK, D), BF16))
        out_specs = (tile, tile)
    return pl.pallas_call(
        functools.partial(_combine_kernel, last=last),
        out_shape=out_shape,
        grid_spec=pltpu.PrefetchScalarGridSpec(
            num_scalar_prefetch=1,
            grid=(NTOK // COMB_TG,),
            in_specs=[pl.BlockSpec(memory_space=pl.ANY),
                      tile,
                      pl.BlockSpec((1, 1, D), bidx),
                      pl.BlockSpec((1, D), lambda i, d: (0, 0)),
                      pl.BlockSpec((1, 1, D), bidx),
                      pl.BlockSpec((1, 1, D), bidx)],
            out_specs=out_specs,
            scratch_shapes=[pltpu.VMEM((2, COMB_TG, D), F32), pltpu.SemaphoreType.DMA((2,))]),
        compiler_params=_cparams("arbitrary", row_dma=True),
        name="combine",
    )(dest, ys, x1, gt2, nw.reshape(1, D), sc, sh)


def _pair_tables():
    lo, hi = [], []
    for g in range(N_GROUPS):
        for a in range(GROUP_SIZE):
            for b in range(a + 1, GROUP_SIZE):
                lo.append(GROUP_SIZE * g + a)
                hi.append(GROUP_SIZE * g + b)
    return np.asarray(lo, np.int32), np.asarray(hi, np.int32)


def _slot_plan(ri, cnt):
    counts = cnt[:N_BUCKETS, 0].astype(jnp.int32)
    padded = (counts + MOE_BLK - 1) // MOE_BLK * MOE_BLK
    pad_end = jnp.cumsum(padded)
    pad_start = pad_end - padded
    dest = pad_start[ri[0]] + ri[1]
    blk_start = jnp.arange(N_BLOCKS, dtype=jnp.int32) * MOE_BLK
    bb = jnp.minimum(jnp.searchsorted(pad_end, blk_start, side='right'), N_BUCKETS - 1).astype(jnp.int32)
    nval = jnp.clip(pad_start[bb] + counts[bb] - blk_start, 0, MOE_BLK).astype(jnp.int32)
    lo_tab, hi_tab = _pair_tables()
    return dest.astype(jnp.int32), jnp.asarray(lo_tab)[bb], jnp.asarray(hi_tab)[bb], nval


def kernel(x, c, positions, w_ada, b_ada, norm1_w, w_in, conv_w, conv_b, conv_ln_w, conv_ln_b, w_conv_out,
           w_ret_out, w_out, norm2_w, w_router, router_bias, w_exp_gate, w_exp_up, w_exp_down, final_norm_w):
    mod = _ada(c, w_ada, b_ada).reshape(DEPTH, BATCH, N_MOD, 1, D)
    sh1, sc1, gt1, sh2, sc2, gt2 = (mod[:, :, i] for i in range(N_MOD))
    cos, sin = _rope_tables(positions)
    ret_tables = _ret_tables()


    wr_t = w_router.T
    rhi = wr_t.astype(BF16)
    rlo = (wr_t - rhi.astype(F32)).astype(BF16)
    rb = jnp.broadcast_to(router_bias.astype(F32)[:, None], (N_EXPERTS, MIX_TM))
    tri = (jnp.arange(MIX_TM)[:, None] < jnp.arange(MIX_TM)[None, :]).astype(BF16)

    xf = x.reshape(NTOK, D)
    h = _modnorm(x, norm1_w[0], sc1[0], sh1[0]).reshape(NTOK, D)
    out = None
    for l in range(DEPTH):
        p, wg_bf, wu_bf, wd_bf = _inproj(h, w_in, l, cos, sin, (w_exp_gate, w_exp_up, w_exp_down))
        p3 = p.reshape(BATCH, SEQ, P_COLS)
        ya = _conv_branch(p3, l, conv_w[l], conv_b[l], conv_ln_w[l], conv_ln_b[l], w_conv_out)
        rg = _retention(p3, ret_tables)
        x1, hx, ri, cnt = _mix(rg.reshape(NTOK, HEADS * DV), ya.reshape(NTOK, D), p, xf, gt1[l], sc2[l], sh2[l],
                               w_ret_out, w_out, l, norm2_w[l], rhi, rlo, rb, tri)
        dest, elo, ehi, nval = _slot_plan(ri, cnt)
        xs = _dispatch(dest, nval, hx)
        ys = _moe(elo, ehi, nval, xs, wg_bf, wu_bf, wd_bf)
        if l + 1 < DEPTH:
            xf, h = _combine(dest, ys, x1, gt2[l], norm1_w[l + 1], sc1[l + 1], sh1[l + 1], last=False)
        else:
            out = _combine(dest, ys, x1, gt2[l], final_norm_w, sc1[l], sh1[l], last=True)
    return out.reshape(BATCH, SEQ, D)
```

```python
import functools

import numpy as np
import jax
import jax.numpy as jnp
from jax import lax
from jax.experimental import pallas as pl
from jax.experimental.pallas import tpu as pltpu

F32 = jnp.float32
BF16 = jnp.bfloat16

D = 1024
BATCH = 8
SEQ = 2048
DEPTH = 4
NTOK = BATCH * SEQ
N_MOD = 6
EPS = 1e-6

CONV_K = 31
HEADS = 4
DK = 256
DV = 512
CHUNK = 128
ROPE_BASE = 10000.0
HALF = DK // 2

N_EXPERTS = 16
N_GROUPS = 4
GROUP_SIZE = 4
PAIRS = 6
N_BUCKETS = N_GROUPS * PAIRS
BUCKET_ROWS = 32

P_COLS = 9 * D
PCOL_U, PCOL_Q, PCOL_K, PCOL_V, PCOL_G, PCOL_GA, PCOL_GB = 0, 1, 2, 3, 5, 7, 8

MOE_BLK = 256
N_BLOCKS = NTOK // MOE_BLK + N_BUCKETS
SLOT_ROWS = N_BLOCKS * MOE_BLK
TAIL = 128
XROW = D + TAIL

VMEM_LIMIT = 56 * 1024 * 1024

NT_DIMS = (((1,), (1,)), ((), ()))


ROW_DMA_UNROLL = 8


def _cparams(*sem, row_dma=False):
    return pltpu.CompilerParams(dimension_semantics=sem, vmem_limit_bytes=VMEM_LIMIT,
                                disable_bounds_checks=row_dma)


def _sigmoid(x):
    return jax.nn.sigmoid(x)


def _rms(x):
    return x * lax.rsqrt(jnp.mean(x * x, axis=-1, keepdims=True) + EPS)


ADA_TN = 1536


def _ada_kernel(c_ref, w_ref, b_ref, o_ref):
    c = c_ref[...]
    sc = c * _sigmoid(c)
    o_ref[0] = jnp.dot(sc, w_ref[0], precision=lax.Precision.HIGHEST,
                       preferred_element_type=F32) + b_ref[0]


def _ada(c, w_ada, b_ada):
    nj = N_MOD * D // ADA_TN
    return pl.pallas_call(
        _ada_kernel,
        out_shape=jax.ShapeDtypeStruct((DEPTH, BATCH, N_MOD * D), F32),
        grid=(DEPTH, nj),
        in_specs=[pl.BlockSpec((BATCH, D), lambda l, j: (0, 0)),
                  pl.BlockSpec((1, D, ADA_TN), lambda l, j: (l, 0, j)),
                  pl.BlockSpec((1, 1, ADA_TN), lambda l, j: (l, 0, j))],
        out_specs=pl.BlockSpec((1, BATCH, ADA_TN), lambda l, j: (l, 0, j)),
        compiler_params=_cparams("parallel", "parallel"),
        name="ada",
    )(c, w_ada, b_ada.reshape(DEPTH, 1, N_MOD * D))


ROPE_TN = 2048


def _rope_kernel(pos_ref, inv_ref, cos_ref, sin_ref):
    ang = pos_ref[...] * inv_ref[...]
    cos_ref[...] = jnp.cos(ang)
    sin_ref[...] = jnp.sin(ang)


def _rope_tables(positions):
    inv = ROPE_BASE ** (-jnp.arange(HALF, dtype=F32) / HALF)
    pos = positions.astype(F32).reshape(NTOK, 1)
    return pl.pallas_call(
        _rope_kernel,
        out_shape=(jax.ShapeDtypeStruct((NTOK, HALF), F32),) * 2,
        grid=(NTOK // ROPE_TN,),
        in_specs=[pl.BlockSpec((ROPE_TN, 1), lambda i: (i, 0)),
                  pl.BlockSpec((1, HALF), lambda i: (0, 0))],
        out_specs=(pl.BlockSpec((ROPE_TN, HALF), lambda i: (i, 0)),) * 2,
        compiler_params=_cparams("parallel"),
        name="rope_tables",
    )(pos, inv.reshape(1, HALF))


NORM_TS = 512


def _modnorm_kernel(x_ref, w_ref, sc_ref, sh_ref, o_ref):
    y = _rms(x_ref[0]) * w_ref[...]
    o_ref[0] = (y * (1.0 + sc_ref[0]) + sh_ref[0]).astype(o_ref.dtype)


def _modnorm(x, w, sc, sh):
    return pl.pallas_call(
        _modnorm_kernel,
        out_shape=jax.ShapeDtypeStruct((BATCH, SEQ, D), BF16),
        grid=(BATCH, SEQ // NORM_TS),
        in_specs=[pl.BlockSpec((1, NORM_TS, D), lambda b, i: (b, i, 0)),
                  pl.BlockSpec((1, D), lambda b, i: (0, 0)),
                  pl.BlockSpec((1, 1, D), lambda b, i: (b, 0, 0)),
                  pl.BlockSpec((1, 1, D), lambda b, i: (b, 0, 0))],
        out_specs=pl.BlockSpec((1, NORM_TS, D), lambda b, i: (b, i, 0)),
        compiler_params=_cparams("parallel", "parallel"),
        name="modnorm",
    )(x, w.reshape(1, D), sc, sh)


INP_TM = 1024
INP_GROUPS = 9
EXP_SLAB_ROWS = 512
EXP_SLABS_PER_MAT = D // EXP_SLAB_ROWS
EXP_SLABS = N_EXPERTS * EXP_SLABS_PER_MAT


def _inproj_kernel(h_ref, w1_ref, w2_ref, cos_ref, sin_ref, eg_ref, eu_ref, ed_ref,
                   o_ref, og_ref, ou_ref, od_ref, wb1_ref, wb2_ref):
    j = pl.program_id(0)
    m = pl.program_id(1)
    h = h_ref[...]

    step = j * pl.num_programs(1) + m
    for t, (src, dst) in enumerate(((eg_ref, og_ref), (eu_ref, ou_ref), (ed_ref, od_ref))):
        @pl.when((step >= EXP_SLABS * t) & (step < EXP_SLABS * (t + 1)))
        def _():
            dst[0] = src[0, 0].astype(BF16)

    @pl.when(m == 0)
    def _():
        wb1_ref[...] = w1_ref[0].astype(BF16)

    @pl.when((m == 0) & (j == 0))
    def _():
        wb2_ref[...] = w2_ref[0].astype(BF16)

    def proj():
        return jnp.dot(h, wb1_ref[...], preferred_element_type=F32)

    @pl.when(j == 0)
    def _():
        a = proj()
        b = jnp.dot(h, wb2_ref[...], preferred_element_type=F32)
        o_ref[...] = (a * _sigmoid(b)).astype(BF16)

    @pl.when((j == 1) | (j == 2))
    def _():
        t = proj()
        scale = jnp.where(j == 2, DK ** -0.5, 1.0).astype(F32)
        cos = cos_ref[...] * scale
        sin = sin_ref[...] * scale
        for hd in range(HEADS):
            c0 = hd * DK
            t1 = t[:, c0:c0 + HALF]
            t2 = t[:, c0 + HALF:c0 + DK]
            o_ref[:, c0:c0 + HALF] = (t1 * cos - t2 * sin).astype(BF16)
            o_ref[:, c0 + HALF:c0 + DK] = (t1 * sin + t2 * cos).astype(BF16)

    @pl.when((j == 3) | (j == 4))
    def _():
        o_ref[...] = proj().astype(BF16)

    @pl.when((j == 5) | (j == 6))
    def _():
        g = proj()
        o_ref[...] = (g * _sigmoid(g)).astype(BF16)

    @pl.when(j >= 7)
    def _():
        o_ref[...] = _sigmoid(proj()).astype(BF16)


def _inproj(h, w_in, l, cos, sin, w_exp):
    n_m = NTOK // INP_TM
    assert 3 * EXP_SLABS <= INP_GROUPS * n_m

    def rope_idx(j, m):
        return (jnp.where((j == 1) | (j == 2), m, 0), 0)

    def slab(t, j, m):
        s = jnp.clip(j * n_m + m - EXP_SLABS * t, 0, EXP_SLABS - 1)
        return s // EXP_SLABS_PER_MAT, s % EXP_SLABS_PER_MAT

    def exp_in(t):
        return pl.BlockSpec((1, 1, EXP_SLAB_ROWS, D), lambda j, m: (l, *slab(t, j, m), 0))

    def exp_out(t):
        return pl.BlockSpec((1, EXP_SLAB_ROWS, D), lambda j, m: (*slab(t, j, m), 0))

    exp_shape = jax.ShapeDtypeStruct((N_EXPERTS, D, D), BF16)
    return pl.pallas_call(
        _inproj_kernel,
        out_shape=(jax.ShapeDtypeStruct((NTOK, P_COLS), BF16), exp_shape, exp_shape, exp_shape),
        grid=(INP_GROUPS, n_m),
        in_specs=[pl.BlockSpec((INP_TM, D), lambda j, m: (m, 0)),
                  pl.BlockSpec((1, D, D), lambda j, m: (l, 0, jnp.where(j == 0, 0, j + 1))),
                  pl.BlockSpec((1, D, D), lambda j, m: (l, 0, 1), pipeline_mode=pl.Buffered(1)),
                  pl.BlockSpec((INP_TM, HALF), rope_idx),
                  pl.BlockSpec((INP_TM, HALF), rope_idx),
                  exp_in(0), exp_in(1), exp_in(2)],
        out_specs=(pl.BlockSpec((INP_TM, D), lambda j, m: (m, j)), exp_out(0), exp_out(1), exp_out(2)),
        scratch_shapes=[pltpu.VMEM((D, D), BF16), pltpu.VMEM((D, D), BF16)],
        compiler_params=_cparams("arbitrary", "arbitrary"),
        name="inproj",
    )(h, w_in, w_in, cos, sin, *w_exp)


CONV_TS = 256
CONV_HALO = 32
CONV_CW = 128
CONV_GROUP = 8
SUBLANES = 8
CONV_TQ = CONV_TS // SUBLANES
assert CONV_K - 1 <= CONV_HALO <= CONV_TQ and CONV_TQ % CONV_GROUP == 0


def _conv_tile(u_ref, halo_ref, first_tile, cw_ref, cb_ref, lnw_ref, lnb_ref, nat_ref, ext_ref, acc_ref):
    def block(a):
        return pl.ds(a * SUBLANES, SUBLANES)

    n_chunks = D // CONV_CW
    chunk = lambda cc: slice(cc * CONV_CW, (cc + 1) * CONV_CW)
    for cc in range(n_chunks):
        nat_ref[cc] = u_ref[:, chunk(cc)].astype(F32)
    for a in range(CONV_TQ):
        for cc in range(n_chunks):
            ext_ref[block(CONV_HALO + a), chunk(cc)] = nat_ref[cc, pl.ds(a, SUBLANES, stride=CONV_TQ), :]
    halo = jnp.where(first_tile, 0.0, halo_ref[...].astype(F32))
    top = lax.broadcasted_iota(jnp.int32, (SUBLANES, D), 0) == 0
    for a in range(CONV_HALO):
        prev_run = pltpu.roll(ext_ref[block(CONV_TQ + a), :], 1, axis=0)
        ext_ref[block(a), :] = jnp.where(top, halo[a:a + 1, :], prev_run)

    for a0 in range(0, CONV_TQ, CONV_GROUP):
        for cc in range(n_chunks):
            cols = chunk(cc)
            accs = [cb_ref[:, cols]] * CONV_GROUP
            for j in range(CONV_K):
                w8 = cw_ref[j, :, cols]
                for g in range(CONV_GROUP):
                    accs[g] = accs[g] + ext_ref[block(a0 + g + CONV_HALO - (CONV_K - 1) + j), cols] * w8
            for g in range(CONV_GROUP):
                acc_ref[cc, pl.ds(a0 + g, SUBLANES, stride=CONV_TQ), :] = accs[g]

    c = jnp.concatenate([acc_ref[cc] for cc in range(n_chunks)], axis=-1)
    mu = jnp.mean(c, axis=-1, keepdims=True)
    cen = c - mu
    var = jnp.mean(cen * cen, axis=-1, keepdims=True)
    y = cen * lax.rsqrt(var + EPS) * lnw_ref[...] + lnb_ref[...]
    return y * _sigmoid(y)


def _ret_tables():
    hh = jnp.arange(HEADS, dtype=F32)
    log_g = jnp.log1p(-jnp.exp2(-5.0 - hh))
    idx = jnp.arange(CHUNK, dtype=F32)
    rel = idx[:, None] - idx[None, :]
    dmask = jnp.where(rel[None] >= 0, jnp.exp(jnp.maximum(rel, 0.0)[None] * log_g[:, None, None]), 0.0)
    xi = jnp.exp((idx + 1.0)[None, :] * log_g[:, None])[..., None]
    zeta = jnp.exp((CHUNK - 1.0 - idx)[None, :] * log_g[:, None])[..., None]
    g_chunk = jnp.exp(CHUNK * log_g)[:, None, None]
    return (dmask,
            jnp.broadcast_to(xi, (HEADS, CHUNK, DK)),
            jnp.broadcast_to(zeta, (HEADS, CHUNK, DK)),
            jnp.broadcast_to(g_chunk, (HEADS, 1, DV)))


RET_TS = 512


def _ret_kernel(q_ref, k_ref, v01_ref, v23_ref, g01_ref, g23_ref, dm_ref, xi_ref, zt_ref, gc_ref, o_ref, st_ref):
    @pl.when(pl.program_id(1) == 0)
    def _():
        st_ref[...] = jnp.zeros_like(st_ref)

    def body(c, carry):
        r0 = pl.multiple_of(c * CHUNK, CHUNK)
        rows = pl.ds(r0, CHUNK)
        for hd in range(HEADS):
            v_ref, g_ref = (v01_ref, g01_ref) if hd < 2 else (v23_ref, g23_ref)
            vcols = slice((hd % 2) * DV, (hd % 2 + 1) * DV)
            qc = q_ref[0, rows, hd * DK:(hd + 1) * DK]
            kc = k_ref[0, rows, hd * DK:(hd + 1) * DK]
            vc = v_ref[0, rows, vcols]
            scores = lax.dot_general(qc, kc, NT_DIMS, preferred_element_type=F32) * dm_ref[hd]
            inner = jnp.dot(scores.astype(BF16), vc, preferred_element_type=F32)
            st = st_ref[hd]
            qx = (qc.astype(F32) * xi_ref[hd]).astype(BF16)
            cross = jnp.dot(qx, st.astype(BF16), preferred_element_type=F32)
            kzt = (kc.astype(F32) * zt_ref[hd]).T.astype(BF16)
            st_ref[hd] = st * gc_ref[hd] + jnp.dot(kzt, vc, preferred_element_type=F32)
            o = inner + cross
            mu = jnp.mean(o, axis=-1, keepdims=True)
            cen = o - mu
            var = jnp.mean(cen * cen, axis=-1, keepdims=True)
            r = cen * lax.rsqrt(var + EPS)
            o_ref[0, rows, hd * DV:(hd + 1) * DV] = (r * g_ref[0, rows, vcols].astype(F32)).astype(BF16)
        return carry

    lax.fori_loop(0, RET_TS // CHUNK, body, 0)


def _retention(p3, tables):
    dmask, xi, zeta, gch = tables
    tile = lambda col: pl.BlockSpec((1, RET_TS, D), lambda b, i: (b, i, col))
    full = lambda a: pl.BlockSpec(a.shape, lambda b, i: (0, 0, 0))
    return pl.pallas_call(
        _ret_kernel,
        out_shape=jax.ShapeDtypeStruct((BATCH, SEQ, HEADS * DV), BF16),
        grid=(BATCH, SEQ // RET_TS),
        in_specs=[tile(PCOL_Q), tile(PCOL_K), tile(PCOL_V), tile(PCOL_V + 1), tile(PCOL_G), tile(PCOL_G + 1),
                  full(dmask), full(xi), full(zeta), full(gch)],
        out_specs=pl.BlockSpec((1, RET_TS, HEADS * DV), lambda b, i: (b, i, 0)),
        scratch_shapes=[pltpu.VMEM((HEADS, DK, DV), F32)],
        compiler_params=_cparams("parallel", "arbitrary"),
        name="retention",
    )(p3, p3, p3, p3, p3, p3, dmask, xi, zeta, gch)


MIX_TM = CONV_TS


def _route_rows(s, sb):
    row = lambda a, e: a[e:e + 1, :]
    best = None
    gidx = None
    for g in range(N_GROUPS):
        v = [row(sb, GROUP_SIZE * g + i) for i in range(GROUP_SIZE)]
        pair_sums = [v[a] + v[b] for a in range(GROUP_SIZE) for b in range(a + 1, GROUP_SIZE)]
        gs = functools.reduce(jnp.maximum, pair_sums)
        if g == 0:
            best, gidx = gs, jnp.zeros(gs.shape, jnp.int32)
        else:
            upd = gs > best
            gidx = jnp.where(upd, g, gidx)
            best = jnp.where(upd, gs, best)

    def pick(a, i):
        out = row(a, i)
        for g in range(1, N_GROUPS):
            out = jnp.where(gidx == g, row(a, GROUP_SIZE * g + i), out)
        return out

    vb = [pick(sb, i) for i in range(GROUP_SIZE)]
    vs = [pick(s, i) for i in range(GROUP_SIZE)]
    m1, i1, s1 = vb[0], jnp.zeros(gidx.shape, jnp.int32), vs[0]
    for i in range(1, GROUP_SIZE):
        upd = vb[i] > m1
        m1 = jnp.where(upd, vb[i], m1)
        i1 = jnp.where(upd, i, i1)
        s1 = jnp.where(upd, vs[i], s1)
    m2 = i2 = s2 = None
    for i in range(GROUP_SIZE):
        cand = jnp.where(i1 == i, -jnp.inf, vb[i])
        if m2 is None:
            m2, i2, s2 = cand, jnp.zeros(gidx.shape, jnp.int32), vs[0]
        else:
            upd = cand > m2
            m2 = jnp.where(upd, cand, m2)
            i2 = jnp.where(upd, i, i2)
            s2 = jnp.where(upd, vs[i], s2)
    den = s1 + s2
    w1 = s1 / den
    w2 = s2 / den
    first_low = i1 < i2
    lo = jnp.minimum(i1, i2)
    hi = jnp.maximum(i1, i2)
    pair = jnp.where(lo == 0, hi - 1, jnp.where(lo == 1, hi + 1, 5))
    bucket = gidx * PAIRS + pair
    return bucket, jnp.where(first_low, w1, w2), jnp.where(first_low, w2, w1)


def _mix_kernel(u_ref, halo_ref, sga_ref, sgb_ref, rg_ref, x_ref, gt_ref, sc_ref, sh_ref,
                cw_ref, cb_ref, lnw_ref, lnb_ref, wc_ref, wr_ref, wo_ref, n2_ref,
                rhi_ref, rlo_ref, rb_ref, tri_ref,
                x1_ref, hx_ref, ri_ref, cnt_ref, carry_ref, buf_ref, shf_ref, acc_ref):
    m = pl.program_id(0)

    @pl.when(m == 0)
    def _():
        carry_ref[...] = jnp.zeros_like(carry_ref)

    yb = jnp.dot(rg_ref[...], wr_ref[0], preferred_element_type=F32)
    first_tile = m % (SEQ // MIX_TM) == 0
    ca = _conv_tile(u_ref, halo_ref, first_tile, cw_ref, cb_ref, lnw_ref, lnb_ref, buf_ref, shf_ref, acc_ref)
    ya = jnp.dot(ca.astype(BF16), wc_ref[0], preferred_element_type=F32)
    y = sga_ref[...].astype(F32) * ya + sgb_ref[...].astype(F32) * yb
    o = jnp.dot(y.astype(BF16), wo_ref[0], preferred_element_type=F32)
    x1 = x_ref[...] + gt_ref[0] * o
    x1_ref[...] = x1
    h2 = _rms(x1) * n2_ref[...] * (1.0 + sc_ref[0]) + sh_ref[0]
    hx_ref[:, 0:D] = h2

    hi = h2.astype(BF16)
    lo = (h2 - hi.astype(F32)).astype(BF16)
    rhi = rhi_ref[...]
    logits = (lax.dot_general(rhi, hi, NT_DIMS, preferred_element_type=F32)
              + lax.dot_general(rhi, lo, NT_DIMS, preferred_element_type=F32)
              + lax.dot_general(rlo_ref[...], hi, NT_DIMS, preferred_element_type=F32))
    s = _sigmoid(logits)
    bucket, w_lo, w_hi = _route_rows(s, s + rb_ref[...])

    onehot = (lax.broadcasted_iota(jnp.int32, (BUCKET_ROWS, MIX_TM), 0) == bucket).astype(F32)
    prefix = jnp.dot(onehot.astype(BF16), tri_ref[...], preferred_element_type=F32)
    carry = carry_ref[:, 0:1]
    rank = jnp.sum(onehot * (prefix + carry), axis=0, keepdims=True)
    new_carry = carry + jnp.sum(onehot, axis=1, keepdims=True)
    carry_ref[...] = jnp.broadcast_to(new_carry, carry_ref.shape)
    cnt_ref[...] = jnp.broadcast_to(new_carry, cnt_ref.shape)

    rid = lax.broadcasted_iota(jnp.int32, (8, MIX_TM), 0)
    ri_ref[...] = jnp.where(rid == 0, bucket, jnp.where(rid == 1, rank.astype(jnp.int32), 0))
    wid = lax.broadcasted_iota(jnp.int32, (TAIL, MIX_TM), 0)
    wrows = jnp.where(wid == 0, w_lo, jnp.where(wid == 1, w_hi, 0.0))
    hx_ref[:, D:XROW] = wrows.T


def _mix(rg, p, x, gt1, sc2, sh2, conv_w, conv_b, ln_w, ln_b, w_conv_out_bf, w_ret_out_bf, w_out_bf, l,
         norm2_w, rhi, rlo, rb, tri):
    tpb = SEQ // MIX_TM
    hb = MIX_TM // CONV_HALO
    bidx = lambda m: (m // tpb, 0, 0)
    tile = lambda col: pl.BlockSpec((MIX_TM, D), lambda m: (m, col))
    const = lambda shape: pl.BlockSpec(shape, lambda m: (0,) * len(shape))
    weight = lambda rows: pl.BlockSpec((1, rows, D), lambda m: (l, 0, 0), pipeline_mode=pl.Buffered(1))
    cw8 = jnp.broadcast_to(conv_w[:, None, :], (CONV_K, SUBLANES, D))
    cb8 = jnp.broadcast_to(conv_b[None, :], (SUBLANES, D))
    return pl.pallas_call(
        _mix_kernel,
        out_shape=(jax.ShapeDtypeStruct((NTOK, D), F32),
                   jax.ShapeDtypeStruct((NTOK, XROW), F32),
                   jax.ShapeDtypeStruct((8, NTOK), jnp.int32),
                   jax.ShapeDtypeStruct((BUCKET_ROWS, 128), F32)),
        grid=(NTOK // MIX_TM,),
        in_specs=[tile(PCOL_U),
                  pl.BlockSpec((CONV_HALO, D), lambda m: (jnp.maximum(m * hb - 1, 0), PCOL_U)),
                  tile(PCOL_GA),
                  tile(PCOL_GB),
                  pl.BlockSpec((MIX_TM, HEADS * DV), lambda m: (m, 0)),
                  tile(0),
                  pl.BlockSpec((1, 1, D), bidx),
                  pl.BlockSpec((1, 1, D), bidx),
                  pl.BlockSpec((1, 1, D), bidx),
                  const((CONV_K, SUBLANES, D)),
                  const((SUBLANES, D)),
                  const((1, D)),
                  const((1, D)),
                  weight(D),
                  weight(HEADS * DV),
                  weight(D),
                  const((1, D)),
                  const((N_EXPERTS, D)),
                  const((N_EXPERTS, D)),
                  const((N_EXPERTS, MIX_TM)),
                  const((MIX_TM, MIX_TM))],
        out_specs=(pl.BlockSpec((MIX_TM, D), lambda m: (m, 0)),
                   pl.BlockSpec((MIX_TM, XROW), lambda m: (m, 0)),
                   pl.BlockSpec((8, MIX_TM), lambda m: (0, m)),
                   pl.BlockSpec((BUCKET_ROWS, 128), lambda m: (0, 0))),
        scratch_shapes=[pltpu.VMEM((BUCKET_ROWS, 128), F32),
                        pltpu.VMEM((D // CONV_CW, CONV_TS, CONV_CW), F32),
                        pltpu.VMEM(((CONV_HALO + CONV_TQ) * SUBLANES, D), F32),
                        pltpu.VMEM((D // CONV_CW, CONV_TS, CONV_CW), F32)],
        compiler_params=_cparams("arbitrary"),
        name="mix_route",
    )(p, p, p, p, rg, x, gt1, sc2, sh2, cw8, cb8, ln_w.reshape(1, D), ln_b.reshape(1, D),
      w_conv_out_bf, w_ret_out_bf, w_out_bf, norm2_w.reshape(1, D), rhi, rlo, rb, tri)


MOE_MATMULS = 6


def _moe_kernel(elo_ref, ehi_ref, nval_ref, tok_ref, hx_ref, wg1, wu1, wd1, wg2, wu2, wd2, o_ref, xbuf, sem):
    i = pl.program_id(0)
    slot = i % 2
    nval = nval_ref[i]

    def issue_rows(blk, into, lo, hi):
        for r in range(lo, hi):
            tok = tok_ref[blk * MOE_BLK + r]
            pltpu.make_async_copy(hx_ref.at[pl.ds(tok, 1), :], xbuf.at[into, pl.ds(r, 1), :],
                                  sem.at[into]).start()

    def wait_rows(which):
        pltpu.make_async_copy(hx_ref.at[pl.ds(0, MOE_BLK), :], xbuf.at[which], sem.at[which]).wait()

    @pl.when(i == 0)
    def _():
        issue_rows(0, 0, 0, MOE_BLK)

    @pl.when(nval > 0)
    def _():
        wait_rows(slot)
        live = lax.broadcasted_iota(jnp.int32, (MOE_BLK, 1), 0) < nval
        x = xbuf[slot, :, 0:D].astype(BF16)
        w_lo = jnp.where(live, xbuf[slot, :, D:D + 1], 0.0)
        w_hi = jnp.where(live, xbuf[slot, :, D + 1:D + 2], 0.0)
        cuts = [MOE_BLK * k // MOE_MATMULS for k in range(MOE_MATMULS + 1)]
        phase = iter(range(MOE_MATMULS))

        def prefetch():
            k = next(phase)
            issue_rows(i + 1, 1 - slot, cuts[k], cuts[k + 1])

        def expert(wg, wu, wd):
            g = jnp.dot(x, wg[0], preferred_element_type=F32)
            prefetch()
            u = jnp.dot(x, wu[0], preferred_element_type=F32)
            prefetch()
            a = (g * _sigmoid(g) * u).astype(BF16)
            y = jnp.dot(a, wd[0], preferred_element_type=F32)
            prefetch()
            return y

        o_ref[...] = w_lo * expert(wg1, wu1, wd1) + w_hi * expert(wg2, wu2, wd2)

    @pl.when(nval == 0)
    def _():
        @pl.when((i == 0) | (nval_ref[jnp.maximum(i - 1, 0)] > 0))
        def _():
            wait_rows(slot)

        o_ref[...] = jnp.zeros_like(o_ref)


def _moe(elo, ehi, nval, slot_tok, hx, wg, wu, wd):
    def wspec(tab):
        return pl.BlockSpec((1, D, D), lambda i, elo, ehi, nv, tok: ((elo, ehi)[tab][jnp.minimum(i, N_BLOCKS - 1)], 0, 0))

    return pl.pallas_call(
        _moe_kernel,
        out_shape=jax.ShapeDtypeStruct((SLOT_ROWS + MOE_BLK, D), F32),
        grid_spec=pltpu.PrefetchScalarGridSpec(
            num_scalar_prefetch=4,
            grid=(N_BLOCKS + 1,),
            in_specs=[pl.BlockSpec(memory_space=pl.ANY),
                      wspec(0), wspec(0), wspec(0), wspec(1), wspec(1), wspec(1)],
            out_specs=pl.BlockSpec((MOE_BLK, D), lambda i, elo, ehi, nv, tok: (i, 0)),
            scratch_shapes=[pltpu.VMEM((2, MOE_BLK, XROW), F32), pltpu.SemaphoreType.DMA((2,))]),
        compiler_params=_cparams("arbitrary", row_dma=True),
        name="experts",
    )(elo, ehi, nval, slot_tok, hx, wg, wu, wd, wg, wu, wd)


COMB_TG = 512


def _combine_kernel(dest_ref, ys_ref, x1_ref, gt_ref, nw_ref, sc_ref, sh_ref, *rest, last):
    if last:
        hn_ref, ybuf, sem = rest
    else:
        x2_ref, hn_ref, ybuf, sem = rest
    step = pl.program_id(0)
    slot = step % 2

    def gather_tile(tile, into):
        base = tile * COMB_TG

        def issue(t, carry):
            src = ys_ref.at[pl.ds(dest_ref[base + t], 1), :]
            pltpu.make_async_copy(src, ybuf.at[into, pl.ds(t, 1), :], sem.at[into]).start()
            return carry

        lax.fori_loop(0, COMB_TG, issue, 0, unroll=ROW_DMA_UNROLL)

    @pl.when(step == 0)
    def _():
        gather_tile(0, 0)

    @pl.when(step + 1 < pl.num_programs(0))
    def _():
        gather_tile(step + 1, 1 - slot)

    pltpu.make_async_copy(ys_ref.at[pl.ds(0, COMB_TG), :], ybuf.at[slot], sem.at[slot]).wait()

    x2 = x1_ref[...] + gt_ref[0] * ybuf[slot]
    y = _rms(x2) * nw_ref[...]
    if last:
        hn_ref[...] = y
    else:
        x2_ref[...] = x2
        hn_ref[...] = (y * (1.0 + sc_ref[0]) + sh_ref[0]).astype(hn_ref.dtype)


def _combine(dest, ys, x1, gt2, nw, sc, sh, last):
    tpb = SEQ // COMB_TG
    bidx = lambda i, d: (i // tpb, 0, 0)
    tile = pl.BlockSpec((COMB_TG, D), lambda i, d: (i, 0))
    if last:
        out_shape = jax.ShapeDtypeStruct((NTOK, D), F32)
        out_specs = tile
    else:
        out_shape = (jax.ShapeDtypeStruct((NTOK, D), F32), jax.ShapeDtypeStruct((NTOK, D), BF16))
        out_specs = (tile, tile)
    return pl.pallas_call(
        functools.partial(_combine_kernel, last=last),
        out_shape=out_shape,
        grid_spec=pltpu.PrefetchScalarGridSpec(
            num_scalar_prefetch=1,
            grid=(NTOK // COMB_TG,),
            in_specs=[pl.BlockSpec(memory_space=pl.ANY),
                      tile,
                      pl.BlockSpec((1, 1, D), bidx),
                      pl.BlockSpec((1, D), lambda i, d: (0, 0)),
                      pl.BlockSpec((1, 1, D), bidx),
                      pl.BlockSpec((1, 1, D), bidx)],
            out_specs=out_specs,
            scratch_shapes=[pltpu.VMEM((2, COMB_TG, D), F32), pltpu.SemaphoreType.DMA((2,))]),
        compiler_params=_cparams("arbitrary", row_dma=True),
        name="combine",
    )(dest, ys, x1, gt2, nw.reshape(1, D), sc, sh)


def _pair_tables():
    lo, hi = [], []
    for g in range(N_GROUPS):
        for a in range(GROUP_SIZE):
            for b in range(a + 1, GROUP_SIZE):
                lo.append(GROUP_SIZE * g + a)
                hi.append(GROUP_SIZE * g + b)
    return np.asarray(lo, np.int32), np.asarray(hi, np.int32)


def _slot_plan(ri, cnt):
    counts = cnt[:N_BUCKETS, 0].astype(jnp.int32)
    padded = (counts + MOE_BLK - 1) // MOE_BLK * MOE_BLK
    pad_end = jnp.cumsum(padded)
    pad_start = pad_end - padded
    dest = pad_start[ri[0]] + ri[1]
    blk_start = jnp.arange(N_BLOCKS, dtype=jnp.int32) * MOE_BLK
    bb = jnp.minimum(jnp.searchsorted(pad_end, blk_start, side='right'), N_BUCKETS - 1).astype(jnp.int32)
    nval = jnp.clip(pad_start[bb] + counts[bb] - blk_start, 0, MOE_BLK).astype(jnp.int32)
    nval = jnp.concatenate([nval, jnp.zeros((1,), jnp.int32)])
    slot_tok = jnp.zeros((SLOT_ROWS + MOE_BLK,), jnp.int32).at[dest].set(jnp.arange(NTOK, dtype=jnp.int32))
    lo_tab, hi_tab = _pair_tables()
    return dest.astype(jnp.int32), slot_tok, jnp.asarray(lo_tab)[bb], jnp.asarray(hi_tab)[bb], nval


def kernel(x, c, positions, w_ada, b_ada, norm1_w, w_in, conv_w, conv_b, conv_ln_w, conv_ln_b, w_conv_out,
           w_ret_out, w_out, norm2_w, w_router, router_bias, w_exp_gate, w_exp_up, w_exp_down, final_norm_w):
    mod = _ada(c, w_ada, b_ada).reshape(DEPTH, BATCH, N_MOD, 1, D)
    sh1, sc1, gt1, sh2, sc2, gt2 = (mod[:, :, i] for i in range(N_MOD))
    cos, sin = _rope_tables(positions)
    ret_tables = _ret_tables()

    w_conv_out_bf = w_conv_out.astype(BF16)
    w_ret_out_bf = w_ret_out.astype(BF16)
    w_out_bf = w_out.astype(BF16)

    wr_t = w_router.T
    rhi = wr_t.astype(BF16)
    rlo = (wr_t - rhi.astype(F32)).astype(BF16)
    rb = jnp.broadcast_to(router_bias.astype(F32)[:, None], (N_EXPERTS, MIX_TM))
    tri = (jnp.arange(MIX_TM)[:, None] < jnp.arange(MIX_TM)[None, :]).astype(BF16)

    xf = x.reshape(NTOK, D)
    h = _modnorm(x, norm1_w[0], sc1[0], sh1[0]).reshape(NTOK, D)
    out = None
    for l in range(DEPTH):
        p, wg_bf, wu_bf, wd_bf = _inproj(h, w_in, l, cos, sin, (w_exp_gate, w_exp_up, w_exp_down))
        rg = _retention(p.reshape(BATCH, SEQ, P_COLS), ret_tables)
        x1, hx, ri, cnt = _mix(rg.reshape(NTOK, HEADS * DV), p, xf, gt1[l], sc2[l], sh2[l],
                               conv_w[l], conv_b[l], conv_ln_w[l], conv_ln_b[l],
                               w_conv_out_bf, w_ret_out_bf, w_out_bf, l, norm2_w[l], rhi, rlo, rb, tri)
        dest, slot_tok, elo, ehi, nval = _slot_plan(ri, cnt)
        ys = _moe(elo, ehi, nval, slot_tok, hx, wg_bf, wu_bf, wd_bf)
        if l + 1 < DEPTH:
            xf, h = _combine(dest, ys, x1, gt2[l], norm1_w[l + 1], sc1[l + 1], sh1[l + 1], last=False)
        else:
            out = _combine(dest, ys, x1, gt2[l], final_norm_w, sc1[l], sh1[l], last=True)
    return out.reshape(BATCH, SEQ, D)
```

```python
import functools

import numpy as np
import jax
import jax.numpy as jnp
from jax import lax
from jax.experimental import pallas as pl
from jax.experimental.pallas import tpu as pltpu

F32 = jnp.float32
BF16 = jnp.bfloat16
I32 = jnp.int32

D = 1024
BATCH = 8
SEQ = 2048
DEPTH = 4
NTOK = BATCH * SEQ
N_MOD = 6
EPS = 1e-6

CONV_K = 31
HEADS = 4
DK = 256
DV = 512
CHUNK = 128
ROPE_BASE = 10000.0
HALF = DK // 2

N_EXPERTS = 16
N_GROUPS = 4
GROUP_SIZE = 4
PAIRS = 6
N_BUCKETS = N_GROUPS * PAIRS
BUCKET_ROWS = 32

P_COLS = 8 * D
PCOL_Q, PCOL_K, PCOL_V, PCOL_G, PCOL_GA, PCOL_GB = 0, 1, 2, 4, 6, 7

MOE_BLK = 256
N_BLOCKS = NTOK // MOE_BLK + N_BUCKETS
SLOT_ROWS = N_BLOCKS * MOE_BLK
PACKED = D // 2
TAIL = 128
XROW = PACKED + TAIL

VMEM_LIMIT = 56 * 1024 * 1024

NT_DIMS = (((1,), (1,)), ((), ()))


ROW_DMA_UNROLL = 8


def _cparams(*sem, row_dma=False):
    return pltpu.CompilerParams(dimension_semantics=sem, vmem_limit_bytes=VMEM_LIMIT,
                                disable_bounds_checks=row_dma)


def _sigmoid(x):
    return jax.nn.sigmoid(x)


def _rms(x):
    return x * lax.rsqrt(jnp.mean(x * x, axis=-1, keepdims=True) + EPS)


def _pack_bf16_pairs(v):
    half = v.shape[-1] // 2
    hi = lax.bitcast_convert_type(v[:, :half].astype(BF16).astype(F32), I32)
    lo = lax.bitcast_convert_type(v[:, half:].astype(BF16).astype(F32), I32)
    return hi | lax.shift_right_logical(lo, 16)


def _unpack_bf16_pairs(w):
    hi = lax.bitcast_convert_type(w & jnp.int32(-65536), F32)
    lo = lax.bitcast_convert_type(lax.shift_left(w, 16), F32)
    return jnp.concatenate([hi, lo], axis=-1)


ADA_TN = 1536


def _ada_kernel(c_ref, w_ref, b_ref, o_ref):
    c = c_ref[...]
    sc = c * _sigmoid(c)
    o_ref[0] = jnp.dot(sc, w_ref[0], precision=lax.Precision.HIGHEST,
                       preferred_element_type=F32) + b_ref[0]


def _ada(c, w_ada, b_ada):
    nj = N_MOD * D // ADA_TN
    return pl.pallas_call(
        _ada_kernel,
        out_shape=jax.ShapeDtypeStruct((DEPTH, BATCH, N_MOD * D), F32),
        grid=(DEPTH, nj),
        in_specs=[pl.BlockSpec((BATCH, D), lambda l, j: (0, 0)),
                  pl.BlockSpec((1, D, ADA_TN), lambda l, j: (l, 0, j)),
                  pl.BlockSpec((1, 1, ADA_TN), lambda l, j: (l, 0, j))],
        out_specs=pl.BlockSpec((1, BATCH, ADA_TN), lambda l, j: (l, 0, j)),
        compiler_params=_cparams("parallel", "parallel"),
        name="ada",
    )(c, w_ada, b_ada.reshape(DEPTH, 1, N_MOD * D))


ROPE_TN = 2048


def _rope_kernel(pos_ref, inv_ref, cos_ref, sin_ref):
    ang = pos_ref[...] * inv_ref[...]
    cos_ref[...] = jnp.cos(ang)
    sin_ref[...] = jnp.sin(ang)


def _rope_tables(positions):
    inv = ROPE_BASE ** (-jnp.arange(HALF, dtype=F32) / HALF)
    pos = positions.astype(F32).reshape(NTOK, 1)
    return pl.pallas_call(
        _rope_kernel,
        out_shape=(jax.ShapeDtypeStruct((NTOK, HALF), F32),) * 2,
        grid=(NTOK // ROPE_TN,),
        in_specs=[pl.BlockSpec((ROPE_TN, 1), lambda i: (i, 0)),
                  pl.BlockSpec((1, HALF), lambda i: (0, 0))],
        out_specs=(pl.BlockSpec((ROPE_TN, HALF), lambda i: (i, 0)),) * 2,
        compiler_params=_cparams("parallel"),
        name="rope_tables",
    )(pos, inv.reshape(1, HALF))


NORM_TS = 512


def _modnorm_kernel(x_ref, w_ref, sc_ref, sh_ref, o_ref):
    y = _rms(x_ref[0]) * w_ref[...]
    o_ref[0] = (y * (1.0 + sc_ref[0]) + sh_ref[0]).astype(o_ref.dtype)


def _modnorm(x, w, sc, sh):
    return pl.pallas_call(
        _modnorm_kernel,
        out_shape=jax.ShapeDtypeStruct((BATCH, SEQ, D), BF16),
        grid=(BATCH, SEQ // NORM_TS),
        in_specs=[pl.BlockSpec((1, NORM_TS, D), lambda b, i: (b, i, 0)),
                  pl.BlockSpec((1, D), lambda b, i: (0, 0)),
                  pl.BlockSpec((1, 1, D), lambda b, i: (b, 0, 0)),
                  pl.BlockSpec((1, 1, D), lambda b, i: (b, 0, 0))],
        out_specs=pl.BlockSpec((1, NORM_TS, D), lambda b, i: (b, i, 0)),
        compiler_params=_cparams("parallel", "parallel"),
        name="modnorm",
    )(x, w.reshape(1, D), sc, sh)


INP_TM = 1024
INP_GROUPS = 9
EXP_SLAB_ROWS = 512
EXP_SLABS_PER_MAT = D // EXP_SLAB_ROWS
EXP_SLABS = N_EXPERTS * EXP_SLABS_PER_MAT


def _inproj_kernel(h_ref, w1_ref, w2_ref, cos_ref, sin_ref, eg_ref, eu_ref, ed_ref,
                   o_ref, u_ref, og_ref, ou_ref, od_ref, wb1_ref, wb2_ref):
    j = pl.program_id(0)
    m = pl.program_id(1)
    h = h_ref[...]

    step = j * pl.num_programs(1) + m
    for t, (src, dst) in enumerate(((eg_ref, og_ref), (eu_ref, ou_ref), (ed_ref, od_ref))):
        @pl.when((step >= EXP_SLABS * t) & (step < EXP_SLABS * (t + 1)))
        def _():
            dst[0] = src[0, 0].astype(BF16)

    @pl.when(m == 0)
    def _():
        wb1_ref[...] = w1_ref[0].astype(BF16)

    @pl.when((m == 0) & (j == 0))
    def _():
        wb2_ref[...] = w2_ref[0].astype(BF16)

    def proj():
        return jnp.dot(h, wb1_ref[...], preferred_element_type=F32)

    @pl.when(j == 0)
    def _():
        a = proj()
        b = jnp.dot(h, wb2_ref[...], preferred_element_type=F32)
        u_ref[...] = (a * _sigmoid(b)).astype(BF16)

    @pl.when((j == 1) | (j == 2))
    def _():
        t = proj()
        scale = jnp.where(j == 2, DK ** -0.5, 1.0).astype(F32)
        cos = cos_ref[...] * scale
        sin = sin_ref[...] * scale
        for hd in range(HEADS):
            c0 = hd * DK
            t1 = t[:, c0:c0 + HALF]
            t2 = t[:, c0 + HALF:c0 + DK]
            o_ref[:, c0:c0 + HALF] = (t1 * cos - t2 * sin).astype(BF16)
            o_ref[:, c0 + HALF:c0 + DK] = (t1 * sin + t2 * cos).astype(BF16)

    @pl.when((j == 3) | (j == 4))
    def _():
        o_ref[...] = proj().astype(BF16)

    @pl.when((j == 5) | (j == 6))
    def _():
        g = proj()
        o_ref[...] = (g * _sigmoid(g)).astype(BF16)

    @pl.when(j >= 7)
    def _():
        o_ref[...] = _sigmoid(proj()).astype(BF16)


def _inproj(h, w_in, l, cos, sin, w_exp):
    n_m = NTOK // INP_TM
    tiles_per_seq = SEQ // INP_TM
    assert 3 * EXP_SLABS <= INP_GROUPS * n_m

    def rope_idx(j, m):
        return (jnp.where((j == 1) | (j == 2), m, 0), 0)

    def slab(t, j, m):
        s = jnp.clip(j * n_m + m - EXP_SLABS * t, 0, EXP_SLABS - 1)
        return s // EXP_SLABS_PER_MAT, s % EXP_SLABS_PER_MAT

    def exp_in(t):
        return pl.BlockSpec((1, 1, EXP_SLAB_ROWS, D), lambda j, m: (l, *slab(t, j, m), 0))

    def exp_out(t):
        return pl.BlockSpec((1, EXP_SLAB_ROWS, D), lambda j, m: (*slab(t, j, m), 0))

    def p_idx(j, m):
        return (jnp.where(j == 0, 0, m), jnp.maximum(j - 1, 0))

    def u_idx(j, m):
        mm = jnp.where(j == 0, m, n_m - 1)
        return (mm % tiles_per_seq, mm // tiles_per_seq)

    exp_shape = jax.ShapeDtypeStruct((N_EXPERTS, D, D), BF16)
    return pl.pallas_call(
        _inproj_kernel,
        out_shape=(jax.ShapeDtypeStruct((NTOK, P_COLS), BF16),
                   jax.ShapeDtypeStruct((SEQ, BATCH * D), BF16),
                   exp_shape, exp_shape, exp_shape),
        grid=(INP_GROUPS, n_m),
        in_specs=[pl.BlockSpec((INP_TM, D), lambda j, m: (m, 0)),
                  pl.BlockSpec((1, D, D), lambda j, m: (l, 0, jnp.where(j == 0, 0, j + 1))),
                  pl.BlockSpec((1, D, D), lambda j, m: (l, 0, 1), pipeline_mode=pl.Buffered(1)),
                  pl.BlockSpec((INP_TM, HALF), rope_idx),
                  pl.BlockSpec((INP_TM, HALF), rope_idx),
                  exp_in(0), exp_in(1), exp_in(2)],
        out_specs=(pl.BlockSpec((INP_TM, D), p_idx),
                   pl.BlockSpec((INP_TM, D), u_idx),
                   exp_out(0), exp_out(1), exp_out(2)),
        scratch_shapes=[pltpu.VMEM((D, D), BF16), pltpu.VMEM((D, D), BF16)],
        compiler_params=_cparams("arbitrary", "arbitrary"),
        name="inproj",
    )(h, w_in, w_in, cos, sin, *w_exp)


CONV_TQ = 32
CONV_ROWS = CONV_TQ * BATCH
CONV_CW = 128
CONV_GROUP = 8
SUBLANES = 8
assert BATCH == SUBLANES and CONV_K - 1 <= CONV_TQ and CONV_TQ % CONV_GROUP == 0


def _conv_kernel(cur_ref, prev_ref, sga_ref, cw_ref, cb_ref, lnw_ref, lnb_ref, wo_ref, perm_ref, o_ref,
                 ext_ref, acc_ref, wbf_ref):
    i = pl.program_id(0)

    @pl.when(i == 0)
    def _():
        wbf_ref[...] = wo_ref[0].astype(BF16)

    ext_ref[0:CONV_ROWS, :] = jnp.where(i > 0, prev_ref[...].astype(F32), 0.0)
    ext_ref[CONV_ROWS:2 * CONV_ROWS, :] = cur_ref[...].astype(F32)

    def group(gi, carry):
        row0 = pl.multiple_of(gi * (CONV_GROUP * SUBLANES), CONV_GROUP * SUBLANES)

        def block(a):
            return pl.ds(row0 + a * SUBLANES, SUBLANES)

        for cc in range(D // CONV_CW):
            cols = slice(cc * CONV_CW, (cc + 1) * CONV_CW)
            accs = [cb_ref[:, cols]] * CONV_GROUP
            for j in range(CONV_K):
                w8 = cw_ref[j, :, cols]
                for g in range(CONV_GROUP):
                    accs[g] = accs[g] + ext_ref[block(CONV_TQ + g - (CONV_K - 1) + j), cols] * w8
            for g in range(CONV_GROUP):
                acc_ref[block(g), cols] = accs[g]
        return carry

    lax.fori_loop(0, CONV_TQ // CONV_GROUP, group, 0)

    c = acc_ref[...]
    mu = jnp.mean(c, axis=-1, keepdims=True)
    cen = c - mu
    var = jnp.mean(cen * cen, axis=-1, keepdims=True)
    y = cen * lax.rsqrt(var + EPS) * lnw_ref[...] + lnb_ref[...]
    y = (y * _sigmoid(y)).astype(BF16)
    y = jnp.dot(perm_ref[...], y, preferred_element_type=F32).astype(BF16)
    out = jnp.dot(y, wbf_ref[...], preferred_element_type=F32)
    gate = sga_ref[...].reshape(CONV_ROWS, D).astype(F32)
    o_ref[...] = (out * gate).astype(BF16).reshape(BATCH, CONV_TQ, D)


def _conv_branch(u_tm, p3, l, conv_w, conv_b, ln_w, ln_b, w_conv_out):
    cw8 = jnp.broadcast_to(conv_w[:, None, :], (CONV_K, SUBLANES, D))
    cb8 = jnp.broadcast_to(conv_b[None, :], (SUBLANES, D))
    r = np.arange(CONV_ROWS)
    perm = np.zeros((CONV_ROWS, CONV_ROWS), np.float32)
    perm[(r % BATCH) * CONV_TQ + r // BATCH, r] = 1.0
    const = lambda shape: pl.BlockSpec(shape, lambda i: (0,) * len(shape))
    return pl.pallas_call(
        _conv_kernel,
        out_shape=jax.ShapeDtypeStruct((BATCH, SEQ, D), BF16),
        grid=(SEQ // CONV_TQ,),
        in_specs=[pl.BlockSpec((CONV_ROWS, D), lambda i: (i, 0)),
                  pl.BlockSpec((CONV_ROWS, D), lambda i: (jnp.maximum(i - 1, 0), 0)),
                  pl.BlockSpec((BATCH, CONV_TQ, D), lambda i: (0, i, PCOL_GA)),
                  const((CONV_K, SUBLANES, D)),
                  const((SUBLANES, D)),
                  const((1, D)),
                  const((1, D)),
                  pl.BlockSpec((1, D, D), lambda i: (l, 0, 0), pipeline_mode=pl.Buffered(1)),
                  const((CONV_ROWS, CONV_ROWS))],
        out_specs=pl.BlockSpec((BATCH, CONV_TQ, D), lambda i: (0, i, 0)),
        scratch_shapes=[pltpu.VMEM((2 * CONV_ROWS, D), F32),
                        pltpu.VMEM((CONV_ROWS, D), F32),
                        pltpu.VMEM((D, D), BF16)],
        compiler_params=_cparams("arbitrary"),
        name="conv_branch",
    )(u_tm.reshape(SEQ * BATCH, D), u_tm.reshape(SEQ * BATCH, D), p3, cw8, cb8,
      ln_w.reshape(1, D), ln_b.reshape(1, D), w_conv_out, jnp.asarray(perm, BF16))


def _ret_tables():
    hh = jnp.arange(HEADS, dtype=F32)
    log_g = jnp.log1p(-jnp.exp2(-5.0 - hh))
    idx = jnp.arange(CHUNK, dtype=F32)
    rel = idx[:, None] - idx[None, :]
    dmask = jnp.where(rel[None] >= 0, jnp.exp(jnp.maximum(rel, 0.0)[None] * log_g[:, None, None]), 0.0)
    xi = jnp.exp((idx + 1.0)[None, :] * log_g[:, None])[..., None]
    zeta = jnp.exp((CHUNK - 1.0 - idx)[None, :] * log_g[:, None])[..., None]
    g_chunk = jnp.exp(CHUNK * log_g)[:, None, None]
    return (dmask,
            jnp.broadcast_to(xi, (HEADS, CHUNK, DK)),
            jnp.broadcast_to(zeta, (HEADS, CHUNK, DK)),
            jnp.broadcast_to(g_chunk, (HEADS, 1, DV)))


RET_TS = 512


def _ret_kernel(q_ref, k_ref, v01_ref, v23_ref, g01_ref, g23_ref, dm_ref, xi_ref, zt_ref, gc_ref, o_ref, st_ref):
    @pl.when(pl.program_id(1) == 0)
    def _():
        st_ref[...] = jnp.zeros_like(st_ref)

    def body(c, carry):
        r0 = pl.multiple_of(c * CHUNK, CHUNK)
        rows = pl.ds(r0, CHUNK)
        for hd in range(HEADS):
            v_ref, g_ref = (v01_ref, g01_ref) if hd < 2 else (v23_ref, g23_ref)
            vcols = slice((hd % 2) * DV, (hd % 2 + 1) * DV)
            qc = q_ref[0, rows, hd * DK:(hd + 1) * DK]
            kc = k_ref[0, rows, hd * DK:(hd + 1) * DK]
            vc = v_ref[0, rows, vcols]
            scores = lax.dot_general(qc, kc, NT_DIMS, preferred_element_type=F32) * dm_ref[hd]
            inner = jnp.dot(scores.astype(BF16), vc, preferred_element_type=F32)
            st = st_ref[hd]
            qx = (qc.astype(F32) * xi_ref[hd]).astype(BF16)
            cross = jnp.dot(qx, st.astype(BF16), preferred_element_type=F32)
            kzt = (kc.astype(F32) * zt_ref[hd]).T.astype(BF16)
            st_ref[hd] = st * gc_ref[hd] + jnp.dot(kzt, vc, preferred_element_type=F32)
            o = inner + cross
            mu = jnp.mean(o, axis=-1, keepdims=True)
            cen = o - mu
            var = jnp.mean(cen * cen, axis=-1, keepdims=True)
            r = cen * lax.rsqrt(var + EPS)
            o_ref[0, rows, hd * DV:(hd + 1) * DV] = (r * g_ref[0, rows, vcols].astype(F32)).astype(BF16)
        return carry

    lax.fori_loop(0, RET_TS // CHUNK, body, 0)


def _retention(p3, tables):
    dmask, xi, zeta, gch = tables
    tile = lambda col: pl.BlockSpec((1, RET_TS, D), lambda b, i: (b, i, col))
    full = lambda a: pl.BlockSpec(a.shape, lambda b, i: (0, 0, 0))
    return pl.pallas_call(
        _ret_kernel,
        out_shape=jax.ShapeDtypeStruct((BATCH, SEQ, HEADS * DV), BF16),
        grid=(BATCH, SEQ // RET_TS),
        in_specs=[tile(PCOL_Q), tile(PCOL_K), tile(PCOL_V), tile(PCOL_V + 1), tile(PCOL_G), tile(PCOL_G + 1),
                  full(dmask), full(xi), full(zeta), full(gch)],
        out_specs=pl.BlockSpec((1, RET_TS, HEADS * DV), lambda b, i: (b, i, 0)),
        scratch_shapes=[pltpu.VMEM((HEADS, DK, DV), F32)],
        compiler_params=_cparams("parallel", "arbitrary"),
        name="retention",
    )(p3, p3, p3, p3, p3, p3, dmask, xi, zeta, gch)


MIX_TM = 512


def _route_rows(s, sb):
    row = lambda a, e: a[e:e + 1, :]
    best = None
    gidx = None
    for g in range(N_GROUPS):
        v = [row(sb, GROUP_SIZE * g + i) for i in range(GROUP_SIZE)]
        pair_sums = [v[a] + v[b] for a in range(GROUP_SIZE) for b in range(a + 1, GROUP_SIZE)]
        gs = functools.reduce(jnp.maximum, pair_sums)
        if g == 0:
            best, gidx = gs, jnp.zeros(gs.shape, jnp.int32)
        else:
            upd = gs > best
            gidx = jnp.where(upd, g, gidx)
            best = jnp.where(upd, gs, best)

    def pick(a, i):
        out = row(a, i)
        for g in range(1, N_GROUPS):
            out = jnp.where(gidx == g, row(a, GROUP_SIZE * g + i), out)
        return out

    vb = [pick(sb, i) for i in range(GROUP_SIZE)]
    vs = [pick(s, i) for i in range(GROUP_SIZE)]
    m1, i1, s1 = vb[0], jnp.zeros(gidx.shape, jnp.int32), vs[0]
    for i in range(1, GROUP_SIZE):
        upd = vb[i] > m1
        m1 = jnp.where(upd, vb[i], m1)
        i1 = jnp.where(upd, i, i1)
        s1 = jnp.where(upd, vs[i], s1)
    m2 = i2 = s2 = None
    for i in range(GROUP_SIZE):
        cand = jnp.where(i1 == i, -jnp.inf, vb[i])
        if m2 is None:
            m2, i2, s2 = cand, jnp.zeros(gidx.shape, jnp.int32), vs[0]
        else:
            upd = cand > m2
            m2 = jnp.where(upd, cand, m2)
            i2 = jnp.where(upd, i, i2)
            s2 = jnp.where(upd, vs[i], s2)
    den = s1 + s2
    w1 = s1 / den
    w2 = s2 / den
    first_low = i1 < i2
    lo = jnp.minimum(i1, i2)
    hi = jnp.maximum(i1, i2)
    pair = jnp.where(lo == 0, hi - 1, jnp.where(lo == 1, hi + 1, 5))
    bucket = gidx * PAIRS + pair
    return bucket, jnp.where(first_low, w1, w2), jnp.where(first_low, w2, w1)


def _mix_kernel(rg_ref, ya_ref, sgb_ref, x_ref, gt_ref, sc_ref, sh_ref, wr_ref, wo_ref, n2_ref,
                rhi_ref, rlo_ref, rb_ref, tri_ref,
                x1_ref, hx_ref, ri_ref, cnt_ref, carry_ref, wrb_ref, wob_ref):
    m = pl.program_id(0)

    @pl.when(m == 0)
    def _():
        carry_ref[...] = jnp.zeros_like(carry_ref)
        wrb_ref[...] = wr_ref[0].astype(BF16)
        wob_ref[...] = wo_ref[0].astype(BF16)

    yb = jnp.dot(rg_ref[...], wrb_ref[...], preferred_element_type=F32)
    y = ya_ref[...].astype(F32) + sgb_ref[...].astype(F32) * yb
    o = jnp.dot(y.astype(BF16), wob_ref[...], preferred_element_type=F32)
    x1 = x_ref[...] + gt_ref[0] * o
    x1_ref[...] = x1
    h2 = _rms(x1) * n2_ref[...] * (1.0 + sc_ref[0]) + sh_ref[0]
    hx_ref[:, 0:PACKED] = _pack_bf16_pairs(h2)

    hi = h2.astype(BF16)
    lo = (h2 - hi.astype(F32)).astype(BF16)
    rhi = rhi_ref[...]
    logits = (lax.dot_general(rhi, hi, NT_DIMS, preferred_element_type=F32)
              + lax.dot_general(rhi, lo, NT_DIMS, preferred_element_type=F32)
              + lax.dot_general(rlo_ref[...], hi, NT_DIMS, preferred_element_type=F32))
    s = _sigmoid(logits)
    bucket, w_lo, w_hi = _route_rows(s, s + rb_ref[...])

    onehot = (lax.broadcasted_iota(jnp.int32, (BUCKET_ROWS, MIX_TM), 0) == bucket).astype(F32)
    prefix = jnp.dot(onehot.astype(BF16), tri_ref[...], preferred_element_type=F32)
    carry = carry_ref[:, 0:1]
    rank = jnp.sum(onehot * (prefix + carry), axis=0, keepdims=True)
    new_carry = carry + jnp.sum(onehot, axis=1, keepdims=True)
    carry_ref[...] = jnp.broadcast_to(new_carry, carry_ref.shape)
    cnt_ref[...] = jnp.broadcast_to(new_carry, cnt_ref.shape)

    rid = lax.broadcasted_iota(jnp.int32, (8, MIX_TM), 0)
    ri_ref[...] = jnp.where(rid == 0, bucket, jnp.where(rid == 1, rank.astype(jnp.int32), 0))
    wid = lax.broadcasted_iota(jnp.int32, (TAIL, MIX_TM), 0)
    wrows = jnp.where(wid == 0, w_lo, jnp.where(wid == 1, w_hi, 0.0))
    hx_ref[:, PACKED:XROW] = lax.bitcast_convert_type(wrows.T, I32)


def _mix(rg, ya, p, x, gt1, sc2, sh2, w_ret_out, w_out, l, norm2_w, rhi, rlo, rb, tri):
    tpb = SEQ // MIX_TM
    bidx = lambda m: (m // tpb, 0, 0)
    return pl.pallas_call(
        _mix_kernel,
        out_shape=(jax.ShapeDtypeStruct((NTOK, D), F32),
                   jax.ShapeDtypeStruct((NTOK, XROW), I32),
                   jax.ShapeDtypeStruct((8, NTOK), jnp.int32),
                   jax.ShapeDtypeStruct((BUCKET_ROWS, 128), F32)),
        grid=(NTOK // MIX_TM,),
        in_specs=[pl.BlockSpec((MIX_TM, HEADS * DV), lambda m: (m, 0)),
                  pl.BlockSpec((MIX_TM, D), lambda m: (m, 0)),
                  pl.BlockSpec((MIX_TM, D), lambda m: (m, PCOL_GB)),
                  pl.BlockSpec((MIX_TM, D), lambda m: (m, 0)),
                  pl.BlockSpec((1, 1, D), bidx),
                  pl.BlockSpec((1, 1, D), bidx),
                  pl.BlockSpec((1, 1, D), bidx),
                  pl.BlockSpec((1, HEADS * DV, D), lambda m: (l, 0, 0), pipeline_mode=pl.Buffered(1)),
                  pl.BlockSpec((1, D, D), lambda m: (l, 0, 0), pipeline_mode=pl.Buffered(1)),
                  pl.BlockSpec((1, D), lambda m: (0, 0)),
                  pl.BlockSpec((N_EXPERTS, D), lambda m: (0, 0)),
                  pl.BlockSpec((N_EXPERTS, D), lambda m: (0, 0)),
                  pl.BlockSpec((N_EXPERTS, MIX_TM), lambda m: (0, 0)),
                  pl.BlockSpec((MIX_TM, MIX_TM), lambda m: (0, 0))],
        out_specs=(pl.BlockSpec((MIX_TM, D), lambda m: (m, 0)),
                   pl.BlockSpec((MIX_TM, XROW), lambda m: (m, 0)),
                   pl.BlockSpec((8, MIX_TM), lambda m: (0, m)),
                   pl.BlockSpec((BUCKET_ROWS, 128), lambda m: (0, 0))),
        scratch_shapes=[pltpu.VMEM((BUCKET_ROWS, 128), F32),
                        pltpu.VMEM((HEADS * DV, D), BF16), pltpu.VMEM((D, D), BF16)],
        compiler_params=_cparams("arbitrary"),
        name="mix_route",
    )(rg, ya, p, x, gt1, sc2, sh2, w_ret_out, w_out, norm2_w.reshape(1, D), rhi, rlo, rb, tri)


DISP_TG = 512


def _dispatch_kernel(dest_ref, nval_ref, src_ref, xs_ref, zero_ref, sem, zsem):
    step = pl.program_id(0)
    base = step * DISP_TG

    @pl.when(step == 0)
    def _():
        zero_ref[...] = jnp.zeros_like(zero_ref)

        def zero_copy(b):
            r0 = pl.multiple_of(b * MOE_BLK, MOE_BLK)
            return pltpu.make_async_copy(zero_ref, xs_ref.at[pl.ds(r0, MOE_BLK), :], zsem)

        def zissue(b, carry):
            @pl.when(nval_ref[b] < MOE_BLK)
            def _():
                zero_copy(b).start()
            return carry

        lax.fori_loop(0, N_BLOCKS, zissue, 0)

        def zdrain(b, carry):
            @pl.when(nval_ref[b] < MOE_BLK)
            def _():
                zero_copy(b).wait()
            return carry

        lax.fori_loop(0, N_BLOCKS, zdrain, 0)

    def row_copy(t, d):
        return pltpu.make_async_copy(src_ref.at[pl.ds(t, 1), :], xs_ref.at[pl.ds(d, 1), :], sem)

    def issue(t, carry):
        row_copy(t, dest_ref[base + t]).start()
        return carry

    lax.fori_loop(0, DISP_TG, issue, 0, unroll=ROW_DMA_UNROLL)
    pltpu.make_async_copy(src_ref, xs_ref.at[pl.ds(0, DISP_TG), :], sem).wait()


def _dispatch(dest, nval, hx):
    return pl.pallas_call(
        _dispatch_kernel,
        out_shape=jax.ShapeDtypeStruct((SLOT_ROWS, XROW), I32),
        grid_spec=pltpu.PrefetchScalarGridSpec(
            num_scalar_prefetch=2,
            grid=(NTOK // DISP_TG,),
            in_specs=[pl.BlockSpec((DISP_TG, XROW), lambda i, d, nv: (i, 0))],
            out_specs=pl.BlockSpec(memory_space=pl.ANY),
            scratch_shapes=[pltpu.VMEM((MOE_BLK, XROW), I32),
                            pltpu.SemaphoreType.DMA(()), pltpu.SemaphoreType.DMA(())]),
        compiler_params=_cparams("arbitrary", row_dma=True),
        name="dispatch",
    )(dest, nval, hx)


def _moe_kernel(elo_ref, ehi_ref, nval_ref, xs_ref, wg1, wu1, wd1, wg2, wu2, wd2, o_ref):
    i = pl.program_id(0)
    nval = nval_ref[i]

    @pl.when(nval > 0)
    def _():
        x = _unpack_bf16_pairs(xs_ref[:, 0:PACKED]).astype(BF16)
        w_lo = lax.bitcast_convert_type(xs_ref[:, PACKED:PACKED + 1], F32)
        w_hi = lax.bitcast_convert_type(xs_ref[:, PACKED + 1:PACKED + 2], F32)

        def expert(wg, wu, wd):
            g = jnp.dot(x, wg[0], preferred_element_type=F32)
            u = jnp.dot(x, wu[0], preferred_element_type=F32)
            a = (g * _sigmoid(g) * u).astype(BF16)
            return jnp.dot(a, wd[0], preferred_element_type=F32)

        y = w_lo * expert(wg1, wu1, wd1) + w_hi * expert(wg2, wu2, wd2)
        o_ref[...] = _pack_bf16_pairs(y)

    @pl.when(nval == 0)
    def _():
        o_ref[...] = jnp.zeros_like(o_ref)


def _moe(elo, ehi, nval, xs, wg, wu, wd):
    wspec = lambda tab: pl.BlockSpec((1, D, D), lambda i, elo, ehi, nv: ((elo, ehi)[tab][i], 0, 0))
    return pl.pallas_call(
        _moe_kernel,
        out_shape=jax.ShapeDtypeStruct((SLOT_ROWS, PACKED), I32),
        grid_spec=pltpu.PrefetchScalarGridSpec(
            num_scalar_prefetch=3,
            grid=(N_BLOCKS,),
            in_specs=[pl.BlockSpec((MOE_BLK, XROW), lambda i, elo, ehi, nv: (i, 0)),
                      wspec(0), wspec(0), wspec(0), wspec(1), wspec(1), wspec(1)],
            out_specs=pl.BlockSpec((MOE_BLK, PACKED), lambda i, elo, ehi, nv: (i, 0))),
        compiler_params=_cparams("arbitrary"),
        name="experts",
    )(elo, ehi, nval, xs, wg, wu, wd, wg, wu, wd)


COMB_TG = 512


def _combine_kernel(dest_ref, ys_ref, x1_ref, gt_ref, nw_ref, sc_ref, sh_ref, *rest, last):
    if last:
        hn_ref, ybuf, sem = rest
    else:
        x2_ref, hn_ref, ybuf, sem = rest
    step = pl.program_id(0)
    slot = step % 2

    def gather_tile(tile, into):
        base = tile * COMB_TG

        def issue(t, carry):
            src = ys_ref.at[pl.ds(dest_ref[base + t], 1), :]
            pltpu.make_async_copy(src, ybuf.at[into, pl.ds(t, 1), :], sem.at[into]).start()
            return carry

        lax.fori_loop(0, COMB_TG, issue, 0, unroll=ROW_DMA_UNROLL)

    @pl.when(step == 0)
    def _():
        gather_tile(0, 0)

    @pl.when(step + 1 < pl.num_programs(0))
    def _():
        gather_tile(step + 1, 1 - slot)

    pltpu.make_async_copy(ys_ref.at[pl.ds(0, COMB_TG), :], ybuf.at[slot], sem.at[slot]).wait()

    x2 = x1_ref[...] + gt_ref[0] * _unpack_bf16_pairs(ybuf[slot])
    y = _rms(x2) * nw_ref[...]
    if last:
        hn_ref[...] = y
    else:
        x2_ref[...] = x2
        hn_ref[...] = (y * (1.0 + sc_ref[0]) + sh_ref[0]).astype(hn_ref.dtype)


def _combine(dest, ys, x1, gt2, nw, sc, sh, last):
    tpb = SEQ // COMB_TG
    bidx = lambda i, d: (i // tpb, 0, 0)
    tile = pl.BlockSpec((COMB_TG, D), lambda i, d: (i, 0))
    if last:
        out_shape = jax.ShapeDtypeStruct((NTOK, D), F32)
        out_specs = tile
    else:
        out_shape = (jax.ShapeDtypeStruct((NTOK, D), F32), jax.ShapeDtypeStruct((NTOK, D), BF16))
        out_specs = (tile, tile)
    return pl.pallas_call(
        functools.partial(_combine_kernel, last=last),
        out_shape=out_shape,
        grid_spec=pltpu.PrefetchScalarGridSpec(
            num_scalar_prefetch=1,
            grid=(NTOK // COMB_TG,),
            in_specs=[pl.BlockSpec(memory_space=pl.ANY),
                      tile,
                      pl.BlockSpec((1, 1, D), bidx),
                      pl.BlockSpec((1, D), lambda i, d: (0, 0)),
                      pl.BlockSpec((1, 1, D), bidx),
                      pl.BlockSpec((1, 1, D), bidx)],
            out_specs=out_specs,
            scratch_shapes=[pltpu.VMEM((2, COMB_TG, PACKED), I32), pltpu.SemaphoreType.DMA((2,))]),
        compiler_params=_cparams("arbitrary", row_dma=True),
        name="combine",
    )(dest, ys, x1, gt2, nw.reshape(1, D), sc, sh)


def _pair_tables():
    lo, hi = [], []
    for g in range(N_GROUPS):
        for a in range(GROUP_SIZE):
            for b in range(a + 1, GROUP_SIZE):
                lo.append(GROUP_SIZE * g + a)
                hi.append(GROUP_SIZE * g + b)
    return np.asarray(lo, np.int32), np.asarray(hi, np.int32)


def _slot_plan(ri, cnt):
    counts = cnt[:N_BUCKETS, 0].astype(jnp.int32)
    padded = (counts + MOE_BLK - 1) // MOE_BLK * MOE_BLK
    pad_end = jnp.cumsum(padded)
    pad_start = pad_end - padded
    dest = pad_start[ri[0]] + ri[1]
    blk_start = jnp.arange(N_BLOCKS, dtype=jnp.int32) * MOE_BLK
    bb = jnp.minimum(jnp.sum(blk_start[:, None] >= pad_end[None, :], axis=1), N_BUCKETS - 1).astype(jnp.int32)
    nval = jnp.clip(pad_start[bb] + counts[bb] - blk_start, 0, MOE_BLK).astype(jnp.int32)
    lo_tab, hi_tab = _pair_tables()
    return dest.astype(jnp.int32), jnp.asarray(lo_tab)[bb], jnp.asarray(hi_tab)[bb], nval


def kernel(x, c, positions, w_ada, b_ada, norm1_w, w_in, conv_w, conv_b, conv_ln_w, conv_ln_b, w_conv_out,
           w_ret_out, w_out, norm2_w, w_router, router_bias, w_exp_gate, w_exp_up, w_exp_down, final_norm_w):
    mod = _ada(c, w_ada, b_ada).reshape(DEPTH, BATCH, N_MOD, 1, D)
    sh1, sc1, gt1, sh2, sc2, gt2 = (mod[:, :, i] for i in range(N_MOD))
    cos, sin = _rope_tables(positions)
    ret_tables = _ret_tables()

    wr_t = w_router.T
    rhi = wr_t.astype(BF16)
    rlo = (wr_t - rhi.astype(F32)).astype(BF16)
    rb = jnp.broadcast_to(router_bias.astype(F32)[:, None], (N_EXPERTS, MIX_TM))
    tri = (jnp.arange(MIX_TM)[:, None] < jnp.arange(MIX_TM)[None, :]).astype(BF16)

    xf = x.reshape(NTOK, D)
    h = _modnorm(x, norm1_w[0], sc1[0], sh1[0]).reshape(NTOK, D)
    out = None
    for l in range(DEPTH):
        p, u_tm, wg_bf, wu_bf, wd_bf = _inproj(h, w_in, l, cos, sin, (w_exp_gate, w_exp_up, w_exp_down))
        p3 = p.reshape(BATCH, SEQ, P_COLS)
        ya = _conv_branch(u_tm, p3, l, conv_w[l], conv_b[l], conv_ln_w[l], conv_ln_b[l], w_conv_out)
        rg = _retention(p3, ret_tables)
        x1, hx, ri, cnt = _mix(rg.reshape(NTOK, HEADS * DV), ya.reshape(NTOK, D), p, xf, gt1[l], sc2[l], sh2[l],
                               w_ret_out, w_out, l, norm2_w[l], rhi, rlo, rb, tri)
        dest, elo, ehi, nval = _slot_plan(ri, cnt)
        xs = _dispatch(dest, nval, hx)
        ys = _moe(elo, ehi, nval, xs, wg_bf, wu_bf, wd_bf)
        if l + 1 < DEPTH:
            xf, h = _combine(dest, ys, x1, gt2[l], norm1_w[l + 1], sc1[l + 1], sh1[l + 1], last=False)
        else:
            out = _combine(dest, ys, x1, gt2[l], final_norm_w, sc1[l], sh1[l], last=True)
    return out.reshape(BATCH, SEQ, D)
```

```python
import functools

import numpy as np
import jax
import jax.numpy as jnp
from jax import lax
from jax.experimental import pallas as pl
from jax.experimental.pallas import tpu as pltpu

F32 = jnp.float32
BF16 = jnp.bfloat16

D = 1024
BATCH = 8
SEQ = 2048
DEPTH = 4
NTOK = BATCH * SEQ
N_MOD = 6
EPS = 1e-6

CONV_K = 31
HEADS = 4
DK = 256
DV = 512
CHUNK = 128
ROPE_BASE = 10000.0
HALF = DK // 2

N_EXPERTS = 16
N_GROUPS = 4
GROUP_SIZE = 4
PAIRS = 6
N_BUCKETS = N_GROUPS * PAIRS
BUCKET_ROWS = 32

P_COLS = 9 * D
PCOL_U, PCOL_Q, PCOL_K, PCOL_V, PCOL_G, PCOL_GA, PCOL_GB = 0, 1, 2, 3, 5, 7, 8

MOE_BLK = 256
N_BLOCKS = NTOK // MOE_BLK + N_BUCKETS
SLOT_ROWS = N_BLOCKS * MOE_BLK
TAIL = 128
XROW = D + TAIL

VMEM_LIMIT = 56 * 1024 * 1024

NT_DIMS = (((1,), (1,)), ((), ()))


def _cparams(*sem, row_dma=False):
    return pltpu.CompilerParams(dimension_semantics=sem, vmem_limit_bytes=VMEM_LIMIT,
                                disable_bounds_checks=row_dma)


def _sigmoid(x):
    return 0.5 * jnp.tanh(0.5 * x) + 0.5


def _rms(x):
    return x * lax.rsqrt(jnp.mean(x * x, axis=-1, keepdims=True) + EPS)


ADA_TN = 1536


def _ada_kernel(c_ref, w_ref, b_ref, o_ref):
    c = c_ref[...]
    sc = c * _sigmoid(c)
    o_ref[0] = jnp.dot(sc, w_ref[0], precision=lax.Precision.HIGHEST,
                       preferred_element_type=F32) + b_ref[0]


def _ada(c, w_ada, b_ada):
    nj = N_MOD * D // ADA_TN
    return pl.pallas_call(
        _ada_kernel,
        out_shape=jax.ShapeDtypeStruct((DEPTH, BATCH, N_MOD * D), F32),
        grid=(DEPTH, nj),
        in_specs=[pl.BlockSpec((BATCH, D), lambda l, j: (0, 0)),
                  pl.BlockSpec((1, D, ADA_TN), lambda l, j: (l, 0, j)),
                  pl.BlockSpec((1, 1, ADA_TN), lambda l, j: (l, 0, j))],
        out_specs=pl.BlockSpec((1, BATCH, ADA_TN), lambda l, j: (l, 0, j)),
        compiler_params=_cparams("parallel", "parallel"),
        name="ada",
    )(c, w_ada, b_ada.reshape(DEPTH, 1, N_MOD * D))


ROPE_TN = 2048


def _rope_kernel(pos_ref, inv_ref, cos_ref, sin_ref):
    ang = pos_ref[...] * inv_ref[...]
    cos_ref[...] = jnp.cos(ang)
    sin_ref[...] = jnp.sin(ang)


def _rope_tables(positions):
    inv = ROPE_BASE ** (-jnp.arange(HALF, dtype=F32) / HALF)
    pos = positions.astype(F32).reshape(NTOK, 1)
    return pl.pallas_call(
        _rope_kernel,
        out_shape=(jax.ShapeDtypeStruct((NTOK, HALF), F32),) * 2,
        grid=(NTOK // ROPE_TN,),
        in_specs=[pl.BlockSpec((ROPE_TN, 1), lambda i: (i, 0)),
                  pl.BlockSpec((1, HALF), lambda i: (0, 0))],
        out_specs=(pl.BlockSpec((ROPE_TN, HALF), lambda i: (i, 0)),) * 2,
        compiler_params=_cparams("parallel"),
        name="rope_tables",
    )(pos, inv.reshape(1, HALF))


NORM_TS = 512


def _modnorm_kernel(x_ref, w_ref, sc_ref, sh_ref, o_ref):
    y = _rms(x_ref[0]) * w_ref[...]
    o_ref[0] = (y * (1.0 + sc_ref[0]) + sh_ref[0]).astype(o_ref.dtype)


def _modnorm(x, w, sc, sh):
    return pl.pallas_call(
        _modnorm_kernel,
        out_shape=jax.ShapeDtypeStruct((BATCH, SEQ, D), BF16),
        grid=(BATCH, SEQ // NORM_TS),
        in_specs=[pl.BlockSpec((1, NORM_TS, D), lambda b, i: (b, i, 0)),
                  pl.BlockSpec((1, D), lambda b, i: (0, 0)),
                  pl.BlockSpec((1, 1, D), lambda b, i: (b, 0, 0)),
                  pl.BlockSpec((1, 1, D), lambda b, i: (b, 0, 0))],
        out_specs=pl.BlockSpec((1, NORM_TS, D), lambda b, i: (b, i, 0)),
        compiler_params=_cparams("parallel", "parallel"),
        name="modnorm",
    )(x, w.reshape(1, D), sc, sh)


INP_TM = 1024
INP_GROUPS = 9
EXP_SLAB_ROWS = 512
EXP_SLABS_PER_MAT = D // EXP_SLAB_ROWS
EXP_SLABS = N_EXPERTS * EXP_SLABS_PER_MAT


def _inproj_kernel(h_ref, w1_ref, w2_ref, cos_ref, sin_ref, eg_ref, eu_ref, ed_ref,
                   o_ref, og_ref, ou_ref, od_ref, wb1_ref, wb2_ref):
    j = pl.program_id(0)
    m = pl.program_id(1)
    h = h_ref[...]

    step = j * pl.num_programs(1) + m
    for t, (src, dst) in enumerate(((eg_ref, og_ref), (eu_ref, ou_ref), (ed_ref, od_ref))):
        @pl.when((step >= EXP_SLABS * t) & (step < EXP_SLABS * (t + 1)))
        def _():
            dst[0] = src[0, 0].astype(BF16)

    @pl.when(m == 0)
    def _():
        wb1_ref[...] = w1_ref[0].astype(BF16)

    @pl.when((m == 0) & (j == 0))
    def _():
        wb2_ref[...] = w2_ref[0].astype(BF16)

    def proj():
        return jnp.dot(h, wb1_ref[...], preferred_element_type=F32)

    @pl.when(j == 0)
    def _():
        a = proj()
        b = jnp.dot(h, wb2_ref[...], preferred_element_type=F32)
        o_ref[...] = (a * _sigmoid(b)).astype(BF16)

    @pl.when((j == 1) | (j == 2))
    def _():
        t = proj()
        scale = jnp.where(j == 2, DK ** -0.5, 1.0).astype(F32)
        cos = cos_ref[...] * scale
        sin = sin_ref[...] * scale
        for hd in range(HEADS):
            c0 = hd * DK
            t1 = t[:, c0:c0 + HALF]
            t2 = t[:, c0 + HALF:c0 + DK]
            o_ref[:, c0:c0 + HALF] = (t1 * cos - t2 * sin).astype(BF16)
            o_ref[:, c0 + HALF:c0 + DK] = (t1 * sin + t2 * cos).astype(BF16)

    @pl.when((j == 3) | (j == 4))
    def _():
        o_ref[...] = proj().astype(BF16)

    @pl.when((j == 5) | (j == 6))
    def _():
        g = proj()
        o_ref[...] = (g * _sigmoid(g)).astype(BF16)

    @pl.when(j >= 7)
    def _():
        o_ref[...] = _sigmoid(proj()).astype(BF16)


def _inproj(h, w_in, l, cos, sin, w_exp):
    n_m = NTOK // INP_TM
    assert 3 * EXP_SLABS <= INP_GROUPS * n_m

    def rope_idx(j, m):
        return (jnp.where((j == 1) | (j == 2), m, 0), 0)

    def slab(t, j, m):
        s = jnp.clip(j * n_m + m - EXP_SLABS * t, 0, EXP_SLABS - 1)
        return s // EXP_SLABS_PER_MAT, s % EXP_SLABS_PER_MAT

    def exp_in(t):
        return pl.BlockSpec((1, 1, EXP_SLAB_ROWS, D), lambda j, m: (l, *slab(t, j, m), 0))

    def exp_out(t):
        return pl.BlockSpec((1, EXP_SLAB_ROWS, D), lambda j, m: (*slab(t, j, m), 0))

    exp_shape = jax.ShapeDtypeStruct((N_EXPERTS, D, D), BF16)
    return pl.pallas_call(
        _inproj_kernel,
        out_shape=(jax.ShapeDtypeStruct((NTOK, P_COLS), BF16), exp_shape, exp_shape, exp_shape),
        grid=(INP_GROUPS, n_m),
        in_specs=[pl.BlockSpec((INP_TM, D), lambda j, m: (m, 0)),
                  pl.BlockSpec((1, D, D), lambda j, m: (l, 0, jnp.where(j == 0, 0, j + 1))),
                  pl.BlockSpec((1, D, D), lambda j, m: (l, 0, 1), pipeline_mode=pl.Buffered(1)),
                  pl.BlockSpec((INP_TM, HALF), rope_idx),
                  pl.BlockSpec((INP_TM, HALF), rope_idx),
                  exp_in(0), exp_in(1), exp_in(2)],
        out_specs=(pl.BlockSpec((INP_TM, D), lambda j, m: (m, j)), exp_out(0), exp_out(1), exp_out(2)),
        scratch_shapes=[pltpu.VMEM((D, D), BF16), pltpu.VMEM((D, D), BF16)],
        compiler_params=_cparams("arbitrary", "arbitrary"),
        name="inproj",
    )(h, w_in, w_in, cos, sin, *w_exp)


CONV_TS = 512
CONV_HALO = 32
CONV_RC = 64
CONV_CW = 128
CONV_SH = CONV_TS + 24
SUBLANES = 8


def _conv_kernel(u_ref, halo_ref, sga_ref, cw_ref, cb_ref, lnw_ref, lnb_ref, wo_ref, o_ref,
                 buf_ref, sh_ref, acc_ref, wbf_ref):
    i = pl.program_id(1)

    @pl.when((pl.program_id(0) == 0) & (i == 0))
    def _():
        wbf_ref[...] = wo_ref[0].astype(BF16)

    halo = halo_ref[0].astype(F32)
    buf_ref[0:CONV_HALO, :] = jnp.where(i > 0, halo, 0.0)
    buf_ref[CONV_HALO:CONV_HALO + CONV_TS, :] = u_ref[0].astype(F32)
    for r in range(1, SUBLANES):
        sh_ref[r - 1] = buf_ref[r:r + CONV_SH, :]

    groups = CONV_RC // SUBLANES

    def body(ci, carry):
        r0 = pl.multiple_of(ci * CONV_RC, CONV_RC)
        for cc in range(D // CONV_CW):
            cols = slice(cc * CONV_CW, (cc + 1) * CONV_CW)
            accs = [cb_ref[:, cols]] * groups
            for off in range(2, CONV_K + 2):
                q, r = divmod(off, SUBLANES)
                w8 = cw_ref[off - 2, :, cols]
                for g in range(groups):
                    rows = pl.ds(r0 + SUBLANES * (q + g), SUBLANES)
                    win = buf_ref[rows, cols] if r == 0 else sh_ref[r - 1, rows, cols]
                    accs[g] = accs[g] + win * w8
            for g in range(groups):
                acc_ref[pl.ds(r0 + SUBLANES * g, SUBLANES), cols] = accs[g]
        return carry

    lax.fori_loop(0, CONV_TS // CONV_RC, body, 0)

    c = acc_ref[...]
    mu = jnp.mean(c, axis=-1, keepdims=True)
    cen = c - mu
    var = jnp.mean(cen * cen, axis=-1, keepdims=True)
    y = cen * lax.rsqrt(var + EPS) * lnw_ref[...] + lnb_ref[...]
    y = y * _sigmoid(y)
    out = jnp.dot(y.astype(BF16), wbf_ref[...], preferred_element_type=F32)
    o_ref[0] = (out * sga_ref[0].astype(F32)).astype(BF16)


def _conv_branch(p3, l, conv_w, conv_b, ln_w, ln_b, w_conv_out):
    hb = CONV_TS // CONV_HALO
    cw8 = jnp.broadcast_to(conv_w[:, None, :], (CONV_K, SUBLANES, D))
    cb8 = jnp.broadcast_to(conv_b[None, :], (SUBLANES, D))
    return pl.pallas_call(
        _conv_kernel,
        out_shape=jax.ShapeDtypeStruct((BATCH, SEQ, D), BF16),
        grid=(BATCH, SEQ // CONV_TS),
        in_specs=[pl.BlockSpec((1, CONV_TS, D), lambda b, i: (b, i, PCOL_U)),
                  pl.BlockSpec((1, CONV_HALO, D), lambda b, i: (b, jnp.maximum(i * hb - 1, 0), PCOL_U)),
                  pl.BlockSpec((1, CONV_TS, D), lambda b, i: (b, i, PCOL_GA)),
                  pl.BlockSpec((CONV_K, SUBLANES, D), lambda b, i: (0, 0, 0)),
                  pl.BlockSpec((SUBLANES, D), lambda b, i: (0, 0)),
                  pl.BlockSpec((1, D), lambda b, i: (0, 0)),
                  pl.BlockSpec((1, D), lambda b, i: (0, 0)),
                  pl.BlockSpec((1, D, D), lambda b, i: (l, 0, 0), pipeline_mode=pl.Buffered(1))],
        out_specs=pl.BlockSpec((1, CONV_TS, D), lambda b, i: (b, i, 0)),
        scratch_shapes=[pltpu.VMEM((CONV_HALO + CONV_TS, D), F32),
                        pltpu.VMEM((SUBLANES - 1, CONV_SH, D), F32),
                        pltpu.VMEM((CONV_TS, D), F32),
                        pltpu.VMEM((D, D), BF16)],
        compiler_params=_cparams("arbitrary", "arbitrary"),
        name="conv_branch",
    )(p3, p3, p3, cw8, cb8, ln_w.reshape(1, D), ln_b.reshape(1, D), w_conv_out)


def _ret_tables():
    hh = jnp.arange(HEADS, dtype=F32)
    log_g = jnp.log1p(-jnp.exp2(-5.0 - hh))
    idx = jnp.arange(CHUNK, dtype=F32)
    rel = idx[:, None] - idx[None, :]
    dmask = jnp.where(rel[None] >= 0, jnp.exp(jnp.maximum(rel, 0.0)[None] * log_g[:, None, None]), 0.0)
    xi = jnp.exp((idx + 1.0)[None, :] * log_g[:, None])[..., None]
    zeta = jnp.exp((CHUNK - 1.0 - idx)[None, :] * log_g[:, None])[..., None]
    g_chunk = jnp.exp(CHUNK * log_g)[:, None, None]
    return (dmask,
            jnp.broadcast_to(xi, (HEADS, CHUNK, DK)),
            jnp.broadcast_to(zeta, (HEADS, CHUNK, DK)),
            jnp.broadcast_to(g_chunk, (HEADS, 1, DV)))


RET_TS = 512


def _ret_kernel(q_ref, k_ref, v01_ref, v23_ref, g01_ref, g23_ref, dm_ref, xi_ref, zt_ref, gc_ref, o_ref, st_ref):
    @pl.when(pl.program_id(1) == 0)
    def _():
        st_ref[...] = jnp.zeros_like(st_ref)

    def body(c, carry):
        r0 = pl.multiple_of(c * CHUNK, CHUNK)
        rows = pl.ds(r0, CHUNK)
        for hd in range(HEADS):
            v_ref, g_ref = (v01_ref, g01_ref) if hd < 2 else (v23_ref, g23_ref)
            vcols = slice((hd % 2) * DV, (hd % 2 + 1) * DV)
            qc = q_ref[0, rows, hd * DK:(hd + 1) * DK]
            kc = k_ref[0, rows, hd * DK:(hd + 1) * DK]
            vc = v_ref[0, rows, vcols]
            scores = lax.dot_general(qc, kc, NT_DIMS, preferred_element_type=F32) * dm_ref[hd]
            inner = jnp.dot(scores.astype(BF16), vc, preferred_element_type=F32)
            st = st_ref[hd]
            qx = (qc.astype(F32) * xi_ref[hd]).astype(BF16)
            cross = jnp.dot(qx, st.astype(BF16), preferred_element_type=F32)
            kzt = (kc.astype(F32) * zt_ref[hd]).T.astype(BF16)
            st_ref[hd] = st * gc_ref[hd] + jnp.dot(kzt, vc, preferred_element_type=F32)
            o = inner + cross
            mu = jnp.mean(o, axis=-1, keepdims=True)
            cen = o - mu
            var = jnp.mean(cen * cen, axis=-1, keepdims=True)
            r = cen * lax.rsqrt(var + EPS)
            o_ref[0, rows, hd * DV:(hd + 1) * DV] = (r * g_ref[0, rows, vcols].astype(F32)).astype(BF16)
        return carry

    lax.fori_loop(0, RET_TS // CHUNK, body, 0)


def _retention(p3, tables):
    dmask, xi, zeta, gch = tables
    tile = lambda col: pl.BlockSpec((1, RET_TS, D), lambda b, i: (b, i, col))
    full = lambda a: pl.BlockSpec(a.shape, lambda b, i: (0, 0, 0))
    return pl.pallas_call(
        _ret_kernel,
        out_shape=jax.ShapeDtypeStruct((BATCH, SEQ, HEADS * DV), BF16),
        grid=(BATCH, SEQ // RET_TS),
        in_specs=[tile(PCOL_Q), tile(PCOL_K), tile(PCOL_V), tile(PCOL_V + 1), tile(PCOL_G), tile(PCOL_G + 1),
                  full(dmask), full(xi), full(zeta), full(gch)],
        out_specs=pl.BlockSpec((1, RET_TS, HEADS * DV), lambda b, i: (b, i, 0)),
        scratch_shapes=[pltpu.VMEM((HEADS, DK, DV), F32)],
        compiler_params=_cparams("parallel", "arbitrary"),
        name="retention",
    )(p3, p3, p3, p3, p3, p3, dmask, xi, zeta, gch)


MIX_TM = 512


def _route_rows(s, sb):
    row = lambda a, e: a[e:e + 1, :]
    best = None
    gidx = None
    for g in range(N_GROUPS):
        v = [row(sb, GROUP_SIZE * g + i) for i in range(GROUP_SIZE)]
        pair_sums = [v[a] + v[b] for a in range(GROUP_SIZE) for b in range(a + 1, GROUP_SIZE)]
        gs = functools.reduce(jnp.maximum, pair_sums)
        if g == 0:
            best, gidx = gs, jnp.zeros(gs.shape, jnp.int32)
        else:
            upd = gs > best
            gidx = jnp.where(upd, g, gidx)
            best = jnp.where(upd, gs, best)

    def pick(a, i):
        out = row(a, i)
        for g in range(1, N_GROUPS):
            out = jnp.where(gidx == g, row(a, GROUP_SIZE * g + i), out)
        return out

    vb = [pick(sb, i) for i in range(GROUP_SIZE)]
    vs = [pick(s, i) for i in range(GROUP_SIZE)]
    m1, i1, s1 = vb[0], jnp.zeros(gidx.shape, jnp.int32), vs[0]
    for i in range(1, GROUP_SIZE):
        upd = vb[i] > m1
        m1 = jnp.where(upd, vb[i], m1)
        i1 = jnp.where(upd, i, i1)
        s1 = jnp.where(upd, vs[i], s1)
    m2 = i2 = s2 = None
    for i in range(GROUP_SIZE):
        cand = jnp.where(i1 == i, -jnp.inf, vb[i])
        if m2 is None:
            m2, i2, s2 = cand, jnp.zeros(gidx.shape, jnp.int32), vs[0]
        else:
            upd = cand > m2
            m2 = jnp.where(upd, cand, m2)
            i2 = jnp.where(upd, i, i2)
            s2 = jnp.where(upd, vs[i], s2)
    den = s1 + s2
    w1 = s1 / den
    w2 = s2 / den
    first_low = i1 < i2
    lo = jnp.minimum(i1, i2)
    hi = jnp.maximum(i1, i2)
    pair = jnp.where(lo == 0, hi - 1, jnp.where(lo == 1, hi + 1, 5))
    bucket = gidx * PAIRS + pair
    return bucket, jnp.where(first_low, w1, w2), jnp.where(first_low, w2, w1)


def _mix_kernel(rg_ref, ya_ref, sgb_ref, x_ref, gt_ref, sc_ref, sh_ref, wr_ref, wo_ref, n2_ref,
                rhi_ref, rlo_ref, rb_ref, tri_ref,
                x1_ref, hx_ref, ri_ref, cnt_ref, carry_ref, wrb_ref, wob_ref):
    m = pl.program_id(0)

    @pl.when(m == 0)
    def _():
        carry_ref[...] = jnp.zeros_like(carry_ref)
        wrb_ref[...] = wr_ref[0].astype(BF16)
        wob_ref[...] = wo_ref[0].astype(BF16)

    yb = jnp.dot(rg_ref[...], wrb_ref[...], preferred_element_type=F32)
    y = ya_ref[...].astype(F32) + sgb_ref[...].astype(F32) * yb
    o = jnp.dot(y.astype(BF16), wob_ref[...], preferred_element_type=F32)
    x1 = x_ref[...] + gt_ref[0] * o
    x1_ref[...] = x1
    h2 = _rms(x1) * n2_ref[...] * (1.0 + sc_ref[0]) + sh_ref[0]
    hx_ref[:, 0:D] = h2

    hi = h2.astype(BF16)
    lo = (h2 - hi.astype(F32)).astype(BF16)
    rhi = rhi_ref[...]
    logits = (lax.dot_general(rhi, hi, NT_DIMS, preferred_element_type=F32)
              + lax.dot_general(rhi, lo, NT_DIMS, preferred_element_type=F32)
              + lax.dot_general(rlo_ref[...], hi, NT_DIMS, preferred_element_type=F32))
    s = _sigmoid(logits)
    bucket, w_lo, w_hi = _route_rows(s, s + rb_ref[...])

    onehot = (lax.broadcasted_iota(jnp.int32, (BUCKET_ROWS, MIX_TM), 0) == bucket).astype(F32)
    prefix = jnp.dot(onehot.astype(BF16), tri_ref[...], preferred_element_type=F32)
    carry = carry_ref[:, 0:1]
    rank = jnp.sum(onehot * (prefix + carry), axis=0, keepdims=True)
    new_carry = carry + jnp.sum(onehot, axis=1, keepdims=True)
    carry_ref[...] = jnp.broadcast_to(new_carry, carry_ref.shape)
    cnt_ref[...] = jnp.broadcast_to(new_carry, cnt_ref.shape)

    rid = lax.broadcasted_iota(jnp.int32, (8, MIX_TM), 0)
    ri_ref[...] = jnp.where(rid == 0, bucket, jnp.where(rid == 1, rank.astype(jnp.int32), 0))
    wid = lax.broadcasted_iota(jnp.int32, (TAIL, MIX_TM), 0)
    wrows = jnp.where(wid == 0, w_lo, jnp.where(wid == 1, w_hi, 0.0))
    hx_ref[:, D:XROW] = wrows.T


def _mix(rg, ya, p, x, gt1, sc2, sh2, w_ret_out, w_out, l, norm2_w, rhi, rlo, rb, tri):
    tpb = SEQ // MIX_TM
    bidx = lambda m: (m // tpb, 0, 0)
    return pl.pallas_call(
        _mix_kernel,
        out_shape=(jax.ShapeDtypeStruct((NTOK, D), F32),
                   jax.ShapeDtypeStruct((NTOK, XROW), F32),
                   jax.ShapeDtypeStruct((8, NTOK), jnp.int32),
                   jax.ShapeDtypeStruct((BUCKET_ROWS, 128), F32)),
        grid=(NTOK // MIX_TM,),
        in_specs=[pl.BlockSpec((MIX_TM, HEADS * DV), lambda m: (m, 0)),
                  pl.BlockSpec((MIX_TM, D), lambda m: (m, 0)),
                  pl.BlockSpec((MIX_TM, D), lambda m: (m, PCOL_GB)),
                  pl.BlockSpec((MIX_TM, D), lambda m: (m, 0)),
                  pl.BlockSpec((1, 1, D), bidx),
                  pl.BlockSpec((1, 1, D), bidx),
                  pl.BlockSpec((1, 1, D), bidx),
                  pl.BlockSpec((1, HEADS * DV, D), lambda m: (l, 0, 0), pipeline_mode=pl.Buffered(1)),
                  pl.BlockSpec((1, D, D), lambda m: (l, 0, 0), pipeline_mode=pl.Buffered(1)),
                  pl.BlockSpec((1, D), lambda m: (0, 0)),
                  pl.BlockSpec((N_EXPERTS, D), lambda m: (0, 0)),
                  pl.BlockSpec((N_EXPERTS, D), lambda m: (0, 0)),
                  pl.BlockSpec((N_EXPERTS, MIX_TM), lambda m: (0, 0)),
                  pl.BlockSpec((MIX_TM, MIX_TM), lambda m: (0, 0))],
        out_specs=(pl.BlockSpec((MIX_TM, D), lambda m: (m, 0)),
                   pl.BlockSpec((MIX_TM, XROW), lambda m: (m, 0)),
                   pl.BlockSpec((8, MIX_TM), lambda m: (0, m)),
                   pl.BlockSpec((BUCKET_ROWS, 128), lambda m: (0, 0))),
        scratch_shapes=[pltpu.VMEM((BUCKET_ROWS, 128), F32),
                        pltpu.VMEM((HEADS * DV, D), BF16), pltpu.VMEM((D, D), BF16)],
        compiler_params=_cparams("arbitrary"),
        name="mix_route",
    )(rg, ya, p, x, gt1, sc2, sh2, w_ret_out, w_out, norm2_w.reshape(1, D), rhi, rlo, rb, tri)


DISP_TG = 512


def _dispatch_kernel(dest_ref, nval_ref, src_ref, xs_ref, zero_ref, sem, zsem):
    step = pl.program_id(0)
    base = step * DISP_TG

    @pl.when(step == 0)
    def _():
        zero_ref[...] = jnp.zeros_like(zero_ref)

        def zero_copy(b):
            r0 = pl.multiple_of(b * MOE_BLK, MOE_BLK)
            return pltpu.make_async_copy(zero_ref, xs_ref.at[pl.ds(r0, MOE_BLK), :], zsem)

        def zissue(b, carry):
            @pl.when(nval_ref[b] < MOE_BLK)
            def _():
                zero_copy(b).start()
            return carry

        lax.fori_loop(0, N_BLOCKS, zissue, 0)

        def zdrain(b, carry):
            @pl.when(nval_ref[b] < MOE_BLK)
            def _():
                zero_copy(b).wait()
            return carry

        lax.fori_loop(0, N_BLOCKS, zdrain, 0)

    def row_copy(t, d):
        return pltpu.make_async_copy(src_ref.at[pl.ds(t, 1), :], xs_ref.at[pl.ds(d, 1), :], sem)

    for t in range(DISP_TG):
        row_copy(t, dest_ref[base + t]).start()
    pltpu.make_async_copy(src_ref, xs_ref.at[pl.ds(0, DISP_TG), :], sem).wait()


def _dispatch(dest, nval, hx):
    return pl.pallas_call(
        _dispatch_kernel,
        out_shape=jax.ShapeDtypeStruct((SLOT_ROWS, XROW), F32),
        grid_spec=pltpu.PrefetchScalarGridSpec(
            num_scalar_prefetch=2,
            grid=(NTOK // DISP_TG,),
            in_specs=[pl.BlockSpec((DISP_TG, XROW), lambda i, d, nv: (i, 0))],
            out_specs=pl.BlockSpec(memory_space=pl.ANY),
            scratch_shapes=[pltpu.VMEM((MOE_BLK, XROW), F32),
                            pltpu.SemaphoreType.DMA(()), pltpu.SemaphoreType.DMA(())]),
        compiler_params=_cparams("arbitrary", row_dma=True),
        name="dispatch",
    )(dest, nval, hx)


def _moe_kernel(elo_ref, ehi_ref, nval_ref, xs_ref, wg1, wu1, wd1, wg2, wu2, wd2, o_ref):
    i = pl.program_id(0)
    nval = nval_ref[i]

    def run(rows):
        x = xs_ref[0:rows, 0:D].astype(BF16)
        w_lo = xs_ref[0:rows, D:D + 1]
        w_hi = xs_ref[0:rows, D + 1:D + 2]

        def expert(wg, wu, wd):
            g = jnp.dot(x, wg[0], preferred_element_type=F32)
            u = jnp.dot(x, wu[0], preferred_element_type=F32)
            a = (g * _sigmoid(g) * u).astype(BF16)
            return jnp.dot(a, wd[0], preferred_element_type=F32)

        o_ref[0:rows, :] = w_lo * expert(wg1, wu1, wd1) + w_hi * expert(wg2, wu2, wd2)

    @pl.when(nval > MOE_BLK // 2)
    def _():
        run(MOE_BLK)

    @pl.when((nval > 0) & (nval <= MOE_BLK // 2))
    def _():
        run(MOE_BLK // 2)
        o_ref[MOE_BLK // 2:MOE_BLK, :] = jnp.zeros((MOE_BLK // 2, D), F32)

    @pl.when(nval == 0)
    def _():
        o_ref[...] = jnp.zeros_like(o_ref)


def _moe(elo, ehi, nval, xs, wg, wu, wd):
    wspec = lambda tab: pl.BlockSpec((1, D, D), lambda i, elo, ehi, nv: ((elo, ehi)[tab][i], 0, 0))
    return pl.pallas_call(
        _moe_kernel,
        out_shape=jax.ShapeDtypeStruct((SLOT_ROWS, D), F32),
        grid_spec=pltpu.PrefetchScalarGridSpec(
            num_scalar_prefetch=3,
            grid=(N_BLOCKS,),
            in_specs=[pl.BlockSpec((MOE_BLK, XROW), lambda i, elo, ehi, nv: (i, 0)),
                      wspec(0), wspec(0), wspec(0), wspec(1), wspec(1), wspec(1)],
            out_specs=pl.BlockSpec((MOE_BLK, D), lambda i, elo, ehi, nv: (i, 0))),
        compiler_params=_cparams("arbitrary"),
        name="experts",
    )(elo, ehi, nval, xs, wg, wu, wd, wg, wu, wd)


COMB_TG = 512


def _combine_kernel(dest_ref, ys_ref, x1_ref, gt_ref, nw_ref, sc_ref, sh_ref, *rest, last):
    if last:
        hn_ref, ybuf, sem = rest
    else:
        x2_ref, hn_ref, ybuf, sem = rest
    step = pl.program_id(0)
    slot = step % 2

    def gather_tile(tile, into):
        base = tile * COMB_TG

        for t in range(COMB_TG):
            src = ys_ref.at[pl.ds(dest_ref[base + t], 1), :]
            pltpu.make_async_copy(src, ybuf.at[into, pl.ds(t, 1), :], sem.at[into]).start()

    @pl.when(step == 0)
    def _():
        gather_tile(0, 0)

    @pl.when(step + 1 < pl.num_programs(0))
    def _():
        gather_tile(step + 1, 1 - slot)

    pltpu.make_async_copy(ys_ref.at[pl.ds(0, COMB_TG), :], ybuf.at[slot], sem.at[slot]).wait()

    x2 = x1_ref[...] + gt_ref[0] * ybuf[slot]
    y = _rms(x2) * nw_ref[...]
    if last:
        hn_ref[...] = y
    else:
        x2_ref[...] = x2
        hn_ref[...] = (y * (1.0 + sc_ref[0]) + sh_ref[0]).astype(hn_ref.dtype)


def _combine(dest, ys, x1, gt2, nw, sc, sh, last):
    tpb = SEQ // COMB_TG
    bidx = lambda i, d: (i // tpb, 0, 0)
    tile = pl.BlockSpec((COMB_TG, D), lambda i, d: (i, 0))
    if last:
        out_shape = jax.ShapeDtypeStruct((NTOK, D), F32)
        out_specs = tile
    else:
        out_shape = (jax.ShapeDtypeStruct((NTOK, D), F32), jax.ShapeDtypeStruct((NTOK, D), BF16))
        out_specs = (tile, tile)
    return pl.pallas_call(
        functools.partial(_combine_kernel, last=last),
        out_shape=out_shape,
        grid_spec=pltpu.PrefetchScalarGridSpec(
            num_scalar_prefetch=1,
            grid=(NTOK // COMB_TG,),
            in_specs=[pl.BlockSpec(memory_space=pl.ANY),
                      tile,
                      pl.BlockSpec((1, 1, D), bidx),
                      pl.BlockSpec((1, D), lambda i, d: (0, 0)),
                      pl.BlockSpec((1, 1, D), bidx),
                      pl.BlockSpec((1, 1, D), bidx)],
            out_specs=out_specs,
            scratch_shapes=[pltpu.VMEM((2, COMB_TG, D), F32), pltpu.SemaphoreType.DMA((2,))]),
        compiler_params=_cparams("arbitrary", row_dma=True),
        name="combine",
    )(dest, ys, x1, gt2, nw.reshape(1, D), sc, sh)


def _pair_tables():
    lo, hi = [], []
    for g in range(N_GROUPS):
        for a in range(GROUP_SIZE):
            for b in range(a + 1, GROUP_SIZE):
                lo.append(GROUP_SIZE * g + a)
                hi.append(GROUP_SIZE * g + b)
    return np.asarray(lo, np.int32), np.asarray(hi, np.int32)


def _slot_plan(ri, cnt):
    counts = cnt[:N_BUCKETS, 0].astype(jnp.int32)
    padded = (counts + MOE_BLK - 1) // MOE_BLK * MOE_BLK
    pad_end = jnp.cumsum(padded)
    pad_start = pad_end - padded
    dest = pad_start[ri[0]] + ri[1]
    blk_start = jnp.arange(N_BLOCKS, dtype=jnp.int32) * MOE_BLK
    bb = jnp.minimum(jnp.sum(blk_start[:, None] >= pad_end[None, :], axis=1), N_BUCKETS - 1).astype(jnp.int32)
    nval = jnp.clip(pad_start[bb] + counts[bb] - blk_start, 0, MOE_BLK).astype(jnp.int32)
    lo_tab, hi_tab = _pair_tables()
    return dest.astype(jnp.int32), jnp.asarray(lo_tab)[bb], jnp.asarray(hi_tab)[bb], nval


def kernel(x, c, positions, w_ada, b_ada, norm1_w, w_in, conv_w, conv_b, conv_ln_w, conv_ln_b, w_conv_out,
           w_ret_out, w_out, norm2_w, w_router, router_bias, w_exp_gate, w_exp_up, w_exp_down, final_norm_w):
    mod = _ada(c, w_ada, b_ada).reshape(DEPTH, BATCH, N_MOD, 1, D)
    sh1, sc1, gt1, sh2, sc2, gt2 = (mod[:, :, i] for i in range(N_MOD))
    cos, sin = _rope_tables(positions)
    ret_tables = _ret_tables()

    wr_t = w_router.T
    rhi = wr_t.astype(BF16)
    rlo = (wr_t - rhi.astype(F32)).astype(BF16)
    rb = jnp.broadcast_to(router_bias.astype(F32)[:, None], (N_EXPERTS, MIX_TM))
    tri = (jnp.arange(MIX_TM)[:, None] < jnp.arange(MIX_TM)[None, :]).astype(BF16)

    xf = x.reshape(NTOK, D)
    h = _modnorm(x, norm1_w[0], sc1[0], sh1[0]).reshape(NTOK, D)
    out = None
    for l in range(DEPTH):
        p, wg_bf, wu_bf, wd_bf = _inproj(h, w_in, l, cos, sin, (w_exp_gate, w_exp_up, w_exp_down))
        p3 = p.reshape(BATCH, SEQ, P_COLS)
        ya = _conv_branch(p3, l, conv_w[l], conv_b[l], conv_ln_w[l], conv_ln_b[l], w_conv_out)
        rg = _retention(p3, ret_tables)
        x1, hx, ri, cnt = _mix(rg.reshape(NTOK, HEADS * DV), ya.reshape(NTOK, D), p, xf, gt1[l], sc2[l], sh2[l],
                               w_ret_out, w_out, l, norm2_w[l], rhi, rlo, rb, tri)
        dest, elo, ehi, nval = _slot_plan(ri, cnt)
        xs = _dispatch(dest, nval, hx)
        ys = _moe(elo, ehi, nval, xs, wg_bf, wu_bf, wd_bf)
        if l + 1 < DEPTH:
            xf, h = _combine(dest, ys, x1, gt2[l], norm1_w[l + 1], sc1[l + 1], sh1[l + 1], last=False)
        else:
            out = _combine(dest, ys, x1, gt2[l], final_norm_w, sc1[l], sh1[l], last=True)
    return out.reshape(BATCH, SEQ, D)
```

```python
import functools

import numpy as np
import jax
import jax.numpy as jnp
from jax import lax
from jax.experimental import pallas as pl
from jax.experimental.pallas import tpu as pltpu

F32 = jnp.float32
BF16 = jnp.bfloat16

D = 1024
BATCH = 8
SEQ = 2048
DEPTH = 4
NTOK = BATCH * SEQ
N_MOD = 6
EPS = 1e-6

CONV_K = 31
HEADS = 4
DK = 256
DV = 512
CHUNK = 256
ROPE_BASE = 10000.0
HALF = DK // 2

N_EXPERTS = 16
N_GROUPS = 4
GROUP_SIZE = 4
PAIR_ORDER = ((0, 1), (0, 2), (0, 3), (1, 3), (1, 2), (3, 2))
PAIRS = len(PAIR_ORDER)
N_BUCKETS = N_GROUPS * PAIRS
BUCKET_ROWS = 32

P_COLS = 9 * D
PCOL_U, PCOL_Q, PCOL_K, PCOL_V, PCOL_G, PCOL_GA, PCOL_GB = 0, 1, 2, 3, 5, 7, 8

MOE_BLK = 256
N_BLOCKS = NTOK // MOE_BLK + N_BUCKETS
SLOT_ROWS = N_BLOCKS * MOE_BLK
TAIL = 128
XROW = D + TAIL

VMEM_LIMIT = 56 * 1024 * 1024

NT_DIMS = (((1,), (1,)), ((), ()))


def _cparams(*sem, row_dma=False):
    return pltpu.CompilerParams(dimension_semantics=sem, vmem_limit_bytes=VMEM_LIMIT,
                                disable_bounds_checks=row_dma)


def _sigmoid(x):
    return 0.5 * jnp.tanh(0.5 * x) + 0.5


def _rms(x):
    return x * lax.rsqrt(jnp.mean(x * x, axis=-1, keepdims=True) + EPS)


ADA_TN = 1536


def _ada_kernel(c_ref, w_ref, b_ref, o_ref):
    c = c_ref[...]
    sc = c * _sigmoid(c)
    o_ref[0] = jnp.dot(sc, w_ref[0], precision=lax.Precision.HIGHEST,
                       preferred_element_type=F32) + b_ref[0]


def _ada(c, w_ada, b_ada):
    nj = N_MOD * D // ADA_TN
    return pl.pallas_call(
        _ada_kernel,
        out_shape=jax.ShapeDtypeStruct((DEPTH, BATCH, N_MOD * D), F32),
        grid=(DEPTH, nj),
        in_specs=[pl.BlockSpec((BATCH, D), lambda l, j: (0, 0)),
                  pl.BlockSpec((1, D, ADA_TN), lambda l, j: (l, 0, j)),
                  pl.BlockSpec((1, 1, ADA_TN), lambda l, j: (l, 0, j))],
        out_specs=pl.BlockSpec((1, BATCH, ADA_TN), lambda l, j: (l, 0, j)),
        compiler_params=_cparams("parallel", "parallel"),
        name="ada",
    )(c, w_ada, b_ada.reshape(DEPTH, 1, N_MOD * D))


ROPE_TN = 2048


def _rope_kernel(pos_ref, inv_ref, cos_ref, sin_ref):
    ang = pos_ref[...] * inv_ref[...]
    cos_ref[...] = jnp.cos(ang)
    sin_ref[...] = jnp.sin(ang)


def _rope_tables(positions):
    inv = ROPE_BASE ** (-jnp.arange(HALF, dtype=F32) / HALF)
    pos = positions.astype(F32).reshape(NTOK, 1)
    return pl.pallas_call(
        _rope_kernel,
        out_shape=(jax.ShapeDtypeStruct((NTOK, HALF), F32),) * 2,
        grid=(NTOK // ROPE_TN,),
        in_specs=[pl.BlockSpec((ROPE_TN, 1), lambda i: (i, 0)),
                  pl.BlockSpec((1, HALF), lambda i: (0, 0))],
        out_specs=(pl.BlockSpec((ROPE_TN, HALF), lambda i: (i, 0)),) * 2,
        compiler_params=_cparams("parallel"),
        name="rope_tables",
    )(pos, inv.reshape(1, HALF))


NORM_TS = 512


def _modnorm_kernel(x_ref, w_ref, sc_ref, sh_ref, o_ref):
    y = _rms(x_ref[0]) * w_ref[...]
    o_ref[0] = (y * (1.0 + sc_ref[0]) + sh_ref[0]).astype(o_ref.dtype)


def _modnorm(x, w, sc, sh):
    return pl.pallas_call(
        _modnorm_kernel,
        out_shape=jax.ShapeDtypeStruct((BATCH, SEQ, D), BF16),
        grid=(BATCH, SEQ // NORM_TS),
        in_specs=[pl.BlockSpec((1, NORM_TS, D), lambda b, i: (b, i, 0)),
                  pl.BlockSpec((1, D), lambda b, i: (0, 0)),
                  pl.BlockSpec((1, 1, D), lambda b, i: (b, 0, 0)),
                  pl.BlockSpec((1, 1, D), lambda b, i: (b, 0, 0))],
        out_specs=pl.BlockSpec((1, NORM_TS, D), lambda b, i: (b, i, 0)),
        compiler_params=_cparams("parallel", "parallel"),
        name="modnorm",
    )(x, w.reshape(1, D), sc, sh)


INP_TM = 1024
INP_GROUPS = 9
EXP_SLAB_ROWS = 512
EXP_SLABS_PER_MAT = D // EXP_SLAB_ROWS
EXP_SLABS = N_EXPERTS * EXP_SLABS_PER_MAT


def _inproj_kernel(h_ref, w1_ref, w2_ref, cos_ref, sin_ref, eg_ref, eu_ref, ed_ref,
                   o_ref, og_ref, ou_ref, od_ref, wb1_ref, wb2_ref):
    j = pl.program_id(0)
    m = pl.program_id(1)
    h = h_ref[...]

    step = j * pl.num_programs(1) + m
    for t, (src, dst) in enumerate(((eg_ref, og_ref), (eu_ref, ou_ref), (ed_ref, od_ref))):
        @pl.when((step >= EXP_SLABS * t) & (step < EXP_SLABS * (t + 1)))
        def _():
            dst[0] = src[0, 0].astype(BF16)

    @pl.when(m == 0)
    def _():
        wb1_ref[...] = w1_ref[0].astype(BF16)

    @pl.when((m == 0) & (j == 0))
    def _():
        wb2_ref[...] = w2_ref[0].astype(BF16)

    def proj():
        return jnp.dot(h, wb1_ref[...], preferred_element_type=F32)

    @pl.when(j == 0)
    def _():
        a = proj()
        b = jnp.dot(h, wb2_ref[...], preferred_element_type=F32)
        o_ref[...] = (a * _sigmoid(b)).astype(BF16)

    @pl.when((j == 1) | (j == 2))
    def _():
        t = proj()
        scale = jnp.where(j == 2, DK ** -0.5, 1.0).astype(F32)
        cos = cos_ref[...] * scale
        sin = sin_ref[...] * scale
        for hd in range(HEADS):
            c0 = hd * DK
            t1 = t[:, c0:c0 + HALF]
            t2 = t[:, c0 + HALF:c0 + DK]
            o_ref[:, c0:c0 + HALF] = (t1 * cos - t2 * sin).astype(BF16)
            o_ref[:, c0 + HALF:c0 + DK] = (t1 * sin + t2 * cos).astype(BF16)

    @pl.when((j == 3) | (j == 4))
    def _():
        o_ref[...] = proj().astype(BF16)

    @pl.when((j == 5) | (j == 6))
    def _():
        g = proj()
        o_ref[...] = (g * _sigmoid(g)).astype(BF16)

    @pl.when(j >= 7)
    def _():
        o_ref[...] = _sigmoid(proj()).astype(BF16)


def _inproj(h, w_in, l, cos, sin, w_exp):
    n_m = NTOK // INP_TM
    assert 3 * EXP_SLABS <= INP_GROUPS * n_m

    def rope_idx(j, m):
        return (jnp.where((j == 1) | (j == 2), m, 0), 0)

    def slab(t, j, m):
        s = jnp.clip(j * n_m + m - EXP_SLABS * t, 0, EXP_SLABS - 1)
        return s // EXP_SLABS_PER_MAT, s % EXP_SLABS_PER_MAT

    def exp_in(t):
        return pl.BlockSpec((1, 1, EXP_SLAB_ROWS, D), lambda j, m: (l, *slab(t, j, m), 0))

    def exp_out(t):
        return pl.BlockSpec((1, EXP_SLAB_ROWS, D), lambda j, m: (*slab(t, j, m), 0))

    exp_shape = jax.ShapeDtypeStruct((N_EXPERTS, D, D), BF16)
    return pl.pallas_call(
        _inproj_kernel,
        out_shape=(jax.ShapeDtypeStruct((NTOK, P_COLS), BF16), exp_shape, exp_shape, exp_shape),
        grid=(INP_GROUPS, n_m),
        in_specs=[pl.BlockSpec((INP_TM, D), lambda j, m: (m, 0)),
                  pl.BlockSpec((1, D, D), lambda j, m: (l, 0, jnp.where(j == 0, 0, j + 1))),
                  pl.BlockSpec((1, D, D), lambda j, m: (l, 0, 1), pipeline_mode=pl.Buffered(1)),
                  pl.BlockSpec((INP_TM, HALF), rope_idx),
                  pl.BlockSpec((INP_TM, HALF), rope_idx),
                  exp_in(0), exp_in(1), exp_in(2)],
        out_specs=(pl.BlockSpec((INP_TM, D), lambda j, m: (m, j)), exp_out(0), exp_out(1), exp_out(2)),
        scratch_shapes=[pltpu.VMEM((D, D), BF16), pltpu.VMEM((D, D), BF16)],
        compiler_params=_cparams("arbitrary", "arbitrary"),
        name="inproj",
    )(h, w_in, w_in, cos, sin, *w_exp)


CONV_TS = 512
CONV_HALO = 32
CONV_RC = 64
CONV_CW = 128
CONV_SH = CONV_TS + 24
SUBLANES = 8


def _conv_kernel(u_ref, halo_ref, sga_ref, cw_ref, cb_ref, lnw_ref, lnb_ref, wo_ref, o_ref,
                 buf_ref, sh_ref, acc_ref, wbf_ref):
    i = pl.program_id(1)

    @pl.when((pl.program_id(0) == 0) & (i == 0))
    def _():
        wbf_ref[...] = wo_ref[0].astype(BF16)

    halo = halo_ref[0].astype(F32)
    buf_ref[0:CONV_HALO, :] = jnp.where(i > 0, halo, 0.0)
    buf_ref[CONV_HALO:CONV_HALO + CONV_TS, :] = u_ref[0].astype(F32)
    for r in range(1, SUBLANES):
        sh_ref[r - 1] = buf_ref[r:r + CONV_SH, :]

    groups = CONV_RC // SUBLANES

    def body(ci, carry):
        r0 = pl.multiple_of(ci * CONV_RC, CONV_RC)
        for cc in range(D // CONV_CW):
            cols = slice(cc * CONV_CW, (cc + 1) * CONV_CW)
            accs = [cb_ref[:, cols]] * groups
            for off in range(2, CONV_K + 2):
                q, r = divmod(off, SUBLANES)
                w8 = cw_ref[off - 2, :, cols]
                for g in range(groups):
                    rows = pl.ds(r0 + SUBLANES * (q + g), SUBLANES)
                    win = buf_ref[rows, cols] if r == 0 else sh_ref[r - 1, rows, cols]
                    accs[g] = accs[g] + win * w8
            for g in range(groups):
                acc_ref[pl.ds(r0 + SUBLANES * g, SUBLANES), cols] = accs[g]
        return carry

    lax.fori_loop(0, CONV_TS // CONV_RC, body, 0)

    c = acc_ref[...]
    mu = jnp.mean(c, axis=-1, keepdims=True)
    cen = c - mu
    var = jnp.mean(cen * cen, axis=-1, keepdims=True)
    y = cen * lax.rsqrt(var + EPS) * lnw_ref[...] + lnb_ref[...]
    y = y * _sigmoid(y)
    out = jnp.dot(y.astype(BF16), wbf_ref[...], preferred_element_type=F32)
    o_ref[0] = (out * sga_ref[0].astype(F32)).astype(BF16)


def _conv_branch(p3, l, conv_w, conv_b, ln_w, ln_b, w_conv_out):
    hb = CONV_TS // CONV_HALO
    cw8 = jnp.broadcast_to(conv_w[:, None, :], (CONV_K, SUBLANES, D))
    cb8 = jnp.broadcast_to(conv_b[None, :], (SUBLANES, D))
    return pl.pallas_call(
        _conv_kernel,
        out_shape=jax.ShapeDtypeStruct((BATCH, SEQ, D), BF16),
        grid=(BATCH, SEQ // CONV_TS),
        in_specs=[pl.BlockSpec((1, CONV_TS, D), lambda b, i: (b, i, PCOL_U)),
                  pl.BlockSpec((1, CONV_HALO, D), lambda b, i: (b, jnp.maximum(i * hb - 1, 0), PCOL_U)),
                  pl.BlockSpec((1, CONV_TS, D), lambda b, i: (b, i, PCOL_GA)),
                  pl.BlockSpec((CONV_K, SUBLANES, D), lambda b, i: (0, 0, 0)),
                  pl.BlockSpec((SUBLANES, D), lambda b, i: (0, 0)),
                  pl.BlockSpec((1, D), lambda b, i: (0, 0)),
                  pl.BlockSpec((1, D), lambda b, i: (0, 0)),
                  pl.BlockSpec((1, D, D), lambda b, i: (l, 0, 0), pipeline_mode=pl.Buffered(1))],
        out_specs=pl.BlockSpec((1, CONV_TS, D), lambda b, i: (b, i, 0)),
        scratch_shapes=[pltpu.VMEM((CONV_HALO + CONV_TS, D), F32),
                        pltpu.VMEM((SUBLANES - 1, CONV_SH, D), F32),
                        pltpu.VMEM((CONV_TS, D), F32),
                        pltpu.VMEM((D, D), BF16)],
        compiler_params=_cparams("arbitrary", "arbitrary"),
        name="conv_branch",
    )(p3, p3, p3, cw8, cb8, ln_w.reshape(1, D), ln_b.reshape(1, D), w_conv_out)


def _ret_tables():
    hh = jnp.arange(HEADS, dtype=F32)
    log_g = jnp.log1p(-jnp.exp2(-5.0 - hh))
    idx = jnp.arange(CHUNK, dtype=F32)
    rel = idx[:, None] - idx[None, :]
    dmask = jnp.where(rel[None] >= 0, jnp.exp(jnp.maximum(rel, 0.0)[None] * log_g[:, None, None]), 0.0)
    xi = jnp.exp((idx + 1.0)[None, :] * log_g[:, None])[..., None]
    zeta = jnp.exp((CHUNK - 1.0 - idx)[None, :] * log_g[:, None])[..., None]
    g_chunk = jnp.exp(CHUNK * log_g)[:, None, None]
    return (dmask,
            jnp.broadcast_to(xi, (HEADS, CHUNK, DK)),
            jnp.broadcast_to(zeta, (HEADS, CHUNK, DK)),
            jnp.broadcast_to(g_chunk, (HEADS, 1, DV)))


RET_TS = 512


def _ret_kernel(q_ref, k_ref, v01_ref, v23_ref, g01_ref, g23_ref, dm_ref, xi_ref, zt_ref, gc_ref, o_ref, st_ref):
    @pl.when(pl.program_id(1) == 0)
    def _():
        st_ref[...] = jnp.zeros_like(st_ref)

    def body(c, carry):
        r0 = pl.multiple_of(c * CHUNK, CHUNK)
        rows = pl.ds(r0, CHUNK)
        for hd in range(HEADS):
            v_ref, g_ref = (v01_ref, g01_ref) if hd < 2 else (v23_ref, g23_ref)
            vcols = slice((hd % 2) * DV, (hd % 2 + 1) * DV)
            qc = q_ref[0, rows, hd * DK:(hd + 1) * DK]
            kc = k_ref[0, rows, hd * DK:(hd + 1) * DK]
            vc = v_ref[0, rows, vcols]
            scores = lax.dot_general(qc, kc, NT_DIMS, preferred_element_type=F32) * dm_ref[hd]
            inner = jnp.dot(scores.astype(BF16), vc, preferred_element_type=F32)
            st = st_ref[hd]
            qx = (qc.astype(F32) * xi_ref[hd]).astype(BF16)
            cross = jnp.dot(qx, st.astype(BF16), preferred_element_type=F32)
            kzt = (kc.astype(F32) * zt_ref[hd]).T.astype(BF16)
            st_ref[hd] = st * gc_ref[hd] + jnp.dot(kzt, vc, preferred_element_type=F32)
            o = inner + cross
            mu = jnp.mean(o, axis=-1, keepdims=True)
            cen = o - mu
            var = jnp.mean(cen * cen, axis=-1, keepdims=True)
            r = cen * lax.rsqrt(var + EPS)
            o_ref[0, rows, hd * DV:(hd + 1) * DV] = (r * g_ref[0, rows, vcols].astype(F32)).astype(BF16)
        return carry

    lax.fori_loop(0, RET_TS // CHUNK, body, 0)


def _retention(p3, tables):
    dmask, xi, zeta, gch = tables
    tile = lambda col: pl.BlockSpec((1, RET_TS, D), lambda b, i: (b, i, col))
    full = lambda a: pl.BlockSpec(a.shape, lambda b, i: (0, 0, 0))
    return pl.pallas_call(
        _ret_kernel,
        out_shape=jax.ShapeDtypeStruct((BATCH, SEQ, HEADS * DV), BF16),
        grid=(BATCH, SEQ // RET_TS),
        in_specs=[tile(PCOL_Q), tile(PCOL_K), tile(PCOL_V), tile(PCOL_V + 1), tile(PCOL_G), tile(PCOL_G + 1),
                  full(dmask), full(xi), full(zeta), full(gch)],
        out_specs=pl.BlockSpec((1, RET_TS, HEADS * DV), lambda b, i: (b, i, 0)),
        scratch_shapes=[pltpu.VMEM((HEADS, DK, DV), F32)],
        compiler_params=_cparams("parallel", "arbitrary"),
        name="retention",
    )(p3, p3, p3, p3, p3, p3, dmask, xi, zeta, gch)


MIX_TM = 512


def _route_rows(s, sb):
    row = lambda a, e: a[e:e + 1, :]
    best = None
    gidx = None
    for g in range(N_GROUPS):
        v = [row(sb, GROUP_SIZE * g + i) for i in range(GROUP_SIZE)]
        pair_sums = [v[a] + v[b] for a in range(GROUP_SIZE) for b in range(a + 1, GROUP_SIZE)]
        gs = functools.reduce(jnp.maximum, pair_sums)
        if g == 0:
            best, gidx = gs, jnp.zeros(gs.shape, jnp.int32)
        else:
            upd = gs > best
            gidx = jnp.where(upd, g, gidx)
            best = jnp.where(upd, gs, best)

    def pick(a, i):
        out = row(a, i)
        for g in range(1, N_GROUPS):
            out = jnp.where(gidx == g, row(a, GROUP_SIZE * g + i), out)
        return out

    vb = [pick(sb, i) for i in range(GROUP_SIZE)]
    vs = [pick(s, i) for i in range(GROUP_SIZE)]
    m1, i1, s1 = vb[0], jnp.zeros(gidx.shape, jnp.int32), vs[0]
    for i in range(1, GROUP_SIZE):
        upd = vb[i] > m1
        m1 = jnp.where(upd, vb[i], m1)
        i1 = jnp.where(upd, i, i1)
        s1 = jnp.where(upd, vs[i], s1)
    m2 = i2 = s2 = None
    for i in range(GROUP_SIZE):
        cand = jnp.where(i1 == i, -jnp.inf, vb[i])
        if m2 is None:
            m2, i2, s2 = cand, jnp.zeros(gidx.shape, jnp.int32), vs[0]
        else:
            upd = cand > m2
            m2 = jnp.where(upd, cand, m2)
            i2 = jnp.where(upd, i, i2)
            s2 = jnp.where(upd, vs[i], s2)
    den = s1 + s2
    w1 = s1 / den
    w2 = s2 / den
    lo = jnp.minimum(i1, i2)
    hi = jnp.maximum(i1, i2)
    pair = jnp.zeros(gidx.shape, jnp.int32)
    first = jnp.zeros(gidx.shape, jnp.int32)
    for p, (fa, fb) in enumerate(PAIR_ORDER):
        hit = (lo == min(fa, fb)) & (hi == max(fa, fb))
        pair = jnp.where(hit, p, pair)
        first = jnp.where(hit, fa, first)
    bucket = gidx * PAIRS + pair
    first_is_top1 = i1 == first
    return bucket, jnp.where(first_is_top1, w1, w2), jnp.where(first_is_top1, w2, w1)


def _mix_kernel(rg_ref, ya_ref, sgb_ref, x_ref, gt_ref, sc_ref, sh_ref, wr_ref, wo_ref, n2_ref,
                rhi_ref, rlo_ref, rb_ref, tri_ref,
                x1_ref, hx_ref, ri_ref, cnt_ref, carry_ref, wrb_ref, wob_ref):
    m = pl.program_id(0)

    @pl.when(m == 0)
    def _():
        carry_ref[...] = jnp.zeros_like(carry_ref)
        wrb_ref[...] = wr_ref[0].astype(BF16)
        wob_ref[...] = wo_ref[0].astype(BF16)

    yb = jnp.dot(rg_ref[...], wrb_ref[...], preferred_element_type=F32)
    y = ya_ref[...].astype(F32) + sgb_ref[...].astype(F32) * yb
    o = jnp.dot(y.astype(BF16), wob_ref[...], preferred_element_type=F32)
    x1 = x_ref[...] + gt_ref[0] * o
    x1_ref[...] = x1
    h2 = _rms(x1) * n2_ref[...] * (1.0 + sc_ref[0]) + sh_ref[0]
    hx_ref[:, 0:D] = h2

    hi = h2.astype(BF16)
    lo = (h2 - hi.astype(F32)).astype(BF16)
    rhi = rhi_ref[...]
    logits = (lax.dot_general(rhi, hi, NT_DIMS, preferred_element_type=F32)
              + lax.dot_general(rhi, lo, NT_DIMS, preferred_element_type=F32)
              + lax.dot_general(rlo_ref[...], hi, NT_DIMS, preferred_element_type=F32))
    s = _sigmoid(logits)
    bucket, w_lo, w_hi = _route_rows(s, s + rb_ref[...])

    onehot = (lax.broadcasted_iota(jnp.int32, (BUCKET_ROWS, MIX_TM), 0) == bucket).astype(F32)
    prefix = jnp.dot(onehot.astype(BF16), tri_ref[...], preferred_element_type=F32)
    carry = carry_ref[:, 0:1]
    rank = jnp.sum(onehot * (prefix + carry), axis=0, keepdims=True)
    new_carry = carry + jnp.sum(onehot, axis=1, keepdims=True)
    carry_ref[...] = jnp.broadcast_to(new_carry, carry_ref.shape)
    cnt_ref[...] = jnp.broadcast_to(new_carry, cnt_ref.shape)

    rid = lax.broadcasted_iota(jnp.int32, (8, MIX_TM), 0)
    ri_ref[...] = jnp.where(rid == 0, bucket, jnp.where(rid == 1, rank.astype(jnp.int32), 0))
    wid = lax.broadcasted_iota(jnp.int32, (TAIL, MIX_TM), 0)
    wrows = jnp.where(wid == 0, w_lo, jnp.where(wid == 1, w_hi, 0.0))
    hx_ref[:, D:XROW] = wrows.T


def _mix(rg, ya, p, x, gt1, sc2, sh2, w_ret_out, w_out, l, norm2_w, rhi, rlo, rb, tri):
    tpb = SEQ // MIX_TM
    bidx = lambda m: (m // tpb, 0, 0)
    return pl.pallas_call(
        _mix_kernel,
        out_shape=(jax.ShapeDtypeStruct((NTOK, D), F32),
                   jax.ShapeDtypeStruct((NTOK, XROW), F32),
                   jax.ShapeDtypeStruct((8, NTOK), jnp.int32),
                   jax.ShapeDtypeStruct((BUCKET_ROWS, 128), F32)),
        grid=(NTOK // MIX_TM,),
        in_specs=[pl.BlockSpec((MIX_TM, HEADS * DV), lambda m: (m, 0)),
                  pl.BlockSpec((MIX_TM, D), lambda m: (m, 0)),
                  pl.BlockSpec((MIX_TM, D), lambda m: (m, PCOL_GB)),
                  pl.BlockSpec((MIX_TM, D), lambda m: (m, 0)),
                  pl.BlockSpec((1, 1, D), bidx),
                  pl.BlockSpec((1, 1, D), bidx),
                  pl.BlockSpec((1, 1, D), bidx),
                  pl.BlockSpec((1, HEADS * DV, D), lambda m: (l, 0, 0), pipeline_mode=pl.Buffered(1)),
                  pl.BlockSpec((1, D, D), lambda m: (l, 0, 0), pipeline_mode=pl.Buffered(1)),
                  pl.BlockSpec((1, D), lambda m: (0, 0)),
                  pl.BlockSpec((N_EXPERTS, D), lambda m: (0, 0)),
                  pl.BlockSpec((N_EXPERTS, D), lambda m: (0, 0)),
                  pl.BlockSpec((N_EXPERTS, MIX_TM), lambda m: (0, 0)),
                  pl.BlockSpec((MIX_TM, MIX_TM), lambda m: (0, 0))],
        out_specs=(pl.BlockSpec((MIX_TM, D), lambda m: (m, 0)),
                   pl.BlockSpec((MIX_TM, XROW), lambda m: (m, 0)),
                   pl.BlockSpec((8, MIX_TM), lambda m: (0, m)),
                   pl.BlockSpec((BUCKET_ROWS, 128), lambda m: (0, 0))),
        scratch_shapes=[pltpu.VMEM((BUCKET_ROWS, 128), F32),
                        pltpu.VMEM((HEADS * DV, D), BF16), pltpu.VMEM((D, D), BF16)],
        compiler_params=_cparams("arbitrary"),
        name="mix_route",
    )(rg, ya, p, x, gt1, sc2, sh2, w_ret_out, w_out, norm2_w.reshape(1, D), rhi, rlo, rb, tri)


DISP_TG = 512


def _dispatch_kernel(dest_ref, nval_ref, src_ref, xs_ref, zero_ref, sem, zsem):
    step = pl.program_id(0)
    base = step * DISP_TG

    @pl.when(step == 0)
    def _():
        zero_ref[...] = jnp.zeros_like(zero_ref)

        def zero_copy(b):
            r0 = pl.multiple_of(b * MOE_BLK, MOE_BLK)
            return pltpu.make_async_copy(zero_ref, xs_ref.at[pl.ds(r0, MOE_BLK), :], zsem)

        def zissue(b, carry):
            @pl.when(nval_ref[b] < MOE_BLK)
            def _():
                zero_copy(b).start()
            return carry

        lax.fori_loop(0, N_BLOCKS, zissue, 0)

        def zdrain(b, carry):
            @pl.when(nval_ref[b] < MOE_BLK)
            def _():
                zero_copy(b).wait()
            return carry

        lax.fori_loop(0, N_BLOCKS, zdrain, 0)

    def row_copy(t, d):
        return pltpu.make_async_copy(src_ref.at[pl.ds(t, 1), :], xs_ref.at[pl.ds(d, 1), :], sem)

    for t in range(DISP_TG):
        row_copy(t, dest_ref[base + t]).start()
    pltpu.make_async_copy(src_ref, xs_ref.at[pl.ds(0, DISP_TG), :], sem).wait()


def _dispatch(dest, nval, hx):
    return pl.pallas_call(
        _dispatch_kernel,
        out_shape=jax.ShapeDtypeStruct((SLOT_ROWS, XROW), F32),
        grid_spec=pltpu.PrefetchScalarGridSpec(
            num_scalar_prefetch=2,
            grid=(NTOK // DISP_TG,),
            in_specs=[pl.BlockSpec((DISP_TG, XROW), lambda i, d, nv: (i, 0))],
            out_specs=pl.BlockSpec(memory_space=pl.ANY),
            scratch_shapes=[pltpu.VMEM((MOE_BLK, XROW), F32),
                            pltpu.SemaphoreType.DMA(()), pltpu.SemaphoreType.DMA(())]),
        compiler_params=_cparams("arbitrary", row_dma=True),
        name="dispatch",
    )(dest, nval, hx)


def _moe_kernel(elo_ref, ehi_ref, nval_ref, xs_ref, wg1, wu1, wd1, wg2, wu2, wd2, o_ref):
    i = pl.program_id(0)
    nval = nval_ref[i]

    def run(rows):
        x = xs_ref[0:rows, 0:D].astype(BF16)
        w_lo = xs_ref[0:rows, D:D + 1]
        w_hi = xs_ref[0:rows, D + 1:D + 2]

        def expert(wg, wu, wd):
            g = jnp.dot(x, wg[0], preferred_element_type=F32)
            u = jnp.dot(x, wu[0], preferred_element_type=F32)
            a = (g * _sigmoid(g) * u).astype(BF16)
            return jnp.dot(a, wd[0], preferred_element_type=F32)

        o_ref[0:rows, :] = w_lo * expert(wg1, wu1, wd1) + w_hi * expert(wg2, wu2, wd2)

    @pl.when(nval > MOE_BLK // 2)
    def _():
        run(MOE_BLK)

    @pl.when((nval > 0) & (nval <= MOE_BLK // 2))
    def _():
        run(MOE_BLK // 2)
        o_ref[MOE_BLK // 2:MOE_BLK, :] = jnp.zeros((MOE_BLK // 2, D), F32)

    @pl.when(nval == 0)
    def _():
        o_ref[...] = jnp.zeros_like(o_ref)


def _moe(elo, ehi, nval, xs, wg, wu, wd):
    wspec = lambda tab: pl.BlockSpec((1, D, D), lambda i, elo, ehi, nv: ((elo, ehi)[tab][i], 0, 0))
    return pl.pallas_call(
        _moe_kernel,
        out_shape=jax.ShapeDtypeStruct((SLOT_ROWS, D), F32),
        grid_spec=pltpu.PrefetchScalarGridSpec(
            num_scalar_prefetch=3,
            grid=(N_BLOCKS,),
            in_specs=[pl.BlockSpec((MOE_BLK, XROW), lambda i, elo, ehi, nv: (i, 0)),
                      wspec(0), wspec(0), wspec(0), wspec(1), wspec(1), wspec(1)],
            out_specs=pl.BlockSpec((MOE_BLK, D), lambda i, elo, ehi, nv: (i, 0))),
        compiler_params=_cparams("arbitrary"),
        name="experts",
    )(elo, ehi, nval, xs, wg, wu, wd, wg, wu, wd)


COMB_TG = 512


def _combine_kernel(dest_ref, ys_ref, x1_ref, gt_ref, nw_ref, sc_ref, sh_ref, *rest, last):
    if last:
        hn_ref, ybuf, sem = rest
    else:
        x2_ref, hn_ref, ybuf, sem = rest
    step = pl.program_id(0)
    slot = step % 2

    def gather_tile(tile, into):
        base = tile * COMB_TG

        for t in range(COMB_TG):
            src = ys_ref.at[pl.ds(dest_ref[base + t], 1), :]
            pltpu.make_async_copy(src, ybuf.at[into, pl.ds(t, 1), :], sem.at[into]).start()

    @pl.when(step == 0)
    def _():
        gather_tile(0, 0)

    @pl.when(step + 1 < pl.num_programs(0))
    def _():
        gather_tile(step + 1, 1 - slot)

    pltpu.make_async_copy(ys_ref.at[pl.ds(0, COMB_TG), :], ybuf.at[slot], sem.at[slot]).wait()

    x2 = x1_ref[...] + gt_ref[0] * ybuf[slot]
    y = _rms(x2) * nw_ref[...]
    if last:
        hn_ref[...] = y
    else:
        x2_ref[...] = x2
        hn_ref[...] = (y * (1.0 + sc_ref[0]) + sh_ref[0]).astype(hn_ref.dtype)


def _combine(dest, ys, x1, gt2, nw, sc, sh, last):
    tpb = SEQ // COMB_TG
    bidx = lambda i, d: (i // tpb, 0, 0)
    tile = pl.BlockSpec((COMB_TG, D), lambda i, d: (i, 0))
    if last:
        out_shape = jax.ShapeDtypeStruct((NTOK, D), F32)
        out_specs = tile
    else:
        out_shape = (jax.ShapeDtypeStruct((NTOK, D), F32), jax.ShapeDtypeStruct((NTOK, D), BF16))
        out_specs = (tile, tile)
    return pl.pallas_call(
        functools.partial(_combine_kernel, last=last),
        out_shape=out_shape,
        grid_spec=pltpu.PrefetchScalarGridSpec(
            num_scalar_prefetch=1,
            grid=(NTOK // COMB_TG,),
            in_specs=[pl.BlockSpec(memory_space=pl.ANY),
                      tile,
                      pl.BlockSpec((1, 1, D), bidx),
                      pl.BlockSpec((1, D), lambda i, d: (0, 0)),
                      pl.BlockSpec((1, 1, D), bidx),
                      pl.BlockSpec((1, 1, D), bidx)],
            out_specs=out_specs,
            scratch_shapes=[pltpu.VMEM((2, COMB_TG, D), F32), pltpu.SemaphoreType.DMA((2,))]),
        compiler_params=_cparams("arbitrary", row_dma=True),
        name="combine",
    )(dest, ys, x1, gt2, nw.reshape(1, D), sc, sh)


def _pair_tables():
    first = [GROUP_SIZE * g + fa for g in range(N_GROUPS) for fa, _ in PAIR_ORDER]
    second = [GROUP_SIZE * g + fb for g in range(N_GROUPS) for _, fb in PAIR_ORDER]
    return np.asarray(first, np.int32), np.asarray(second, np.int32)


def _slot_plan(ri, cnt):
    counts = cnt[:N_BUCKETS, 0].astype(jnp.int32)
    padded = (counts + MOE_BLK - 1) // MOE_BLK * MOE_BLK
    pad_end = jnp.cumsum(padded)
    pad_start = pad_end - padded
    dest = pad_start[ri[0]] + ri[1]
    blk_start = jnp.arange(N_BLOCKS, dtype=jnp.int32) * MOE_BLK
    bb = jnp.minimum(jnp.sum(blk_start[:, None] >= pad_end[None, :], axis=1), N_BUCKETS - 1).astype(jnp.int32)
    nval = jnp.clip(pad_start[bb] + counts[bb] - blk_start, 0, MOE_BLK).astype(jnp.int32)
    lo_tab, hi_tab = _pair_tables()
    return dest.astype(jnp.int32), jnp.asarray(lo_tab)[bb], jnp.asarray(hi_tab)[bb], nval


def kernel(x, c, positions, w_ada, b_ada, norm1_w, w_in, conv_w, conv_b, conv_ln_w, conv_ln_b, w_conv_out,
           w_ret_out, w_out, norm2_w, w_router, router_bias, w_exp_gate, w_exp_up, w_exp_down, final_norm_w):
    mod = _ada(c, w_ada, b_ada).reshape(DEPTH, BATCH, N_MOD, 1, D)
    sh1, sc1, gt1, sh2, sc2, gt2 = (mod[:, :, i] for i in range(N_MOD))
    cos, sin = _rope_tables(positions)
    ret_tables = _ret_tables()

    wr_t = w_router.T
    rhi = wr_t.astype(BF16)
    rlo = (wr_t - rhi.astype(F32)).astype(BF16)
    rb = jnp.broadcast_to(router_bias.astype(F32)[:, None], (N_EXPERTS, MIX_TM))
    tri = (jnp.arange(MIX_TM)[:, None] < jnp.arange(MIX_TM)[None, :]).astype(BF16)

    xf = x.reshape(NTOK, D)
    h = _modnorm(x, norm1_w[0], sc1[0], sh1[0]).reshape(NTOK, D)
    out = None
    for l in range(DEPTH):
        p, wg_bf, wu_bf, wd_bf = _inproj(h, w_in, l, cos, sin, (w_exp_gate, w_exp_up, w_exp_down))
        p3 = p.reshape(BATCH, SEQ, P_COLS)
        ya = _conv_branch(p3, l, conv_w[l], conv_b[l], conv_ln_w[l], conv_ln_b[l], w_conv_out)
        rg = _retention(p3, ret_tables)
        x1, hx, ri, cnt = _mix(rg.reshape(NTOK, HEADS * DV), ya.reshape(NTOK, D), p, xf, gt1[l], sc2[l], sh2[l],
                               w_ret_out, w_out, l, norm2_w[l], rhi, rlo, rb, tri)
        dest, elo, ehi, nval = _slot_plan(ri, cnt)
        xs = _dispatch(dest, nval, hx)
        ys = _moe(elo, ehi, nval, xs, wg_bf, wu_bf, wd_bf)
        if l + 1 < DEPTH:
            xf, h = _combine(dest, ys, x1, gt2[l], norm1_w[l + 1], sc1[l + 1], sh1[l + 1], last=False)
        else:
            out = _combine(dest, ys, x1, gt2[l], final_norm_w, sc1[l], sh1[l], last=True)
    return out.reshape(BATCH, SEQ, D)
```

```python
import functools

import numpy as np
import jax
import jax.numpy as jnp
from jax import lax
from jax.experimental import pallas as pl
from jax.experimental.pallas import tpu as pltpu

F32 = jnp.float32
BF16 = jnp.bfloat16

D = 1024
BATCH = 8
SEQ = 2048
DEPTH = 4
NTOK = BATCH * SEQ
N_MOD = 6
EPS = 1e-6

CONV_K = 31
HEADS = 4
DK = 256
DV = 512
CHUNK = 256
ROPE_BASE = 10000.0
HALF = DK // 2

N_EXPERTS = 16
N_GROUPS = 4
GROUP_SIZE = 4
PAIR_ORDER = ((0, 1), (0, 2), (0, 3), (1, 3), (1, 2), (3, 2))
PAIRS = len(PAIR_ORDER)
N_BUCKETS = N_GROUPS * PAIRS
BUCKET_ROWS = 32

P_COLS = 9 * D
PCOL_U, PCOL_Q, PCOL_K, PCOL_V, PCOL_G, PCOL_GA, PCOL_GB = 0, 1, 2, 3, 5, 7, 8

MOE_BLK = 256
N_BLOCKS = NTOK // MOE_BLK + N_BUCKETS
SLOT_ROWS = N_BLOCKS * MOE_BLK
TAIL = 128
XROW = D + TAIL

VMEM_LIMIT = 56 * 1024 * 1024

NT_DIMS = (((1,), (1,)), ((), ()))


def _cparams(*sem, row_dma=False):
    return pltpu.CompilerParams(dimension_semantics=sem, vmem_limit_bytes=VMEM_LIMIT,
                                disable_bounds_checks=row_dma)


def _sigmoid(x):
    return 0.5 * jnp.tanh(0.5 * x) + 0.5


def _rms(x):
    return x * lax.rsqrt(jnp.mean(x * x, axis=-1, keepdims=True) + EPS)


ADA_TN = 3072


def _ada_kernel(c_ref, w_ref, b_ref, o_ref):
    c = c_ref[...]
    sc = c * _sigmoid(c)
    o_ref[0] = jnp.dot(sc, w_ref[0], precision=lax.Precision.HIGHEST,
                       preferred_element_type=F32) + b_ref[0]


def _ada(c, w_ada, b_ada):
    nj = N_MOD * D // ADA_TN
    return pl.pallas_call(
        _ada_kernel,
        out_shape=jax.ShapeDtypeStruct((DEPTH, BATCH, N_MOD * D), F32),
        grid=(DEPTH, nj),
        in_specs=[pl.BlockSpec((BATCH, D), lambda l, j: (0, 0)),
                  pl.BlockSpec((1, D, ADA_TN), lambda l, j: (l, 0, j)),
                  pl.BlockSpec((1, 1, ADA_TN), lambda l, j: (l, 0, j))],
        out_specs=pl.BlockSpec((1, BATCH, ADA_TN), lambda l, j: (l, 0, j)),
        compiler_params=_cparams("parallel", "parallel"),
        name="ada",
    )(c, w_ada, b_ada.reshape(DEPTH, 1, N_MOD * D))


ROPE_TN = 2048


def _rope_kernel(pos_ref, inv_ref, cos_ref, sin_ref):
    ang = pos_ref[...] * inv_ref[...]
    cos_ref[...] = jnp.cos(ang)
    sin_ref[...] = jnp.sin(ang)


def _rope_tables(positions):
    inv = ROPE_BASE ** (-jnp.arange(HALF, dtype=F32) / HALF)
    pos = positions.astype(F32).reshape(NTOK, 1)
    return pl.pallas_call(
        _rope_kernel,
        out_shape=(jax.ShapeDtypeStruct((NTOK, HALF), F32),) * 2,
        grid=(NTOK // ROPE_TN,),
        in_specs=[pl.BlockSpec((ROPE_TN, 1), lambda i: (i, 0)),
                  pl.BlockSpec((1, HALF), lambda i: (0, 0))],
        out_specs=(pl.BlockSpec((ROPE_TN, HALF), lambda i: (i, 0)),) * 2,
        compiler_params=_cparams("parallel"),
        name="rope_tables",
    )(pos, inv.reshape(1, HALF))


NORM_TS = 512


def _modnorm_kernel(x_ref, w_ref, sc_ref, sh_ref, o_ref):
    y = _rms(x_ref[0]) * w_ref[...]
    o_ref[0] = (y * (1.0 + sc_ref[0]) + sh_ref[0]).astype(o_ref.dtype)


def _modnorm(x, w, sc, sh):
    return pl.pallas_call(
        _modnorm_kernel,
        out_shape=jax.ShapeDtypeStruct((BATCH, SEQ, D), BF16),
        grid=(BATCH, SEQ // NORM_TS),
        in_specs=[pl.BlockSpec((1, NORM_TS, D), lambda b, i: (b, i, 0)),
                  pl.BlockSpec((1, D), lambda b, i: (0, 0)),
                  pl.BlockSpec((1, 1, D), lambda b, i: (b, 0, 0)),
                  pl.BlockSpec((1, 1, D), lambda b, i: (b, 0, 0))],
        out_specs=pl.BlockSpec((1, NORM_TS, D), lambda b, i: (b, i, 0)),
        compiler_params=_cparams("parallel", "parallel"),
        name="modnorm",
    )(x, w.reshape(1, D), sc, sh)


INP_TM = 1024
INP_GROUPS = 9
EXP_SLAB_ROWS = 512
EXP_SLABS_PER_MAT = D // EXP_SLAB_ROWS
EXP_SLABS = N_EXPERTS * EXP_SLABS_PER_MAT


def _inproj_kernel(h_ref, w1_ref, w2_ref, cos_ref, sin_ref, eg_ref, eu_ref, ed_ref,
                   o_ref, og_ref, ou_ref, od_ref, wb1_ref, wb2_ref):
    j = pl.program_id(0)
    m = pl.program_id(1)
    h = h_ref[...]

    step = j * pl.num_programs(1) + m
    for t, (src, dst) in enumerate(((eg_ref, og_ref), (eu_ref, ou_ref), (ed_ref, od_ref))):
        @pl.when((step >= EXP_SLABS * t) & (step < EXP_SLABS * (t + 1)))
        def _():
            dst[0] = src[0, 0].astype(BF16)

    @pl.when(m == 0)
    def _():
        wb1_ref[...] = w1_ref[0].astype(BF16)

    @pl.when((m == 0) & (j == 0))
    def _():
        wb2_ref[...] = w2_ref[0].astype(BF16)

    def proj():
        return jnp.dot(h, wb1_ref[...], preferred_element_type=F32)

    @pl.when(j == 0)
    def _():
        a = proj()
        b = jnp.dot(h, wb2_ref[...], preferred_element_type=F32)
        o_ref[...] = (a * _sigmoid(b)).astype(BF16)

    @pl.when((j == 1) | (j == 2))
    def _():
        t = proj()
        scale = jnp.where(j == 2, DK ** -0.5, 1.0).astype(F32)
        cos = cos_ref[...] * scale
        sin = sin_ref[...] * scale
        for hd in range(HEADS):
            c0 = hd * DK
            t1 = t[:, c0:c0 + HALF]
            t2 = t[:, c0 + HALF:c0 + DK]
            o_ref[:, c0:c0 + HALF] = (t1 * cos - t2 * sin).astype(BF16)
            o_ref[:, c0 + HALF:c0 + DK] = (t1 * sin + t2 * cos).astype(BF16)

    @pl.when((j == 3) | (j == 4))
    def _():
        o_ref[...] = proj().astype(BF16)

    @pl.when((j == 5) | (j == 6))
    def _():
        g = proj()
        o_ref[...] = (g * _sigmoid(g)).astype(BF16)

    @pl.when(j >= 7)
    def _():
        o_ref[...] = _sigmoid(proj()).astype(BF16)


def _inproj(h, w_in, l, cos, sin, w_exp):
    n_m = NTOK // INP_TM
    assert 3 * EXP_SLABS <= INP_GROUPS * n_m

    def rope_idx(j, m):
        return (jnp.where((j == 1) | (j == 2), m, 0), 0)

    def slab(t, j, m):
        s = jnp.clip(j * n_m + m - EXP_SLABS * t, 0, EXP_SLABS - 1)
        return s // EXP_SLABS_PER_MAT, s % EXP_SLABS_PER_MAT

    def exp_in(t):
        return pl.BlockSpec((1, 1, EXP_SLAB_ROWS, D), lambda j, m: (l, *slab(t, j, m), 0))

    def exp_out(t):
        return pl.BlockSpec((1, EXP_SLAB_ROWS, D), lambda j, m: (*slab(t, j, m), 0))

    exp_shape = jax.ShapeDtypeStruct((N_EXPERTS, D, D), BF16)
    return pl.pallas_call(
        _inproj_kernel,
        out_shape=(jax.ShapeDtypeStruct((NTOK, P_COLS), BF16), exp_shape, exp_shape, exp_shape),
        grid=(INP_GROUPS, n_m),
        in_specs=[pl.BlockSpec((INP_TM, D), lambda j, m: (m, 0)),
                  pl.BlockSpec((1, D, D), lambda j, m: (l, 0, jnp.where(j == 0, 0, j + 1))),
                  pl.BlockSpec((1, D, D), lambda j, m: (l, 0, 1), pipeline_mode=pl.Buffered(1)),
                  pl.BlockSpec((INP_TM, HALF), rope_idx),
                  pl.BlockSpec((INP_TM, HALF), rope_idx),
                  exp_in(0), exp_in(1), exp_in(2)],
        out_specs=(pl.BlockSpec((INP_TM, D), lambda j, m: (m, j)), exp_out(0), exp_out(1), exp_out(2)),
        scratch_shapes=[pltpu.VMEM((D, D), BF16), pltpu.VMEM((D, D), BF16)],
        compiler_params=_cparams("arbitrary", "arbitrary"),
        name="inproj",
    )(h, w_in, w_in, cos, sin, *w_exp)


CONV_TS = 512
CONV_HALO = 32
CONV_RC = 64
CONV_CW = 128
CONV_SH = CONV_TS + 24
SUBLANES = 8


def _conv_kernel(u_ref, halo_ref, sga_ref, cw_ref, cb_ref, lnw_ref, lnb_ref, wo_ref, o_ref,
                 buf_ref, sh_ref, acc_ref, wbf_ref):
    i = pl.program_id(1)

    @pl.when((pl.program_id(0) == 0) & (i == 0))
    def _():
        wbf_ref[...] = wo_ref[0].astype(BF16)

    halo = halo_ref[0].astype(F32)
    buf_ref[0:CONV_HALO, :] = jnp.where(i > 0, halo, 0.0)
    buf_ref[CONV_HALO:CONV_HALO + CONV_TS, :] = u_ref[0].astype(F32)
    for r in range(1, SUBLANES):
        sh_ref[r - 1] = buf_ref[r:r + CONV_SH, :]

    groups = CONV_RC // SUBLANES

    def body(ci, carry):
        r0 = pl.multiple_of(ci * CONV_RC, CONV_RC)
        for cc in range(D // CONV_CW):
            cols = slice(cc * CONV_CW, (cc + 1) * CONV_CW)
            accs = [cb_ref[:, cols]] * groups
            for off in range(2, CONV_K + 2):
                q, r = divmod(off, SUBLANES)
                w8 = cw_ref[off - 2, :, cols]
                for g in range(groups):
                    rows = pl.ds(r0 + SUBLANES * (q + g), SUBLANES)
                    win = buf_ref[rows, cols] if r == 0 else sh_ref[r - 1, rows, cols]
                    accs[g] = accs[g] + win * w8
            for g in range(groups):
                acc_ref[pl.ds(r0 + SUBLANES * g, SUBLANES), cols] = accs[g]
        return carry

    lax.fori_loop(0, CONV_TS // CONV_RC, body, 0)

    c = acc_ref[...]
    mu = jnp.mean(c, axis=-1, keepdims=True)
    cen = c - mu
    var = jnp.mean(cen * cen, axis=-1, keepdims=True)
    y = cen * lax.rsqrt(var + EPS) * lnw_ref[...] + lnb_ref[...]
    y = y * _sigmoid(y)
    out = jnp.dot(y.astype(BF16), wbf_ref[...], preferred_element_type=F32)
    o_ref[0] = (out * sga_ref[0].astype(F32)).astype(BF16)


def _conv_branch(p3, l, conv_w, conv_b, ln_w, ln_b, w_conv_out):
    hb = CONV_TS // CONV_HALO
    cw8 = jnp.broadcast_to(conv_w[:, None, :], (CONV_K, SUBLANES, D))
    cb8 = jnp.broadcast_to(conv_b[None, :], (SUBLANES, D))
    return pl.pallas_call(
        _conv_kernel,
        out_shape=jax.ShapeDtypeStruct((BATCH, SEQ, D), BF16),
        grid=(BATCH, SEQ // CONV_TS),
        in_specs=[pl.BlockSpec((1, CONV_TS, D), lambda b, i: (b, i, PCOL_U)),
                  pl.BlockSpec((1, CONV_HALO, D), lambda b, i: (b, jnp.maximum(i * hb - 1, 0), PCOL_U)),
                  pl.BlockSpec((1, CONV_TS, D), lambda b, i: (b, i, PCOL_GA)),
                  pl.BlockSpec((CONV_K, SUBLANES, D), lambda b, i: (0, 0, 0)),
                  pl.BlockSpec((SUBLANES, D), lambda b, i: (0, 0)),
                  pl.BlockSpec((1, D), lambda b, i: (0, 0)),
                  pl.BlockSpec((1, D), lambda b, i: (0, 0)),
                  pl.BlockSpec((1, D, D), lambda b, i: (l, 0, 0), pipeline_mode=pl.Buffered(1))],
        out_specs=pl.BlockSpec((1, CONV_TS, D), lambda b, i: (b, i, 0)),
        scratch_shapes=[pltpu.VMEM((CONV_HALO + CONV_TS, D), F32),
                        pltpu.VMEM((SUBLANES - 1, CONV_SH, D), F32),
                        pltpu.VMEM((CONV_TS, D), F32),
                        pltpu.VMEM((D, D), BF16)],
        compiler_params=_cparams("arbitrary", "arbitrary"),
        name="conv_branch",
    )(p3, p3, p3, cw8, cb8, ln_w.reshape(1, D), ln_b.reshape(1, D), w_conv_out)


def _ret_tables():
    hh = jnp.arange(HEADS, dtype=F32)
    log_g = jnp.log1p(-jnp.exp2(-5.0 - hh))
    idx = jnp.arange(CHUNK, dtype=F32)
    rel = idx[:, None] - idx[None, :]
    dmask = jnp.where(rel[None] >= 0, jnp.exp(jnp.maximum(rel, 0.0)[None] * log_g[:, None, None]), 0.0)
    xi = jnp.exp((idx + 1.0)[None, :] * log_g[:, None])[..., None]
    zeta = jnp.exp((CHUNK - 1.0 - idx)[None, :] * log_g[:, None])[..., None]
    g_chunk = jnp.exp(CHUNK * log_g)[:, None, None]
    return (dmask,
            jnp.broadcast_to(xi, (HEADS, CHUNK, DK)),
            jnp.broadcast_to(zeta, (HEADS, CHUNK, DK)),
            jnp.broadcast_to(g_chunk, (HEADS, 1, DV)))


RET_TS = 512


def _ret_kernel(q_ref, k_ref, v01_ref, v23_ref, g01_ref, g23_ref, dm_ref, xi_ref, zt_ref, gc_ref, o_ref, st_ref):
    @pl.when(pl.program_id(1) == 0)
    def _():
        st_ref[...] = jnp.zeros_like(st_ref)

    def body(c, carry):
        r0 = pl.multiple_of(c * CHUNK, CHUNK)
        rows = pl.ds(r0, CHUNK)
        for hd in range(HEADS):
            v_ref, g_ref = (v01_ref, g01_ref) if hd < 2 else (v23_ref, g23_ref)
            vcols = slice((hd % 2) * DV, (hd % 2 + 1) * DV)
            qc = q_ref[0, rows, hd * DK:(hd + 1) * DK]
            kc = k_ref[0, rows, hd * DK:(hd + 1) * DK]
            vc = v_ref[0, rows, vcols]
            scores = lax.dot_general(qc, kc, NT_DIMS, preferred_element_type=F32) * dm_ref[hd]
            inner = jnp.dot(scores.astype(BF16), vc, preferred_element_type=F32)
            st = st_ref[hd]
            qx = (qc.astype(F32) * xi_ref[hd]).astype(BF16)
            cross = jnp.dot(qx, st.astype(BF16), preferred_element_type=F32)
            kzt = (kc.astype(F32) * zt_ref[hd]).T.astype(BF16)
            st_ref[hd] = st * gc_ref[hd] + jnp.dot(kzt, vc, preferred_element_type=F32)
            o = inner + cross
            mu = jnp.mean(o, axis=-1, keepdims=True)
            cen = o - mu
            var = jnp.mean(cen * cen, axis=-1, keepdims=True)
            r = cen * lax.rsqrt(var + EPS)
            o_ref[0, rows, hd * DV:(hd + 1) * DV] = (r * g_ref[0, rows, vcols].astype(F32)).astype(BF16)
        return carry

    lax.fori_loop(0, RET_TS // CHUNK, body, 0)


def _retention(p3, tables):
    dmask, xi, zeta, gch = tables
    tile = lambda col: pl.BlockSpec((1, RET_TS, D), lambda b, i: (b, i, col))
    full = lambda a: pl.BlockSpec(a.shape, lambda b, i: (0, 0, 0))
    return pl.pallas_call(
        _ret_kernel,
        out_shape=jax.ShapeDtypeStruct((BATCH, SEQ, HEADS * DV), BF16),
        grid=(BATCH, SEQ // RET_TS),
        in_specs=[tile(PCOL_Q), tile(PCOL_K), tile(PCOL_V), tile(PCOL_V + 1), tile(PCOL_G), tile(PCOL_G + 1),
                  full(dmask), full(xi), full(zeta), full(gch)],
        out_specs=pl.BlockSpec((1, RET_TS, HEADS * DV), lambda b, i: (b, i, 0)),
        scratch_shapes=[pltpu.VMEM((HEADS, DK, DV), F32)],
        compiler_params=_cparams("parallel", "arbitrary"),
        name="retention",
    )(p3, p3, p3, p3, p3, p3, dmask, xi, zeta, gch)


MIX_TM = 512


def _route_rows(s, sb):
    row = lambda a, e: a[e:e + 1, :]
    best = None
    gidx = None
    for g in range(N_GROUPS):
        v = [row(sb, GROUP_SIZE * g + i) for i in range(GROUP_SIZE)]
        pair_sums = [v[a] + v[b] for a in range(GROUP_SIZE) for b in range(a + 1, GROUP_SIZE)]
        gs = functools.reduce(jnp.maximum, pair_sums)
        if g == 0:
            best, gidx = gs, jnp.zeros(gs.shape, jnp.int32)
        else:
            upd = gs > best
            gidx = jnp.where(upd, g, gidx)
            best = jnp.where(upd, gs, best)

    def pick(a, i):
        out = row(a, i)
        for g in range(1, N_GROUPS):
            out = jnp.where(gidx == g, row(a, GROUP_SIZE * g + i), out)
        return out

    vb = [pick(sb, i) for i in range(GROUP_SIZE)]
    vs = [pick(s, i) for i in range(GROUP_SIZE)]
    m1, i1, s1 = vb[0], jnp.zeros(gidx.shape, jnp.int32), vs[0]
    for i in range(1, GROUP_SIZE):
        upd = vb[i] > m1
        m1 = jnp.where(upd, vb[i], m1)
        i1 = jnp.where(upd, i, i1)
        s1 = jnp.where(upd, vs[i], s1)
    m2 = i2 = s2 = None
    for i in range(GROUP_SIZE):
        cand = jnp.where(i1 == i, -jnp.inf, vb[i])
        if m2 is None:
            m2, i2, s2 = cand, jnp.zeros(gidx.shape, jnp.int32), vs[0]
        else:
            upd = cand > m2
            m2 = jnp.where(upd, cand, m2)
            i2 = jnp.where(upd, i, i2)
            s2 = jnp.where(upd, vs[i], s2)
    den = s1 + s2
    w1 = s1 / den
    w2 = s2 / den
    lo = jnp.minimum(i1, i2)
    hi = jnp.maximum(i1, i2)
    pair = jnp.zeros(gidx.shape, jnp.int32)
    first = jnp.zeros(gidx.shape, jnp.int32)
    for p, (fa, fb) in enumerate(PAIR_ORDER):
        hit = (lo == min(fa, fb)) & (hi == max(fa, fb))
        pair = jnp.where(hit, p, pair)
        first = jnp.where(hit, fa, first)
    bucket = gidx * PAIRS + pair
    first_is_top1 = i1 == first
    return bucket, jnp.where(first_is_top1, w1, w2), jnp.where(first_is_top1, w2, w1)


def _mix_kernel(rg_ref, ya_ref, sgb_ref, x_ref, gt_ref, sc_ref, sh_ref, wr_ref, wo_ref, n2_ref,
                rhi_ref, rlo_ref, rb_ref, tri_ref,
                x1_ref, hx_ref, ri_ref, cnt_ref, carry_ref, wrb_ref, wob_ref):
    m = pl.program_id(0)

    @pl.when(m == 0)
    def _():
        carry_ref[...] = jnp.zeros_like(carry_ref)
        wrb_ref[...] = wr_ref[0].astype(BF16)
        wob_ref[...] = wo_ref[0].astype(BF16)

    yb = jnp.dot(rg_ref[...], wrb_ref[...], preferred_element_type=F32)
    y = ya_ref[...].astype(F32) + sgb_ref[...].astype(F32) * yb
    o = jnp.dot(y.astype(BF16), wob_ref[...], preferred_element_type=F32)
    x1 = x_ref[...] + gt_ref[0] * o
    x1_ref[...] = x1
    h2 = _rms(x1) * n2_ref[...] * (1.0 + sc_ref[0]) + sh_ref[0]
    hx_ref[:, 0:D] = h2

    hi = h2.astype(BF16)
    lo = (h2 - hi.astype(F32)).astype(BF16)
    rhi = rhi_ref[...]
    logits = (lax.dot_general(rhi, hi, NT_DIMS, preferred_element_type=F32)
              + lax.dot_general(rhi, lo, NT_DIMS, preferred_element_type=F32)
              + lax.dot_general(rlo_ref[...], hi, NT_DIMS, preferred_element_type=F32))
    s = _sigmoid(logits)
    bucket, w_lo, w_hi = _route_rows(s, s + rb_ref[...])

    onehot = (lax.broadcasted_iota(jnp.int32, (BUCKET_ROWS, MIX_TM), 0) == bucket).astype(F32)
    prefix = jnp.dot(onehot.astype(BF16), tri_ref[...], preferred_element_type=F32)
    carry = carry_ref[:, 0:1]
    rank = jnp.sum(onehot * (prefix + carry), axis=0, keepdims=True)
    new_carry = carry + jnp.sum(onehot, axis=1, keepdims=True)
    carry_ref[...] = jnp.broadcast_to(new_carry, carry_ref.shape)
    cnt_ref[...] = jnp.broadcast_to(new_carry, cnt_ref.shape)

    rid = lax.broadcasted_iota(jnp.int32, (8, MIX_TM), 0)
    ri_ref[...] = jnp.where(rid == 0, bucket, jnp.where(rid == 1, rank.astype(jnp.int32), 0))
    wid = lax.broadcasted_iota(jnp.int32, (TAIL, MIX_TM), 0)
    wrows = jnp.where(wid == 0, w_lo, jnp.where(wid == 1, w_hi, 0.0))
    hx_ref[:, D:XROW] = wrows.T


def _mix(rg, ya, p, x, gt1, sc2, sh2, w_ret_out, w_out, l, norm2_w, rhi, rlo, rb, tri):
    tpb = SEQ // MIX_TM
    bidx = lambda m: (m // tpb, 0, 0)
    return pl.pallas_call(
        _mix_kernel,
        out_shape=(jax.ShapeDtypeStruct((NTOK, D), F32),
                   jax.ShapeDtypeStruct((NTOK, XROW), F32),
                   jax.ShapeDtypeStruct((8, NTOK), jnp.int32),
                   jax.ShapeDtypeStruct((BUCKET_ROWS, 128), F32)),
        grid=(NTOK // MIX_TM,),
        in_specs=[pl.BlockSpec((MIX_TM, HEADS * DV), lambda m: (m, 0)),
                  pl.BlockSpec((MIX_TM, D), lambda m: (m, 0)),
                  pl.BlockSpec((MIX_TM, D), lambda m: (m, PCOL_GB)),
                  pl.BlockSpec((MIX_TM, D), lambda m: (m, 0)),
                  pl.BlockSpec((1, 1, D), bidx),
                  pl.BlockSpec((1, 1, D), bidx),
                  pl.BlockSpec((1, 1, D), bidx),
                  pl.BlockSpec((1, HEADS * DV, D), lambda m: (l, 0, 0), pipeline_mode=pl.Buffered(1)),
                  pl.BlockSpec((1, D, D), lambda m: (l, 0, 0), pipeline_mode=pl.Buffered(1)),
                  pl.BlockSpec((1, D), lambda m: (0, 0)),
                  pl.BlockSpec((N_EXPERTS, D), lambda m: (0, 0)),
                  pl.BlockSpec((N_EXPERTS, D), lambda m: (0, 0)),
                  pl.BlockSpec((N_EXPERTS, MIX_TM), lambda m: (0, 0)),
                  pl.BlockSpec((MIX_TM, MIX_TM), lambda m: (0, 0))],
        out_specs=(pl.BlockSpec((MIX_TM, D), lambda m: (m, 0)),
                   pl.BlockSpec((MIX_TM, XROW), lambda m: (m, 0)),
                   pl.BlockSpec((8, MIX_TM), lambda m: (0, m)),
                   pl.BlockSpec((BUCKET_ROWS, 128), lambda m: (0, 0))),
        scratch_shapes=[pltpu.VMEM((BUCKET_ROWS, 128), F32),
                        pltpu.VMEM((HEADS * DV, D), BF16), pltpu.VMEM((D, D), BF16)],
        compiler_params=_cparams("arbitrary"),
        name="mix_route",
    )(rg, ya, p, x, gt1, sc2, sh2, w_ret_out, w_out, norm2_w.reshape(1, D), rhi, rlo, rb, tri)


DISP_TG = 512


def _dispatch_kernel(dest_ref, nval_ref, src_ref, xs_ref, zero_ref, sem, zsem):
    step = pl.program_id(0)
    base = step * DISP_TG

    @pl.when(step == 0)
    def _():
        zero_ref[...] = jnp.zeros_like(zero_ref)

        def zero_copy(b):
            r0 = pl.multiple_of(b * MOE_BLK, MOE_BLK)
            return pltpu.make_async_copy(zero_ref, xs_ref.at[pl.ds(r0, MOE_BLK), :], zsem)

        def zissue(b, carry):
            @pl.when(nval_ref[b] < MOE_BLK)
            def _():
                zero_copy(b).start()
            return carry

        lax.fori_loop(0, N_BLOCKS, zissue, 0)

        def zdrain(b, carry):
            @pl.when(nval_ref[b] < MOE_BLK)
            def _():
                zero_copy(b).wait()
            return carry

        lax.fori_loop(0, N_BLOCKS, zdrain, 0)

    def row_copy(t, d):
        return pltpu.make_async_copy(src_ref.at[pl.ds(t, 1), :], xs_ref.at[pl.ds(d, 1), :], sem)

    for t in range(DISP_TG):
        row_copy(t, dest_ref[base + t]).start(priority=t % 2)
    pltpu.make_async_copy(src_ref, xs_ref.at[pl.ds(0, DISP_TG), :], sem).wait()


def _dispatch(dest, nval, hx):
    return pl.pallas_call(
        _dispatch_kernel,
        out_shape=jax.ShapeDtypeStruct((SLOT_ROWS, XROW), F32),
        grid_spec=pltpu.PrefetchScalarGridSpec(
            num_scalar_prefetch=2,
            grid=(NTOK // DISP_TG,),
            in_specs=[pl.BlockSpec((DISP_TG, XROW), lambda i, d, nv: (i, 0))],
            out_specs=pl.BlockSpec(memory_space=pl.ANY),
            scratch_shapes=[pltpu.VMEM((MOE_BLK, XROW), F32),
                            pltpu.SemaphoreType.DMA(()), pltpu.SemaphoreType.DMA(())]),
        compiler_params=_cparams("arbitrary", row_dma=True),
        name="dispatch",
    )(dest, nval, hx)


def _moe_kernel(elo_ref, ehi_ref, nval_ref, xs_ref, wg1, wu1, wd1, wg2, wu2, wd2, o_ref):
    i = pl.program_id(0)
    nval = nval_ref[i]

    def run(rows):
        x = xs_ref[0:rows, 0:D].astype(BF16)
        w_lo = xs_ref[0:rows, D:D + 1]
        w_hi = xs_ref[0:rows, D + 1:D + 2]

        def expert(wg, wu, wd):
            g = jnp.dot(x, wg[0], preferred_element_type=F32)
            u = jnp.dot(x, wu[0], preferred_element_type=F32)
            a = (g * _sigmoid(g) * u).astype(BF16)
            return jnp.dot(a, wd[0], preferred_element_type=F32)

        o_ref[0:rows, :] = w_lo * expert(wg1, wu1, wd1) + w_hi * expert(wg2, wu2, wd2)

    @pl.when(nval > MOE_BLK // 2)
    def _():
        run(MOE_BLK)

    @pl.when((nval > 0) & (nval <= MOE_BLK // 2))
    def _():
        run(MOE_BLK // 2)
        o_ref[MOE_BLK // 2:MOE_BLK, :] = jnp.zeros((MOE_BLK // 2, D), F32)

    @pl.when(nval == 0)
    def _():
        o_ref[...] = jnp.zeros_like(o_ref)


def _moe(elo, ehi, nval, xs, wg, wu, wd):
    wspec = lambda tab: pl.BlockSpec((1, D, D), lambda i, elo, ehi, nv: ((elo, ehi)[tab][i], 0, 0))
    return pl.pallas_call(
        _moe_kernel,
        out_shape=jax.ShapeDtypeStruct((SLOT_ROWS, D), F32),
        grid_spec=pltpu.PrefetchScalarGridSpec(
            num_scalar_prefetch=3,
            grid=(N_BLOCKS,),
            in_specs=[pl.BlockSpec((MOE_BLK, XROW), lambda i, elo, ehi, nv: (i, 0)),
                      wspec(0), wspec(0), wspec(0), wspec(1), wspec(1), wspec(1)],
            out_specs=pl.BlockSpec((MOE_BLK, D), lambda i, elo, ehi, nv: (i, 0))),
        compiler_params=_cparams("arbitrary"),
        name="experts",
    )(elo, ehi, nval, xs, wg, wu, wd, wg, wu, wd)


COMB_TG = 512


def _combine_kernel(dest_ref, ys_ref, x1_ref, gt_ref, nw_ref, sc_ref, sh_ref, *rest, last):
    if last:
        hn_ref, ybuf, sem = rest
    else:
        x2_ref, hn_ref, ybuf, sem = rest
    step = pl.program_id(0)
    slot = step % 2

    def gather_tile(tile, into):
        base = tile * COMB_TG

        for t in range(COMB_TG):
            src = ys_ref.at[pl.ds(dest_ref[base + t], 1), :]
            pltpu.make_async_copy(src, ybuf.at[into, pl.ds(t, 1), :], sem.at[into]).start(priority=t % 2)

    @pl.when(step == 0)
    def _():
        gather_tile(0, 0)

    @pl.when(step + 1 < pl.num_programs(0))
    def _():
        gather_tile(step + 1, 1 - slot)

    pltpu.make_async_copy(ys_ref.at[pl.ds(0, COMB_TG), :], ybuf.at[slot], sem.at[slot]).wait()

    x2 = x1_ref[...] + gt_ref[0] * ybuf[slot]
    y = _rms(x2) * nw_ref[...]
    if last:
        hn_ref[...] = y
    else:
        x2_ref[...] = x2
        hn_ref[...] = (y * (1.0 + sc_ref[0]) + sh_ref[0]).astype(hn_ref.dtype)


def _combine(dest, ys, x1, gt2, nw, sc, sh, last):
    tpb = SEQ // COMB_TG
    bidx = lambda i, d: (i // tpb, 0, 0)
    tile = pl.BlockSpec((COMB_TG, D), lambda i, d: (i, 0))
    if last:
        out_shape = jax.ShapeDtypeStruct((NTOK, D), F32)
        out_specs = tile
    else:
        out_shape = (jax.ShapeDtypeStruct((NTOK, D), F32), jax.ShapeDtypeStruct((NTOK, D), BF16))
        out_specs = (tile, tile)
    return pl.pallas_call(
        functools.partial(_combine_kernel, last=last),
        out_shape=out_shape,
        grid_spec=pltpu.PrefetchScalarGridSpec(
            num_scalar_prefetch=1,
            grid=(NTOK // COMB_TG,),
            in_specs=[pl.BlockSpec(memory_space=pl.ANY),
                      tile,
                      pl.BlockSpec((1, 1, D), bidx),
                      pl.BlockSpec((1, D), lambda i, d: (0, 0)),
                      pl.BlockSpec((1, 1, D), bidx),
                      pl.BlockSpec((1, 1, D), bidx)],
            out_specs=out_specs,
            scratch_shapes=[pltpu.VMEM((2, COMB_TG, D), F32), pltpu.SemaphoreType.DMA((2,))]),
        compiler_params=_cparams("arbitrary", row_dma=True),
        name="combine",
    )(dest, ys, x1, gt2, nw.reshape(1, D), sc, sh)


def _pair_tables():
    first = [GROUP_SIZE * g + fa for g in range(N_GROUPS) for fa, _ in PAIR_ORDER]
    second = [GROUP_SIZE * g + fb for g in range(N_GROUPS) for _, fb in PAIR_ORDER]
    return np.asarray(first, np.int32), np.asarray(second, np.int32)


def _slot_plan(ri, cnt):
    counts = cnt[:N_BUCKETS, 0].astype(jnp.int32)
    padded = (counts + MOE_BLK - 1) // MOE_BLK * MOE_BLK
    pad_end = jnp.cumsum(padded)
    pad_start = pad_end - padded
    dest = pad_start[ri[0]] + ri[1]
    blk_start = jnp.arange(N_BLOCKS, dtype=jnp.int32) * MOE_BLK
    bb = jnp.minimum(jnp.sum(blk_start[:, None] >= pad_end[None, :], axis=1), N_BUCKETS - 1).astype(jnp.int32)
    nval = jnp.clip(pad_start[bb] + counts[bb] - blk_start, 0, MOE_BLK).astype(jnp.int32)
    lo_tab, hi_tab = _pair_tables()
    return dest.astype(jnp.int32), jnp.asarray(lo_tab)[bb], jnp.asarray(hi_tab)[bb], nval


def kernel(x, c, positions, w_ada, b_ada, norm1_w, w_in, conv_w, conv_b, conv_ln_w, conv_ln_b, w_conv_out,
           w_ret_out, w_out, norm2_w, w_router, router_bias, w_exp_gate, w_exp_up, w_exp_down, final_norm_w):
    mod = _ada(c, w_ada, b_ada).reshape(DEPTH, BATCH, N_MOD, 1, D)
    sh1, sc1, gt1, sh2, sc2, gt2 = (mod[:, :, i] for i in range(N_MOD))
    cos, sin = _rope_tables(positions)
    ret_tables = _ret_tables()

    wr_t = w_router.T
    rhi = wr_t.astype(BF16)
    rlo = (wr_t - rhi.astype(F32)).astype(BF16)
    rb = jnp.broadcast_to(router_bias.astype(F32)[:, None], (N_EXPERTS, MIX_TM))
    tri = (jnp.arange(MIX_TM)[:, None] < jnp.arange(MIX_TM)[None, :]).astype(BF16)

    xf = x.reshape(NTOK, D)
    h = _modnorm(x, norm1_w[0], sc1[0], sh1[0]).reshape(NTOK, D)
    out = None
    for l in range(DEPTH):
        p, wg_bf, wu_bf, wd_bf = _inproj(h, w_in, l, cos, sin, (w_exp_gate, w_exp_up, w_exp_down))
        p3 = p.reshape(BATCH, SEQ, P_COLS)
        ya = _conv_branch(p3, l, conv_w[l], conv_b[l], conv_ln_w[l], conv_ln_b[l], w_conv_out)
        rg = _retention(p3, ret_tables)
        x1, hx, ri, cnt = _mix(rg.reshape(NTOK, HEADS * DV), ya.reshape(NTOK, D), p, xf, gt1[l], sc2[l], sh2[l],
                               w_ret_out, w_out, l, norm2_w[l], rhi, rlo, rb, tri)
        dest, elo, ehi, nval = _slot_plan(ri, cnt)
        xs = _dispatch(dest, nval, hx)
        ys = _moe(elo, ehi, nval, xs, wg_bf, wu_bf, wd_bf)
        if l + 1 < DEPTH:
            xf, h = _combine(dest, ys, x1, gt2[l], norm1_w[l + 1], sc1[l + 1], sh1[l + 1], last=False)
        else:
            out = _combine(dest, ys, x1, gt2[l], final_norm_w, sc1[l], sh1[l], last=True)
    return out.reshape(BATCH, SEQ, D)
```

```python
import functools

import numpy as np
import jax
import jax.numpy as jnp
from jax import lax
from jax.experimental import pallas as pl
from jax.experimental.pallas import tpu as pltpu

F32 = jnp.float32
BF16 = jnp.bfloat16

D = 1024
BATCH = 8
SEQ = 2048
DEPTH = 4
NTOK = BATCH * SEQ
N_MOD = 6
EPS = 1e-6

CONV_K = 31
HEADS = 4
DK = 256
DV = 512
CHUNK = 256
ROPE_BASE = 10000.0
HALF = DK // 2

N_EXPERTS = 16
N_GROUPS = 4
GROUP_SIZE = 4
PAIR_ORDER = ((0, 1), (0, 2), (0, 3), (1, 3), (1, 2), (3, 2))
PAIRS = len(PAIR_ORDER)
N_BUCKETS = N_GROUPS * PAIRS
BUCKET_ROWS = 32

P_COLS = 9 * D
PCOL_U, PCOL_Q, PCOL_K, PCOL_V, PCOL_G, PCOL_GA, PCOL_GB = 0, 1, 2, 3, 5, 7, 8

MOE_BLK = 256
N_BLOCKS = NTOK // MOE_BLK + N_BUCKETS
SLOT_ROWS = N_BLOCKS * MOE_BLK
TAIL = 128
XROW = D + TAIL

VMEM_LIMIT = 56 * 1024 * 1024

NT_DIMS = (((1,), (1,)), ((), ()))


def _cparams(*sem, row_dma=False):
    return pltpu.CompilerParams(dimension_semantics=sem, vmem_limit_bytes=VMEM_LIMIT,
                                disable_bounds_checks=row_dma)


def _sigmoid(x):
    return 0.5 * jnp.tanh(0.5 * x) + 0.5


def _rms(x):
    return x * lax.rsqrt(jnp.mean(x * x, axis=-1, keepdims=True) + EPS)


ADA_TN = 3072


def _ada_kernel(c_ref, w_ref, b_ref, o_ref):
    c = c_ref[...]
    sc = c * _sigmoid(c)
    o_ref[0] = jnp.dot(sc, w_ref[0], precision=lax.Precision.HIGHEST,
                       preferred_element_type=F32) + b_ref[0]


def _ada(c, w_ada, b_ada):
    nj = N_MOD * D // ADA_TN
    return pl.pallas_call(
        _ada_kernel,
        out_shape=jax.ShapeDtypeStruct((DEPTH, BATCH, N_MOD * D), F32),
        grid=(DEPTH, nj),
        in_specs=[pl.BlockSpec((BATCH, D), lambda l, j: (0, 0)),
                  pl.BlockSpec((1, D, ADA_TN), lambda l, j: (l, 0, j)),
                  pl.BlockSpec((1, 1, ADA_TN), lambda l, j: (l, 0, j))],
        out_specs=pl.BlockSpec((1, BATCH, ADA_TN), lambda l, j: (l, 0, j)),
        compiler_params=_cparams("parallel", "parallel"),
        name="ada",
    )(c, w_ada, b_ada.reshape(DEPTH, 1, N_MOD * D))


ROPE_TN = 2048


def _rope_kernel(pos_ref, inv_ref, cos_ref, sin_ref):
    ang = pos_ref[...] * inv_ref[...]
    cos_ref[...] = jnp.cos(ang)
    sin_ref[...] = jnp.sin(ang)


def _rope_tables(positions):
    inv = ROPE_BASE ** (-jnp.arange(HALF, dtype=F32) / HALF)
    pos = positions.astype(F32).reshape(NTOK, 1)
    return pl.pallas_call(
        _rope_kernel,
        out_shape=(jax.ShapeDtypeStruct((NTOK, HALF), F32),) * 2,
        grid=(NTOK // ROPE_TN,),
        in_specs=[pl.BlockSpec((ROPE_TN, 1), lambda i: (i, 0)),
                  pl.BlockSpec((1, HALF), lambda i: (0, 0))],
        out_specs=(pl.BlockSpec((ROPE_TN, HALF), lambda i: (i, 0)),) * 2,
        compiler_params=_cparams("parallel"),
        name="rope_tables",
    )(pos, inv.reshape(1, HALF))


NORM_TS = 512


def _modnorm_kernel(x_ref, w_ref, sc_ref, sh_ref, o_ref):
    y = _rms(x_ref[0]) * w_ref[...]
    o_ref[0] = (y * (1.0 + sc_ref[0]) + sh_ref[0]).astype(o_ref.dtype)


def _modnorm(x, w, sc, sh):
    return pl.pallas_call(
        _modnorm_kernel,
        out_shape=jax.ShapeDtypeStruct((BATCH, SEQ, D), BF16),
        grid=(BATCH, SEQ // NORM_TS),
        in_specs=[pl.BlockSpec((1, NORM_TS, D), lambda b, i: (b, i, 0)),
                  pl.BlockSpec((1, D), lambda b, i: (0, 0)),
                  pl.BlockSpec((1, 1, D), lambda b, i: (b, 0, 0)),
                  pl.BlockSpec((1, 1, D), lambda b, i: (b, 0, 0))],
        out_specs=pl.BlockSpec((1, NORM_TS, D), lambda b, i: (b, i, 0)),
        compiler_params=_cparams("parallel", "parallel"),
        name="modnorm",
    )(x, w.reshape(1, D), sc, sh)


INP_TM = 1024
INP_GROUPS = 9


def _inproj_kernel(h_ref, w1_ref, w2_ref, cos_ref, sin_ref, o_ref, wb1_ref, wb2_ref):
    j = pl.program_id(0)
    m = pl.program_id(1)
    h = h_ref[...]

    @pl.when(m == 0)
    def _():
        wb1_ref[...] = w1_ref[0].astype(BF16)

    @pl.when((m == 0) & (j == 0))
    def _():
        wb2_ref[...] = w2_ref[0].astype(BF16)

    def proj():
        return jnp.dot(h, wb1_ref[...], preferred_element_type=F32)

    @pl.when(j == 0)
    def _():
        a = proj()
        b = jnp.dot(h, wb2_ref[...], preferred_element_type=F32)
        o_ref[...] = (a * _sigmoid(b)).astype(BF16)

    @pl.when((j == 1) | (j == 2))
    def _():
        t = proj()
        scale = jnp.where(j == 2, DK ** -0.5, 1.0).astype(F32)
        cos = cos_ref[...] * scale
        sin = sin_ref[...] * scale
        for hd in range(HEADS):
            c0 = hd * DK
            t1 = t[:, c0:c0 + HALF]
            t2 = t[:, c0 + HALF:c0 + DK]
            o_ref[:, c0:c0 + HALF] = (t1 * cos - t2 * sin).astype(BF16)
            o_ref[:, c0 + HALF:c0 + DK] = (t1 * sin + t2 * cos).astype(BF16)

    @pl.when((j == 3) | (j == 4))
    def _():
        o_ref[...] = proj().astype(BF16)

    @pl.when((j == 5) | (j == 6))
    def _():
        g = proj()
        o_ref[...] = (g * _sigmoid(g)).astype(BF16)

    @pl.when(j >= 7)
    def _():
        o_ref[...] = _sigmoid(proj()).astype(BF16)


def _inproj(h, w_in, l, cos, sin):
    def rope_idx(j, m):
        return (jnp.where((j == 1) | (j == 2), m, 0), 0)

    return pl.pallas_call(
        _inproj_kernel,
        out_shape=jax.ShapeDtypeStruct((NTOK, P_COLS), BF16),
        grid=(INP_GROUPS, NTOK // INP_TM),
        in_specs=[pl.BlockSpec((INP_TM, D), lambda j, m: (m, 0)),
                  pl.BlockSpec((1, D, D), lambda j, m: (l, 0, jnp.where(j == 0, 0, j + 1))),
                  pl.BlockSpec((1, D, D), lambda j, m: (l, 0, 1), pipeline_mode=pl.Buffered(1)),
                  pl.BlockSpec((INP_TM, HALF), rope_idx),
                  pl.BlockSpec((INP_TM, HALF), rope_idx)],
        out_specs=pl.BlockSpec((INP_TM, D), lambda j, m: (m, j)),
        scratch_shapes=[pltpu.VMEM((D, D), BF16), pltpu.VMEM((D, D), BF16)],
        compiler_params=_cparams("arbitrary", "arbitrary"),
        name="inproj",
    )(h, w_in, w_in, cos, sin)


CONV_TS = 512
CONV_HALO = 32
CONV_RC = 64
CONV_CW = 128
CONV_SH = CONV_TS + 24
SUBLANES = 8


def _conv_kernel(u_ref, halo_ref, sga_ref, cw_ref, cb_ref, lnw_ref, lnb_ref, wo_ref, eg_ref, eu_ref, ed_ref,
                 o_ref, og_ref, ou_ref, od_ref, buf_ref, sh_ref, acc_ref, wbf_ref):
    i = pl.program_id(1)

    @pl.when((pl.program_id(0) == 0) & (i == 0))
    def _():
        wbf_ref[...] = wo_ref[0].astype(BF16)

    for src, dst in ((eg_ref, og_ref), (eu_ref, ou_ref), (ed_ref, od_ref)):
        dst[0] = src[0, 0].astype(BF16)

    halo = halo_ref[0].astype(F32)
    buf_ref[0:CONV_HALO, :] = jnp.where(i > 0, halo, 0.0)
    buf_ref[CONV_HALO:CONV_HALO + CONV_TS, :] = u_ref[0].astype(F32)
    for r in range(1, SUBLANES):
        sh_ref[r - 1] = buf_ref[r:r + CONV_SH, :]

    groups = CONV_RC // SUBLANES

    def body(ci, carry):
        r0 = pl.multiple_of(ci * CONV_RC, CONV_RC)
        for cc in range(D // CONV_CW):
            cols = slice(cc * CONV_CW, (cc + 1) * CONV_CW)
            accs = [cb_ref[:, cols]] * groups
            for off in range(2, CONV_K + 2):
                q, r = divmod(off, SUBLANES)
                w8 = cw_ref[off - 2, :, cols]
                for g in range(groups):
                    rows = pl.ds(r0 + SUBLANES * (q + g), SUBLANES)
                    win = buf_ref[rows, cols] if r == 0 else sh_ref[r - 1, rows, cols]
                    accs[g] = accs[g] + win * w8
            for g in range(groups):
                acc_ref[pl.ds(r0 + SUBLANES * g, SUBLANES), cols] = accs[g]
        return carry

    lax.fori_loop(0, CONV_TS // CONV_RC, body, 0)

    c = acc_ref[...]
    mu = jnp.mean(c, axis=-1, keepdims=True)
    cen = c - mu
    var = jnp.mean(cen * cen, axis=-1, keepdims=True)
    y = cen * lax.rsqrt(var + EPS) * lnw_ref[...] + lnb_ref[...]
    y = y * _sigmoid(y)
    out = jnp.dot(y.astype(BF16), wbf_ref[...], preferred_element_type=F32)
    o_ref[0] = (out * sga_ref[0].astype(F32)).astype(BF16)


def _conv_branch(p3, l, conv_w, conv_b, ln_w, ln_b, w_conv_out, w_exp):
    hb = CONV_TS // CONV_HALO
    n_i = SEQ // CONV_TS
    cw8 = jnp.broadcast_to(conv_w[:, None, :], (CONV_K, SUBLANES, D))
    cb8 = jnp.broadcast_to(conv_b[None, :], (SUBLANES, D))
    slab_rows = N_EXPERTS * D // (BATCH * n_i)
    per_mat = D // slab_rows
    assert per_mat * slab_rows == D and slab_rows % 16 == 0

    def slab(b, i):
        s = b * n_i + i
        return s // per_mat, s % per_mat

    exp_in = pl.BlockSpec((1, 1, slab_rows, D), lambda b, i: (l, *slab(b, i), 0))
    exp_out = pl.BlockSpec((1, slab_rows, D), lambda b, i: (*slab(b, i), 0))
    exp_shape = jax.ShapeDtypeStruct((N_EXPERTS, D, D), BF16)
    return pl.pallas_call(
        _conv_kernel,
        out_shape=(jax.ShapeDtypeStruct((BATCH, SEQ, D), BF16), exp_shape, exp_shape, exp_shape),
        grid=(BATCH, n_i),
        in_specs=[pl.BlockSpec((1, CONV_TS, D), lambda b, i: (b, i, PCOL_U)),
                  pl.BlockSpec((1, CONV_HALO, D), lambda b, i: (b, jnp.maximum(i * hb - 1, 0), PCOL_U)),
                  pl.BlockSpec((1, CONV_TS, D), lambda b, i: (b, i, PCOL_GA)),
                  pl.BlockSpec((CONV_K, SUBLANES, D), lambda b, i: (0, 0, 0)),
                  pl.BlockSpec((SUBLANES, D), lambda b, i: (0, 0)),
                  pl.BlockSpec((1, D), lambda b, i: (0, 0)),
                  pl.BlockSpec((1, D), lambda b, i: (0, 0)),
                  pl.BlockSpec((1, D, D), lambda b, i: (l, 0, 0), pipeline_mode=pl.Buffered(1)),
                  exp_in, exp_in, exp_in],
        out_specs=(pl.BlockSpec((1, CONV_TS, D), lambda b, i: (b, i, 0)), exp_out, exp_out, exp_out),
        scratch_shapes=[pltpu.VMEM((CONV_HALO + CONV_TS, D), F32),
                        pltpu.VMEM((SUBLANES - 1, CONV_SH, D), F32),
                        pltpu.VMEM((CONV_TS, D), F32),
                        pltpu.VMEM((D, D), BF16)],
        compiler_params=_cparams("arbitrary", "arbitrary"),
        name="conv_branch",
    )(p3, p3, p3, cw8, cb8, ln_w.reshape(1, D), ln_b.reshape(1, D), w_conv_out, *w_exp)


def _ret_tables():
    hh = jnp.arange(HEADS, dtype=F32)
    log_g = jnp.log1p(-jnp.exp2(-5.0 - hh))
    idx = jnp.arange(CHUNK, dtype=F32)
    rel = idx[:, None] - idx[None, :]
    dmask = jnp.where(rel[None] >= 0, jnp.exp(jnp.maximum(rel, 0.0)[None] * log_g[:, None, None]), 0.0)
    xi = jnp.exp((idx + 1.0)[None, :] * log_g[:, None])[..., None]
    zeta = jnp.exp((CHUNK - 1.0 - idx)[None, :] * log_g[:, None])[..., None]
    g_chunk = jnp.exp(CHUNK * log_g)[:, None, None]
    return (dmask,
            jnp.broadcast_to(xi, (HEADS, CHUNK, DK)),
            jnp.broadcast_to(zeta, (HEADS, CHUNK, DK)),
            jnp.broadcast_to(g_chunk, (HEADS, 1, DV)))


RET_TS = 512


def _ret_kernel(q_ref, k_ref, v01_ref, v23_ref, g01_ref, g23_ref, dm_ref, xi_ref, zt_ref, gc_ref, o_ref, st_ref):
    @pl.when(pl.program_id(1) == 0)
    def _():
        st_ref[...] = jnp.zeros_like(st_ref)

    def body(c, carry):
        r0 = pl.multiple_of(c * CHUNK, CHUNK)
        rows = pl.ds(r0, CHUNK)
        for hd in range(HEADS):
            v_ref, g_ref = (v01_ref, g01_ref) if hd < 2 else (v23_ref, g23_ref)
            vcols = slice((hd % 2) * DV, (hd % 2 + 1) * DV)
            qc = q_ref[0, rows, hd * DK:(hd + 1) * DK]
            kc = k_ref[0, rows, hd * DK:(hd + 1) * DK]
            vc = v_ref[0, rows, vcols]
            scores = lax.dot_general(qc, kc, NT_DIMS, preferred_element_type=F32) * dm_ref[hd]
            inner = jnp.dot(scores.astype(BF16), vc, preferred_element_type=F32)
            st = st_ref[hd]
            qx = (qc.astype(F32) * xi_ref[hd]).astype(BF16)
            cross = jnp.dot(qx, st.astype(BF16), preferred_element_type=F32)
            kzt = (kc.astype(F32) * zt_ref[hd]).T.astype(BF16)
            st_ref[hd] = st * gc_ref[hd] + jnp.dot(kzt, vc, preferred_element_type=F32)
            o = inner + cross
            mu = jnp.mean(o, axis=-1, keepdims=True)
            cen = o - mu
            var = jnp.mean(cen * cen, axis=-1, keepdims=True)
            r = cen * lax.rsqrt(var + EPS)
            o_ref[0, rows, hd * DV:(hd + 1) * DV] = (r * g_ref[0, rows, vcols].astype(F32)).astype(BF16)
        return carry

    lax.fori_loop(0, RET_TS // CHUNK, body, 0)


def _retention(p3, tables):
    dmask, xi, zeta, gch = tables
    tile = lambda col: pl.BlockSpec((1, RET_TS, D), lambda b, i: (b, i, col))
    full = lambda a: pl.BlockSpec(a.shape, lambda b, i: (0, 0, 0))
    return pl.pallas_call(
        _ret_kernel,
        out_shape=jax.ShapeDtypeStruct((BATCH, SEQ, HEADS * DV), BF16),
        grid=(BATCH, SEQ // RET_TS),
        in_specs=[tile(PCOL_Q), tile(PCOL_K), tile(PCOL_V), tile(PCOL_V + 1), tile(PCOL_G), tile(PCOL_G + 1),
                  full(dmask), full(xi), full(zeta), full(gch)],
        out_specs=pl.BlockSpec((1, RET_TS, HEADS * DV), lambda b, i: (b, i, 0)),
        scratch_shapes=[pltpu.VMEM((HEADS, DK, DV), F32)],
        compiler_params=_cparams("parallel", "arbitrary"),
        name="retention",
    )(p3, p3, p3, p3, p3, p3, dmask, xi, zeta, gch)


MIX_TM = 512


def _route_rows(s, sb):
    row = lambda a, e: a[e:e + 1, :]
    best = None
    gidx = None
    for g in range(N_GROUPS):
        v = [row(sb, GROUP_SIZE * g + i) for i in range(GROUP_SIZE)]
        pair_sums = [v[a] + v[b] for a in range(GROUP_SIZE) for b in range(a + 1, GROUP_SIZE)]
        gs = functools.reduce(jnp.maximum, pair_sums)
        if g == 0:
            best, gidx = gs, jnp.zeros(gs.shape, jnp.int32)
        else:
            upd = gs > best
            gidx = jnp.where(upd, g, gidx)
            best = jnp.where(upd, gs, best)

    def pick(a, i):
        out = row(a, i)
        for g in range(1, N_GROUPS):
            out = jnp.where(gidx == g, row(a, GROUP_SIZE * g + i), out)
        return out

    vb = [pick(sb, i) for i in range(GROUP_SIZE)]
    vs = [pick(s, i) for i in range(GROUP_SIZE)]
    m1, i1, s1 = vb[0], jnp.zeros(gidx.shape, jnp.int32), vs[0]
    for i in range(1, GROUP_SIZE):
        upd = vb[i] > m1
        m1 = jnp.where(upd, vb[i], m1)
        i1 = jnp.where(upd, i, i1)
        s1 = jnp.where(upd, vs[i], s1)
    m2 = i2 = s2 = None
    for i in range(GROUP_SIZE):
        cand = jnp.where(i1 == i, -jnp.inf, vb[i])
        if m2 is None:
            m2, i2, s2 = cand, jnp.zeros(gidx.shape, jnp.int32), vs[0]
        else:
            upd = cand > m2
            m2 = jnp.where(upd, cand, m2)
            i2 = jnp.where(upd, i, i2)
            s2 = jnp.where(upd, vs[i], s2)
    den = s1 + s2
    w1 = s1 / den
    w2 = s2 / den
    lo = jnp.minimum(i1, i2)
    hi = jnp.maximum(i1, i2)
    pair = jnp.zeros(gidx.shape, jnp.int32)
    first = jnp.zeros(gidx.shape, jnp.int32)
    for p, (fa, fb) in enumerate(PAIR_ORDER):
        hit = (lo == min(fa, fb)) & (hi == max(fa, fb))
        pair = jnp.where(hit, p, pair)
        first = jnp.where(hit, fa, first)
    bucket = gidx * PAIRS + pair
    first_is_top1 = i1 == first
    return bucket, jnp.where(first_is_top1, w1, w2), jnp.where(first_is_top1, w2, w1)


def _mix_kernel(rg_ref, ya_ref, sgb_ref, x_ref, gt_ref, sc_ref, sh_ref, wr_ref, wo_ref, n2_ref,
                rhi_ref, rlo_ref, rb_ref, tri_ref,
                x1_ref, hx_ref, ri_ref, cnt_ref, carry_ref, wrb_ref, wob_ref):
    m = pl.program_id(0)

    @pl.when(m == 0)
    def _():
        carry_ref[...] = jnp.zeros_like(carry_ref)
        wrb_ref[...] = wr_ref[0].astype(BF16)
        wob_ref[...] = wo_ref[0].astype(BF16)

    yb = jnp.dot(rg_ref[...], wrb_ref[...], preferred_element_type=F32)
    y = ya_ref[...].astype(F32) + sgb_ref[...].astype(F32) * yb
    o = jnp.dot(y.astype(BF16), wob_ref[...], preferred_element_type=F32)
    x1 = x_ref[...] + gt_ref[0] * o
    x1_ref[...] = x1
    h2 = _rms(x1) * n2_ref[...] * (1.0 + sc_ref[0]) + sh_ref[0]
    hx_ref[:, 0:D] = h2

    hi = h2.astype(BF16)
    lo = (h2 - hi.astype(F32)).astype(BF16)
    rhi = rhi_ref[...]
    logits = (lax.dot_general(rhi, hi, NT_DIMS, preferred_element_type=F32)
              + lax.dot_general(rhi, lo, NT_DIMS, preferred_element_type=F32)
              + lax.dot_general(rlo_ref[...], hi, NT_DIMS, preferred_element_type=F32))
    s = _sigmoid(logits)
    bucket, w_lo, w_hi = _route_rows(s, s + rb_ref[...])

    onehot = (lax.broadcasted_iota(jnp.int32, (BUCKET_ROWS, MIX_TM), 0) == bucket).astype(F32)
    prefix = jnp.dot(onehot.astype(BF16), tri_ref[...], preferred_element_type=F32)
    carry = carry_ref[:, 0:1]
    rank = jnp.sum(onehot * (prefix + carry), axis=0, keepdims=True)
    new_carry = carry + jnp.sum(onehot, axis=1, keepdims=True)
    carry_ref[...] = jnp.broadcast_to(new_carry, carry_ref.shape)
    cnt_ref[...] = jnp.broadcast_to(new_carry, cnt_ref.shape)

    rid = lax.broadcasted_iota(jnp.int32, (8, MIX_TM), 0)
    ri_ref[...] = jnp.where(rid == 0, bucket, jnp.where(rid == 1, rank.astype(jnp.int32), 0))
    wid = lax.broadcasted_iota(jnp.int32, (TAIL, MIX_TM), 0)
    wrows = jnp.where(wid == 0, w_lo, jnp.where(wid == 1, w_hi, 0.0))
    hx_ref[:, D:XROW] = wrows.T


def _mix(rg, ya, p, x, gt1, sc2, sh2, w_ret_out, w_out, l, norm2_w, rhi, rlo, rb, tri):
    tpb = SEQ // MIX_TM
    bidx = lambda m: (m // tpb, 0, 0)
    return pl.pallas_call(
        _mix_kernel,
        out_shape=(jax.ShapeDtypeStruct((NTOK, D), F32),
                   jax.ShapeDtypeStruct((NTOK, XROW), F32),
                   jax.ShapeDtypeStruct((8, NTOK), jnp.int32),
                   jax.ShapeDtypeStruct((BUCKET_ROWS, 128), F32)),
        grid=(NTOK // MIX_TM,),
        in_specs=[pl.BlockSpec((MIX_TM, HEADS * DV), lambda m: (m, 0)),
                  pl.BlockSpec((MIX_TM, D), lambda m: (m, 0)),
                  pl.BlockSpec((MIX_TM, D), lambda m: (m, PCOL_GB)),
                  pl.BlockSpec((MIX_TM, D), lambda m: (m, 0)),
                  pl.BlockSpec((1, 1, D), bidx),
                  pl.BlockSpec((1, 1, D), bidx),
                  pl.BlockSpec((1, 1, D), bidx),
                  pl.BlockSpec((1, HEADS * DV, D), lambda m: (l, 0, 0), pipeline_mode=pl.Buffered(1)),
                  pl.BlockSpec((1, D, D), lambda m: (l, 0, 0), pipeline_mode=pl.Buffered(1)),
                  pl.BlockSpec((1, D), lambda m: (0, 0)),
                  pl.BlockSpec((N_EXPERTS, D), lambda m: (0, 0)),
                  pl.BlockSpec((N_EXPERTS, D), lambda m: (0, 0)),
                  pl.BlockSpec((N_EXPERTS, MIX_TM), lambda m: (0, 0)),
                  pl.BlockSpec((MIX_TM, MIX_TM), lambda m: (0, 0))],
        out_specs=(pl.BlockSpec((MIX_TM, D), lambda m: (m, 0)),
                   pl.BlockSpec((MIX_TM, XROW), lambda m: (m, 0)),
                   pl.BlockSpec((8, MIX_TM), lambda m: (0, m)),
                   pl.BlockSpec((BUCKET_ROWS, 128), lambda m: (0, 0))),
        scratch_shapes=[pltpu.VMEM((BUCKET_ROWS, 128), F32),
                        pltpu.VMEM((HEADS * DV, D), BF16), pltpu.VMEM((D, D), BF16)],
        compiler_params=_cparams("arbitrary"),
        name="mix_route",
    )(rg, ya, p, x, gt1, sc2, sh2, w_ret_out, w_out, norm2_w.reshape(1, D), rhi, rlo, rb, tri)


DISP_TG = 512


def _dispatch_kernel(dest_ref, nval_ref, src_ref, xs_ref, zero_ref, sem, zsem):
    step = pl.program_id(0)
    base = step * DISP_TG

    @pl.when(step == 0)
    def _():
        zero_ref[...] = jnp.zeros_like(zero_ref)

        def zero_copy(b):
            r0 = pl.multiple_of(b * MOE_BLK, MOE_BLK)
            return pltpu.make_async_copy(zero_ref, xs_ref.at[pl.ds(r0, MOE_BLK), :], zsem)

        def zissue(b, carry):
            @pl.when(nval_ref[b] < MOE_BLK)
            def _():
                zero_copy(b).start()
            return carry

        lax.fori_loop(0, N_BLOCKS, zissue, 0)

        def zdrain(b, carry):
            @pl.when(nval_ref[b] < MOE_BLK)
            def _():
                zero_copy(b).wait()
            return carry

        lax.fori_loop(0, N_BLOCKS, zdrain, 0)

    def row_copy(t, d):
        return pltpu.make_async_copy(src_ref.at[pl.ds(t, 1), :], xs_ref.at[pl.ds(d, 1), :], sem)

    for t in range(DISP_TG):
        row_copy(t, dest_ref[base + t]).start(priority=t % 2)
    pltpu.make_async_copy(src_ref, xs_ref.at[pl.ds(0, DISP_TG), :], sem).wait()


def _dispatch(dest, nval, hx):
    return pl.pallas_call(
        _dispatch_kernel,
        out_shape=jax.ShapeDtypeStruct((SLOT_ROWS, XROW), F32),
        grid_spec=pltpu.PrefetchScalarGridSpec(
            num_scalar_prefetch=2,
            grid=(NTOK // DISP_TG,),
            in_specs=[pl.BlockSpec((DISP_TG, XROW), lambda i, d, nv: (i, 0))],
            out_specs=pl.BlockSpec(memory_space=pl.ANY),
            scratch_shapes=[pltpu.VMEM((MOE_BLK, XROW), F32),
                            pltpu.SemaphoreType.DMA(()), pltpu.SemaphoreType.DMA(())]),
        compiler_params=_cparams("arbitrary", row_dma=True),
        name="dispatch",
    )(dest, nval, hx)


def _moe_kernel(elo_ref, ehi_ref, nval_ref, xs_ref, wg1, wu1, wd1, wg2, wu2, wd2, o_ref):
    i = pl.program_id(0)
    nval = nval_ref[i]

    def run(rows):
        x = xs_ref[0:rows, 0:D].astype(BF16)
        w_lo = xs_ref[0:rows, D:D + 1]
        w_hi = xs_ref[0:rows, D + 1:D + 2]

        def expert(wg, wu, wd):
            g = jnp.dot(x, wg[0], preferred_element_type=F32)
            u = jnp.dot(x, wu[0], preferred_element_type=F32)
            a = (g * _sigmoid(g) * u).astype(BF16)
            return jnp.dot(a, wd[0], preferred_element_type=F32)

        o_ref[0:rows, :] = w_lo * expert(wg1, wu1, wd1) + w_hi * expert(wg2, wu2, wd2)

    @pl.when(nval > MOE_BLK // 2)
    def _():
        run(MOE_BLK)

    @pl.when((nval > 0) & (nval <= MOE_BLK // 2))
    def _():
        run(MOE_BLK // 2)
        o_ref[MOE_BLK // 2:MOE_BLK, :] = jnp.zeros((MOE_BLK // 2, D), F32)

    @pl.when(nval == 0)
    def _():
        o_ref[...] = jnp.zeros_like(o_ref)


def _moe(elo, ehi, nval, xs, wg, wu, wd):
    wspec = lambda tab: pl.BlockSpec((1, D, D), lambda i, elo, ehi, nv: ((elo, ehi)[tab][i], 0, 0))
    return pl.pallas_call(
        _moe_kernel,
        out_shape=jax.ShapeDtypeStruct((SLOT_ROWS, D), F32),
        grid_spec=pltpu.PrefetchScalarGridSpec(
            num_scalar_prefetch=3,
            grid=(N_BLOCKS,),
            in_specs=[pl.BlockSpec((MOE_BLK, XROW), lambda i, elo, ehi, nv: (i, 0)),
                      wspec(0), wspec(0), wspec(0), wspec(1), wspec(1), wspec(1)],
            out_specs=pl.BlockSpec((MOE_BLK, D), lambda i, elo, ehi, nv: (i, 0))),
        compiler_params=_cparams("arbitrary"),
        name="experts",
    )(elo, ehi, nval, xs, wg, wu, wd, wg, wu, wd)


COMB_TG = 512


def _combine_kernel(dest_ref, ys_ref, x1_ref, gt_ref, nw_ref, sc_ref, sh_ref, *rest, last):
    if last:
        hn_ref, ybuf, sem = rest
    else:
        x2_ref, hn_ref, ybuf, sem = rest
    step = pl.program_id(0)
    slot = step % 2

    def gather_tile(tile, into):
        base = tile * COMB_TG

        for t in range(COMB_TG):
            src = ys_ref.at[pl.ds(dest_ref[base + t], 1), :]
            pltpu.make_async_copy(src, ybuf.at[into, pl.ds(t, 1), :], sem.at[into]).start(priority=t % 2)

    @pl.when(step == 0)
    def _():
        gather_tile(0, 0)

    @pl.when(step + 1 < pl.num_programs(0))
    def _():
        gather_tile(step + 1, 1 - slot)

    pltpu.make_async_copy(ys_ref.at[pl.ds(0, COMB_TG), :], ybuf.at[slot], sem.at[slot]).wait()

    x2 = x1_ref[...] + gt_ref[0] * ybuf[slot]
    y = _rms(x2) * nw_ref[...]
    if last:
        hn_ref[...] = y
    else:
        x2_ref[...] = x2
        hn_ref[...] = (y * (1.0 + sc_ref[0]) + sh_ref[0]).astype(hn_ref.dtype)


def _combine(dest, ys, x1, gt2, nw, sc, sh, last):
    tpb = SEQ // COMB_TG
    bidx = lambda i, d: (i // tpb, 0, 0)
    tile = pl.BlockSpec((COMB_TG, D), lambda i, d: (i, 0))
    if last:
        out_shape = jax.ShapeDtypeStruct((NTOK, D), F32)
        out_specs = tile
    else:
        out_shape = (jax.ShapeDtypeStruct((NTOK, D), F32), jax.ShapeDtypeStruct((NTOK, D), BF16))
        out_specs = (tile, tile)
    return pl.pallas_call(
        functools.partial(_combine_kernel, last=last),
        out_shape=out_shape,
        grid_spec=pltpu.PrefetchScalarGridSpec(
            num_scalar_prefetch=1,
            grid=(NTOK // COMB_TG,),
            in_specs=[pl.BlockSpec(memory_space=pl.ANY),
                      tile,
                      pl.BlockSpec((1, 1, D), bidx),
                      pl.BlockSpec((1, D), lambda i, d: (0, 0)),
                      pl.BlockSpec((1, 1, D), bidx),
                      pl.BlockSpec((1, 1, D), bidx)],
            out_specs=out_specs,
            scratch_shapes=[pltpu.VMEM((2, COMB_TG, D), F32), pltpu.SemaphoreType.DMA((2,))]),
        compiler_params=_cparams("arbitrary", row_dma=True),
        name="combine",
    )(dest, ys, x1, gt2, nw.reshape(1, D), sc, sh)


def _pair_tables():
    first = [GROUP_SIZE * g + fa for g in range(N_GROUPS) for fa, _ in PAIR_ORDER]
    second = [GROUP_SIZE * g + fb for g in range(N_GROUPS) for _, fb in PAIR_ORDER]
    return np.asarray(first, np.int32), np.asarray(second, np.int32)


def _slot_plan(ri, cnt):
    counts = cnt[:N_BUCKETS, 0].astype(jnp.int32)
    padded = (counts + MOE_BLK - 1) // MOE_BLK * MOE_BLK
    pad_end = jnp.cumsum(padded)
    pad_start = pad_end - padded
    dest = pad_start[ri[0]] + ri[1]
    blk_start = jnp.arange(N_BLOCKS, dtype=jnp.int32) * MOE_BLK
    bb = jnp.minimum(jnp.sum(blk_start[:, None] >= pad_end[None, :], axis=1), N_BUCKETS - 1).astype(jnp.int32)
    nval = jnp.clip(pad_start[bb] + counts[bb] - blk_start, 0, MOE_BLK).astype(jnp.int32)
    lo_tab, hi_tab = _pair_tables()
    return dest.astype(jnp.int32), jnp.asarray(lo_tab)[bb], jnp.asarray(hi_tab)[bb], nval


def kernel(x, c, positions, w_ada, b_ada, norm1_w, w_in, conv_w, conv_b, conv_ln_w, conv_ln_b, w_conv_out,
           w_ret_out, w_out, norm2_w, w_router, router_bias, w_exp_gate, w_exp_up, w_exp_down, final_norm_w):
    mod = _ada(c, w_ada, b_ada).reshape(DEPTH, BATCH, N_MOD, 1, D)
    sh1, sc1, gt1, sh2, sc2, gt2 = (mod[:, :, i] for i in range(N_MOD))
    cos, sin = _rope_tables(positions)
    ret_tables = _ret_tables()

    wr_t = w_router.T
    rhi = wr_t.astype(BF16)
    rlo = (wr_t - rhi.astype(F32)).astype(BF16)
    rb = jnp.broadcast_to(router_bias.astype(F32)[:, None], (N_EXPERTS, MIX_TM))
    tri = (jnp.arange(MIX_TM)[:, None] < jnp.arange(MIX_TM)[None, :]).astype(BF16)

    xf = x.reshape(NTOK, D)
    h = _modnorm(x, norm1_w[0], sc1[0], sh1[0]).reshape(NTOK, D)
    out = None
    for l in range(DEPTH):
        p = _inproj(h, w_in, l, cos, sin)
        p3 = p.reshape(BATCH, SEQ, P_COLS)
        ya, wg_bf, wu_bf, wd_bf = _conv_branch(p3, l, conv_w[l], conv_b[l], conv_ln_w[l], conv_ln_b[l], w_conv_out,
                                               (w_exp_gate, w_exp_up, w_exp_down))
        rg = _retention(p3, ret_tables)
        x1, hx, ri, cnt = _mix(rg.reshape(NTOK, HEADS * DV), ya.reshape(NTOK, D), p, xf, gt1[l], sc2[l], sh2[l],
                               w_ret_out, w_out, l, norm2_w[l], rhi, rlo, rb, tri)
        dest, elo, ehi, nval = _slot_plan(ri, cnt)
        xs = _dispatch(dest, nval, hx)
        ys = _moe(elo, ehi, nval, xs, wg_bf, wu_bf, wd_bf)
        if l + 1 < DEPTH:
            xf, h = _combine(dest, ys, x1, gt2[l], norm1_w[l + 1], sc1[l + 1], sh1[l + 1], last=False)
        else:
            out = _combine(dest, ys, x1, gt2[l], final_norm_w, sc1[l], sh1[l], last=True)
    return out.reshape(BATCH, SEQ, D)
```

```python
import functools

import numpy as np
import jax
import jax.numpy as jnp
from jax import lax
from jax.experimental import pallas as pl
from jax.experimental.pallas import tpu as pltpu

F32 = jnp.float32
BF16 = jnp.bfloat16

D = 1024
BATCH = 8
SEQ = 2048
DEPTH = 4
NTOK = BATCH * SEQ
N_MOD = 6
EPS = 1e-6

CONV_K = 31
HEADS = 4
DK = 256
DV = 512
CHUNK = 256
ROPE_BASE = 10000.0
HALF = DK // 2

N_EXPERTS = 16
N_GROUPS = 4
GROUP_SIZE = 4
PAIR_ORDER = ((0, 1), (0, 2), (0, 3), (1, 3), (1, 2), (3, 2))
PAIRS = len(PAIR_ORDER)
N_BUCKETS = N_GROUPS * PAIRS
BUCKET_ROWS = 32

P_COLS = 9 * D
PCOL_U, PCOL_Q, PCOL_K, PCOL_V, PCOL_G, PCOL_GA, PCOL_GB = 0, 1, 2, 3, 5, 7, 8

MOE_BLK = 256
N_BLOCKS = NTOK // MOE_BLK + N_BUCKETS
SLOT_ROWS = N_BLOCKS * MOE_BLK
TAIL = 128
XROW = D + TAIL

VMEM_LIMIT = 56 * 1024 * 1024

NT_DIMS = (((1,), (1,)), ((), ()))


def _cparams(*sem, row_dma=False):
    return pltpu.CompilerParams(dimension_semantics=sem, vmem_limit_bytes=VMEM_LIMIT,
                                disable_bounds_checks=row_dma)


def _sigmoid(x):
    return 0.5 * jnp.tanh(0.5 * x) + 0.5


def _rms(x):
    return x * lax.rsqrt(jnp.mean(x * x, axis=-1, keepdims=True) + EPS)


ADA_TN = 3072


def _ada_kernel(c_ref, w_ref, b_ref, o_ref):
    c = c_ref[...]
    sc = c * _sigmoid(c)
    o_ref[0] = jnp.dot(sc, w_ref[0], precision=lax.Precision.HIGHEST,
                       preferred_element_type=F32) + b_ref[0]


def _ada(c, w_ada, b_ada):
    nj = N_MOD * D // ADA_TN
    return pl.pallas_call(
        _ada_kernel,
        out_shape=jax.ShapeDtypeStruct((DEPTH, BATCH, N_MOD * D), F32),
        grid=(DEPTH, nj),
        in_specs=[pl.BlockSpec((BATCH, D), lambda l, j: (0, 0)),
                  pl.BlockSpec((1, D, ADA_TN), lambda l, j: (l, 0, j)),
                  pl.BlockSpec((1, 1, ADA_TN), lambda l, j: (l, 0, j))],
        out_specs=pl.BlockSpec((1, BATCH, ADA_TN), lambda l, j: (l, 0, j)),
        compiler_params=_cparams("parallel", "parallel"),
        name="ada",
    )(c, w_ada, b_ada.reshape(DEPTH, 1, N_MOD * D))


ROPE_TN = 2048


def _rope_kernel(pos_ref, inv_ref, cos_ref, sin_ref):
    ang = pos_ref[...] * inv_ref[...]
    cos_ref[...] = jnp.cos(ang)
    sin_ref[...] = jnp.sin(ang)


def _rope_tables(positions):
    inv = ROPE_BASE ** (-jnp.arange(HALF, dtype=F32) / HALF)
    pos = positions.astype(F32).reshape(NTOK, 1)
    return pl.pallas_call(
        _rope_kernel,
        out_shape=(jax.ShapeDtypeStruct((NTOK, HALF), F32),) * 2,
        grid=(NTOK // ROPE_TN,),
        in_specs=[pl.BlockSpec((ROPE_TN, 1), lambda i: (i, 0)),
                  pl.BlockSpec((1, HALF), lambda i: (0, 0))],
        out_specs=(pl.BlockSpec((ROPE_TN, HALF), lambda i: (i, 0)),) * 2,
        compiler_params=_cparams("parallel"),
        name="rope_tables",
    )(pos, inv.reshape(1, HALF))


NORM_TS = 512


def _modnorm_kernel(x_ref, w_ref, sc_ref, sh_ref, o_ref):
    y = _rms(x_ref[0]) * w_ref[...]
    o_ref[0] = (y * (1.0 + sc_ref[0]) + sh_ref[0]).astype(o_ref.dtype)


def _modnorm(x, w, sc, sh):
    return pl.pallas_call(
        _modnorm_kernel,
        out_shape=jax.ShapeDtypeStruct((BATCH, SEQ, D), BF16),
        grid=(BATCH, SEQ // NORM_TS),
        in_specs=[pl.BlockSpec((1, NORM_TS, D), lambda b, i: (b, i, 0)),
                  pl.BlockSpec((1, D), lambda b, i: (0, 0)),
                  pl.BlockSpec((1, 1, D), lambda b, i: (b, 0, 0)),
                  pl.BlockSpec((1, 1, D), lambda b, i: (b, 0, 0))],
        out_specs=pl.BlockSpec((1, NORM_TS, D), lambda b, i: (b, i, 0)),
        compiler_params=_cparams("parallel", "parallel"),
        name="modnorm",
    )(x, w.reshape(1, D), sc, sh)


INP_TM = 1024
INP_GROUPS = 9


def _inproj_kernel(h_ref, w1_ref, w2_ref, cos_ref, sin_ref, o_ref, wb1_ref, wb2_ref):
    j = pl.program_id(0)
    m = pl.program_id(1)
    h = h_ref[...]

    @pl.when(m == 0)
    def _():
        wb1_ref[...] = w1_ref[0].astype(BF16)

    @pl.when((m == 0) & (j == 0))
    def _():
        wb2_ref[...] = w2_ref[0].astype(BF16)

    def proj():
        return jnp.dot(h, wb1_ref[...], preferred_element_type=F32)

    @pl.when(j == 0)
    def _():
        a = proj()
        b = jnp.dot(h, wb2_ref[...], preferred_element_type=F32)
        o_ref[...] = (a * _sigmoid(b)).astype(BF16)

    @pl.when((j == 1) | (j == 2))
    def _():
        t = proj()
        scale = jnp.where(j == 2, DK ** -0.5, 1.0).astype(F32)
        cos = cos_ref[...] * scale
        sin = sin_ref[...] * scale
        for hd in range(HEADS):
            c0 = hd * DK
            t1 = t[:, c0:c0 + HALF]
            t2 = t[:, c0 + HALF:c0 + DK]
            o_ref[:, c0:c0 + HALF] = (t1 * cos - t2 * sin).astype(BF16)
            o_ref[:, c0 + HALF:c0 + DK] = (t1 * sin + t2 * cos).astype(BF16)

    @pl.when((j == 3) | (j == 4))
    def _():
        o_ref[...] = proj().astype(BF16)

    @pl.when((j == 5) | (j == 6))
    def _():
        g = proj()
        o_ref[...] = (g * _sigmoid(g)).astype(BF16)

    @pl.when(j >= 7)
    def _():
        o_ref[...] = _sigmoid(proj()).astype(BF16)


def _inproj(h, w_in, l, cos, sin):
    def rope_idx(j, m):
        return (jnp.where((j == 1) | (j == 2), m, 0), 0)

    return pl.pallas_call(
        _inproj_kernel,
        out_shape=jax.ShapeDtypeStruct((NTOK, P_COLS), BF16),
        grid=(INP_GROUPS, NTOK // INP_TM),
        in_specs=[pl.BlockSpec((INP_TM, D), lambda j, m: (m, 0)),
                  pl.BlockSpec((1, D, D), lambda j, m: (l, 0, jnp.where(j == 0, 0, j + 1))),
                  pl.BlockSpec((1, D, D), lambda j, m: (l, 0, 1), pipeline_mode=pl.Buffered(1)),
                  pl.BlockSpec((INP_TM, HALF), rope_idx),
                  pl.BlockSpec((INP_TM, HALF), rope_idx)],
        out_specs=pl.BlockSpec((INP_TM, D), lambda j, m: (m, j)),
        scratch_shapes=[pltpu.VMEM((D, D), BF16), pltpu.VMEM((D, D), BF16)],
        compiler_params=_cparams("arbitrary", "arbitrary"),
        name="inproj",
    )(h, w_in, w_in, cos, sin)


CONV_TS = 512
CONV_HALO = 32
CONV_RC = 64
CONV_CW = 128
CONV_SH = CONV_TS + 24
SUBLANES = 8


def _conv_kernel(u_ref, halo_ref, sga_ref, cw_ref, cb_ref, lnw_ref, lnb_ref, wo_ref, eg_ref, eu_ref, ed_ref,
                 o_ref, og_ref, ou_ref, od_ref, buf_ref, sh_ref, acc_ref, wbf_ref):
    i = pl.program_id(1)

    @pl.when((pl.program_id(0) == 0) & (i == 0))
    def _():
        wbf_ref[...] = wo_ref[0].astype(BF16)

    for src, dst in ((eg_ref, og_ref), (eu_ref, ou_ref), (ed_ref, od_ref)):
        dst[0] = src[0, 0].astype(BF16)

    halo = halo_ref[0].astype(F32)
    buf_ref[0:CONV_HALO, :] = jnp.where(i > 0, halo, 0.0)
    buf_ref[CONV_HALO:CONV_HALO + CONV_TS, :] = u_ref[0].astype(F32)
    for r in range(1, SUBLANES):
        sh_ref[r - 1] = buf_ref[r:r + CONV_SH, :]

    groups = CONV_RC // SUBLANES

    def body(ci, carry):
        r0 = pl.multiple_of(ci * CONV_RC, CONV_RC)
        for cc in range(D // CONV_CW):
            cols = slice(cc * CONV_CW, (cc + 1) * CONV_CW)
            accs = [cb_ref[:, cols]] * groups
            for off in range(2, CONV_K + 2):
                q, r = divmod(off, SUBLANES)
                w8 = cw_ref[off - 2, :, cols]
                for g in range(groups):
                    rows = pl.ds(r0 + SUBLANES * (q + g), SUBLANES)
                    win = buf_ref[rows, cols] if r == 0 else sh_ref[r - 1, rows, cols]
                    accs[g] = accs[g] + win * w8
            for g in range(groups):
                acc_ref[pl.ds(r0 + SUBLANES * g, SUBLANES), cols] = accs[g]
        return carry

    lax.fori_loop(0, CONV_TS // CONV_RC, body, 0)

    c = acc_ref[...]
    mu = jnp.mean(c, axis=-1, keepdims=True)
    cen = c - mu
    var = jnp.mean(cen * cen, axis=-1, keepdims=True)
    y = cen * lax.rsqrt(var + EPS) * lnw_ref[...] + lnb_ref[...]
    y = y * _sigmoid(y)
    out = jnp.dot(y.astype(BF16), wbf_ref[...], preferred_element_type=F32)
    o_ref[0] = (out * sga_ref[0].astype(F32)).astype(BF16)


def _conv_branch(p3, l, conv_w, conv_b, ln_w, ln_b, w_conv_out, w_exp):
    hb = CONV_TS // CONV_HALO
    n_i = SEQ // CONV_TS
    cw8 = jnp.broadcast_to(conv_w[:, None, :], (CONV_K, SUBLANES, D))
    cb8 = jnp.broadcast_to(conv_b[None, :], (SUBLANES, D))
    slab_rows = N_EXPERTS * D // (BATCH * n_i)
    per_mat = D // slab_rows
    assert per_mat * slab_rows == D and slab_rows % 16 == 0

    def slab(b, i):
        s = b * n_i + i
        return s // per_mat, s % per_mat

    exp_in = pl.BlockSpec((1, 1, slab_rows, D), lambda b, i: (l, *slab(b, i), 0))
    exp_out = pl.BlockSpec((1, slab_rows, D), lambda b, i: (*slab(b, i), 0))
    exp_shape = jax.ShapeDtypeStruct((N_EXPERTS, D, D), BF16)
    return pl.pallas_call(
        _conv_kernel,
        out_shape=(jax.ShapeDtypeStruct((BATCH, SEQ, D), BF16), exp_shape, exp_shape, exp_shape),
        grid=(BATCH, n_i),
        in_specs=[pl.BlockSpec((1, CONV_TS, D), lambda b, i: (b, i, PCOL_U)),
                  pl.BlockSpec((1, CONV_HALO, D), lambda b, i: (b, jnp.maximum(i * hb - 1, 0), PCOL_U)),
                  pl.BlockSpec((1, CONV_TS, D), lambda b, i: (b, i, PCOL_GA)),
                  pl.BlockSpec((CONV_K, SUBLANES, D), lambda b, i: (0, 0, 0)),
                  pl.BlockSpec((SUBLANES, D), lambda b, i: (0, 0)),
                  pl.BlockSpec((1, D), lambda b, i: (0, 0)),
                  pl.BlockSpec((1, D), lambda b, i: (0, 0)),
                  pl.BlockSpec((1, D, D), lambda b, i: (l, 0, 0), pipeline_mode=pl.Buffered(1)),
                  exp_in, exp_in, exp_in],
        out_specs=(pl.BlockSpec((1, CONV_TS, D), lambda b, i: (b, i, 0)), exp_out, exp_out, exp_out),
        scratch_shapes=[pltpu.VMEM((CONV_HALO + CONV_TS, D), F32),
                        pltpu.VMEM((SUBLANES - 1, CONV_SH, D), F32),
                        pltpu.VMEM((CONV_TS, D), F32),
                        pltpu.VMEM((D, D), BF16)],
        compiler_params=_cparams("arbitrary", "arbitrary"),
        name="conv_branch",
    )(p3, p3, p3, cw8, cb8, ln_w.reshape(1, D), ln_b.reshape(1, D), w_conv_out, *w_exp)


def _ret_tables():
    hh = jnp.arange(HEADS, dtype=F32)
    log_g = jnp.log1p(-jnp.exp2(-5.0 - hh))
    idx = jnp.arange(CHUNK, dtype=F32)
    rel = idx[:, None] - idx[None, :]
    dmask = jnp.where(rel[None] >= 0, jnp.exp(jnp.maximum(rel, 0.0)[None] * log_g[:, None, None]), 0.0)
    xi = jnp.exp((idx + 1.0)[None, :] * log_g[:, None])[..., None]
    zeta = jnp.exp((CHUNK - 1.0 - idx)[None, :] * log_g[:, None])[..., None]
    g_chunk = jnp.exp(CHUNK * log_g)[:, None, None]
    return (dmask,
            jnp.broadcast_to(xi, (HEADS, CHUNK, DK)),
            jnp.broadcast_to(zeta, (HEADS, CHUNK, DK)),
            jnp.broadcast_to(g_chunk, (HEADS, 1, DV)))


RET_TS = 512


def _ret_kernel(q_ref, k_ref, v01_ref, v23_ref, g01_ref, g23_ref, dm_ref, xi_ref, zt_ref, gc_ref, o_ref, st_ref):
    @pl.when(pl.program_id(1) == 0)
    def _():
        st_ref[...] = jnp.zeros_like(st_ref)

    def body(c, carry):
        r0 = pl.multiple_of(c * CHUNK, CHUNK)
        rows = pl.ds(r0, CHUNK)
        for hd in range(HEADS):
            v_ref, g_ref = (v01_ref, g01_ref) if hd < 2 else (v23_ref, g23_ref)
            vcols = slice((hd % 2) * DV, (hd % 2 + 1) * DV)
            qc = q_ref[0, rows, hd * DK:(hd + 1) * DK]
            kc = k_ref[0, rows, hd * DK:(hd + 1) * DK]
            vc = v_ref[0, rows, vcols]
            scores = lax.dot_general(qc, kc, NT_DIMS, preferred_element_type=F32) * dm_ref[hd]
            inner = jnp.dot(scores.astype(BF16), vc, preferred_element_type=F32)
            st = st_ref[hd]
            qx = (qc.astype(F32) * xi_ref[hd]).astype(BF16)
            cross = jnp.dot(qx, st.astype(BF16), preferred_element_type=F32)
            kzt = (kc.astype(F32) * zt_ref[hd]).T.astype(BF16)
            st_ref[hd] = st * gc_ref[hd] + jnp.dot(kzt, vc, preferred_element_type=F32)
            o = inner + cross
            mu = jnp.mean(o, axis=-1, keepdims=True)
            cen = o - mu
            var = jnp.mean(cen * cen, axis=-1, keepdims=True)
            r = cen * lax.rsqrt(var + EPS)
            o_ref[0, rows, hd * DV:(hd + 1) * DV] = (r * g_ref[0, rows, vcols].astype(F32)).astype(BF16)
        return carry

    lax.fori_loop(0, RET_TS // CHUNK, body, 0)


def _retention(p3, tables):
    dmask, xi, zeta, gch = tables
    tile = lambda col: pl.BlockSpec((1, RET_TS, D), lambda b, i: (b, i, col))
    full = lambda a: pl.BlockSpec(a.shape, lambda b, i: (0, 0, 0))
    return pl.pallas_call(
        _ret_kernel,
        out_shape=jax.ShapeDtypeStruct((BATCH, SEQ, HEADS * DV), BF16),
        grid=(BATCH, SEQ // RET_TS),
        in_specs=[tile(PCOL_Q), tile(PCOL_K), tile(PCOL_V), tile(PCOL_V + 1), tile(PCOL_G), tile(PCOL_G + 1),
                  full(dmask), full(xi), full(zeta), full(gch)],
        out_specs=pl.BlockSpec((1, RET_TS, HEADS * DV), lambda b, i: (b, i, 0)),
        scratch_shapes=[pltpu.VMEM((HEADS, DK, DV), F32)],
        compiler_params=_cparams("parallel", "arbitrary"),
        name="retention",
    )(p3, p3, p3, p3, p3, p3, dmask, xi, zeta, gch)


MIX_TM = 512


def _route_rows(s, sb):
    row = lambda a, e: a[e:e + 1, :]
    best = None
    gidx = None
    for g in range(N_GROUPS):
        v = [row(sb, GROUP_SIZE * g + i) for i in range(GROUP_SIZE)]
        pair_sums = [v[a] + v[b] for a in range(GROUP_SIZE) for b in range(a + 1, GROUP_SIZE)]
        gs = functools.reduce(jnp.maximum, pair_sums)
        if g == 0:
            best, gidx = gs, jnp.zeros(gs.shape, jnp.int32)
        else:
            upd = gs > best
            gidx = jnp.where(upd, g, gidx)
            best = jnp.where(upd, gs, best)

    def pick(a, i):
        out = row(a, i)
        for g in range(1, N_GROUPS):
            out = jnp.where(gidx == g, row(a, GROUP_SIZE * g + i), out)
        return out

    vb = [pick(sb, i) for i in range(GROUP_SIZE)]
    vs = [pick(s, i) for i in range(GROUP_SIZE)]
    m1, i1, s1 = vb[0], jnp.zeros(gidx.shape, jnp.int32), vs[0]
    for i in range(1, GROUP_SIZE):
        upd = vb[i] > m1
        m1 = jnp.where(upd, vb[i], m1)
        i1 = jnp.where(upd, i, i1)
        s1 = jnp.where(upd, vs[i], s1)
    m2 = i2 = s2 = None
    for i in range(GROUP_SIZE):
        cand = jnp.where(i1 == i, -jnp.inf, vb[i])
        if m2 is None:
            m2, i2, s2 = cand, jnp.zeros(gidx.shape, jnp.int32), vs[0]
        else:
            upd = cand > m2
            m2 = jnp.where(upd, cand, m2)
            i2 = jnp.where(upd, i, i2)
            s2 = jnp.where(upd, vs[i], s2)
    den = s1 + s2
    w1 = s1 / den
    w2 = s2 / den
    lo = jnp.minimum(i1, i2)
    hi = jnp.maximum(i1, i2)
    pair = jnp.zeros(gidx.shape, jnp.int32)
    first = jnp.zeros(gidx.shape, jnp.int32)
    for p, (fa, fb) in enumerate(PAIR_ORDER):
        hit = (lo == min(fa, fb)) & (hi == max(fa, fb))
        pair = jnp.where(hit, p, pair)
        first = jnp.where(hit, fa, first)
    bucket = gidx * PAIRS + pair
    first_is_top1 = i1 == first
    return bucket, jnp.where(first_is_top1, w1, w2), jnp.where(first_is_top1, w2, w1)


def _mix_kernel(rg_ref, ya_ref, sgb_ref, x_ref, gt_ref, sc_ref, sh_ref, wr_ref, wo_ref, n2_ref,
                rhi_ref, rlo_ref, rb_ref, tri_ref,
                x1_ref, hx_ref, ri_ref, cnt_ref, carry_ref, wrb_ref, wob_ref):
    m = pl.program_id(0)

    @pl.when(m == 0)
    def _():
        carry_ref[...] = jnp.zeros_like(carry_ref)
        wrb_ref[...] = wr_ref[0].astype(BF16)
        wob_ref[...] = wo_ref[0].astype(BF16)

    yb = jnp.dot(rg_ref[...], wrb_ref[...], preferred_element_type=F32)
    y = ya_ref[...].astype(F32) + sgb_ref[...].astype(F32) * yb
    o = jnp.dot(y.astype(BF16), wob_ref[...], preferred_element_type=F32)
    x1 = x_ref[...] + gt_ref[0] * o
    x1_ref[...] = x1
    h2 = _rms(x1) * n2_ref[...] * (1.0 + sc_ref[0]) + sh_ref[0]
    hx_ref[:, 0:D] = h2

    hi = h2.astype(BF16)
    lo = (h2 - hi.astype(F32)).astype(BF16)
    rhi = rhi_ref[...]
    logits = (lax.dot_general(rhi, hi, NT_DIMS, preferred_element_type=F32)
              + lax.dot_general(rhi, lo, NT_DIMS, preferred_element_type=F32)
              + lax.dot_general(rlo_ref[...], hi, NT_DIMS, preferred_element_type=F32))
    s = _sigmoid(logits)
    bucket, w_lo, w_hi = _route_rows(s, s + rb_ref[...])

    onehot = (lax.broadcasted_iota(jnp.int32, (BUCKET_ROWS, MIX_TM), 0) == bucket).astype(F32)
    prefix = jnp.dot(onehot.astype(BF16), tri_ref[...], preferred_element_type=F32)
    carry = carry_ref[:, 0:1]
    rank = jnp.sum(onehot * (prefix + carry), axis=0, keepdims=True)
    new_carry = carry + jnp.sum(onehot, axis=1, keepdims=True)
    carry_ref[...] = jnp.broadcast_to(new_carry, carry_ref.shape)
    cnt_ref[...] = jnp.broadcast_to(new_carry, cnt_ref.shape)

    rid = lax.broadcasted_iota(jnp.int32, (8, MIX_TM), 0)
    ri_ref[...] = jnp.where(rid == 0, bucket, jnp.where(rid == 1, rank.astype(jnp.int32), 0))
    wid = lax.broadcasted_iota(jnp.int32, (TAIL, MIX_TM), 0)
    wrows = jnp.where(wid == 0, w_lo, jnp.where(wid == 1, w_hi, 0.0))
    hx_ref[:, D:XROW] = wrows.T


def _mix(rg, ya, p, x, gt1, sc2, sh2, w_ret_out, w_out, l, norm2_w, rhi, rlo, rb, tri):
    tpb = SEQ // MIX_TM
    bidx = lambda m: (m // tpb, 0, 0)
    return pl.pallas_call(
        _mix_kernel,
        out_shape=(jax.ShapeDtypeStruct((NTOK, D), F32),
                   jax.ShapeDtypeStruct((NTOK, XROW), F32),
                   jax.ShapeDtypeStruct((8, NTOK), jnp.int32),
                   jax.ShapeDtypeStruct((BUCKET_ROWS, 128), F32)),
        grid=(NTOK // MIX_TM,),
        in_specs=[pl.BlockSpec((MIX_TM, HEADS * DV), lambda m: (m, 0)),
                  pl.BlockSpec((MIX_TM, D), lambda m: (m, 0)),
                  pl.BlockSpec((MIX_TM, D), lambda m: (m, PCOL_GB)),
                  pl.BlockSpec((MIX_TM, D), lambda m: (m, 0)),
                  pl.BlockSpec((1, 1, D), bidx),
                  pl.BlockSpec((1, 1, D), bidx),
                  pl.BlockSpec((1, 1, D), bidx),
                  pl.BlockSpec((1, HEADS * DV, D), lambda m: (l, 0, 0), pipeline_mode=pl.Buffered(1)),
                  pl.BlockSpec((1, D, D), lambda m: (l, 0, 0), pipeline_mode=pl.Buffered(1)),
                  pl.BlockSpec((1, D), lambda m: (0, 0)),
                  pl.BlockSpec((N_EXPERTS, D), lambda m: (0, 0)),
                  pl.BlockSpec((N_EXPERTS, D), lambda m: (0, 0)),
                  pl.BlockSpec((N_EXPERTS, MIX_TM), lambda m: (0, 0)),
                  pl.BlockSpec((MIX_TM, MIX_TM), lambda m: (0, 0))],
        out_specs=(pl.BlockSpec((MIX_TM, D), lambda m: (m, 0)),
                   pl.BlockSpec((MIX_TM, XROW), lambda m: (m, 0)),
                   pl.BlockSpec((8, MIX_TM), lambda m: (0, m)),
                   pl.BlockSpec((BUCKET_ROWS, 128), lambda m: (0, 0))),
        scratch_shapes=[pltpu.VMEM((BUCKET_ROWS, 128), F32),
                        pltpu.VMEM((HEADS * DV, D), BF16), pltpu.VMEM((D, D), BF16)],
        compiler_params=_cparams("arbitrary"),
        name="mix_route",
    )(rg, ya, p, x, gt1, sc2, sh2, w_ret_out, w_out, norm2_w.reshape(1, D), rhi, rlo, rb, tri)


DISP_TG = 512


def _dispatch_kernel(dest_ref, nval_ref, src_ref, xs_ref, zero_ref, sem, zsem):
    step = pl.program_id(0)
    base = step * DISP_TG

    @pl.when(step == 0)
    def _():
        zero_ref[...] = jnp.zeros_like(zero_ref)

        def zero_copy(b):
            r0 = pl.multiple_of(b * MOE_BLK, MOE_BLK)
            return pltpu.make_async_copy(zero_ref, xs_ref.at[pl.ds(r0, MOE_BLK), :], zsem)

        def zissue(b, carry):
            @pl.when(nval_ref[b] < MOE_BLK)
            def _():
                zero_copy(b).start()
            return carry

        lax.fori_loop(0, N_BLOCKS, zissue, 0)

        def zdrain(b, carry):
            @pl.when(nval_ref[b] < MOE_BLK)
            def _():
                zero_copy(b).wait()
            return carry

        lax.fori_loop(0, N_BLOCKS, zdrain, 0)

    def row_copy(t, d):
        return pltpu.make_async_copy(src_ref.at[pl.ds(t, 1), :], xs_ref.at[pl.ds(d, 1), :], sem)

    for t in range(DISP_TG):
        row_copy(t, dest_ref[base + t]).start(priority=t % 2)
    pltpu.make_async_copy(src_ref, xs_ref.at[pl.ds(0, DISP_TG), :], sem).wait()


def _dispatch(dest, nval, hx):
    return pl.pallas_call(
        _dispatch_kernel,
        out_shape=jax.ShapeDtypeStruct((SLOT_ROWS, XROW), F32),
        grid_spec=pltpu.PrefetchScalarGridSpec(
            num_scalar_prefetch=2,
            grid=(NTOK // DISP_TG,),
            in_specs=[pl.BlockSpec((DISP_TG, XROW), lambda i, d, nv: (i, 0))],
            out_specs=pl.BlockSpec(memory_space=pl.ANY),
            scratch_shapes=[pltpu.VMEM((MOE_BLK, XROW), F32),
                            pltpu.SemaphoreType.DMA(()), pltpu.SemaphoreType.DMA(())]),
        compiler_params=_cparams("arbitrary", row_dma=True),
        name="dispatch",
    )(dest, nval, hx)


def _moe_kernel(elo_ref, ehi_ref, nval_ref, xs_ref, wg1, wu1, wd1, wg2, wu2, wd2, o_ref):
    i = pl.program_id(0)
    nval = nval_ref[i]

    def run(rows):
        x = xs_ref[0:rows, 0:D].astype(BF16)
        w_lo = xs_ref[0:rows, D:D + 1]
        w_hi = xs_ref[0:rows, D + 1:D + 2]

        def expert(wg, wu, wd):
            g = jnp.dot(x, wg[0], preferred_element_type=F32)
            u = jnp.dot(x, wu[0], preferred_element_type=F32)
            a = (g * _sigmoid(g) * u).astype(BF16)
            return jnp.dot(a, wd[0], preferred_element_type=F32)

        o_ref[0:rows, :] = w_lo * expert(wg1, wu1, wd1) + w_hi * expert(wg2, wu2, wd2)

    @pl.when(nval > MOE_BLK // 2)
    def _():
        run(MOE_BLK)

    @pl.when((nval > 0) & (nval <= MOE_BLK // 2))
    def _():
        run(MOE_BLK // 2)
        o_ref[MOE_BLK // 2:MOE_BLK, :] = jnp.zeros((MOE_BLK // 2, D), F32)

    @pl.when(nval == 0)
    def _():
        o_ref[...] = jnp.zeros_like(o_ref)


def _moe(elo, ehi, nval, xs, wg, wu, wd):
    wspec = lambda tab: pl.BlockSpec((1, D, D), lambda i, elo, ehi, nv: ((elo, ehi)[tab][i], 0, 0))
    return pl.pallas_call(
        _moe_kernel,
        out_shape=jax.ShapeDtypeStruct((SLOT_ROWS, D), F32),
        grid_spec=pltpu.PrefetchScalarGridSpec(
            num_scalar_prefetch=3,
            grid=(N_BLOCKS,),
            in_specs=[pl.BlockSpec((MOE_BLK, XROW), lambda i, elo, ehi, nv: (i, 0)),
                      wspec(0), wspec(0), wspec(0), wspec(1), wspec(1), wspec(1)],
            out_specs=pl.BlockSpec((MOE_BLK, D), lambda i, elo, ehi, nv: (i, 0))),
        compiler_params=_cparams("arbitrary"),
        name="experts",
    )(elo, ehi, nval, xs, wg, wu, wd, wg, wu, wd)


COMB_TG = 512


def _combine_kernel(dest_ref, ys_ref, x1_ref, gt_ref, nw_ref, sc_ref, sh_ref, *rest, last):
    if last:
        hn_ref, ybuf, sem = rest
    else:
        x2_ref, hn_ref, ybuf, sem = rest
    step = pl.program_id(0)
    slot = step % 2

    def gather_tile(tile, into):
        base = tile * COMB_TG

        for t in range(COMB_TG):
            src = ys_ref.at[pl.ds(dest_ref[base + t], 1), :]
            pltpu.make_async_copy(src, ybuf.at[into, pl.ds(t, 1), :], sem.at[into]).start(priority=t % 2)

    @pl.when(step == 0)
    def _():
        gather_tile(0, 0)

    @pl.when(step + 1 < pl.num_programs(0))
    def _():
        gather_tile(step + 1, 1 - slot)

    pltpu.make_async_copy(ys_ref.at[pl.ds(0, COMB_TG), :], ybuf.at[slot], sem.at[slot]).wait()

    x2 = x1_ref[...] + gt_ref[0] * ybuf[slot]
    y = _rms(x2) * nw_ref[...]
    if last:
        hn_ref[...] = y
    else:
        x2_ref[...] = x2
        hn_ref[...] = (y * (1.0 + sc_ref[0]) + sh_ref[0]).astype(hn_ref.dtype)


def _combine(dest, ys, x1, gt2, nw, sc, sh, last):
    tpb = SEQ // COMB_TG
    bidx = lambda i, d: (i // tpb, 0, 0)
    tile = pl.BlockSpec((COMB_TG, D), lambda i, d: (i, 0))
    if last:
        out_shape = jax.ShapeDtypeStruct((NTOK, D), F32)
        out_specs = tile
    else:
        out_shape = (jax.ShapeDtypeStruct((NTOK, D), F32), jax.ShapeDtypeStruct((NTOK, D), BF16))
        out_specs = (tile, tile)
    return pl.pallas_call(
        functools.partial(_combine_kernel, last=last),
        out_shape=out_shape,
        grid_spec=pltpu.PrefetchScalarGridSpec(
            num_scalar_prefetch=1,
            grid=(NTOK // COMB_TG,),
            in_specs=[pl.BlockSpec(memory_space=pl.ANY),
                      tile,
                      pl.BlockSpec((1, 1, D), bidx),
                      pl.BlockSpec((1, D), lambda i, d: (0, 0)),
                      pl.BlockSpec((1, 1, D), bidx),
                      pl.BlockSpec((1, 1, D), bidx)],
            out_specs=out_specs,
            scratch_shapes=[pltpu.VMEM((2, COMB_TG, D), F32), pltpu.SemaphoreType.DMA((2,))]),
        compiler_params=_cparams("arbitrary", row_dma=True),
        name="combine",
    )(dest, ys, x1, gt2, nw.reshape(1, D), sc, sh)


def _pair_tables():
    first = [GROUP_SIZE * g + fa for g in range(N_GROUPS) for fa, _ in PAIR_ORDER]
    second = [GROUP_SIZE * g + fb for g in range(N_GROUPS) for _, fb in PAIR_ORDER]
    return np.asarray(first, np.int32), np.asarray(second, np.int32)


def _plan_kernel(cnt_ref, first_ref, second_ref, ri_ref, dest_ref, elo_ref, ehi_ref, nval_ref, start_ref):
    def per_bucket(b, carry):
        start, blk = carry
        count = cnt_ref[b]
        n_blk = (count + MOE_BLK - 1) // MOE_BLK
        start_ref[b] = start

        def per_block(k, c):
            elo_ref[blk + k] = first_ref[b]
            ehi_ref[blk + k] = second_ref[b]
            nval_ref[blk + k] = jnp.minimum(count - k * MOE_BLK, MOE_BLK)
            return c

        lax.fori_loop(0, n_blk, per_block, 0)
        return start + n_blk * MOE_BLK, blk + n_blk

    _, used = lax.fori_loop(0, N_BUCKETS, per_bucket, (jnp.int32(0), jnp.int32(0)))

    def unused_block(k, c):
        elo_ref[k] = first_ref[N_BUCKETS - 1]
        ehi_ref[k] = second_ref[N_BUCKETS - 1]
        nval_ref[k] = 0
        return c

    lax.fori_loop(used, N_BLOCKS, unused_block, 0)

    bucket = ri_ref[0:1, :]
    rank = ri_ref[1:2, :]
    dest = jnp.zeros_like(rank)
    for b in range(N_BUCKETS):
        dest = jnp.where(bucket == b, start_ref[b] + rank, dest)
    dest_ref[...] = dest


def _slot_plan(ri, cnt):
    counts = cnt[:, 0].astype(jnp.int32)
    first_tab, second_tab = _pair_tables()
    smem_out = pl.BlockSpec(memory_space=pltpu.SMEM)
    blocks = jax.ShapeDtypeStruct((N_BLOCKS,), jnp.int32)
    dest, elo, ehi, nval = pl.pallas_call(
        _plan_kernel,
        out_shape=(jax.ShapeDtypeStruct((1, NTOK), jnp.int32), blocks, blocks, blocks),
        grid_spec=pltpu.PrefetchScalarGridSpec(
            num_scalar_prefetch=3,
            grid=(1,),
            in_specs=[pl.BlockSpec((8, NTOK), lambda i, c, f, s: (0, 0))],
            out_specs=(pl.BlockSpec((1, NTOK), lambda i, c, f, s: (0, 0)), smem_out, smem_out, smem_out),
            scratch_shapes=[pltpu.SMEM((BUCKET_ROWS,), jnp.int32)]),
        compiler_params=_cparams("arbitrary"),
        name="slot_plan",
    )(counts, jnp.asarray(first_tab), jnp.asarray(second_tab), ri)
    return dest.reshape(NTOK), elo, ehi, nval


def kernel(x, c, positions, w_ada, b_ada, norm1_w, w_in, conv_w, conv_b, conv_ln_w, conv_ln_b, w_conv_out,
           w_ret_out, w_out, norm2_w, w_router, router_bias, w_exp_gate, w_exp_up, w_exp_down, final_norm_w):
    mod = _ada(c, w_ada, b_ada).reshape(DEPTH, BATCH, N_MOD, 1, D)
    sh1, sc1, gt1, sh2, sc2, gt2 = (mod[:, :, i] for i in range(N_MOD))
    cos, sin = _rope_tables(positions)
    ret_tables = _ret_tables()

    wr_t = w_router.T
    rhi = wr_t.astype(BF16)
    rlo = (wr_t - rhi.astype(F32)).astype(BF16)
    rb = jnp.broadcast_to(router_bias.astype(F32)[:, None], (N_EXPERTS, MIX_TM))
    tri = (jnp.arange(MIX_TM)[:, None] < jnp.arange(MIX_TM)[None, :]).astype(BF16)

    xf = x.reshape(NTOK, D)
    h = _modnorm(x, norm1_w[0], sc1[0], sh1[0]).reshape(NTOK, D)
    out = None
    for l in range(DEPTH):
        p = _inproj(h, w_in, l, cos, sin)
        p3 = p.reshape(BATCH, SEQ, P_COLS)
        ya, wg_bf, wu_bf, wd_bf = _conv_branch(p3, l, conv_w[l], conv_b[l], conv_ln_w[l], conv_ln_b[l], w_conv_out,
                                               (w_exp_gate, w_exp_up, w_exp_down))
        rg = _retention(p3, ret_tables)
        x1, hx, ri, cnt = _mix(rg.reshape(NTOK, HEADS * DV), ya.reshape(NTOK, D), p, xf, gt1[l], sc2[l], sh2[l],
                               w_ret_out, w_out, l, norm2_w[l], rhi, rlo, rb, tri)
        dest, elo, ehi, nval = _slot_plan(ri, cnt)
        xs = _dispatch(dest, nval, hx)
        ys = _moe(elo, ehi, nval, xs, wg_bf, wu_bf, wd_bf)
        if l + 1 < DEPTH:
            xf, h = _combine(dest, ys, x1, gt2[l], norm1_w[l + 1], sc1[l + 1], sh1[l + 1], last=False)
        else:
            out = _combine(dest, ys, x1, gt2[l], final_norm_w, sc1[l], sh1[l], last=True)
    return out.reshape(BATCH, SEQ, D)
```

```python
import functools

import numpy as np
import jax
import jax.numpy as jnp
from jax import lax
from jax.experimental import pallas as pl
from jax.experimental.pallas import tpu as pltpu

F32 = jnp.float32
BF16 = jnp.bfloat16

D = 1024
BATCH = 8
SEQ = 2048
DEPTH = 4
NTOK = BATCH * SEQ
N_MOD = 6
EPS = 1e-6

CONV_K = 31
HEADS = 4
DK = 256
DV = 512
CHUNK = 256
ROPE_BASE = 10000.0
HALF = DK // 2

N_EXPERTS = 16
N_GROUPS = 4
GROUP_SIZE = 4
PAIR_ORDER = ((0, 1), (0, 2), (0, 3), (1, 3), (1, 2), (3, 2))
PAIRS = len(PAIR_ORDER)
N_BUCKETS = N_GROUPS * PAIRS
BUCKET_ROWS = 32

P_COLS = 9 * D
PCOL_U, PCOL_Q, PCOL_K, PCOL_V, PCOL_G, PCOL_GA, PCOL_GB = 0, 1, 2, 3, 5, 7, 8

MOE_BLK = 256
N_BLOCKS = NTOK // MOE_BLK + N_BUCKETS
SLOT_ROWS = N_BLOCKS * MOE_BLK
TAIL = 128
XROW = D + TAIL

VMEM_LIMIT = 56 * 1024 * 1024

NT_DIMS = (((1,), (1,)), ((), ()))


def _cparams(*sem, row_dma=False):
    return pltpu.CompilerParams(dimension_semantics=sem, vmem_limit_bytes=VMEM_LIMIT,
                                disable_bounds_checks=row_dma)


def _sigmoid(x):
    return 0.5 * jnp.tanh(0.5 * x) + 0.5


def _rms(x):
    return x * lax.rsqrt(jnp.mean(x * x, axis=-1, keepdims=True) + EPS)


ADA_TN = 3072


def _ada_kernel(c_ref, w_ref, b_ref, o_ref):
    c = c_ref[...]
    sc = c * _sigmoid(c)
    o_ref[0] = jnp.dot(sc, w_ref[0], precision=lax.Precision.HIGHEST,
                       preferred_element_type=F32) + b_ref[0]


def _ada(c, w_ada, b_ada):
    nj = N_MOD * D // ADA_TN
    return pl.pallas_call(
        _ada_kernel,
        out_shape=jax.ShapeDtypeStruct((DEPTH, BATCH, N_MOD * D), F32),
        grid=(DEPTH, nj),
        in_specs=[pl.BlockSpec((BATCH, D), lambda l, j: (0, 0)),
                  pl.BlockSpec((1, D, ADA_TN), lambda l, j: (l, 0, j)),
                  pl.BlockSpec((1, 1, ADA_TN), lambda l, j: (l, 0, j))],
        out_specs=pl.BlockSpec((1, BATCH, ADA_TN), lambda l, j: (l, 0, j)),
        compiler_params=_cparams("parallel", "parallel"),
        name="ada",
    )(c, w_ada, b_ada.reshape(DEPTH, 1, N_MOD * D))


NORM_TS = 512


def _modnorm_kernel(x_ref, w_ref, sc_ref, sh_ref, pos_ref, inv_ref, o_ref, cos_ref, sin_ref):
    y = _rms(x_ref[0]) * w_ref[...]
    o_ref[0] = (y * (1.0 + sc_ref[0]) + sh_ref[0]).astype(o_ref.dtype)
    ang = pos_ref[0] * inv_ref[...]
    cos_ref[0] = jnp.cos(ang)
    sin_ref[0] = jnp.sin(ang)


def _modnorm(x, w, sc, sh, positions):
    inv = ROPE_BASE ** (-jnp.arange(HALF, dtype=F32) / HALF)
    pos = positions.astype(F32).reshape(BATCH, SEQ, 1)
    tile = lambda width: pl.BlockSpec((1, NORM_TS, width), lambda b, i: (b, i, 0))
    table = jax.ShapeDtypeStruct((BATCH, SEQ, HALF), F32)
    h, cos, sin = pl.pallas_call(
        _modnorm_kernel,
        out_shape=(jax.ShapeDtypeStruct((BATCH, SEQ, D), BF16), table, table),
        grid=(BATCH, SEQ // NORM_TS),
        in_specs=[tile(D),
                  pl.BlockSpec((1, D), lambda b, i: (0, 0)),
                  pl.BlockSpec((1, 1, D), lambda b, i: (b, 0, 0)),
                  pl.BlockSpec((1, 1, D), lambda b, i: (b, 0, 0)),
                  tile(1),
                  pl.BlockSpec((1, HALF), lambda b, i: (0, 0))],
        out_specs=(tile(D), tile(HALF), tile(HALF)),
        compiler_params=_cparams("parallel", "parallel"),
        name="modnorm",
    )(x, w.reshape(1, D), sc, sh, pos, inv.reshape(1, HALF))
    return h.reshape(NTOK, D), cos.reshape(NTOK, HALF), sin.reshape(NTOK, HALF)


INP_TM = 1024
INP_GROUPS = 9


def _inproj_kernel(h_ref, w1_ref, w2_ref, cos_ref, sin_ref, o_ref, wb1_ref, wb2_ref):
    j = pl.program_id(0)
    m = pl.program_id(1)
    h = h_ref[...]

    @pl.when(m == 0)
    def _():
        wb1_ref[...] = w1_ref[0].astype(BF16)

    @pl.when((m == 0) & (j == 0))
    def _():
        wb2_ref[...] = w2_ref[0].astype(BF16)

    def proj():
        return jnp.dot(h, wb1_ref[...], preferred_element_type=F32)

    @pl.when(j == 0)
    def _():
        a = proj()
        b = jnp.dot(h, wb2_ref[...], preferred_element_type=F32)
        o_ref[...] = (a * _sigmoid(b)).astype(BF16)

    @pl.when((j == 1) | (j == 2))
    def _():
        t = proj()
        scale = jnp.where(j == 2, DK ** -0.5, 1.0).astype(F32)
        cos = cos_ref[...] * scale
        sin = sin_ref[...] * scale
        for hd in range(HEADS):
            c0 = hd * DK
            t1 = t[:, c0:c0 + HALF]
            t2 = t[:, c0 + HALF:c0 + DK]
            o_ref[:, c0:c0 + HALF] = (t1 * cos - t2 * sin).astype(BF16)
            o_ref[:, c0 + HALF:c0 + DK] = (t1 * sin + t2 * cos).astype(BF16)

    @pl.when((j == 3) | (j == 4))
    def _():
        o_ref[...] = proj().astype(BF16)

    @pl.when((j == 5) | (j == 6))
    def _():
        g = proj()
        o_ref[...] = (g * _sigmoid(g)).astype(BF16)

    @pl.when(j >= 7)
    def _():
        o_ref[...] = _sigmoid(proj()).astype(BF16)


def _inproj(h, w_in, l, cos, sin):
    def rope_idx(j, m):
        return (jnp.where((j == 1) | (j == 2), m, 0), 0)

    return pl.pallas_call(
        _inproj_kernel,
        out_shape=jax.ShapeDtypeStruct((NTOK, P_COLS), BF16),
        grid=(INP_GROUPS, NTOK // INP_TM),
        in_specs=[pl.BlockSpec((INP_TM, D), lambda j, m: (m, 0)),
                  pl.BlockSpec((1, D, D), lambda j, m: (l, 0, jnp.where(j == 0, 0, j + 1))),
                  pl.BlockSpec((1, D, D), lambda j, m: (l, 0, 1), pipeline_mode=pl.Buffered(1)),
                  pl.BlockSpec((INP_TM, HALF), rope_idx),
                  pl.BlockSpec((INP_TM, HALF), rope_idx)],
        out_specs=pl.BlockSpec((INP_TM, D), lambda j, m: (m, j)),
        scratch_shapes=[pltpu.VMEM((D, D), BF16), pltpu.VMEM((D, D), BF16)],
        compiler_params=_cparams("arbitrary", "arbitrary"),
        name="inproj",
    )(h, w_in, w_in, cos, sin)


CONV_TS = 512
CONV_HALO = 32
CONV_RC = 32
CONV_CW = 128
CONV_SH = CONV_TS + 24
SUBLANES = 8


def _conv_kernel(u_ref, halo_ref, sga_ref, cw_ref, cb_ref, lnw_ref, lnb_ref, wo_ref, eg_ref, eu_ref, ed_ref,
                 o_ref, og_ref, ou_ref, od_ref, buf_ref, sh_ref, acc_ref, wbf_ref):
    i = pl.program_id(1)

    @pl.when((pl.program_id(0) == 0) & (i == 0))
    def _():
        wbf_ref[...] = wo_ref[0].astype(BF16)

    for src, dst in ((eg_ref, og_ref), (eu_ref, ou_ref), (ed_ref, od_ref)):
        dst[0] = src[0, 0].astype(BF16)

    halo = halo_ref[0].astype(F32)
    buf_ref[0:CONV_HALO, :] = jnp.where(i > 0, halo, 0.0)
    buf_ref[CONV_HALO:CONV_HALO + CONV_TS, :] = u_ref[0].astype(F32)
    for r in range(1, SUBLANES):
        sh_ref[r - 1] = buf_ref[r:r + CONV_SH, :]

    groups = CONV_RC // SUBLANES

    def body(ci, carry):
        r0 = pl.multiple_of(ci * CONV_RC, CONV_RC)
        for cc in range(D // CONV_CW):
            cols = slice(cc * CONV_CW, (cc + 1) * CONV_CW)
            accs = [cb_ref[:, cols]] * groups
            for off in range(2, CONV_K + 2):
                q, r = divmod(off, SUBLANES)
                w8 = cw_ref[off - 2, :, cols]
                for g in range(groups):
                    rows = pl.ds(r0 + SUBLANES * (q + g), SUBLANES)
                    win = buf_ref[rows, cols] if r == 0 else sh_ref[r - 1, rows, cols]
                    accs[g] = accs[g] + win * w8
            for g in range(groups):
                acc_ref[pl.ds(r0 + SUBLANES * g, SUBLANES), cols] = accs[g]
        return carry

    lax.fori_loop(0, CONV_TS // CONV_RC, body, 0)

    c = acc_ref[...]
    mu = jnp.mean(c, axis=-1, keepdims=True)
    cen = c - mu
    var = jnp.mean(cen * cen, axis=-1, keepdims=True)
    y = cen * lax.rsqrt(var + EPS) * lnw_ref[...] + lnb_ref[...]
    y = y * _sigmoid(y)
    out = jnp.dot(y.astype(BF16), wbf_ref[...], preferred_element_type=F32)
    o_ref[0] = (out * sga_ref[0].astype(F32)).astype(BF16)


def _conv_branch(p3, l, conv_w, conv_b, ln_w, ln_b, w_conv_out, w_exp):
    hb = CONV_TS // CONV_HALO
    n_i = SEQ // CONV_TS
    cw8 = jnp.broadcast_to(conv_w[:, None, :], (CONV_K, SUBLANES, D))
    cb8 = jnp.broadcast_to(conv_b[None, :], (SUBLANES, D))
    slab_rows = N_EXPERTS * D // (BATCH * n_i)
    per_mat = D // slab_rows
    assert per_mat * slab_rows == D and slab_rows % 16 == 0

    def slab(b, i):
        s = b * n_i + i
        return s // per_mat, s % per_mat

    exp_in = pl.BlockSpec((1, 1, slab_rows, D), lambda b, i: (l, *slab(b, i), 0))
    exp_out = pl.BlockSpec((1, slab_rows, D), lambda b, i: (*slab(b, i), 0))
    exp_shape = jax.ShapeDtypeStruct((N_EXPERTS, D, D), BF16)
    return pl.pallas_call(
        _conv_kernel,
        out_shape=(jax.ShapeDtypeStruct((BATCH, SEQ, D), BF16), exp_shape, exp_shape, exp_shape),
        grid=(BATCH, n_i),
        in_specs=[pl.BlockSpec((1, CONV_TS, D), lambda b, i: (b, i, PCOL_U)),
                  pl.BlockSpec((1, CONV_HALO, D), lambda b, i: (b, jnp.maximum(i * hb - 1, 0), PCOL_U)),
                  pl.BlockSpec((1, CONV_TS, D), lambda b, i: (b, i, PCOL_GA)),
                  pl.BlockSpec((CONV_K, SUBLANES, D), lambda b, i: (0, 0, 0)),
                  pl.BlockSpec((SUBLANES, D), lambda b, i: (0, 0)),
                  pl.BlockSpec((1, D), lambda b, i: (0, 0)),
                  pl.BlockSpec((1, D), lambda b, i: (0, 0)),
                  pl.BlockSpec((1, D, D), lambda b, i: (l, 0, 0), pipeline_mode=pl.Buffered(1)),
                  exp_in, exp_in, exp_in],
        out_specs=(pl.BlockSpec((1, CONV_TS, D), lambda b, i: (b, i, 0)), exp_out, exp_out, exp_out),
        scratch_shapes=[pltpu.VMEM((CONV_HALO + CONV_TS, D), F32),
                        pltpu.VMEM((SUBLANES - 1, CONV_SH, D), F32),
                        pltpu.VMEM((CONV_TS, D), F32),
                        pltpu.VMEM((D, D), BF16)],
        compiler_params=_cparams("arbitrary", "arbitrary"),
        name="conv_branch",
    )(p3, p3, p3, cw8, cb8, ln_w.reshape(1, D), ln_b.reshape(1, D), w_conv_out, *w_exp)


def _ret_tables():
    hh = jnp.arange(HEADS, dtype=F32)
    log_g = jnp.log1p(-jnp.exp2(-5.0 - hh))
    idx = jnp.arange(CHUNK, dtype=F32)
    rel = idx[:, None] - idx[None, :]
    dmask = jnp.where(rel[None] >= 0, jnp.exp(jnp.maximum(rel, 0.0)[None] * log_g[:, None, None]), 0.0)
    xi = jnp.exp((idx + 1.0)[None, :] * log_g[:, None])[..., None]
    zeta = jnp.exp((CHUNK - 1.0 - idx)[None, :] * log_g[:, None])[..., None]
    g_chunk = jnp.exp(CHUNK * log_g)[:, None, None]
    return (dmask,
            jnp.broadcast_to(xi, (HEADS, CHUNK, DK)),
            jnp.broadcast_to(zeta, (HEADS, CHUNK, DK)),
            jnp.broadcast_to(g_chunk, (HEADS, 1, DV)))


RET_TS = 512


def _ret_kernel(q_ref, k_ref, v01_ref, v23_ref, g01_ref, g23_ref, dm_ref, xi_ref, zt_ref, gc_ref, o_ref, st_ref):
    @pl.when(pl.program_id(1) == 0)
    def _():
        st_ref[...] = jnp.zeros_like(st_ref)

    def body(c, carry):
        r0 = pl.multiple_of(c * CHUNK, CHUNK)
        rows = pl.ds(r0, CHUNK)
        for hd in range(HEADS):
            v_ref, g_ref = (v01_ref, g01_ref) if hd < 2 else (v23_ref, g23_ref)
            vcols = slice((hd % 2) * DV, (hd % 2 + 1) * DV)
            qc = q_ref[0, rows, hd * DK:(hd + 1) * DK]
            kc = k_ref[0, rows, hd * DK:(hd + 1) * DK]
            vc = v_ref[0, rows, vcols]
            scores = lax.dot_general(qc, kc, NT_DIMS, preferred_element_type=F32) * dm_ref[hd]
            inner = jnp.dot(scores.astype(BF16), vc, preferred_element_type=F32)
            st = st_ref[hd]
            qx = (qc.astype(F32) * xi_ref[hd]).astype(BF16)
            cross = jnp.dot(qx, st.astype(BF16), preferred_element_type=F32)
            kzt = (kc.astype(F32) * zt_ref[hd]).T.astype(BF16)
            st_ref[hd] = st * gc_ref[hd] + jnp.dot(kzt, vc, preferred_element_type=F32)
            o = inner + cross
            mu = jnp.mean(o, axis=-1, keepdims=True)
            cen = o - mu
            var = jnp.mean(cen * cen, axis=-1, keepdims=True)
            r = cen * lax.rsqrt(var + EPS)
            o_ref[0, rows, hd * DV:(hd + 1) * DV] = (r * g_ref[0, rows, vcols].astype(F32)).astype(BF16)
        return carry

    lax.fori_loop(0, RET_TS // CHUNK, body, 0)


def _retention(p3, tables):
    dmask, xi, zeta, gch = tables
    tile = lambda col: pl.BlockSpec((1, RET_TS, D), lambda b, i: (b, i, col))
    full = lambda a: pl.BlockSpec(a.shape, lambda b, i: (0, 0, 0))
    return pl.pallas_call(
        _ret_kernel,
        out_shape=jax.ShapeDtypeStruct((BATCH, SEQ, HEADS * DV), BF16),
        grid=(BATCH, SEQ // RET_TS),
        in_specs=[tile(PCOL_Q), tile(PCOL_K), tile(PCOL_V), tile(PCOL_V + 1), tile(PCOL_G), tile(PCOL_G + 1),
                  full(dmask), full(xi), full(zeta), full(gch)],
        out_specs=pl.BlockSpec((1, RET_TS, HEADS * DV), lambda b, i: (b, i, 0)),
        scratch_shapes=[pltpu.VMEM((HEADS, DK, DV), F32)],
        compiler_params=_cparams("parallel", "arbitrary"),
        name="retention",
    )(p3, p3, p3, p3, p3, p3, dmask, xi, zeta, gch)


MIX_TM = 512


def _route_rows(s, sb):
    row = lambda a, e: a[e:e + 1, :]
    best = None
    gidx = None
    for g in range(N_GROUPS):
        v = [row(sb, GROUP_SIZE * g + i) for i in range(GROUP_SIZE)]
        pair_sums = [v[a] + v[b] for a in range(GROUP_SIZE) for b in range(a + 1, GROUP_SIZE)]
        gs = functools.reduce(jnp.maximum, pair_sums)
        if g == 0:
            best, gidx = gs, jnp.zeros(gs.shape, jnp.int32)
        else:
            upd = gs > best
            gidx = jnp.where(upd, g, gidx)
            best = jnp.where(upd, gs, best)

    def pick(a, i):
        out = row(a, i)
        for g in range(1, N_GROUPS):
            out = jnp.where(gidx == g, row(a, GROUP_SIZE * g + i), out)
        return out

    vb = [pick(sb, i) for i in range(GROUP_SIZE)]
    vs = [pick(s, i) for i in range(GROUP_SIZE)]
    m1, i1, s1 = vb[0], jnp.zeros(gidx.shape, jnp.int32), vs[0]
    for i in range(1, GROUP_SIZE):
        upd = vb[i] > m1
        m1 = jnp.where(upd, vb[i], m1)
        i1 = jnp.where(upd, i, i1)
        s1 = jnp.where(upd, vs[i], s1)
    m2 = i2 = s2 = None
    for i in range(GROUP_SIZE):
        cand = jnp.where(i1 == i, -jnp.inf, vb[i])
        if m2 is None:
            m2, i2, s2 = cand, jnp.zeros(gidx.shape, jnp.int32), vs[0]
        else:
            upd = cand > m2
            m2 = jnp.where(upd, cand, m2)
            i2 = jnp.where(upd, i, i2)
            s2 = jnp.where(upd, vs[i], s2)
    den = s1 + s2
    w1 = s1 / den
    w2 = s2 / den
    lo = jnp.minimum(i1, i2)
    hi = jnp.maximum(i1, i2)
    pair = jnp.zeros(gidx.shape, jnp.int32)
    first = jnp.zeros(gidx.shape, jnp.int32)
    for p, (fa, fb) in enumerate(PAIR_ORDER):
        hit = (lo == min(fa, fb)) & (hi == max(fa, fb))
        pair = jnp.where(hit, p, pair)
        first = jnp.where(hit, fa, first)
    bucket = gidx * PAIRS + pair
    first_is_top1 = i1 == first
    return bucket, jnp.where(first_is_top1, w1, w2), jnp.where(first_is_top1, w2, w1)


def _mix_kernel(rg_ref, ya_ref, sgb_ref, x_ref, gt_ref, sc_ref, sh_ref, wr_ref, wo_ref, n2_ref,
                rhi_ref, rlo_ref, rb_ref, tri_ref,
                x1_ref, hx_ref, ri_ref, cnt_ref, carry_ref, wrb_ref, wob_ref):
    m = pl.program_id(0)

    @pl.when(m == 0)
    def _():
        carry_ref[...] = jnp.zeros_like(carry_ref)
        wrb_ref[...] = wr_ref[0].astype(BF16)
        wob_ref[...] = wo_ref[0].astype(BF16)

    yb = jnp.dot(rg_ref[...], wrb_ref[...], preferred_element_type=F32)
    y = ya_ref[...].astype(F32) + sgb_ref[...].astype(F32) * yb
    o = jnp.dot(y.astype(BF16), wob_ref[...], preferred_element_type=F32)
    x1 = x_ref[...] + gt_ref[0] * o
    x1_ref[...] = x1
    h2 = _rms(x1) * n2_ref[...] * (1.0 + sc_ref[0]) + sh_ref[0]
    hx_ref[:, 0:D] = h2

    hi = h2.astype(BF16)
    lo = (h2 - hi.astype(F32)).astype(BF16)
    rhi = rhi_ref[...]
    logits = (lax.dot_general(rhi, hi, NT_DIMS, preferred_element_type=F32)
              + lax.dot_general(rhi, lo, NT_DIMS, preferred_element_type=F32)
              + lax.dot_general(rlo_ref[...], hi, NT_DIMS, preferred_element_type=F32))
    s = _sigmoid(logits)
    bucket, w_lo, w_hi = _route_rows(s, s + rb_ref[...])

    onehot = (lax.broadcasted_iota(jnp.int32, (BUCKET_ROWS, MIX_TM), 0) == bucket).astype(F32)
    prefix = jnp.dot(onehot.astype(BF16), tri_ref[...], preferred_element_type=F32)
    carry = carry_ref[:, 0:1]
    rank = jnp.sum(onehot * (prefix + carry), axis=0, keepdims=True)
    new_carry = carry + jnp.sum(onehot, axis=1, keepdims=True)
    carry_ref[...] = jnp.broadcast_to(new_carry, carry_ref.shape)
    cnt_ref[...] = jnp.broadcast_to(new_carry, cnt_ref.shape)

    rid = lax.broadcasted_iota(jnp.int32, (8, MIX_TM), 0)
    ri_ref[...] = jnp.where(rid == 0, bucket, jnp.where(rid == 1, rank.astype(jnp.int32), 0))
    wid = lax.broadcasted_iota(jnp.int32, (TAIL, MIX_TM), 0)
    wrows = jnp.where(wid == 0, w_lo, jnp.where(wid == 1, w_hi, 0.0))
    hx_ref[:, D:XROW] = wrows.T


def _mix(rg, ya, p, x, gt1, sc2, sh2, w_ret_out, w_out, l, norm2_w, rhi, rlo, rb, tri):
    tpb = SEQ // MIX_TM
    bidx = lambda m: (m // tpb, 0, 0)
    return pl.pallas_call(
        _mix_kernel,
        out_shape=(jax.ShapeDtypeStruct((NTOK, D), F32),
                   jax.ShapeDtypeStruct((NTOK, XROW), F32),
                   jax.ShapeDtypeStruct((8, NTOK), jnp.int32),
                   jax.ShapeDtypeStruct((BUCKET_ROWS, 128), F32)),
        grid=(NTOK // MIX_TM,),
        in_specs=[pl.BlockSpec((MIX_TM, HEADS * DV), lambda m: (m, 0)),
                  pl.BlockSpec((MIX_TM, D), lambda m: (m, 0)),
                  pl.BlockSpec((MIX_TM, D), lambda m: (m, PCOL_GB)),
                  pl.BlockSpec((MIX_TM, D), lambda m: (m, 0)),
                  pl.BlockSpec((1, 1, D), bidx),
                  pl.BlockSpec((1, 1, D), bidx),
                  pl.BlockSpec((1, 1, D), bidx),
                  pl.BlockSpec((1, HEADS * DV, D), lambda m: (l, 0, 0), pipeline_mode=pl.Buffered(1)),
                  pl.BlockSpec((1, D, D), lambda m: (l, 0, 0), pipeline_mode=pl.Buffered(1)),
                  pl.BlockSpec((1, D), lambda m: (0, 0)),
                  pl.BlockSpec((N_EXPERTS, D), lambda m: (0, 0)),
                  pl.BlockSpec((N_EXPERTS, D), lambda m: (0, 0)),
                  pl.BlockSpec((N_EXPERTS, MIX_TM), lambda m: (0, 0)),
                  pl.BlockSpec((MIX_TM, MIX_TM), lambda m: (0, 0))],
        out_specs=(pl.BlockSpec((MIX_TM, D), lambda m: (m, 0)),
                   pl.BlockSpec((MIX_TM, XROW), lambda m: (m, 0)),
                   pl.BlockSpec((8, MIX_TM), lambda m: (0, m)),
                   pl.BlockSpec((BUCKET_ROWS, 128), lambda m: (0, 0))),
        scratch_shapes=[pltpu.VMEM((BUCKET_ROWS, 128), F32),
                        pltpu.VMEM((HEADS * DV, D), BF16), pltpu.VMEM((D, D), BF16)],
        compiler_params=_cparams("arbitrary"),
        name="mix_route",
    )(rg, ya, p, x, gt1, sc2, sh2, w_ret_out, w_out, norm2_w.reshape(1, D), rhi, rlo, rb, tri)


DISP_TG = 512


def _dispatch_kernel(dest_ref, nval_ref, src_ref, xs_ref, zero_ref, sem, zsem):
    step = pl.program_id(0)
    base = step * DISP_TG

    @pl.when(step == 0)
    def _():
        zero_ref[...] = jnp.zeros_like(zero_ref)

        def zero_copy(b):
            r0 = pl.multiple_of(b * MOE_BLK, MOE_BLK)
            return pltpu.make_async_copy(zero_ref, xs_ref.at[pl.ds(r0, MOE_BLK), :], zsem)

        def zissue(b, carry):
            @pl.when(nval_ref[b] < MOE_BLK)
            def _():
                zero_copy(b).start()
            return carry

        lax.fori_loop(0, N_BLOCKS, zissue, 0)

        def zdrain(b, carry):
            @pl.when(nval_ref[b] < MOE_BLK)
            def _():
                zero_copy(b).wait()
            return carry

        lax.fori_loop(0, N_BLOCKS, zdrain, 0)

    def row_copy(t, d):
        return pltpu.make_async_copy(src_ref.at[pl.ds(t, 1), :], xs_ref.at[pl.ds(d, 1), :], sem)

    for t in range(DISP_TG):
        row_copy(t, dest_ref[base + t]).start(priority=t % 2)
    pltpu.make_async_copy(src_ref, xs_ref.at[pl.ds(0, DISP_TG), :], sem).wait()


def _dispatch(dest, nval, hx):
    return pl.pallas_call(
        _dispatch_kernel,
        out_shape=jax.ShapeDtypeStruct((SLOT_ROWS, XROW), F32),
        grid_spec=pltpu.PrefetchScalarGridSpec(
            num_scalar_prefetch=2,
            grid=(NTOK // DISP_TG,),
            in_specs=[pl.BlockSpec((DISP_TG, XROW), lambda i, d, nv: (i, 0))],
            out_specs=pl.BlockSpec(memory_space=pl.ANY),
            scratch_shapes=[pltpu.VMEM((MOE_BLK, XROW), F32),
                            pltpu.SemaphoreType.DMA(()), pltpu.SemaphoreType.DMA(())]),
        compiler_params=_cparams("arbitrary", row_dma=True),
        name="dispatch",
    )(dest, nval, hx)


def _moe_kernel(elo_ref, ehi_ref, nval_ref, xs_ref, wg1, wu1, wd1, wg2, wu2, wd2, o_ref):
    i = pl.program_id(0)
    nval = nval_ref[i]

    def run(rows):
        x = xs_ref[0:rows, 0:D].astype(BF16)
        w_lo = xs_ref[0:rows, D:D + 1]
        w_hi = xs_ref[0:rows, D + 1:D + 2]

        def expert(wg, wu, wd):
            g = jnp.dot(x, wg[0], preferred_element_type=F32)
            u = jnp.dot(x, wu[0], preferred_element_type=F32)
            a = (g * _sigmoid(g) * u).astype(BF16)
            return jnp.dot(a, wd[0], preferred_element_type=F32)

        o_ref[0:rows, :] = w_lo * expert(wg1, wu1, wd1) + w_hi * expert(wg2, wu2, wd2)

    @pl.when(nval > MOE_BLK // 2)
    def _():
        run(MOE_BLK)

    @pl.when((nval > 0) & (nval <= MOE_BLK // 2))
    def _():
        run(MOE_BLK // 2)
        o_ref[MOE_BLK // 2:MOE_BLK, :] = jnp.zeros((MOE_BLK // 2, D), F32)

    @pl.when(nval == 0)
    def _():
        o_ref[...] = jnp.zeros_like(o_ref)


def _moe(elo, ehi, nval, xs, wg, wu, wd):
    wspec = lambda tab: pl.BlockSpec((1, D, D), lambda i, elo, ehi, nv: ((elo, ehi)[tab][i], 0, 0))
    return pl.pallas_call(
        _moe_kernel,
        out_shape=jax.ShapeDtypeStruct((SLOT_ROWS, D), F32),
        grid_spec=pltpu.PrefetchScalarGridSpec(
            num_scalar_prefetch=3,
            grid=(N_BLOCKS,),
            in_specs=[pl.BlockSpec((MOE_BLK, XROW), lambda i, elo, ehi, nv: (i, 0)),
                      wspec(0), wspec(0), wspec(0), wspec(1), wspec(1), wspec(1)],
            out_specs=pl.BlockSpec((MOE_BLK, D), lambda i, elo, ehi, nv: (i, 0))),
        compiler_params=_cparams("arbitrary"),
        name="experts",
    )(elo, ehi, nval, xs, wg, wu, wd, wg, wu, wd)


COMB_TG = 512


def _combine_kernel(dest_ref, ys_ref, x1_ref, gt_ref, nw_ref, sc_ref, sh_ref, *rest, last):
    if last:
        hn_ref, ybuf, sem = rest
    else:
        x2_ref, hn_ref, ybuf, sem = rest
    step = pl.program_id(0)
    slot = step % 2

    def gather_tile(tile, into):
        base = tile * COMB_TG

        for t in range(COMB_TG):
            src = ys_ref.at[pl.ds(dest_ref[base + t], 1), :]
            pltpu.make_async_copy(src, ybuf.at[into, pl.ds(t, 1), :], sem.at[into]).start(priority=t % 2)

    @pl.when(step == 0)
    def _():
        gather_tile(0, 0)

    @pl.when(step + 1 < pl.num_programs(0))
    def _():
        gather_tile(step + 1, 1 - slot)

    pltpu.make_async_copy(ys_ref.at[pl.ds(0, COMB_TG), :], ybuf.at[slot], sem.at[slot]).wait()

    x2 = x1_ref[...] + gt_ref[0] * ybuf[slot]
    y = _rms(x2) * nw_ref[...]
    if last:
        hn_ref[...] = y
    else:
        x2_ref[...] = x2
        hn_ref[...] = (y * (1.0 + sc_ref[0]) + sh_ref[0]).astype(hn_ref.dtype)


def _combine(dest, ys, x1, gt2, nw, sc, sh, last):
    tpb = SEQ // COMB_TG
    bidx = lambda i, d: (i // tpb, 0, 0)
    tile = pl.BlockSpec((COMB_TG, D), lambda i, d: (i, 0))
    if last:
        out_shape = jax.ShapeDtypeStruct((NTOK, D), F32)
        out_specs = tile
    else:
        out_shape = (jax.ShapeDtypeStruct((NTOK, D), F32), jax.ShapeDtypeStruct((NTOK, D), BF16))
        out_specs = (tile, tile)
    return pl.pallas_call(
        functools.partial(_combine_kernel, last=last),
        out_shape=out_shape,
        grid_spec=pltpu.PrefetchScalarGridSpec(
            num_scalar_prefetch=1,
            grid=(NTOK // COMB_TG,),
            in_specs=[pl.BlockSpec(memory_space=pl.ANY),
                      tile,
                      pl.BlockSpec((1, 1, D), bidx),
                      pl.BlockSpec((1, D), lambda i, d: (0, 0)),
                      pl.BlockSpec((1, 1, D), bidx),
                      pl.BlockSpec((1, 1, D), bidx)],
            out_specs=out_specs,
            scratch_shapes=[pltpu.VMEM((2, COMB_TG, D), F32), pltpu.SemaphoreType.DMA((2,))]),
        compiler_params=_cparams("arbitrary", row_dma=True),
        name="combine",
    )(dest, ys, x1, gt2, nw.reshape(1, D), sc, sh)


def _pair_tables():
    first = [GROUP_SIZE * g + fa for g in range(N_GROUPS) for fa, _ in PAIR_ORDER]
    second = [GROUP_SIZE * g + fb for g in range(N_GROUPS) for _, fb in PAIR_ORDER]
    return np.asarray(first, np.int32), np.asarray(second, np.int32)


def _plan_kernel(cnt_ref, first_ref, second_ref, ri_ref, dest_ref, elo_ref, ehi_ref, nval_ref, start_ref):
    def per_bucket(b, carry):
        start, blk = carry
        count = cnt_ref[b]
        n_blk = (count + MOE_BLK - 1) // MOE_BLK
        start_ref[b] = start

        def per_block(k, c):
            elo_ref[blk + k] = first_ref[b]
            ehi_ref[blk + k] = second_ref[b]
            nval_ref[blk + k] = jnp.minimum(count - k * MOE_BLK, MOE_BLK)
            return c

        lax.fori_loop(0, n_blk, per_block, 0)
        return start + n_blk * MOE_BLK, blk + n_blk

    _, used = lax.fori_loop(0, N_BUCKETS, per_bucket, (jnp.int32(0), jnp.int32(0)))

    def unused_block(k, c):
        elo_ref[k] = first_ref[N_BUCKETS - 1]
        ehi_ref[k] = second_ref[N_BUCKETS - 1]
        nval_ref[k] = 0
        return c

    lax.fori_loop(used, N_BLOCKS, unused_block, 0)

    bucket = ri_ref[0:1, :]
    rank = ri_ref[1:2, :]
    dest = jnp.zeros_like(rank)
    for b in range(N_BUCKETS):
        dest = jnp.where(bucket == b, start_ref[b] + rank, dest)
    dest_ref[...] = dest


def _slot_plan(ri, cnt):
    counts = cnt[:, 0].astype(jnp.int32)
    first_tab, second_tab = _pair_tables()
    smem_out = pl.BlockSpec(memory_space=pltpu.SMEM)
    blocks = jax.ShapeDtypeStruct((N_BLOCKS,), jnp.int32)
    dest, elo, ehi, nval = pl.pallas_call(
        _plan_kernel,
        out_shape=(jax.ShapeDtypeStruct((1, NTOK), jnp.int32), blocks, blocks, blocks),
        grid_spec=pltpu.PrefetchScalarGridSpec(
            num_scalar_prefetch=3,
            grid=(1,),
            in_specs=[pl.BlockSpec((8, NTOK), lambda i, c, f, s: (0, 0))],
            out_specs=(pl.BlockSpec((1, NTOK), lambda i, c, f, s: (0, 0)), smem_out, smem_out, smem_out),
            scratch_shapes=[pltpu.SMEM((BUCKET_ROWS,), jnp.int32)]),
        compiler_params=_cparams("arbitrary"),
        name="slot_plan",
    )(counts, jnp.asarray(first_tab), jnp.asarray(second_tab), ri)
    return dest.reshape(NTOK), elo, ehi, nval


def kernel(x, c, positions, w_ada, b_ada, norm1_w, w_in, conv_w, conv_b, conv_ln_w, conv_ln_b, w_conv_out,
           w_ret_out, w_out, norm2_w, w_router, router_bias, w_exp_gate, w_exp_up, w_exp_down, final_norm_w):
    mod = _ada(c, w_ada, b_ada).reshape(DEPTH, BATCH, N_MOD, 1, D)
    sh1, sc1, gt1, sh2, sc2, gt2 = (mod[:, :, i] for i in range(N_MOD))
    ret_tables = _ret_tables()

    wr_t = w_router.T
    rhi = wr_t.astype(BF16)
    rlo = (wr_t - rhi.astype(F32)).astype(BF16)
    rb = jnp.broadcast_to(router_bias.astype(F32)[:, None], (N_EXPERTS, MIX_TM))
    tri = (jnp.arange(MIX_TM)[:, None] < jnp.arange(MIX_TM)[None, :]).astype(BF16)

    xf = x.reshape(NTOK, D)
    h, cos, sin = _modnorm(x, norm1_w[0], sc1[0], sh1[0], positions)
    out = None
    for l in range(DEPTH):
        p = _inproj(h, w_in, l, cos, sin)
        p3 = p.reshape(BATCH, SEQ, P_COLS)
        ya, wg_bf, wu_bf, wd_bf = _conv_branch(p3, l, conv_w[l], conv_b[l], conv_ln_w[l], conv_ln_b[l], w_conv_out,
                                               (w_exp_gate, w_exp_up, w_exp_down))
        rg = _retention(p3, ret_tables)
        x1, hx, ri, cnt = _mix(rg.reshape(NTOK, HEADS * DV), ya.reshape(NTOK, D), p, xf, gt1[l], sc2[l], sh2[l],
                               w_ret_out, w_out, l, norm2_w[l], rhi, rlo, rb, tri)
        dest, elo, ehi, nval = _slot_plan(ri, cnt)
        xs = _dispatch(dest, nval, hx)
        ys = _moe(elo, ehi, nval, xs, wg_bf, wu_bf, wd_bf)
        if l + 1 < DEPTH:
            xf, h = _combine(dest, ys, x1, gt2[l], norm1_w[l + 1], sc1[l + 1], sh1[l + 1], last=False)
        else:
            out = _combine(dest, ys, x1, gt2[l], final_norm_w, sc1[l], sh1[l], last=True)
    return out.reshape(BATCH, SEQ, D)
```

```python
import functools

import numpy as np
import jax
import jax.numpy as jnp
from jax import lax
from jax.experimental import pallas as pl
from jax.experimental.pallas import tpu as pltpu

F32 = jnp.float32
BF16 = jnp.bfloat16

D = 1024
BATCH = 8
SEQ = 2048
DEPTH = 4
NTOK = BATCH * SEQ
N_MOD = 6
EPS = 1e-6

CONV_K = 31
HEADS = 4
DK = 256
DV = 512
CHUNK = 256
ROPE_BASE = 10000.0
HALF = DK // 2

N_EXPERTS = 16
N_GROUPS = 4
GROUP_SIZE = 4
PAIR_ORDER = ((0, 1), (0, 2), (0, 3), (1, 3), (1, 2), (3, 2))
PAIRS = len(PAIR_ORDER)
N_BUCKETS = N_GROUPS * PAIRS
BUCKET_ROWS = 32

P_COLS = 9 * D
PCOL_U, PCOL_Q, PCOL_K, PCOL_V, PCOL_G, PCOL_GA, PCOL_GB = 0, 1, 2, 3, 5, 7, 8

MOE_BLK = 256
N_BLOCKS = NTOK // MOE_BLK + N_BUCKETS
SLOT_ROWS = N_BLOCKS * MOE_BLK
TAIL = 128
XROW = D + TAIL

VMEM_LIMIT = 56 * 1024 * 1024

NT_DIMS = (((1,), (1,)), ((), ()))


def _cparams(*sem, row_dma=False):
    return pltpu.CompilerParams(dimension_semantics=sem, vmem_limit_bytes=VMEM_LIMIT,
                                disable_bounds_checks=row_dma)


def _sigmoid(x):
    return 0.5 * jnp.tanh(0.5 * x) + 0.5


def _rms(x):
    return x * lax.rsqrt(jnp.mean(x * x, axis=-1, keepdims=True) + EPS)


ADA_TN = 3072


def _ada_kernel(c_ref, w_ref, b_ref, o_ref):
    c = c_ref[...]
    sc = c * _sigmoid(c)
    o_ref[0] = jnp.dot(sc, w_ref[0], precision=lax.Precision.HIGHEST,
                       preferred_element_type=F32) + b_ref[0]


def _ada(c, w_ada, b_ada):
    nj = N_MOD * D // ADA_TN
    return pl.pallas_call(
        _ada_kernel,
        out_shape=jax.ShapeDtypeStruct((DEPTH, BATCH, N_MOD * D), F32),
        grid=(DEPTH, nj),
        in_specs=[pl.BlockSpec((BATCH, D), lambda l, j: (0, 0)),
                  pl.BlockSpec((1, D, ADA_TN), lambda l, j: (l, 0, j)),
                  pl.BlockSpec((1, 1, ADA_TN), lambda l, j: (l, 0, j))],
        out_specs=pl.BlockSpec((1, BATCH, ADA_TN), lambda l, j: (l, 0, j)),
        compiler_params=_cparams("parallel", "parallel"),
        name="ada",
    )(c, w_ada, b_ada.reshape(DEPTH, 1, N_MOD * D))


NORM_TS = 512


def _modnorm_kernel(x_ref, w_ref, sc_ref, sh_ref, pos_ref, inv_ref, o_ref, cos_ref, sin_ref):
    y = _rms(x_ref[0]) * w_ref[...]
    o_ref[0] = (y * (1.0 + sc_ref[0]) + sh_ref[0]).astype(o_ref.dtype)
    ang = pos_ref[0] * inv_ref[...]
    cos_ref[0] = jnp.cos(ang)
    sin_ref[0] = jnp.sin(ang)


def _modnorm(x, w, sc, sh, positions):
    inv = ROPE_BASE ** (-jnp.arange(HALF, dtype=F32) / HALF)
    pos = positions.astype(F32).reshape(BATCH, SEQ, 1)
    tile = lambda width: pl.BlockSpec((1, NORM_TS, width), lambda b, i: (b, i, 0))
    table = jax.ShapeDtypeStruct((BATCH, SEQ, HALF), F32)
    h, cos, sin = pl.pallas_call(
        _modnorm_kernel,
        out_shape=(jax.ShapeDtypeStruct((BATCH, SEQ, D), BF16), table, table),
        grid=(BATCH, SEQ // NORM_TS),
        in_specs=[tile(D),
                  pl.BlockSpec((1, D), lambda b, i: (0, 0)),
                  pl.BlockSpec((1, 1, D), lambda b, i: (b, 0, 0)),
                  pl.BlockSpec((1, 1, D), lambda b, i: (b, 0, 0)),
                  tile(1),
                  pl.BlockSpec((1, HALF), lambda b, i: (0, 0))],
        out_specs=(tile(D), tile(HALF), tile(HALF)),
        compiler_params=_cparams("parallel", "parallel"),
        name="modnorm",
    )(x, w.reshape(1, D), sc, sh, pos, inv.reshape(1, HALF))
    return h.reshape(NTOK, D), cos.reshape(NTOK, HALF), sin.reshape(NTOK, HALF)


INP_TM = 1024
INP_GROUPS = 9


def _inproj_kernel(h_ref, w1_ref, w2_ref, cos_ref, sin_ref, o_ref, wb1_ref, wb2_ref):
    j = pl.program_id(0)
    m = pl.program_id(1)
    h = h_ref[...]

    @pl.when(m == 0)
    def _():
        wb1_ref[...] = w1_ref[0].astype(BF16)

    @pl.when((m == 0) & (j == 0))
    def _():
        wb2_ref[...] = w2_ref[0].astype(BF16)

    def proj():
        return jnp.dot(h, wb1_ref[...], preferred_element_type=F32)

    @pl.when(j == 0)
    def _():
        a = proj()
        b = jnp.dot(h, wb2_ref[...], preferred_element_type=F32)
        o_ref[...] = (a * _sigmoid(b)).astype(BF16)

    @pl.when((j == 1) | (j == 2))
    def _():
        t = proj()
        scale = jnp.where(j == 2, DK ** -0.5, 1.0).astype(F32)
        cos = cos_ref[...] * scale
        sin = sin_ref[...] * scale
        for hd in range(HEADS):
            c0 = hd * DK
            t1 = t[:, c0:c0 + HALF]
            t2 = t[:, c0 + HALF:c0 + DK]
            o_ref[:, c0:c0 + HALF] = (t1 * cos - t2 * sin).astype(BF16)
            o_ref[:, c0 + HALF:c0 + DK] = (t1 * sin + t2 * cos).astype(BF16)

    @pl.when((j == 3) | (j == 4))
    def _():
        o_ref[...] = proj().astype(BF16)

    @pl.when((j == 5) | (j == 6))
    def _():
        g = proj()
        o_ref[...] = (g * _sigmoid(g)).astype(BF16)

    @pl.when(j >= 7)
    def _():
        o_ref[...] = _sigmoid(proj()).astype(BF16)


def _inproj(h, w_in, l, cos, sin):
    def rope_idx(j, m):
        return (jnp.where((j == 1) | (j == 2), m, 0), 0)

    return pl.pallas_call(
        _inproj_kernel,
        out_shape=jax.ShapeDtypeStruct((NTOK, P_COLS), BF16),
        grid=(INP_GROUPS, NTOK // INP_TM),
        in_specs=[pl.BlockSpec((INP_TM, D), lambda j, m: (m, 0)),
                  pl.BlockSpec((1, D, D), lambda j, m: (l, 0, jnp.where(j == 0, 0, j + 1))),
                  pl.BlockSpec((1, D, D), lambda j, m: (l, 0, 1), pipeline_mode=pl.Buffered(1)),
                  pl.BlockSpec((INP_TM, HALF), rope_idx),
                  pl.BlockSpec((INP_TM, HALF), rope_idx)],
        out_specs=pl.BlockSpec((INP_TM, D), lambda j, m: (m, j)),
        scratch_shapes=[pltpu.VMEM((D, D), BF16), pltpu.VMEM((D, D), BF16)],
        compiler_params=_cparams("arbitrary", "arbitrary"),
        name="inproj",
    )(h, w_in, w_in, cos, sin)


CONV_TS = 512
CONV_HALO = 32
CONV_RC = 32
CONV_CW = 128
CONV_SH = CONV_TS + 24
SUBLANES = 8


def _conv_kernel(u_ref, halo_ref, sga_ref, cw_ref, cb_ref, lnw_ref, lnb_ref, wo_ref, eg_ref, eu_ref, ed_ref,
                 o_ref, og_ref, ou_ref, od_ref, buf_ref, sh_ref, acc_ref, wbf_ref):
    i = pl.program_id(1)

    @pl.when((pl.program_id(0) == 0) & (i == 0))
    def _():
        wbf_ref[...] = wo_ref[0].astype(BF16)

    for src, dst in ((eg_ref, og_ref), (eu_ref, ou_ref), (ed_ref, od_ref)):
        dst[0] = src[0, 0].astype(BF16)

    halo = halo_ref[0].astype(F32)
    buf_ref[0:CONV_HALO, :] = jnp.where(i > 0, halo, 0.0)
    buf_ref[CONV_HALO:CONV_HALO + CONV_TS, :] = u_ref[0].astype(F32)
    for r in range(1, SUBLANES):
        sh_ref[r - 1] = buf_ref[r:r + CONV_SH, :]

    groups = CONV_RC // SUBLANES

    def body(ci, carry):
        r0 = pl.multiple_of(ci * CONV_RC, CONV_RC)
        for cc in range(D // CONV_CW):
            cols = slice(cc * CONV_CW, (cc + 1) * CONV_CW)
            accs = [cb_ref[:, cols]] * groups
            for off in range(2, CONV_K + 2):
                q, r = divmod(off, SUBLANES)
                w8 = cw_ref[off - 2, :, cols]
                for g in range(groups):
                    rows = pl.ds(r0 + SUBLANES * (q + g), SUBLANES)
                    win = buf_ref[rows, cols] if r == 0 else sh_ref[r - 1, rows, cols]
                    accs[g] = accs[g] + win * w8
            for g in range(groups):
                acc_ref[pl.ds(r0 + SUBLANES * g, SUBLANES), cols] = accs[g]
        return carry

    lax.fori_loop(0, CONV_TS // CONV_RC, body, 0)

    c = acc_ref[...]
    mu = jnp.mean(c, axis=-1, keepdims=True)
    cen = c - mu
    var = jnp.mean(cen * cen, axis=-1, keepdims=True)
    y = cen * lax.rsqrt(var + EPS) * lnw_ref[...] + lnb_ref[...]
    y = y * _sigmoid(y)
    out = jnp.dot(y.astype(BF16), wbf_ref[...], preferred_element_type=F32)
    o_ref[0] = (out * sga_ref[0].astype(F32)).astype(BF16)


def _conv_branch(p3, l, conv_w, conv_b, ln_w, ln_b, w_conv_out, w_exp):
    hb = CONV_TS // CONV_HALO
    n_i = SEQ // CONV_TS
    cw8 = jnp.broadcast_to(conv_w[:, None, :], (CONV_K, SUBLANES, D))
    cb8 = jnp.broadcast_to(conv_b[None, :], (SUBLANES, D))
    slab_rows = N_EXPERTS * D // (BATCH * n_i)
    per_mat = D // slab_rows
    assert per_mat * slab_rows == D and slab_rows % 16 == 0

    def slab(b, i):
        s = b * n_i + i
        return s // per_mat, s % per_mat

    exp_in = pl.BlockSpec((1, 1, slab_rows, D), lambda b, i: (l, *slab(b, i), 0))
    exp_out = pl.BlockSpec((1, slab_rows, D), lambda b, i: (*slab(b, i), 0))
    exp_shape = jax.ShapeDtypeStruct((N_EXPERTS, D, D), BF16)
    return pl.pallas_call(
        _conv_kernel,
        out_shape=(jax.ShapeDtypeStruct((BATCH, SEQ, D), BF16), exp_shape, exp_shape, exp_shape),
        grid=(BATCH, n_i),
        in_specs=[pl.BlockSpec((1, CONV_TS, D), lambda b, i: (b, i, PCOL_U)),
                  pl.BlockSpec((1, CONV_HALO, D), lambda b, i: (b, jnp.maximum(i * hb - 1, 0), PCOL_U)),
                  pl.BlockSpec((1, CONV_TS, D), lambda b, i: (b, i, PCOL_GA)),
                  pl.BlockSpec((CONV_K, SUBLANES, D), lambda b, i: (0, 0, 0)),
                  pl.BlockSpec((SUBLANES, D), lambda b, i: (0, 0)),
                  pl.BlockSpec((1, D), lambda b, i: (0, 0)),
                  pl.BlockSpec((1, D), lambda b, i: (0, 0)),
                  pl.BlockSpec((1, D, D), lambda b, i: (l, 0, 0), pipeline_mode=pl.Buffered(1)),
                  exp_in, exp_in, exp_in],
        out_specs=(pl.BlockSpec((1, CONV_TS, D), lambda b, i: (b, i, 0)), exp_out, exp_out, exp_out),
        scratch_shapes=[pltpu.VMEM((CONV_HALO + CONV_TS, D), F32),
                        pltpu.VMEM((SUBLANES - 1, CONV_SH, D), F32),
                        pltpu.VMEM((CONV_TS, D), F32),
                        pltpu.VMEM((D, D), BF16)],
        compiler_params=_cparams("arbitrary", "arbitrary"),
        name="conv_branch",
    )(p3, p3, p3, cw8, cb8, ln_w.reshape(1, D), ln_b.reshape(1, D), w_conv_out, *w_exp)


def _ret_tables():
    hh = jnp.arange(HEADS, dtype=F32)
    log_g = jnp.log1p(-jnp.exp2(-5.0 - hh))
    idx = jnp.arange(CHUNK, dtype=F32)
    rel = idx[:, None] - idx[None, :]
    dmask = jnp.where(rel[None] >= 0, jnp.exp(jnp.maximum(rel, 0.0)[None] * log_g[:, None, None]), 0.0)
    xi = jnp.exp((idx + 1.0)[None, :] * log_g[:, None])[..., None]
    zeta = jnp.exp((CHUNK - 1.0 - idx)[None, :] * log_g[:, None])[..., None]
    g_chunk = jnp.exp(CHUNK * log_g)[:, None, None]
    return (dmask,
            jnp.broadcast_to(xi, (HEADS, CHUNK, DK)),
            jnp.broadcast_to(zeta, (HEADS, CHUNK, DK)),
            jnp.broadcast_to(g_chunk, (HEADS, 1, DV)))


RET_TS = 512


def _ret_kernel(q_ref, k_ref, v01_ref, v23_ref, g01_ref, g23_ref, dm_ref, xi_ref, zt_ref, gc_ref, o_ref, st_ref):
    @pl.when(pl.program_id(1) == 0)
    def _():
        st_ref[...] = jnp.zeros_like(st_ref)

    def body(c, carry):
        r0 = pl.multiple_of(c * CHUNK, CHUNK)
        rows = pl.ds(r0, CHUNK)
        for hd in range(HEADS):
            v_ref, g_ref = (v01_ref, g01_ref) if hd < 2 else (v23_ref, g23_ref)
            vcols = slice((hd % 2) * DV, (hd % 2 + 1) * DV)
            qc = q_ref[0, rows, hd * DK:(hd + 1) * DK]
            kc = k_ref[0, rows, hd * DK:(hd + 1) * DK]
            vc = v_ref[0, rows, vcols]
            scores = lax.dot_general(qc, kc, NT_DIMS, preferred_element_type=F32) * dm_ref[hd]
            inner = jnp.dot(scores.astype(BF16), vc, preferred_element_type=F32)
            st = st_ref[hd]
            qx = (qc.astype(F32) * xi_ref[hd]).astype(BF16)
            cross = jnp.dot(qx, st.astype(BF16), preferred_element_type=F32)
            kzt = (kc.astype(F32) * zt_ref[hd]).T.astype(BF16)
            st_ref[hd] = st * gc_ref[hd] + jnp.dot(kzt, vc, preferred_element_type=F32)
            o = inner + cross
            mu = jnp.mean(o, axis=-1, keepdims=True)
            cen = o - mu
            var = jnp.mean(cen * cen, axis=-1, keepdims=True)
            r = cen * lax.rsqrt(var + EPS)
            o_ref[0, rows, hd * DV:(hd + 1) * DV] = (r * g_ref[0, rows, vcols].astype(F32)).astype(BF16)
        return carry

    lax.fori_loop(0, RET_TS // CHUNK, body, 0)


def _retention(p3, tables):
    dmask, xi, zeta, gch = tables
    tile = lambda col: pl.BlockSpec((1, RET_TS, D), lambda b, i: (b, i, col))
    full = lambda a: pl.BlockSpec(a.shape, lambda b, i: (0, 0, 0))
    return pl.pallas_call(
        _ret_kernel,
        out_shape=jax.ShapeDtypeStruct((BATCH, SEQ, HEADS * DV), BF16),
        grid=(BATCH, SEQ // RET_TS),
        in_specs=[tile(PCOL_Q), tile(PCOL_K), tile(PCOL_V), tile(PCOL_V + 1), tile(PCOL_G), tile(PCOL_G + 1),
                  full(dmask), full(xi), full(zeta), full(gch)],
        out_specs=pl.BlockSpec((1, RET_TS, HEADS * DV), lambda b, i: (b, i, 0)),
        scratch_shapes=[pltpu.VMEM((HEADS, DK, DV), F32)],
        compiler_params=_cparams("parallel", "arbitrary"),
        name="retention",
    )(p3, p3, p3, p3, p3, p3, dmask, xi, zeta, gch)


MIX_TM = 512


def _route_rows(s, sb):
    row = lambda a, e: a[e:e + 1, :]
    best = None
    gidx = None
    for g in range(N_GROUPS):
        v = [row(sb, GROUP_SIZE * g + i) for i in range(GROUP_SIZE)]
        pair_sums = [v[a] + v[b] for a in range(GROUP_SIZE) for b in range(a + 1, GROUP_SIZE)]
        gs = functools.reduce(jnp.maximum, pair_sums)
        if g == 0:
            best, gidx = gs, jnp.zeros(gs.shape, jnp.int32)
        else:
            upd = gs > best
            gidx = jnp.where(upd, g, gidx)
            best = jnp.where(upd, gs, best)

    def pick(a, i):
        out = row(a, i)
        for g in range(1, N_GROUPS):
            out = jnp.where(gidx == g, row(a, GROUP_SIZE * g + i), out)
        return out

    vb = [pick(sb, i) for i in range(GROUP_SIZE)]
    vs = [pick(s, i) for i in range(GROUP_SIZE)]
    m1, i1, s1 = vb[0], jnp.zeros(gidx.shape, jnp.int32), vs[0]
    for i in range(1, GROUP_SIZE):
        upd = vb[i] > m1
        m1 = jnp.where(upd, vb[i], m1)
        i1 = jnp.where(upd, i, i1)
        s1 = jnp.where(upd, vs[i], s1)
    m2 = i2 = s2 = None
    for i in range(GROUP_SIZE):
        cand = jnp.where(i1 == i, -jnp.inf, vb[i])
        if m2 is None:
            m2, i2, s2 = cand, jnp.zeros(gidx.shape, jnp.int32), vs[0]
        else:
            upd = cand > m2
            m2 = jnp.where(upd, cand, m2)
            i2 = jnp.where(upd, i, i2)
            s2 = jnp.where(upd, vs[i], s2)
    den = s1 + s2
    w1 = s1 / den
    w2 = s2 / den
    lo = jnp.minimum(i1, i2)
    hi = jnp.maximum(i1, i2)
    pair = jnp.zeros(gidx.shape, jnp.int32)
    first = jnp.zeros(gidx.shape, jnp.int32)
    for p, (fa, fb) in enumerate(PAIR_ORDER):
        hit = (lo == min(fa, fb)) & (hi == max(fa, fb))
        pair = jnp.where(hit, p, pair)
        first = jnp.where(hit, fa, first)
    bucket = gidx * PAIRS + pair
    first_is_top1 = i1 == first
    return bucket, jnp.where(first_is_top1, w1, w2), jnp.where(first_is_top1, w2, w1)


def _mix_kernel(rg_ref, ya_ref, sgb_ref, x_ref, gt_ref, sc_ref, sh_ref, wr_ref, wo_ref, n2_ref,
                rhi_ref, rlo_ref, rb_ref, tri_ref,
                x1_ref, hx_ref, ri_ref, cnt_ref, carry_ref, wrb_ref, wob_ref):
    m = pl.program_id(0)

    @pl.when(m == 0)
    def _():
        carry_ref[...] = jnp.zeros_like(carry_ref)
        wrb_ref[...] = wr_ref[0].astype(BF16)
        wob_ref[...] = wo_ref[0].astype(BF16)

    yb = jnp.dot(rg_ref[...], wrb_ref[...], preferred_element_type=F32)
    y = ya_ref[...].astype(F32) + sgb_ref[...].astype(F32) * yb
    o = jnp.dot(y.astype(BF16), wob_ref[...], preferred_element_type=F32)
    x1 = x_ref[...] + gt_ref[0] * o
    x1_ref[...] = x1
    h2 = _rms(x1) * n2_ref[...] * (1.0 + sc_ref[0]) + sh_ref[0]
    hx_ref[:, 0:D] = h2

    hi = h2.astype(BF16)
    lo = (h2 - hi.astype(F32)).astype(BF16)
    rhi = rhi_ref[...]
    logits = (lax.dot_general(rhi, hi, NT_DIMS, preferred_element_type=F32)
              + lax.dot_general(rhi, lo, NT_DIMS, preferred_element_type=F32)
              + lax.dot_general(rlo_ref[...], hi, NT_DIMS, preferred_element_type=F32))
    s = _sigmoid(logits)
    bucket, w_lo, w_hi = _route_rows(s, s + rb_ref[...])

    onehot = (lax.broadcasted_iota(jnp.int32, (BUCKET_ROWS, MIX_TM), 0) == bucket).astype(F32)
    prefix = jnp.dot(onehot.astype(BF16), tri_ref[...], preferred_element_type=F32)
    carry = carry_ref[:, 0:1]
    rank = jnp.sum(onehot * (prefix + carry), axis=0, keepdims=True)
    new_carry = carry + jnp.sum(onehot, axis=1, keepdims=True)
    carry_ref[...] = jnp.broadcast_to(new_carry, carry_ref.shape)
    cnt_ref[...] = jnp.broadcast_to(new_carry, cnt_ref.shape)

    rid = lax.broadcasted_iota(jnp.int32, (8, MIX_TM), 0)
    ri_ref[...] = jnp.where(rid == 0, bucket, jnp.where(rid == 1, rank.astype(jnp.int32), 0))
    wid = lax.broadcasted_iota(jnp.int32, (TAIL, MIX_TM), 0)
    wrows = jnp.where(wid == 0, w_lo, jnp.where(wid == 1, w_hi, 0.0))
    hx_ref[:, D:XROW] = wrows.T


def _mix(rg, ya, p, x, gt1, sc2, sh2, w_ret_out, w_out, l, norm2_w, rhi, rlo, rb, tri):
    tpb = SEQ // MIX_TM
    bidx = lambda m: (m // tpb, 0, 0)
    return pl.pallas_call(
        _mix_kernel,
        out_shape=(jax.ShapeDtypeStruct((NTOK, D), F32),
                   jax.ShapeDtypeStruct((NTOK, XROW), F32),
                   jax.ShapeDtypeStruct((8, NTOK), jnp.int32),
                   jax.ShapeDtypeStruct((BUCKET_ROWS, 128), F32)),
        grid=(NTOK // MIX_TM,),
        in_specs=[pl.BlockSpec((MIX_TM, HEADS * DV), lambda m: (m, 0)),
                  pl.BlockSpec((MIX_TM, D), lambda m: (m, 0)),
                  pl.BlockSpec((MIX_TM, D), lambda m: (m, PCOL_GB)),
                  pl.BlockSpec((MIX_TM, D), lambda m: (m, 0)),
                  pl.BlockSpec((1, 1, D), bidx),
                  pl.BlockSpec((1, 1, D), bidx),
                  pl.BlockSpec((1, 1, D), bidx),
                  pl.BlockSpec((1, HEADS * DV, D), lambda m: (l, 0, 0), pipeline_mode=pl.Buffered(1)),
                  pl.BlockSpec((1, D, D), lambda m: (l, 0, 0), pipeline_mode=pl.Buffered(1)),
                  pl.BlockSpec((1, D), lambda m: (0, 0)),
                  pl.BlockSpec((N_EXPERTS, D), lambda m: (0, 0)),
                  pl.BlockSpec((N_EXPERTS, D), lambda m: (0, 0)),
                  pl.BlockSpec((N_EXPERTS, MIX_TM), lambda m: (0, 0)),
                  pl.BlockSpec((MIX_TM, MIX_TM), lambda m: (0, 0))],
        out_specs=(pl.BlockSpec((MIX_TM, D), lambda m: (m, 0)),
                   pl.BlockSpec((MIX_TM, XROW), lambda m: (m, 0)),
                   pl.BlockSpec((8, MIX_TM), lambda m: (0, m)),
                   pl.BlockSpec((BUCKET_ROWS, 128), lambda m: (0, 0))),
        scratch_shapes=[pltpu.VMEM((BUCKET_ROWS, 128), F32),
                        pltpu.VMEM((HEADS * DV, D), BF16), pltpu.VMEM((D, D), BF16)],
        compiler_params=_cparams("arbitrary"),
        name="mix_route",
    )(rg, ya, p, x, gt1, sc2, sh2, w_ret_out, w_out, norm2_w.reshape(1, D), rhi, rlo, rb, tri)


DISP_TG = 1024


def _dispatch_kernel(dest_ref, nval_ref, src_ref, xs_ref, zero_ref, sem, zsem):
    step = pl.program_id(0)
    base = step * DISP_TG

    @pl.when(step == 0)
    def _():
        zero_ref[...] = jnp.zeros_like(zero_ref)

        def zero_copy(b):
            r0 = pl.multiple_of(b * MOE_BLK, MOE_BLK)
            return pltpu.make_async_copy(zero_ref, xs_ref.at[pl.ds(r0, MOE_BLK), :], zsem)

        def zissue(b, carry):
            @pl.when(nval_ref[b] < MOE_BLK)
            def _():
                zero_copy(b).start()
            return carry

        lax.fori_loop(0, N_BLOCKS, zissue, 0)

        def zdrain(b, carry):
            @pl.when(nval_ref[b] < MOE_BLK)
            def _():
                zero_copy(b).wait()
            return carry

        lax.fori_loop(0, N_BLOCKS, zdrain, 0)

    def row_copy(t, d):
        return pltpu.make_async_copy(src_ref.at[pl.ds(t, 1), :], xs_ref.at[pl.ds(d, 1), :], sem)

    for t in range(DISP_TG):
        row_copy(t, dest_ref[base + t]).start(priority=t % 2)
    pltpu.make_async_copy(src_ref, xs_ref.at[pl.ds(0, DISP_TG), :], sem).wait()


def _dispatch(dest, nval, hx):
    return pl.pallas_call(
        _dispatch_kernel,
        out_shape=jax.ShapeDtypeStruct((SLOT_ROWS, XROW), F32),
        grid_spec=pltpu.PrefetchScalarGridSpec(
            num_scalar_prefetch=2,
            grid=(NTOK // DISP_TG,),
            in_specs=[pl.BlockSpec((DISP_TG, XROW), lambda i, d, nv: (i, 0))],
            out_specs=pl.BlockSpec(memory_space=pl.ANY),
            scratch_shapes=[pltpu.VMEM((MOE_BLK, XROW), F32),
                            pltpu.SemaphoreType.DMA(()), pltpu.SemaphoreType.DMA(())]),
        compiler_params=_cparams("arbitrary", row_dma=True),
        name="dispatch",
    )(dest, nval, hx)


def _moe_kernel(elo_ref, ehi_ref, nval_ref, xs_ref, wg1, wu1, wd1, wg2, wu2, wd2, o_ref):
    i = pl.program_id(0)
    nval = nval_ref[i]

    def run(rows):
        x = xs_ref[0:rows, 0:D].astype(BF16)
        w_lo = xs_ref[0:rows, D:D + 1]
        w_hi = xs_ref[0:rows, D + 1:D + 2]

        def expert(wg, wu, wd):
            g = jnp.dot(x, wg[0], preferred_element_type=F32)
            u = jnp.dot(x, wu[0], preferred_element_type=F32)
            a = (g * _sigmoid(g) * u).astype(BF16)
            return jnp.dot(a, wd[0], preferred_element_type=F32)

        o_ref[0:rows, :] = w_lo * expert(wg1, wu1, wd1) + w_hi * expert(wg2, wu2, wd2)

    @pl.when(nval > MOE_BLK // 2)
    def _():
        run(MOE_BLK)

    @pl.when((nval > 0) & (nval <= MOE_BLK // 2))
    def _():
        run(MOE_BLK // 2)
        o_ref[MOE_BLK // 2:MOE_BLK, :] = jnp.zeros((MOE_BLK // 2, D), F32)

    @pl.when(nval == 0)
    def _():
        o_ref[...] = jnp.zeros_like(o_ref)


def _moe(elo, ehi, nval, xs, wg, wu, wd):
    wspec = lambda tab: pl.BlockSpec((1, D, D), lambda i, elo, ehi, nv: ((elo, ehi)[tab][i], 0, 0))
    return pl.pallas_call(
        _moe_kernel,
        out_shape=jax.ShapeDtypeStruct((SLOT_ROWS, D), F32),
        grid_spec=pltpu.PrefetchScalarGridSpec(
            num_scalar_prefetch=3,
            grid=(N_BLOCKS,),
            in_specs=[pl.BlockSpec((MOE_BLK, XROW), lambda i, elo, ehi, nv: (i, 0)),
                      wspec(0), wspec(0), wspec(0), wspec(1), wspec(1), wspec(1)],
            out_specs=pl.BlockSpec((MOE_BLK, D), lambda i, elo, ehi, nv: (i, 0))),
        compiler_params=_cparams("arbitrary"),
        name="experts",
    )(elo, ehi, nval, xs, wg, wu, wd, wg, wu, wd)


COMB_TG = 1024


def _combine_kernel(dest_ref, ys_ref, x1_ref, gt_ref, nw_ref, sc_ref, sh_ref, *rest, last):
    if last:
        hn_ref, ybuf, sem = rest
    else:
        x2_ref, hn_ref, ybuf, sem = rest
    step = pl.program_id(0)
    slot = step % 2

    def gather_tile(tile, into):
        base = tile * COMB_TG

        for t in range(COMB_TG):
            src = ys_ref.at[pl.ds(dest_ref[base + t], 1), :]
            pltpu.make_async_copy(src, ybuf.at[into, pl.ds(t, 1), :], sem.at[into]).start(priority=t % 2)

    @pl.when(step == 0)
    def _():
        gather_tile(0, 0)

    @pl.when(step + 1 < pl.num_programs(0))
    def _():
        gather_tile(step + 1, 1 - slot)

    pltpu.make_async_copy(ys_ref.at[pl.ds(0, COMB_TG), :], ybuf.at[slot], sem.at[slot]).wait()

    x2 = x1_ref[...] + gt_ref[0] * ybuf[slot]
    y = _rms(x2) * nw_ref[...]
    if last:
        hn_ref[...] = y
    else:
        x2_ref[...] = x2
        hn_ref[...] = (y * (1.0 + sc_ref[0]) + sh_ref[0]).astype(hn_ref.dtype)


def _combine(dest, ys, x1, gt2, nw, sc, sh, last):
    tpb = SEQ // COMB_TG
    bidx = lambda i, d: (i // tpb, 0, 0)
    tile = pl.BlockSpec((COMB_TG, D), lambda i, d: (i, 0))
    if last:
        out_shape = jax.ShapeDtypeStruct((NTOK, D), F32)
        out_specs = tile
    else:
        out_shape = (jax.ShapeDtypeStruct((NTOK, D), F32), jax.ShapeDtypeStruct((NTOK, D), BF16))
        out_specs = (tile, tile)
    return pl.pallas_call(
        functools.partial(_combine_kernel, last=last),
        out_shape=out_shape,
        grid_spec=pltpu.PrefetchScalarGridSpec(
            num_scalar_prefetch=1,
            grid=(NTOK // COMB_TG,),
            in_specs=[pl.BlockSpec(memory_space=pl.ANY),
                      tile,
                      pl.BlockSpec((1, 1, D), bidx),
                      pl.BlockSpec((1, D), lambda i, d: (0, 0)),
                      pl.BlockSpec((1, 1, D), bidx),
                      pl.BlockSpec((1, 1, D), bidx)],
            out_specs=out_specs,
            scratch_shapes=[pltpu.VMEM((2, COMB_TG, D), F32), pltpu.SemaphoreType.DMA((2,))]),
        compiler_params=_cparams("arbitrary", row_dma=True),
        name="combine",
    )(dest, ys, x1, gt2, nw.reshape(1, D), sc, sh)


def _pair_tables():
    first = [GROUP_SIZE * g + fa for g in range(N_GROUPS) for fa, _ in PAIR_ORDER]
    second = [GROUP_SIZE * g + fb for g in range(N_GROUPS) for _, fb in PAIR_ORDER]
    return np.asarray(first, np.int32), np.asarray(second, np.int32)


def _plan_kernel(cnt_ref, first_ref, second_ref, ri_ref, dest_ref, elo_ref, ehi_ref, nval_ref, start_ref):
    def per_bucket(b, carry):
        start, blk = carry
        count = cnt_ref[b]
        n_blk = (count + MOE_BLK - 1) // MOE_BLK
        start_ref[b] = start

        def per_block(k, c):
            elo_ref[blk + k] = first_ref[b]
            ehi_ref[blk + k] = second_ref[b]
            nval_ref[blk + k] = jnp.minimum(count - k * MOE_BLK, MOE_BLK)
            return c

        lax.fori_loop(0, n_blk, per_block, 0)
        return start + n_blk * MOE_BLK, blk + n_blk

    _, used = lax.fori_loop(0, N_BUCKETS, per_bucket, (jnp.int32(0), jnp.int32(0)))

    def unused_block(k, c):
        elo_ref[k] = first_ref[N_BUCKETS - 1]
        ehi_ref[k] = second_ref[N_BUCKETS - 1]
        nval_ref[k] = 0
        return c

    lax.fori_loop(used, N_BLOCKS, unused_block, 0)

    bucket = ri_ref[0:1, :]
    rank = ri_ref[1:2, :]
    dest = jnp.zeros_like(rank)
    for b in range(N_BUCKETS):
        dest = jnp.where(bucket == b, start_ref[b] + rank, dest)
    dest_ref[...] = dest


def _slot_plan(ri, cnt):
    counts = cnt[:, 0].astype(jnp.int32)
    first_tab, second_tab = _pair_tables()
    smem_out = pl.BlockSpec(memory_space=pltpu.SMEM)
    blocks = jax.ShapeDtypeStruct((N_BLOCKS,), jnp.int32)
    dest, elo, ehi, nval = pl.pallas_call(
        _plan_kernel,
        out_shape=(jax.ShapeDtypeStruct((1, NTOK), jnp.int32), blocks, blocks, blocks),
        grid_spec=pltpu.PrefetchScalarGridSpec(
            num_scalar_prefetch=3,
            grid=(1,),
            in_specs=[pl.BlockSpec((8, NTOK), lambda i, c, f, s: (0, 0))],
            out_specs=(pl.BlockSpec((1, NTOK), lambda i, c, f, s: (0, 0)), smem_out, smem_out, smem_out),
            scratch_shapes=[pltpu.SMEM((BUCKET_ROWS,), jnp.int32)]),
        compiler_params=_cparams("arbitrary"),
        name="slot_plan",
    )(counts, jnp.asarray(first_tab), jnp.asarray(second_tab), ri)
    return dest.reshape(NTOK), elo, ehi, nval


def kernel(x, c, positions, w_ada, b_ada, norm1_w, w_in, conv_w, conv_b, conv_ln_w, conv_ln_b, w_conv_out,
           w_ret_out, w_out, norm2_w, w_router, router_bias, w_exp_gate, w_exp_up, w_exp_down, final_norm_w):
    mod = _ada(c, w_ada, b_ada).reshape(DEPTH, BATCH, N_MOD, 1, D)
    sh1, sc1, gt1, sh2, sc2, gt2 = (mod[:, :, i] for i in range(N_MOD))
    ret_tables = _ret_tables()

    wr_t = w_router.T
    rhi = wr_t.astype(BF16)
    rlo = (wr_t - rhi.astype(F32)).astype(BF16)
    rb = jnp.broadcast_to(router_bias.astype(F32)[:, None], (N_EXPERTS, MIX_TM))
    tri = (jnp.arange(MIX_TM)[:, None] < jnp.arange(MIX_TM)[None, :]).astype(BF16)

    xf = x.reshape(NTOK, D)
    h, cos, sin = _modnorm(x, norm1_w[0], sc1[0], sh1[0], positions)
    out = None
    for l in range(DEPTH):
        p = _inproj(h, w_in, l, cos, sin)
        p3 = p.reshape(BATCH, SEQ, P_COLS)
        ya, wg_bf, wu_bf, wd_bf = _conv_branch(p3, l, conv_w[l], conv_b[l], conv_ln_w[l], conv_ln_b[l], w_conv_out,
                                               (w_exp_gate, w_exp_up, w_exp_down))
        rg = _retention(p3, ret_tables)
        x1, hx, ri, cnt = _mix(rg.reshape(NTOK, HEADS * DV), ya.reshape(NTOK, D), p, xf, gt1[l], sc2[l], sh2[l],
                               w_ret_out, w_out, l, norm2_w[l], rhi, rlo, rb, tri)
        dest, elo, ehi, nval = _slot_plan(ri, cnt)
        xs = _dispatch(dest, nval, hx)
        ys = _moe(elo, ehi, nval, xs, wg_bf, wu_bf, wd_bf)
        if l + 1 < DEPTH:
            xf, h = _combine(dest, ys, x1, gt2[l], norm1_w[l + 1], sc1[l + 1], sh1[l + 1], last=False)
        else:
            out = _combine(dest, ys, x1, gt2[l], final_norm_w, sc1[l], sh1[l], last=True)
    return out.reshape(BATCH, SEQ, D)
```

```python
import functools

import numpy as np
import jax
import jax.numpy as jnp
from jax import lax
from jax.experimental import pallas as pl
from jax.experimental.pallas import tpu as pltpu

F32 = jnp.float32
BF16 = jnp.bfloat16

D = 1024
BATCH = 8
SEQ = 2048
DEPTH = 4
NTOK = BATCH * SEQ
N_MOD = 6
EPS = 1e-6

CONV_K = 31
HEADS = 4
DK = 256
DV = 512
CHUNK = 256
ROPE_BASE = 10000.0
HALF = DK // 2

N_EXPERTS = 16
N_GROUPS = 4
GROUP_SIZE = 4
PAIR_ORDER = ((0, 1), (0, 2), (0, 3), (1, 3), (1, 2), (3, 2))
PAIRS = len(PAIR_ORDER)
N_BUCKETS = N_GROUPS * PAIRS
BUCKET_ROWS = 32

P_COLS = 9 * D
PCOL_U, PCOL_Q, PCOL_K, PCOL_V, PCOL_G, PCOL_GA, PCOL_GB = 0, 1, 2, 3, 5, 7, 8

MOE_BLK = 256
N_BLOCKS = NTOK // MOE_BLK + N_BUCKETS
SLOT_ROWS = N_BLOCKS * MOE_BLK
TAIL = 128
XROW = D + TAIL

VMEM_LIMIT = 56 * 1024 * 1024

NT_DIMS = (((1,), (1,)), ((), ()))


def _cparams(*sem, row_dma=False):
    return pltpu.CompilerParams(dimension_semantics=sem, vmem_limit_bytes=VMEM_LIMIT,
                                disable_bounds_checks=row_dma)


def _sigmoid(x):
    return 0.5 * jnp.tanh(0.5 * x) + 0.5


def _rms(x):
    return x * lax.rsqrt(jnp.mean(x * x, axis=-1, keepdims=True) + EPS)


ADA_TN = 3072


def _ada_kernel(c_ref, w_ref, b_ref, o_ref):
    c = c_ref[...]
    sc = c * _sigmoid(c)
    o_ref[0] = jnp.dot(sc, w_ref[0], precision=lax.Precision.HIGHEST,
                       preferred_element_type=F32) + b_ref[0]


def _ada(c, w_ada, b_ada):
    nj = N_MOD * D // ADA_TN
    return pl.pallas_call(
        _ada_kernel,
        out_shape=jax.ShapeDtypeStruct((DEPTH, BATCH, N_MOD * D), F32),
        grid=(DEPTH, nj),
        in_specs=[pl.BlockSpec((BATCH, D), lambda l, j: (0, 0)),
                  pl.BlockSpec((1, D, ADA_TN), lambda l, j: (l, 0, j)),
                  pl.BlockSpec((1, 1, ADA_TN), lambda l, j: (l, 0, j))],
        out_specs=pl.BlockSpec((1, BATCH, ADA_TN), lambda l, j: (l, 0, j)),
        compiler_params=_cparams("parallel", "parallel"),
        name="ada",
    )(c, w_ada, b_ada.reshape(DEPTH, 1, N_MOD * D))


NORM_TS = 512


def _modnorm_kernel(x_ref, w_ref, sc_ref, sh_ref, pos_ref, inv_ref, o_ref, cos_ref, sin_ref):
    y = _rms(x_ref[0]) * w_ref[...]
    o_ref[0] = (y * (1.0 + sc_ref[0]) + sh_ref[0]).astype(o_ref.dtype)
    ang = pos_ref[0] * inv_ref[...]
    cos_ref[0] = jnp.cos(ang)
    sin_ref[0] = jnp.sin(ang)


def _modnorm(x, w, sc, sh, positions):
    inv = ROPE_BASE ** (-jnp.arange(HALF, dtype=F32) / HALF)
    pos = positions.astype(F32).reshape(BATCH, SEQ, 1)
    tile = lambda width: pl.BlockSpec((1, NORM_TS, width), lambda b, i: (b, i, 0))
    table = jax.ShapeDtypeStruct((BATCH, SEQ, HALF), F32)
    h, cos, sin = pl.pallas_call(
        _modnorm_kernel,
        out_shape=(jax.ShapeDtypeStruct((BATCH, SEQ, D), BF16), table, table),
        grid=(BATCH, SEQ // NORM_TS),
        in_specs=[tile(D),
                  pl.BlockSpec((1, D), lambda b, i: (0, 0)),
                  pl.BlockSpec((1, 1, D), lambda b, i: (b, 0, 0)),
                  pl.BlockSpec((1, 1, D), lambda b, i: (b, 0, 0)),
                  tile(1),
                  pl.BlockSpec((1, HALF), lambda b, i: (0, 0))],
        out_specs=(tile(D), tile(HALF), tile(HALF)),
        compiler_params=_cparams("parallel", "parallel"),
        name="modnorm",
    )(x, w.reshape(1, D), sc, sh, pos, inv.reshape(1, HALF))
    return h.reshape(NTOK, D), cos.reshape(NTOK, HALF), sin.reshape(NTOK, HALF)


INP_TM = 1024
INP_GROUPS = 9


def _inproj_kernel(h_ref, w1_ref, w2_ref, cos_ref, sin_ref, o_ref, wb1_ref, wb2_ref):
    j = pl.program_id(0)
    m = pl.program_id(1)
    h = h_ref[...]

    @pl.when(m == 0)
    def _():
        wb1_ref[...] = w1_ref[0].astype(BF16)

    @pl.when((m == 0) & (j == 0))
    def _():
        wb2_ref[...] = w2_ref[0].astype(BF16)

    def proj():
        return jnp.dot(h, wb1_ref[...], preferred_element_type=F32)

    @pl.when(j == 0)
    def _():
        a = proj()
        b = jnp.dot(h, wb2_ref[...], preferred_element_type=F32)
        o_ref[...] = (a * _sigmoid(b)).astype(BF16)

    @pl.when((j == 1) | (j == 2))
    def _():
        t = proj()
        scale = jnp.where(j == 2, DK ** -0.5, 1.0).astype(F32)
        cos = cos_ref[...] * scale
        sin = sin_ref[...] * scale
        for hd in range(HEADS):
            c0 = hd * DK
            t1 = t[:, c0:c0 + HALF]
            t2 = t[:, c0 + HALF:c0 + DK]
            o_ref[:, c0:c0 + HALF] = (t1 * cos - t2 * sin).astype(BF16)
            o_ref[:, c0 + HALF:c0 + DK] = (t1 * sin + t2 * cos).astype(BF16)

    @pl.when((j == 3) | (j == 4))
    def _():
        o_ref[...] = proj().astype(BF16)

    @pl.when((j == 5) | (j == 6))
    def _():
        g = proj()
        o_ref[...] = (g * _sigmoid(g)).astype(BF16)

    @pl.when(j >= 7)
    def _():
        o_ref[...] = _sigmoid(proj()).astype(BF16)


def _inproj(h, w_in, l, cos, sin):
    def rope_idx(j, m):
        return (jnp.where((j == 1) | (j == 2), m, 0), 0)

    return pl.pallas_call(
        _inproj_kernel,
        out_shape=jax.ShapeDtypeStruct((NTOK, P_COLS), BF16),
        grid=(INP_GROUPS, NTOK // INP_TM),
        in_specs=[pl.BlockSpec((INP_TM, D), lambda j, m: (m, 0)),
                  pl.BlockSpec((1, D, D), lambda j, m: (l, 0, jnp.where(j == 0, 0, j + 1))),
                  pl.BlockSpec((1, D, D), lambda j, m: (l, 0, 1), pipeline_mode=pl.Buffered(1)),
                  pl.BlockSpec((INP_TM, HALF), rope_idx),
                  pl.BlockSpec((INP_TM, HALF), rope_idx)],
        out_specs=pl.BlockSpec((INP_TM, D), lambda j, m: (m, j)),
        scratch_shapes=[pltpu.VMEM((D, D), BF16), pltpu.VMEM((D, D), BF16)],
        compiler_params=_cparams("arbitrary", "arbitrary"),
        name="inproj",
    )(h, w_in, w_in, cos, sin)


CONV_TS = 512
CONV_HALO = 32
CONV_RC = 32
CONV_CW = 128
CONV_SH = CONV_TS + 24
SUBLANES = 8


def _conv_kernel(u_ref, halo_ref, sga_ref, cw_ref, cb_ref, lnw_ref, lnb_ref, wo_ref, eg_ref, eu_ref, ed_ref,
                 o_ref, og_ref, ou_ref, od_ref, buf_ref, sh_ref, acc_ref, wbf_ref):
    i = pl.program_id(1)

    @pl.when((pl.program_id(0) == 0) & (i == 0))
    def _():
        wbf_ref[...] = wo_ref[0].astype(BF16)

    for src, dst in ((eg_ref, og_ref), (eu_ref, ou_ref), (ed_ref, od_ref)):
        dst[0] = src[0, 0].astype(BF16)

    halo = halo_ref[0].astype(F32)
    buf_ref[0:CONV_HALO, :] = jnp.where(i > 0, halo, 0.0)
    buf_ref[CONV_HALO:CONV_HALO + CONV_TS, :] = u_ref[0].astype(F32)
    for r in range(1, SUBLANES):
        sh_ref[r - 1] = buf_ref[r:r + CONV_SH, :]

    groups = CONV_RC // SUBLANES

    def body(ci, carry):
        r0 = pl.multiple_of(ci * CONV_RC, CONV_RC)
        for cc in range(D // CONV_CW):
            cols = slice(cc * CONV_CW, (cc + 1) * CONV_CW)
            accs = [cb_ref[:, cols]] * groups
            for off in range(2, CONV_K + 2):
                q, r = divmod(off, SUBLANES)
                w8 = cw_ref[off - 2, :, cols]
                for g in range(groups):
                    rows = pl.ds(r0 + SUBLANES * (q + g), SUBLANES)
                    win = buf_ref[rows, cols] if r == 0 else sh_ref[r - 1, rows, cols]
                    accs[g] = accs[g] + win * w8
            for g in range(groups):
                acc_ref[pl.ds(r0 + SUBLANES * g, SUBLANES), cols] = accs[g]
        return carry

    lax.fori_loop(0, CONV_TS // CONV_RC, body, 0)

    c = acc_ref[...]
    mu = jnp.mean(c, axis=-1, keepdims=True)
    cen = c - mu
    var = jnp.mean(cen * cen, axis=-1, keepdims=True)
    y = cen * lax.rsqrt(var + EPS) * lnw_ref[...] + lnb_ref[...]
    y = y * _sigmoid(y)
    out = jnp.dot(y.astype(BF16), wbf_ref[...], preferred_element_type=F32)
    o_ref[0] = (out * sga_ref[0].astype(F32)).astype(BF16)


def _conv_branch(p3, l, conv_w, conv_b, ln_w, ln_b, w_conv_out, w_exp):
    hb = CONV_TS // CONV_HALO
    n_i = SEQ // CONV_TS
    cw8 = jnp.broadcast_to(conv_w[:, None, :], (CONV_K, SUBLANES, D))
    cb8 = jnp.broadcast_to(conv_b[None, :], (SUBLANES, D))
    slab_rows = N_EXPERTS * D // (BATCH * n_i)
    per_mat = D // slab_rows
    assert per_mat * slab_rows == D and slab_rows % 16 == 0

    def slab(b, i):
        s = b * n_i + i
        return s // per_mat, s % per_mat

    exp_in = pl.BlockSpec((1, 1, slab_rows, D), lambda b, i: (l, *slab(b, i), 0))
    exp_out = pl.BlockSpec((1, slab_rows, D), lambda b, i: (*slab(b, i), 0))
    exp_shape = jax.ShapeDtypeStruct((N_EXPERTS, D, D), BF16)
    return pl.pallas_call(
        _conv_kernel,
        out_shape=(jax.ShapeDtypeStruct((BATCH, SEQ, D), BF16), exp_shape, exp_shape, exp_shape),
        grid=(BATCH, n_i),
        in_specs=[pl.BlockSpec((1, CONV_TS, D), lambda b, i: (b, i, PCOL_U)),
                  pl.BlockSpec((1, CONV_HALO, D), lambda b, i: (b, jnp.maximum(i * hb - 1, 0), PCOL_U)),
                  pl.BlockSpec((1, CONV_TS, D), lambda b, i: (b, i, PCOL_GA)),
                  pl.BlockSpec((CONV_K, SUBLANES, D), lambda b, i: (0, 0, 0)),
                  pl.BlockSpec((SUBLANES, D), lambda b, i: (0, 0)),
                  pl.BlockSpec((1, D), lambda b, i: (0, 0)),
                  pl.BlockSpec((1, D), lambda b, i: (0, 0)),
                  pl.BlockSpec((1, D, D), lambda b, i: (l, 0, 0), pipeline_mode=pl.Buffered(1)),
                  exp_in, exp_in, exp_in],
        out_specs=(pl.BlockSpec((1, CONV_TS, D), lambda b, i: (b, i, 0)), exp_out, exp_out, exp_out),
        scratch_shapes=[pltpu.VMEM((CONV_HALO + CONV_TS, D), F32),
                        pltpu.VMEM((SUBLANES - 1, CONV_SH, D), F32),
                        pltpu.VMEM((CONV_TS, D), F32),
                        pltpu.VMEM((D, D), BF16)],
        compiler_params=_cparams("arbitrary", "arbitrary"),
        name="conv_branch",
    )(p3, p3, p3, cw8, cb8, ln_w.reshape(1, D), ln_b.reshape(1, D), w_conv_out, *w_exp)


def _ret_tables():
    hh = jnp.arange(HEADS, dtype=F32)
    log_g = jnp.log1p(-jnp.exp2(-5.0 - hh))
    idx = jnp.arange(CHUNK, dtype=F32)
    rel = idx[:, None] - idx[None, :]
    dmask = jnp.where(rel[None] >= 0, jnp.exp(jnp.maximum(rel, 0.0)[None] * log_g[:, None, None]), 0.0)
    xi = jnp.exp((idx + 1.0)[None, :] * log_g[:, None])[..., None]
    zeta = jnp.exp((CHUNK - 1.0 - idx)[None, :] * log_g[:, None])[..., None]
    g_chunk = jnp.exp(CHUNK * log_g)[:, None, None]
    return (dmask,
            jnp.broadcast_to(xi, (HEADS, CHUNK, DK)),
            jnp.broadcast_to(zeta, (HEADS, CHUNK, DK)),
            jnp.broadcast_to(g_chunk, (HEADS, 1, DV)))


RET_TS = 512


def _ret_kernel(q_ref, k_ref, v01_ref, v23_ref, g01_ref, g23_ref, dm_ref, xi_ref, zt_ref, gc_ref, o_ref, st_ref):
    @pl.when(pl.program_id(1) == 0)
    def _():
        st_ref[...] = jnp.zeros_like(st_ref)

    def body(c, carry):
        r0 = pl.multiple_of(c * CHUNK, CHUNK)
        rows = pl.ds(r0, CHUNK)
        for hd in range(HEADS):
            v_ref, g_ref = (v01_ref, g01_ref) if hd < 2 else (v23_ref, g23_ref)
            vcols = slice((hd % 2) * DV, (hd % 2 + 1) * DV)
            qc = q_ref[0, rows, hd * DK:(hd + 1) * DK]
            kc = k_ref[0, rows, hd * DK:(hd + 1) * DK]
            vc = v_ref[0, rows, vcols]
            scores = lax.dot_general(qc, kc, NT_DIMS, preferred_element_type=F32) * dm_ref[hd]
            inner = jnp.dot(scores.astype(BF16), vc, preferred_element_type=F32)
            st = st_ref[hd]
            qx = (qc.astype(F32) * xi_ref[hd]).astype(BF16)
            cross = jnp.dot(qx, st.astype(BF16), preferred_element_type=F32)
            kzt = (kc.astype(F32) * zt_ref[hd]).T.astype(BF16)
            st_ref[hd] = st * gc_ref[hd] + jnp.dot(kzt, vc, preferred_element_type=F32)
            o = inner + cross
            mu = jnp.mean(o, axis=-1, keepdims=True)
            cen = o - mu
            var = jnp.mean(cen * cen, axis=-1, keepdims=True)
            r = cen * lax.rsqrt(var + EPS)
            o_ref[0, rows, hd * DV:(hd + 1) * DV] = (r * g_ref[0, rows, vcols].astype(F32)).astype(BF16)
        return carry

    lax.fori_loop(0, RET_TS // CHUNK, body, 0)


def _retention(p3, tables):
    dmask, xi, zeta, gch = tables
    tile = lambda col: pl.BlockSpec((1, RET_TS, D), lambda b, i: (b, i, col))
    full = lambda a: pl.BlockSpec(a.shape, lambda b, i: (0, 0, 0))
    return pl.pallas_call(
        _ret_kernel,
        out_shape=jax.ShapeDtypeStruct((BATCH, SEQ, HEADS * DV), BF16),
        grid=(BATCH, SEQ // RET_TS),
        in_specs=[tile(PCOL_Q), tile(PCOL_K), tile(PCOL_V), tile(PCOL_V + 1), tile(PCOL_G), tile(PCOL_G + 1),
                  full(dmask), full(xi), full(zeta), full(gch)],
        out_specs=pl.BlockSpec((1, RET_TS, HEADS * DV), lambda b, i: (b, i, 0)),
        scratch_shapes=[pltpu.VMEM((HEADS, DK, DV), F32)],
        compiler_params=_cparams("parallel", "arbitrary"),
        name="retention",
    )(p3, p3, p3, p3, p3, p3, dmask, xi, zeta, gch)


MIX_TM = 512


def _route_rows(s, sb):
    row = lambda a, e: a[e:e + 1, :]
    best = None
    gidx = None
    for g in range(N_GROUPS):
        v = [row(sb, GROUP_SIZE * g + i) for i in range(GROUP_SIZE)]
        pair_sums = [v[a] + v[b] for a in range(GROUP_SIZE) for b in range(a + 1, GROUP_SIZE)]
        gs = functools.reduce(jnp.maximum, pair_sums)
        if g == 0:
            best, gidx = gs, jnp.zeros(gs.shape, jnp.int32)
        else:
            upd = gs > best
            gidx = jnp.where(upd, g, gidx)
            best = jnp.where(upd, gs, best)

    def pick(a, i):
        out = row(a, i)
        for g in range(1, N_GROUPS):
            out = jnp.where(gidx == g, row(a, GROUP_SIZE * g + i), out)
        return out

    vb = [pick(sb, i) for i in range(GROUP_SIZE)]
    vs = [pick(s, i) for i in range(GROUP_SIZE)]
    m1, i1, s1 = vb[0], jnp.zeros(gidx.shape, jnp.int32), vs[0]
    for i in range(1, GROUP_SIZE):
        upd = vb[i] > m1
        m1 = jnp.where(upd, vb[i], m1)
        i1 = jnp.where(upd, i, i1)
        s1 = jnp.where(upd, vs[i], s1)
    m2 = i2 = s2 = None
    for i in range(GROUP_SIZE):
        cand = jnp.where(i1 == i, -jnp.inf, vb[i])
        if m2 is None:
            m2, i2, s2 = cand, jnp.zeros(gidx.shape, jnp.int32), vs[0]
        else:
            upd = cand > m2
            m2 = jnp.where(upd, cand, m2)
            i2 = jnp.where(upd, i, i2)
            s2 = jnp.where(upd, vs[i], s2)
    den = s1 + s2
    w1 = s1 / den
    w2 = s2 / den
    lo = jnp.minimum(i1, i2)
    hi = jnp.maximum(i1, i2)
    pair = jnp.zeros(gidx.shape, jnp.int32)
    first = jnp.zeros(gidx.shape, jnp.int32)
    for p, (fa, fb) in enumerate(PAIR_ORDER):
        hit = (lo == min(fa, fb)) & (hi == max(fa, fb))
        pair = jnp.where(hit, p, pair)
        first = jnp.where(hit, fa, first)
    bucket = gidx * PAIRS + pair
    first_is_top1 = i1 == first
    return bucket, jnp.where(first_is_top1, w1, w2), jnp.where(first_is_top1, w2, w1)


def _mix_kernel(rg_ref, ya_ref, sgb_ref, x_ref, gt_ref, sc_ref, sh_ref, wr_ref, wo_ref, n2_ref,
                rhi_ref, rlo_ref, rb_ref, tri_ref,
                x1_ref, hx_ref, ri_ref, cnt_ref, carry_ref, wrb_ref, wob_ref):
    m = pl.program_id(0)

    @pl.when(m == 0)
    def _():
        carry_ref[...] = jnp.zeros_like(carry_ref)
        wrb_ref[...] = wr_ref[0].astype(BF16)
        wob_ref[...] = wo_ref[0].astype(BF16)

    yb = jnp.dot(rg_ref[...], wrb_ref[...], preferred_element_type=F32)
    y = ya_ref[...].astype(F32) + sgb_ref[...].astype(F32) * yb
    o = jnp.dot(y.astype(BF16), wob_ref[...], preferred_element_type=F32)
    x1 = x_ref[...] + gt_ref[0] * o
    x1_ref[...] = x1
    h2 = _rms(x1) * n2_ref[...] * (1.0 + sc_ref[0]) + sh_ref[0]
    hx_ref[:, 0:D] = h2

    hi = h2.astype(BF16)
    lo = (h2 - hi.astype(F32)).astype(BF16)
    rhi = rhi_ref[...]
    logits = (lax.dot_general(rhi, hi, NT_DIMS, preferred_element_type=F32)
              + lax.dot_general(rhi, lo, NT_DIMS, preferred_element_type=F32)
              + lax.dot_general(rlo_ref[...], hi, NT_DIMS, preferred_element_type=F32))
    s = _sigmoid(logits)
    bucket, w_lo, w_hi = _route_rows(s, s + rb_ref[...])

    onehot = (lax.broadcasted_iota(jnp.int32, (BUCKET_ROWS, MIX_TM), 0) == bucket).astype(F32)
    prefix = jnp.dot(onehot.astype(BF16), tri_ref[...], preferred_element_type=F32)
    carry = carry_ref[:, 0:1]
    rank = jnp.sum(onehot * (prefix + carry), axis=0, keepdims=True)
    new_carry = carry + jnp.sum(onehot, axis=1, keepdims=True)
    carry_ref[...] = jnp.broadcast_to(new_carry, carry_ref.shape)
    cnt_ref[...] = jnp.broadcast_to(new_carry, cnt_ref.shape)

    rid = lax.broadcasted_iota(jnp.int32, (8, MIX_TM), 0)
    ri_ref[...] = jnp.where(rid == 0, bucket, jnp.where(rid == 1, rank.astype(jnp.int32), 0))
    wid = lax.broadcasted_iota(jnp.int32, (TAIL, MIX_TM), 0)
    wrows = jnp.where(wid == 0, w_lo, jnp.where(wid == 1, w_hi, 0.0))
    hx_ref[:, D:XROW] = wrows.T


def _mix(rg, ya, p, x, gt1, sc2, sh2, w_ret_out, w_out, l, norm2_w, rhi, rlo, rb, tri):
    tpb = SEQ // MIX_TM
    bidx = lambda m: (m // tpb, 0, 0)
    return pl.pallas_call(
        _mix_kernel,
        out_shape=(jax.ShapeDtypeStruct((NTOK, D), F32),
                   jax.ShapeDtypeStruct((NTOK, XROW), F32),
                   jax.ShapeDtypeStruct((8, NTOK), jnp.int32),
                   jax.ShapeDtypeStruct((BUCKET_ROWS, 128), F32)),
        grid=(NTOK // MIX_TM,),
        in_specs=[pl.BlockSpec((MIX_TM, HEADS * DV), lambda m: (m, 0)),
                  pl.BlockSpec((MIX_TM, D), lambda m: (m, 0)),
                  pl.BlockSpec((MIX_TM, D), lambda m: (m, PCOL_GB)),
                  pl.BlockSpec((MIX_TM, D), lambda m: (m, 0)),
                  pl.BlockSpec((1, 1, D), bidx),
                  pl.BlockSpec((1, 1, D), bidx),
                  pl.BlockSpec((1, 1, D), bidx),
                  pl.BlockSpec((1, HEADS * DV, D), lambda m: (l, 0, 0), pipeline_mode=pl.Buffered(1)),
                  pl.BlockSpec((1, D, D), lambda m: (l, 0, 0), pipeline_mode=pl.Buffered(1)),
                  pl.BlockSpec((1, D), lambda m: (0, 0)),
                  pl.BlockSpec((N_EXPERTS, D), lambda m: (0, 0)),
                  pl.BlockSpec((N_EXPERTS, D), lambda m: (0, 0)),
                  pl.BlockSpec((N_EXPERTS, MIX_TM), lambda m: (0, 0)),
                  pl.BlockSpec((MIX_TM, MIX_TM), lambda m: (0, 0))],
        out_specs=(pl.BlockSpec((MIX_TM, D), lambda m: (m, 0)),
                   pl.BlockSpec((MIX_TM, XROW), lambda m: (m, 0)),
                   pl.BlockSpec((8, MIX_TM), lambda m: (0, m)),
                   pl.BlockSpec((BUCKET_ROWS, 128), lambda m: (0, 0))),
        scratch_shapes=[pltpu.VMEM((BUCKET_ROWS, 128), F32),
                        pltpu.VMEM((HEADS * DV, D), BF16), pltpu.VMEM((D, D), BF16)],
        compiler_params=_cparams("arbitrary"),
        name="mix_route",
    )(rg, ya, p, x, gt1, sc2, sh2, w_ret_out, w_out, norm2_w.reshape(1, D), rhi, rlo, rb, tri)


DISP_TG = 1024


def _dispatch_kernel(dest_ref, nval_ref, src_ref, xs_ref, zero_ref, sem, zsem):
    step = pl.program_id(0)
    base = step * DISP_TG

    @pl.when(step == 0)
    def _():
        zero_ref[...] = jnp.zeros_like(zero_ref)

        def zero_copy(b):
            r0 = pl.multiple_of(b * MOE_BLK, MOE_BLK)
            return pltpu.make_async_copy(zero_ref, xs_ref.at[pl.ds(r0, MOE_BLK), :], zsem)

        def zissue(b, carry):
            @pl.when(nval_ref[b] < MOE_BLK)
            def _():
                zero_copy(b).start()
            return carry

        lax.fori_loop(0, N_BLOCKS, zissue, 0)

        def zdrain(b, carry):
            @pl.when(nval_ref[b] < MOE_BLK)
            def _():
                zero_copy(b).wait()
            return carry

        lax.fori_loop(0, N_BLOCKS, zdrain, 0)

    def row_copy(t, d):
        return pltpu.make_async_copy(src_ref.at[pl.ds(t, 1), :], xs_ref.at[pl.ds(d, 1), :], sem)

    for t in range(DISP_TG):
        row_copy(t, dest_ref[base + t]).start(priority=t % 2)
    pltpu.make_async_copy(src_ref, xs_ref.at[pl.ds(0, DISP_TG), :], sem).wait()


def _dispatch(dest, nval, hx):
    return pl.pallas_call(
        _dispatch_kernel,
        out_shape=jax.ShapeDtypeStruct((SLOT_ROWS, XROW), F32),
        grid_spec=pltpu.PrefetchScalarGridSpec(
            num_scalar_prefetch=2,
            grid=(NTOK // DISP_TG,),
            in_specs=[pl.BlockSpec((DISP_TG, XROW), lambda i, d, nv: (i, 0))],
            out_specs=pl.BlockSpec(memory_space=pl.ANY),
            scratch_shapes=[pltpu.VMEM((MOE_BLK, XROW), F32),
                            pltpu.SemaphoreType.DMA(()), pltpu.SemaphoreType.DMA(())]),
        compiler_params=_cparams("arbitrary", row_dma=True),
        name="dispatch",
    )(dest, nval, hx)


def _moe_kernel(elo_ref, ehi_ref, nval_ref, xs_ref, wg1, wu1, wd1, wg2, wu2, wd2, o_ref):
    i = pl.program_id(0)
    nval = nval_ref[i]

    def run(rows):
        x = xs_ref[0:rows, 0:D].astype(BF16)
        w_lo = xs_ref[0:rows, D:D + 1]
        w_hi = xs_ref[0:rows, D + 1:D + 2]

        def expert(wg, wu, wd):
            g = jnp.dot(x, wg[0], preferred_element_type=F32)
            u = jnp.dot(x, wu[0], preferred_element_type=F32)
            a = (g * _sigmoid(g) * u).astype(BF16)
            return jnp.dot(a, wd[0], preferred_element_type=F32)

        o_ref[0:rows, :] = w_lo * expert(wg1, wu1, wd1) + w_hi * expert(wg2, wu2, wd2)

    @pl.when(nval > MOE_BLK // 2)
    def _():
        run(MOE_BLK)

    @pl.when((nval > 0) & (nval <= MOE_BLK // 2))
    def _():
        run(MOE_BLK // 2)
        o_ref[MOE_BLK // 2:MOE_BLK, :] = jnp.zeros((MOE_BLK // 2, D), F32)

    @pl.when(nval == 0)
    def _():
        o_ref[...] = jnp.zeros_like(o_ref)


def _moe(elo, ehi, nval, xs, wg, wu, wd):
    wspec = lambda tab: pl.BlockSpec((1, D, D), lambda i, elo, ehi, nv: ((elo, ehi)[tab][i], 0, 0))
    return pl.pallas_call(
        _moe_kernel,
        out_shape=jax.ShapeDtypeStruct((SLOT_ROWS, D), F32),
        grid_spec=pltpu.PrefetchScalarGridSpec(
            num_scalar_prefetch=3,
            grid=(N_BLOCKS,),
            in_specs=[pl.BlockSpec((MOE_BLK, XROW), lambda i, elo, ehi, nv: (i, 0)),
                      wspec(0), wspec(0), wspec(0), wspec(1), wspec(1), wspec(1)],
            out_specs=pl.BlockSpec((MOE_BLK, D), lambda i, elo, ehi, nv: (i, 0))),
        compiler_params=_cparams("arbitrary"),
        name="experts",
    )(elo, ehi, nval, xs, wg, wu, wd, wg, wu, wd)


COMB_TG = 512


def _combine_kernel(dest_ref, ys_ref, x1_ref, gt_ref, nw_ref, sc_ref, sh_ref, *rest, last):
    if last:
        hn_ref, ybuf, sem = rest
    else:
        x2_ref, hn_ref, ybuf, sem = rest
    step = pl.program_id(0)
    slot = step % 2

    def gather_tile(tile, into):
        base = tile * COMB_TG

        for t in range(COMB_TG):
            src = ys_ref.at[pl.ds(dest_ref[base + t], 1), :]
            pltpu.make_async_copy(src, ybuf.at[into, pl.ds(t, 1), :], sem.at[into]).start(priority=t % 2)

    @pl.when(step == 0)
    def _():
        gather_tile(0, 0)

    @pl.when(step + 1 < pl.num_programs(0))
    def _():
        gather_tile(step + 1, 1 - slot)

    pltpu.make_async_copy(ys_ref.at[pl.ds(0, COMB_TG), :], ybuf.at[slot], sem.at[slot]).wait()

    x2 = x1_ref[...] + gt_ref[0] * ybuf[slot]
    y = _rms(x2) * nw_ref[...]
    if last:
        hn_ref[...] = y
    else:
        x2_ref[...] = x2
        hn_ref[...] = (y * (1.0 + sc_ref[0]) + sh_ref[0]).astype(hn_ref.dtype)


def _combine(dest, ys, x1, gt2, nw, sc, sh, last):
    tpb = SEQ // COMB_TG
    bidx = lambda i, d: (i // tpb, 0, 0)
    tile = pl.BlockSpec((COMB_TG, D), lambda i, d: (i, 0))
    if last:
        out_shape = jax.ShapeDtypeStruct((NTOK, D), F32)
        out_specs = tile
    else:
        out_shape = (jax.ShapeDtypeStruct((NTOK, D), F32), jax.ShapeDtypeStruct((NTOK, D), BF16))
        out_specs = (tile, tile)
    return pl.pallas_call(
        functools.partial(_combine_kernel, last=last),
        out_shape=out_shape,
        grid_spec=pltpu.PrefetchScalarGridSpec(
            num_scalar_prefetch=1,
            grid=(NTOK // COMB_TG,),
            in_specs=[pl.BlockSpec(memory_space=pl.ANY),
                      tile,
                      pl.BlockSpec((1, 1, D), bidx),
                      pl.BlockSpec((1, D), lambda i, d: (0, 0)),
                      pl.BlockSpec((1, 1, D), bidx),
                      pl.BlockSpec((1, 1, D), bidx)],
            out_specs=out_specs,
            scratch_shapes=[pltpu.VMEM((2, COMB_TG, D), F32), pltpu.SemaphoreType.DMA((2,))]),
        compiler_params=_cparams("arbitrary", row_dma=True),
        name="combine",
    )(dest, ys, x1, gt2, nw.reshape(1, D), sc, sh)


def _pair_tables():
    first = [GROUP_SIZE * g + fa for g in range(N_GROUPS) for fa, _ in PAIR_ORDER]
    second = [GROUP_SIZE * g + fb for g in range(N_GROUPS) for _, fb in PAIR_ORDER]
    return np.asarray(first, np.int32), np.asarray(second, np.int32)


def _plan_kernel(cnt_ref, first_ref, second_ref, ri_ref, dest_ref, elo_ref, ehi_ref, nval_ref, start_ref):
    def per_bucket(b, carry):
        start, blk = carry
        count = cnt_ref[b]
        n_blk = (count + MOE_BLK - 1) // MOE_BLK
        start_ref[b] = start

        def per_block(k, c):
            elo_ref[blk + k] = first_ref[b]
            ehi_ref[blk + k] = second_ref[b]
            nval_ref[blk + k] = jnp.minimum(count - k * MOE_BLK, MOE_BLK)
            return c

        lax.fori_loop(0, n_blk, per_block, 0)
        return start + n_blk * MOE_BLK, blk + n_blk

    _, used = lax.fori_loop(0, N_BUCKETS, per_bucket, (jnp.int32(0), jnp.int32(0)))

    def unused_block(k, c):
        elo_ref[k] = first_ref[N_BUCKETS - 1]
        ehi_ref[k] = second_ref[N_BUCKETS - 1]
        nval_ref[k] = 0
        return c

    lax.fori_loop(used, N_BLOCKS, unused_block, 0)

    bucket = ri_ref[0:1, :]
    rank = ri_ref[1:2, :]
    dest = jnp.zeros_like(rank)
    for b in range(N_BUCKETS):
        dest = jnp.where(bucket == b, start_ref[b] + rank, dest)
    dest_ref[...] = dest


def _slot_plan(ri, cnt):
    counts = cnt[:, 0].astype(jnp.int32)
    first_tab, second_tab = _pair_tables()
    smem_out = pl.BlockSpec(memory_space=pltpu.SMEM)
    blocks = jax.ShapeDtypeStruct((N_BLOCKS,), jnp.int32)
    dest, elo, ehi, nval = pl.pallas_call(
        _plan_kernel,
        out_shape=(jax.ShapeDtypeStruct((1, NTOK), jnp.int32), blocks, blocks, blocks),
        grid_spec=pltpu.PrefetchScalarGridSpec(
            num_scalar_prefetch=3,
            grid=(1,),
            in_specs=[pl.BlockSpec((8, NTOK), lambda i, c, f, s: (0, 0))],
            out_specs=(pl.BlockSpec((1, NTOK), lambda i, c, f, s: (0, 0)), smem_out, smem_out, smem_out),
            scratch_shapes=[pltpu.SMEM((BUCKET_ROWS,), jnp.int32)]),
        compiler_params=_cparams("arbitrary"),
        name="slot_plan",
    )(counts, jnp.asarray(first_tab), jnp.asarray(second_tab), ri)
    return dest.reshape(NTOK), elo, ehi, nval


def kernel(x, c, positions, w_ada, b_ada, norm1_w, w_in, conv_w, conv_b, conv_ln_w, conv_ln_b, w_conv_out,
           w_ret_out, w_out, norm2_w, w_router, router_bias, w_exp_gate, w_exp_up, w_exp_down, final_norm_w):
    mod = _ada(c, w_ada, b_ada).reshape(DEPTH, BATCH, N_MOD, 1, D)
    sh1, sc1, gt1, sh2, sc2, gt2 = (mod[:, :, i] for i in range(N_MOD))
    ret_tables = _ret_tables()

    wr_t = w_router.T
    rhi = wr_t.astype(BF16)
    rlo = (wr_t - rhi.astype(F32)).astype(BF16)
    rb = jnp.broadcast_to(router_bias.astype(F32)[:, None], (N_EXPERTS, MIX_TM))
    tri = (jnp.arange(MIX_TM)[:, None] < jnp.arange(MIX_TM)[None, :]).astype(BF16)

    xf = x.reshape(NTOK, D)
    h, cos, sin = _modnorm(x, norm1_w[0], sc1[0], sh1[0], positions)
    out = None
    for l in range(DEPTH):
        p = _inproj(h, w_in, l, cos, sin)
        p3 = p.reshape(BATCH, SEQ, P_COLS)
        ya, wg_bf, wu_bf, wd_bf = _conv_branch(p3, l, conv_w[l], conv_b[l], conv_ln_w[l], conv_ln_b[l], w_conv_out,
                                               (w_exp_gate, w_exp_up, w_exp_down))
        rg = _retention(p3, ret_tables)
        x1, hx, ri, cnt = _mix(rg.reshape(NTOK, HEADS * DV), ya.reshape(NTOK, D), p, xf, gt1[l], sc2[l], sh2[l],
                               w_ret_out, w_out, l, norm2_w[l], rhi, rlo, rb, tri)
        dest, elo, ehi, nval = _slot_plan(ri, cnt)
        xs = _dispatch(dest, nval, hx)
        ys = _moe(elo, ehi, nval, xs, wg_bf, wu_bf, wd_bf)
        if l + 1 < DEPTH:
            xf, h = _combine(dest, ys, x1, gt2[l], norm1_w[l + 1], sc1[l + 1], sh1[l + 1], last=False)
        else:
            out = _combine(dest, ys, x1, gt2[l], final_norm_w, sc1[l], sh1[l], last=True)
    return out.reshape(BATCH, SEQ, D)
```

```python
import functools

import numpy as np
import jax
import jax.numpy as jnp
from jax import lax
from jax.experimental import pallas as pl
from jax.experimental.pallas import tpu as pltpu

F32 = jnp.float32
BF16 = jnp.bfloat16

D = 1024
BATCH = 8
SEQ = 2048
DEPTH = 4
NTOK = BATCH * SEQ
N_MOD = 6
EPS = 1e-6

CONV_K = 31
HEADS = 4
DK = 256
DV = 512
CHUNK = 256
ROPE_BASE = 10000.0
HALF = DK // 2

N_EXPERTS = 16
N_GROUPS = 4
GROUP_SIZE = 4
PAIR_ORDER = ((0, 1), (0, 2), (0, 3), (1, 3), (1, 2), (3, 2))
PAIRS = len(PAIR_ORDER)
N_BUCKETS = N_GROUPS * PAIRS
BUCKET_ROWS = 32

P_COLS = 9 * D
PCOL_U, PCOL_Q, PCOL_K, PCOL_V, PCOL_G, PCOL_GA, PCOL_GB = 0, 1, 2, 3, 5, 7, 8

MOE_BLK = 256
N_BLOCKS = NTOK // MOE_BLK + N_BUCKETS
SLOT_ROWS = N_BLOCKS * MOE_BLK
TAIL = 128
XROW = D + TAIL

VMEM_LIMIT = 56 * 1024 * 1024

NT_DIMS = (((1,), (1,)), ((), ()))


def _cparams(*sem, row_dma=False):
    return pltpu.CompilerParams(dimension_semantics=sem, vmem_limit_bytes=VMEM_LIMIT,
                                disable_bounds_checks=row_dma)


def _sigmoid(x):
    return 0.5 * jnp.tanh(0.5 * x) + 0.5


def _rms(x):
    return x * lax.rsqrt(jnp.mean(x * x, axis=-1, keepdims=True) + EPS)


ADA_TN = 3072


def _ada_kernel(c_ref, w_ref, b_ref, o_ref):
    c = c_ref[...]
    sc = c * _sigmoid(c)
    o_ref[0] = jnp.dot(sc.astype(BF16), w_ref[0].astype(BF16), preferred_element_type=F32) + b_ref[0]


def _ada(c, w_ada, b_ada):
    nj = N_MOD * D // ADA_TN
    return pl.pallas_call(
        _ada_kernel,
        out_shape=jax.ShapeDtypeStruct((DEPTH, BATCH, N_MOD * D), F32),
        grid=(DEPTH, nj),
        in_specs=[pl.BlockSpec((BATCH, D), lambda l, j: (0, 0)),
                  pl.BlockSpec((1, D, ADA_TN), lambda l, j: (l, 0, j)),
                  pl.BlockSpec((1, 1, ADA_TN), lambda l, j: (l, 0, j))],
        out_specs=pl.BlockSpec((1, BATCH, ADA_TN), lambda l, j: (l, 0, j)),
        compiler_params=_cparams("parallel", "parallel"),
        name="ada",
    )(c, w_ada, b_ada.reshape(DEPTH, 1, N_MOD * D))


NORM_TS = 512


def _modnorm_kernel(x_ref, w_ref, sc_ref, sh_ref, pos_ref, inv_ref, o_ref, cos_ref, sin_ref):
    y = _rms(x_ref[0]) * w_ref[...]
    o_ref[0] = (y * (1.0 + sc_ref[0]) + sh_ref[0]).astype(o_ref.dtype)
    ang = pos_ref[0] * inv_ref[...]
    cos_ref[0] = jnp.cos(ang)
    sin_ref[0] = jnp.sin(ang)


def _modnorm(x, w, sc, sh, positions):
    inv = ROPE_BASE ** (-jnp.arange(HALF, dtype=F32) / HALF)
    pos = positions.astype(F32).reshape(BATCH, SEQ, 1)
    tile = lambda width: pl.BlockSpec((1, NORM_TS, width), lambda b, i: (b, i, 0))
    table = jax.ShapeDtypeStruct((BATCH, SEQ, HALF), F32)
    h, cos, sin = pl.pallas_call(
        _modnorm_kernel,
        out_shape=(jax.ShapeDtypeStruct((BATCH, SEQ, D), BF16), table, table),
        grid=(BATCH, SEQ // NORM_TS),
        in_specs=[tile(D),
                  pl.BlockSpec((1, D), lambda b, i: (0, 0)),
                  pl.BlockSpec((1, 1, D), lambda b, i: (b, 0, 0)),
                  pl.BlockSpec((1, 1, D), lambda b, i: (b, 0, 0)),
                  tile(1),
                  pl.BlockSpec((1, HALF), lambda b, i: (0, 0))],
        out_specs=(tile(D), tile(HALF), tile(HALF)),
        compiler_params=_cparams("parallel", "parallel"),
        name="modnorm",
    )(x, w.reshape(1, D), sc, sh, pos, inv.reshape(1, HALF))
    return h.reshape(NTOK, D), cos.reshape(NTOK, HALF), sin.reshape(NTOK, HALF)


INP_TM = 1024
INP_GROUPS = 9


def _inproj_kernel(h_ref, w1_ref, w2_ref, cos_ref, sin_ref, o_ref, wb1_ref, wb2_ref):
    j = pl.program_id(0)
    m = pl.program_id(1)
    h = h_ref[...]

    @pl.when(m == 0)
    def _():
        wb1_ref[...] = w1_ref[0].astype(BF16)

    @pl.when((m == 0) & (j == 0))
    def _():
        wb2_ref[...] = w2_ref[0].astype(BF16)

    def proj():
        return jnp.dot(h, wb1_ref[...], preferred_element_type=F32)

    @pl.when(j == 0)
    def _():
        a = proj()
        b = jnp.dot(h, wb2_ref[...], preferred_element_type=F32)
        o_ref[...] = (a * _sigmoid(b)).astype(BF16)

    @pl.when((j == 1) | (j == 2))
    def _():
        t = proj()
        scale = jnp.where(j == 2, DK ** -0.5, 1.0).astype(F32)
        cos = cos_ref[...] * scale
        sin = sin_ref[...] * scale
        for hd in range(HEADS):
            c0 = hd * DK
            t1 = t[:, c0:c0 + HALF]
            t2 = t[:, c0 + HALF:c0 + DK]
            o_ref[:, c0:c0 + HALF] = (t1 * cos - t2 * sin).astype(BF16)
            o_ref[:, c0 + HALF:c0 + DK] = (t1 * sin + t2 * cos).astype(BF16)

    @pl.when((j == 3) | (j == 4))
    def _():
        o_ref[...] = proj().astype(BF16)

    @pl.when((j == 5) | (j == 6))
    def _():
        g = proj()
        o_ref[...] = (g * _sigmoid(g)).astype(BF16)

    @pl.when(j >= 7)
    def _():
        o_ref[...] = _sigmoid(proj()).astype(BF16)


def _inproj(h, w_in, l, cos, sin):
    def rope_idx(j, m):
        return (jnp.where((j == 1) | (j == 2), m, 0), 0)

    return pl.pallas_call(
        _inproj_kernel,
        out_shape=jax.ShapeDtypeStruct((NTOK, P_COLS), BF16),
        grid=(INP_GROUPS, NTOK // INP_TM),
        in_specs=[pl.BlockSpec((INP_TM, D), lambda j, m: (m, 0)),
                  pl.BlockSpec((1, D, D), lambda j, m: (l, 0, jnp.where(j == 0, 0, j + 1))),
                  pl.BlockSpec((1, D, D), lambda j, m: (l, 0, 1), pipeline_mode=pl.Buffered(1)),
                  pl.BlockSpec((INP_TM, HALF), rope_idx),
                  pl.BlockSpec((INP_TM, HALF), rope_idx)],
        out_specs=pl.BlockSpec((INP_TM, D), lambda j, m: (m, j)),
        scratch_shapes=[pltpu.VMEM((D, D), BF16), pltpu.VMEM((D, D), BF16)],
        compiler_params=_cparams("arbitrary", "arbitrary"),
        name="inproj",
    )(h, w_in, w_in, cos, sin)


CONV_TS = 512
CONV_HALO = 32
CONV_RC = 32
CONV_CW = 128
CONV_SH = CONV_TS + 24
SUBLANES = 8


def _conv_kernel(u_ref, halo_ref, sga_ref, cw_ref, cb_ref, lnw_ref, lnb_ref, wo_ref, eg_ref, eu_ref, ed_ref,
                 o_ref, og_ref, ou_ref, od_ref, buf_ref, sh_ref, acc_ref, wbf_ref):
    i = pl.program_id(1)

    @pl.when((pl.program_id(0) == 0) & (i == 0))
    def _():
        wbf_ref[...] = wo_ref[0].astype(BF16)

    for src, dst in ((eg_ref, og_ref), (eu_ref, ou_ref), (ed_ref, od_ref)):
        dst[0] = src[0, 0].astype(BF16)

    halo = halo_ref[0].astype(F32)
    buf_ref[0:CONV_HALO, :] = jnp.where(i > 0, halo, 0.0)
    buf_ref[CONV_HALO:CONV_HALO + CONV_TS, :] = u_ref[0].astype(F32)
    for r in range(1, SUBLANES):
        sh_ref[r - 1] = buf_ref[r:r + CONV_SH, :]

    groups = CONV_RC // SUBLANES

    def body(ci, carry):
        r0 = pl.multiple_of(ci * CONV_RC, CONV_RC)
        for cc in range(D // CONV_CW):
            cols = slice(cc * CONV_CW, (cc + 1) * CONV_CW)
            accs = [cb_ref[:, cols]] * groups
            for off in range(2, CONV_K + 2):
                q, r = divmod(off, SUBLANES)
                w8 = cw_ref[off - 2, :, cols]
                for g in range(groups):
                    rows = pl.ds(r0 + SUBLANES * (q + g), SUBLANES)
                    win = buf_ref[rows, cols] if r == 0 else sh_ref[r - 1, rows, cols]
                    accs[g] = accs[g] + win * w8
            for g in range(groups):
                acc_ref[pl.ds(r0 + SUBLANES * g, SUBLANES), cols] = accs[g]
        return carry

    lax.fori_loop(0, CONV_TS // CONV_RC, body, 0)

    c = acc_ref[...]
    mu = jnp.mean(c, axis=-1, keepdims=True)
    cen = c - mu
    var = jnp.mean(cen * cen, axis=-1, keepdims=True)
    y = cen * lax.rsqrt(var + EPS) * lnw_ref[...] + lnb_ref[...]
    y = y * _sigmoid(y)
    out = jnp.dot(y.astype(BF16), wbf_ref[...], preferred_element_type=F32)
    o_ref[0] = (out * sga_ref[0].astype(F32)).astype(BF16)


def _conv_branch(p3, l, conv_w, conv_b, ln_w, ln_b, w_conv_out, w_exp):
    hb = CONV_TS // CONV_HALO
    n_i = SEQ // CONV_TS
    cw8 = jnp.broadcast_to(conv_w[:, None, :], (CONV_K, SUBLANES, D))
    cb8 = jnp.broadcast_to(conv_b[None, :], (SUBLANES, D))
    slab_rows = N_EXPERTS * D // (BATCH * n_i)
    per_mat = D // slab_rows
    assert per_mat * slab_rows == D and slab_rows % 16 == 0

    def slab(b, i):
        s = b * n_i + i
        return s // per_mat, s % per_mat

    exp_in = pl.BlockSpec((1, 1, slab_rows, D), lambda b, i: (l, *slab(b, i), 0))
    exp_out = pl.BlockSpec((1, slab_rows, D), lambda b, i: (*slab(b, i), 0))
    exp_shape = jax.ShapeDtypeStruct((N_EXPERTS, D, D), BF16)
    return pl.pallas_call(
        _conv_kernel,
        out_shape=(jax.ShapeDtypeStruct((BATCH, SEQ, D), BF16), exp_shape, exp_shape, exp_shape),
        grid=(BATCH, n_i),
        in_specs=[pl.BlockSpec((1, CONV_TS, D), lambda b, i: (b, i, PCOL_U)),
                  pl.BlockSpec((1, CONV_HALO, D), lambda b, i: (b, jnp.maximum(i * hb - 1, 0), PCOL_U)),
                  pl.BlockSpec((1, CONV_TS, D), lambda b, i: (b, i, PCOL_GA)),
                  pl.BlockSpec((CONV_K, SUBLANES, D), lambda b, i: (0, 0, 0)),
                  pl.BlockSpec((SUBLANES, D), lambda b, i: (0, 0)),
                  pl.BlockSpec((1, D), lambda b, i: (0, 0)),
                  pl.BlockSpec((1, D), lambda b, i: (0, 0)),
                  pl.BlockSpec((1, D, D), lambda b, i: (l, 0, 0), pipeline_mode=pl.Buffered(1)),
                  exp_in, exp_in, exp_in],
        out_specs=(pl.BlockSpec((1, CONV_TS, D), lambda b, i: (b, i, 0)), exp_out, exp_out, exp_out),
        scratch_shapes=[pltpu.VMEM((CONV_HALO + CONV_TS, D), F32),
                        pltpu.VMEM((SUBLANES - 1, CONV_SH, D), F32),
                        pltpu.VMEM((CONV_TS, D), F32),
                        pltpu.VMEM((D, D), BF16)],
        compiler_params=_cparams("arbitrary", "arbitrary"),
        name="conv_branch",
    )(p3, p3, p3, cw8, cb8, ln_w.reshape(1, D), ln_b.reshape(1, D), w_conv_out, *w_exp)


def _ret_tables():
    hh = jnp.arange(HEADS, dtype=F32)
    log_g = jnp.log1p(-jnp.exp2(-5.0 - hh))
    idx = jnp.arange(CHUNK, dtype=F32)
    rel = idx[:, None] - idx[None, :]
    dmask = jnp.where(rel[None] >= 0, jnp.exp(jnp.maximum(rel, 0.0)[None] * log_g[:, None, None]), 0.0)
    xi = jnp.exp((idx + 1.0)[None, :] * log_g[:, None])[..., None]
    zeta = jnp.exp((CHUNK - 1.0 - idx)[None, :] * log_g[:, None])[..., None]
    g_chunk = jnp.exp(CHUNK * log_g)[:, None, None]
    return (dmask,
            jnp.broadcast_to(xi, (HEADS, CHUNK, DK)),
            jnp.broadcast_to(zeta, (HEADS, CHUNK, DK)),
            jnp.broadcast_to(g_chunk, (HEADS, 1, DV)))


RET_TS = 512


def _ret_kernel(q_ref, k_ref, v01_ref, v23_ref, g01_ref, g23_ref, dm_ref, xi_ref, zt_ref, gc_ref, o_ref, st_ref):
    @pl.when(pl.program_id(1) == 0)
    def _():
        st_ref[...] = jnp.zeros_like(st_ref)

    def body(c, carry):
        r0 = pl.multiple_of(c * CHUNK, CHUNK)
        rows = pl.ds(r0, CHUNK)
        for hd in range(HEADS):
            v_ref, g_ref = (v01_ref, g01_ref) if hd < 2 else (v23_ref, g23_ref)
            vcols = slice((hd % 2) * DV, (hd % 2 + 1) * DV)
            qc = q_ref[0, rows, hd * DK:(hd + 1) * DK]
            kc = k_ref[0, rows, hd * DK:(hd + 1) * DK]
            vc = v_ref[0, rows, vcols]
            scores = lax.dot_general(qc, kc, NT_DIMS, preferred_element_type=F32) * dm_ref[hd]
            inner = jnp.dot(scores.astype(BF16), vc, preferred_element_type=F32)
            st = st_ref[hd]
            qx = (qc.astype(F32) * xi_ref[hd]).astype(BF16)
            cross = jnp.dot(qx, st.astype(BF16), preferred_element_type=F32)
            kzt = (kc.astype(F32) * zt_ref[hd]).T.astype(BF16)
            st_ref[hd] = st * gc_ref[hd] + jnp.dot(kzt, vc, preferred_element_type=F32)
            o = inner + cross
            mu = jnp.mean(o, axis=-1, keepdims=True)
            cen = o - mu
            var = jnp.mean(cen * cen, axis=-1, keepdims=True)
            r = cen * lax.rsqrt(var + EPS)
            o_ref[0, rows, hd * DV:(hd + 1) * DV] = (r * g_ref[0, rows, vcols].astype(F32)).astype(BF16)
        return carry

    lax.fori_loop(0, RET_TS // CHUNK, body, 0)


def _retention(p3, tables):
    dmask, xi, zeta, gch = tables
    tile = lambda col: pl.BlockSpec((1, RET_TS, D), lambda b, i: (b, i, col))
    full = lambda a: pl.BlockSpec(a.shape, lambda b, i: (0, 0, 0))
    return pl.pallas_call(
        _ret_kernel,
        out_shape=jax.ShapeDtypeStruct((BATCH, SEQ, HEADS * DV), BF16),
        grid=(BATCH, SEQ // RET_TS),
        in_specs=[tile(PCOL_Q), tile(PCOL_K), tile(PCOL_V), tile(PCOL_V + 1), tile(PCOL_G), tile(PCOL_G + 1),
                  full(dmask), full(xi), full(zeta), full(gch)],
        out_specs=pl.BlockSpec((1, RET_TS, HEADS * DV), lambda b, i: (b, i, 0)),
        scratch_shapes=[pltpu.VMEM((HEADS, DK, DV), F32)],
        compiler_params=_cparams("parallel", "arbitrary"),
        name="retention",
    )(p3, p3, p3, p3, p3, p3, dmask, xi, zeta, gch)


MIX_TM = 512


def _route_rows(s, sb):
    row = lambda a, e: a[e:e + 1, :]
    best = None
    gidx = None
    for g in range(N_GROUPS):
        v = [row(sb, GROUP_SIZE * g + i) for i in range(GROUP_SIZE)]
        pair_sums = [v[a] + v[b] for a in range(GROUP_SIZE) for b in range(a + 1, GROUP_SIZE)]
        gs = functools.reduce(jnp.maximum, pair_sums)
        if g == 0:
            best, gidx = gs, jnp.zeros(gs.shape, jnp.int32)
        else:
            upd = gs > best
            gidx = jnp.where(upd, g, gidx)
            best = jnp.where(upd, gs, best)

    def pick(a, i):
        out = row(a, i)
        for g in range(1, N_GROUPS):
            out = jnp.where(gidx == g, row(a, GROUP_SIZE * g + i), out)
        return out

    vb = [pick(sb, i) for i in range(GROUP_SIZE)]
    vs = [pick(s, i) for i in range(GROUP_SIZE)]
    m1, i1, s1 = vb[0], jnp.zeros(gidx.shape, jnp.int32), vs[0]
    for i in range(1, GROUP_SIZE):
        upd = vb[i] > m1
        m1 = jnp.where(upd, vb[i], m1)
        i1 = jnp.where(upd, i, i1)
        s1 = jnp.where(upd, vs[i], s1)
    m2 = i2 = s2 = None
    for i in range(GROUP_SIZE):
        cand = jnp.where(i1 == i, -jnp.inf, vb[i])
        if m2 is None:
            m2, i2, s2 = cand, jnp.zeros(gidx.shape, jnp.int32), vs[0]
        else:
            upd = cand > m2
            m2 = jnp.where(upd, cand, m2)
            i2 = jnp.where(upd, i, i2)
            s2 = jnp.where(upd, vs[i], s2)
    den = s1 + s2
    w1 = s1 / den
    w2 = s2 / den
    lo = jnp.minimum(i1, i2)
    hi = jnp.maximum(i1, i2)
    pair = jnp.zeros(gidx.shape, jnp.int32)
    first = jnp.zeros(gidx.shape, jnp.int32)
    for p, (fa, fb) in enumerate(PAIR_ORDER):
        hit = (lo == min(fa, fb)) & (hi == max(fa, fb))
        pair = jnp.where(hit, p, pair)
        first = jnp.where(hit, fa, first)
    bucket = gidx * PAIRS + pair
    first_is_top1 = i1 == first
    return bucket, jnp.where(first_is_top1, w1, w2), jnp.where(first_is_top1, w2, w1)


def _mix_kernel(rg_ref, ya_ref, sgb_ref, x_ref, gt_ref, sc_ref, sh_ref, wr_ref, wo_ref, n2_ref,
                rhi_ref, rlo_ref, rb_ref, tri_ref,
                x1_ref, hx_ref, ri_ref, cnt_ref, carry_ref, wrb_ref, wob_ref):
    m = pl.program_id(0)

    @pl.when(m == 0)
    def _():
        carry_ref[...] = jnp.zeros_like(carry_ref)
        wrb_ref[...] = wr_ref[0].astype(BF16)
        wob_ref[...] = wo_ref[0].astype(BF16)

    yb = jnp.dot(rg_ref[...], wrb_ref[...], preferred_element_type=F32)
    y = ya_ref[...].astype(F32) + sgb_ref[...].astype(F32) * yb
    o = jnp.dot(y.astype(BF16), wob_ref[...], preferred_element_type=F32)
    x1 = x_ref[...] + gt_ref[0] * o
    x1_ref[...] = x1
    h2 = _rms(x1) * n2_ref[...] * (1.0 + sc_ref[0]) + sh_ref[0]
    hx_ref[:, 0:D] = h2

    hi = h2.astype(BF16)
    lo = (h2 - hi.astype(F32)).astype(BF16)
    rhi = rhi_ref[...]
    logits = (lax.dot_general(rhi, hi, NT_DIMS, preferred_element_type=F32)
              + lax.dot_general(rhi, lo, NT_DIMS, preferred_element_type=F32)
              + lax.dot_general(rlo_ref[...], hi, NT_DIMS, preferred_element_type=F32))
    s = _sigmoid(logits)
    bucket, w_lo, w_hi = _route_rows(s, s + rb_ref[...])

    onehot = (lax.broadcasted_iota(jnp.int32, (BUCKET_ROWS, MIX_TM), 0) == bucket).astype(F32)
    prefix = jnp.dot(onehot.astype(BF16), tri_ref[...], preferred_element_type=F32)
    carry = carry_ref[:, 0:1]
    rank = jnp.sum(onehot * (prefix + carry), axis=0, keepdims=True)
    new_carry = carry + jnp.sum(onehot, axis=1, keepdims=True)
    carry_ref[...] = jnp.broadcast_to(new_carry, carry_ref.shape)
    cnt_ref[...] = jnp.broadcast_to(new_carry, cnt_ref.shape)

    rid = lax.broadcasted_iota(jnp.int32, (8, MIX_TM), 0)
    ri_ref[...] = jnp.where(rid == 0, bucket, jnp.where(rid == 1, rank.astype(jnp.int32), 0))
    wid = lax.broadcasted_iota(jnp.int32, (TAIL, MIX_TM), 0)
    wrows = jnp.where(wid == 0, w_lo, jnp.where(wid == 1, w_hi, 0.0))
    hx_ref[:, D:XROW] = wrows.T


def _mix(rg, ya, p, x, gt1, sc2, sh2, w_ret_out, w_out, l, norm2_w, rhi, rlo, rb, tri):
    tpb = SEQ // MIX_TM
    bidx = lambda m: (m // tpb, 0, 0)
    return pl.pallas_call(
        _mix_kernel,
        out_shape=(jax.ShapeDtypeStruct((NTOK, D), F32),
                   jax.ShapeDtypeStruct((NTOK, XROW), F32),
                   jax.ShapeDtypeStruct((8, NTOK), jnp.int32),
                   jax.ShapeDtypeStruct((BUCKET_ROWS, 128), F32)),
        grid=(NTOK // MIX_TM,),
        in_specs=[pl.BlockSpec((MIX_TM, HEADS * DV), lambda m: (m, 0)),
                  pl.BlockSpec((MIX_TM, D), lambda m: (m, 0)),
                  pl.BlockSpec((MIX_TM, D), lambda m: (m, PCOL_GB)),
                  pl.BlockSpec((MIX_TM, D), lambda m: (m, 0)),
                  pl.BlockSpec((1, 1, D), bidx),
                  pl.BlockSpec((1, 1, D), bidx),
                  pl.BlockSpec((1, 1, D), bidx),
                  pl.BlockSpec((1, HEADS * DV, D), lambda m: (l, 0, 0), pipeline_mode=pl.Buffered(1)),
                  pl.BlockSpec((1, D, D), lambda m: (l, 0, 0), pipeline_mode=pl.Buffered(1)),
                  pl.BlockSpec((1, D), lambda m: (0, 0)),
                  pl.BlockSpec((N_EXPERTS, D), lambda m: (0, 0)),
                  pl.BlockSpec((N_EXPERTS, D), lambda m: (0, 0)),
                  pl.BlockSpec((N_EXPERTS, MIX_TM), lambda m: (0, 0)),
                  pl.BlockSpec((MIX_TM, MIX_TM), lambda m: (0, 0))],
        out_specs=(pl.BlockSpec((MIX_TM, D), lambda m: (m, 0)),
                   pl.BlockSpec((MIX_TM, XROW), lambda m: (m, 0)),
                   pl.BlockSpec((8, MIX_TM), lambda m: (0, m)),
                   pl.BlockSpec((BUCKET_ROWS, 128), lambda m: (0, 0))),
        scratch_shapes=[pltpu.VMEM((BUCKET_ROWS, 128), F32),
                        pltpu.VMEM((HEADS * DV, D), BF16), pltpu.VMEM((D, D), BF16)],
        compiler_params=_cparams("arbitrary"),
        name="mix_route",
    )(rg, ya, p, x, gt1, sc2, sh2, w_ret_out, w_out, norm2_w.reshape(1, D), rhi, rlo, rb, tri)


DISP_TG = 1024


def _dispatch_kernel(dest_ref, nval_ref, src_ref, xs_ref, zero_ref, sem, zsem):
    step = pl.program_id(0)
    base = step * DISP_TG

    @pl.when(step == 0)
    def _():
        zero_ref[...] = jnp.zeros_like(zero_ref)

        def zero_copy(b):
            r0 = pl.multiple_of(b * MOE_BLK, MOE_BLK)
            return pltpu.make_async_copy(zero_ref, xs_ref.at[pl.ds(r0, MOE_BLK), :], zsem)

        def zissue(b, carry):
            @pl.when(nval_ref[b] < MOE_BLK)
            def _():
                zero_copy(b).start()
            return carry

        lax.fori_loop(0, N_BLOCKS, zissue, 0)

        def zdrain(b, carry):
            @pl.when(nval_ref[b] < MOE_BLK)
            def _():
                zero_copy(b).wait()
            return carry

        lax.fori_loop(0, N_BLOCKS, zdrain, 0)

    def row_copy(t, d):
        return pltpu.make_async_copy(src_ref.at[pl.ds(t, 1), :], xs_ref.at[pl.ds(d, 1), :], sem)

    for t in range(DISP_TG):
        row_copy(t, dest_ref[base + t]).start(priority=t % 2)
    pltpu.make_async_copy(src_ref, xs_ref.at[pl.ds(0, DISP_TG), :], sem).wait()


def _dispatch(dest, nval, hx):
    return pl.pallas_call(
        _dispatch_kernel,
        out_shape=jax.ShapeDtypeStruct((SLOT_ROWS, XROW), F32),
        grid_spec=pltpu.PrefetchScalarGridSpec(
            num_scalar_prefetch=2,
            grid=(NTOK // DISP_TG,),
            in_specs=[pl.BlockSpec((DISP_TG, XROW), lambda i, d, nv: (i, 0))],
            out_specs=pl.BlockSpec(memory_space=pl.ANY),
            scratch_shapes=[pltpu.VMEM((MOE_BLK, XROW), F32),
                            pltpu.SemaphoreType.DMA(()), pltpu.SemaphoreType.DMA(())]),
        compiler_params=_cparams("arbitrary", row_dma=True),
        name="dispatch",
    )(dest, nval, hx)


def _moe_kernel(elo_ref, ehi_ref, nval_ref, xs_ref, wg1, wu1, wd1, wg2, wu2, wd2, o_ref):
    i = pl.program_id(0)
    nval = nval_ref[i]

    def run(rows):
        x = xs_ref[0:rows, 0:D].astype(BF16)
        w_lo = xs_ref[0:rows, D:D + 1]
        w_hi = xs_ref[0:rows, D + 1:D + 2]

        def expert(wg, wu, wd):
            g = jnp.dot(x, wg[0], preferred_element_type=F32)
            u = jnp.dot(x, wu[0], preferred_element_type=F32)
            a = (g * _sigmoid(g) * u).astype(BF16)
            return jnp.dot(a, wd[0], preferred_element_type=F32)

        o_ref[0:rows, :] = w_lo * expert(wg1, wu1, wd1) + w_hi * expert(wg2, wu2, wd2)

    @pl.when(nval > MOE_BLK // 2)
    def _():
        run(MOE_BLK)

    @pl.when((nval > 0) & (nval <= MOE_BLK // 2))
    def _():
        run(MOE_BLK // 2)
        o_ref[MOE_BLK // 2:MOE_BLK, :] = jnp.zeros((MOE_BLK // 2, D), F32)

    @pl.when(nval == 0)
    def _():
        o_ref[...] = jnp.zeros_like(o_ref)


def _moe(elo, ehi, nval, xs, wg, wu, wd):
    wspec = lambda tab: pl.BlockSpec((1, D, D), lambda i, elo, ehi, nv: ((elo, ehi)[tab][i], 0, 0))
    return pl.pallas_call(
        _moe_kernel,
        out_shape=jax.ShapeDtypeStruct((SLOT_ROWS, D), F32),
        grid_spec=pltpu.PrefetchScalarGridSpec(
            num_scalar_prefetch=3,
            grid=(N_BLOCKS,),
            in_specs=[pl.BlockSpec((MOE_BLK, XROW), lambda i, elo, ehi, nv: (i, 0)),
                      wspec(0), wspec(0), wspec(0), wspec(1), wspec(1), wspec(1)],
            out_specs=pl.BlockSpec((MOE_BLK, D), lambda i, elo, ehi, nv: (i, 0))),
        compiler_params=_cparams("arbitrary"),
        name="experts",
    )(elo, ehi, nval, xs, wg, wu, wd, wg, wu, wd)


COMB_TG = 512


def _combine_kernel(dest_ref, ys_ref, x1_ref, gt_ref, nw_ref, sc_ref, sh_ref, *rest, last):
    if last:
        hn_ref, ybuf, sem = rest
    else:
        x2_ref, hn_ref, ybuf, sem = rest
    step = pl.program_id(0)
    slot = step % 2

    def gather_tile(tile, into):
        base = tile * COMB_TG

        for t in range(COMB_TG):
            src = ys_ref.at[pl.ds(dest_ref[base + t], 1), :]
            pltpu.make_async_copy(src, ybuf.at[into, pl.ds(t, 1), :], sem.at[into]).start(priority=t % 2)

    @pl.when(step == 0)
    def _():
        gather_tile(0, 0)

    @pl.when(step + 1 < pl.num_programs(0))
    def _():
        gather_tile(step + 1, 1 - slot)

    pltpu.make_async_copy(ys_ref.at[pl.ds(0, COMB_TG), :], ybuf.at[slot], sem.at[slot]).wait()

    x2 = x1_ref[...] + gt_ref[0] * ybuf[slot]
    y = _rms(x2) * nw_ref[...]
    if last:
        hn_ref[...] = y
    else:
        x2_ref[...] = x2
        hn_ref[...] = (y * (1.0 + sc_ref[0]) + sh_ref[0]).astype(hn_ref.dtype)


def _combine(dest, ys, x1, gt2, nw, sc, sh, last):
    tpb = SEQ // COMB_TG
    bidx = lambda i, d: (i // tpb, 0, 0)
    tile = pl.BlockSpec((COMB_TG, D), lambda i, d: (i, 0))
    if last:
        out_shape = jax.ShapeDtypeStruct((NTOK, D), F32)
        out_specs = tile
    else:
        out_shape = (jax.ShapeDtypeStruct((NTOK, D), F32), jax.ShapeDtypeStruct((NTOK, D), BF16))
        out_specs = (tile, tile)
    return pl.pallas_call(
        functools.partial(_combine_kernel, last=last),
        out_shape=out_shape,
        grid_spec=pltpu.PrefetchScalarGridSpec(
            num_scalar_prefetch=1,
            grid=(NTOK // COMB_TG,),
            in_specs=[pl.BlockSpec(memory_space=pl.ANY),
                      tile,
                      pl.BlockSpec((1, 1, D), bidx),
                      pl.BlockSpec((1, D), lambda i, d: (0, 0)),
                      pl.BlockSpec((1, 1, D), bidx),
                      pl.BlockSpec((1, 1, D), bidx)],
            out_specs=out_specs,
            scratch_shapes=[pltpu.VMEM((2, COMB_TG, D), F32), pltpu.SemaphoreType.DMA((2,))]),
        compiler_params=_cparams("arbitrary", row_dma=True),
        name="combine",
    )(dest, ys, x1, gt2, nw.reshape(1, D), sc, sh)


def _pair_tables():
    first = [GROUP_SIZE * g + fa for g in range(N_GROUPS) for fa, _ in PAIR_ORDER]
    second = [GROUP_SIZE * g + fb for g in range(N_GROUPS) for _, fb in PAIR_ORDER]
    return np.asarray(first, np.int32), np.asarray(second, np.int32)


def _plan_kernel(cnt_ref, first_ref, second_ref, ri_ref, dest_ref, elo_ref, ehi_ref, nval_ref, start_ref):
    def per_bucket(b, carry):
        start, blk = carry
        count = cnt_ref[b]
        n_blk = (count + MOE_BLK - 1) // MOE_BLK
        start_ref[b] = start

        def per_block(k, c):
            elo_ref[blk + k] = first_ref[b]
            ehi_ref[blk + k] = second_ref[b]
            nval_ref[blk + k] = jnp.minimum(count - k * MOE_BLK, MOE_BLK)
            return c

        lax.fori_loop(0, n_blk, per_block, 0)
        return start + n_blk * MOE_BLK, blk + n_blk

    _, used = lax.fori_loop(0, N_BUCKETS, per_bucket, (jnp.int32(0), jnp.int32(0)))

    def unused_block(k, c):
        elo_ref[k] = first_ref[N_BUCKETS - 1]
        ehi_ref[k] = second_ref[N_BUCKETS - 1]
        nval_ref[k] = 0
        return c

    lax.fori_loop(used, N_BLOCKS, unused_block, 0)

    bucket = ri_ref[0:1, :]
    rank = ri_ref[1:2, :]
    dest = jnp.zeros_like(rank)
    for b in range(N_BUCKETS):
        dest = jnp.where(bucket == b, start_ref[b] + rank, dest)
    dest_ref[...] = dest


def _slot_plan(ri, cnt):
    counts = cnt[:, 0].astype(jnp.int32)
    first_tab, second_tab = _pair_tables()
    smem_out = pl.BlockSpec(memory_space=pltpu.SMEM)
    blocks = jax.ShapeDtypeStruct((N_BLOCKS,), jnp.int32)
    dest, elo, ehi, nval = pl.pallas_call(
        _plan_kernel,
        out_shape=(jax.ShapeDtypeStruct((1, NTOK), jnp.int32), blocks, blocks, blocks),
        grid_spec=pltpu.PrefetchScalarGridSpec(
            num_scalar_prefetch=3,
            grid=(1,),
            in_specs=[pl.BlockSpec((8, NTOK), lambda i, c, f, s: (0, 0))],
            out_specs=(pl.BlockSpec((1, NTOK), lambda i, c, f, s: (0, 0)), smem_out, smem_out, smem_out),
            scratch_shapes=[pltpu.SMEM((BUCKET_ROWS,), jnp.int32)]),
        compiler_params=_cparams("arbitrary"),
        name="slot_plan",
    )(counts, jnp.asarray(first_tab), jnp.asarray(second_tab), ri)
    return dest.reshape(NTOK), elo, ehi, nval


def kernel(x, c, positions, w_ada, b_ada, norm1_w, w_in, conv_w, conv_b, conv_ln_w, conv_ln_b, w_conv_out,
           w_ret_out, w_out, norm2_w, w_router, router_bias, w_exp_gate, w_exp_up, w_exp_down, final_norm_w):
    mod = _ada(c, w_ada, b_ada).reshape(DEPTH, BATCH, N_MOD, 1, D)
    sh1, sc1, gt1, sh2, sc2, gt2 = (mod[:, :, i] for i in range(N_MOD))
    ret_tables = _ret_tables()

    wr_t = w_router.T
    rhi = wr_t.astype(BF16)
    rlo = (wr_t - rhi.astype(F32)).astype(BF16)
    rb = jnp.broadcast_to(router_bias.astype(F32)[:, None], (N_EXPERTS, MIX_TM))
    tri = (jnp.arange(MIX_TM)[:, None] < jnp.arange(MIX_TM)[None, :]).astype(BF16)

    xf = x.reshape(NTOK, D)
    h, cos, sin = _modnorm(x, norm1_w[0], sc1[0], sh1[0], positions)
    out = None
    for l in range(DEPTH):
        p = _inproj(h, w_in, l, cos, sin)
        p3 = p.reshape(BATCH, SEQ, P_COLS)
        ya, wg_bf, wu_bf, wd_bf = _conv_branch(p3, l, conv_w[l], conv_b[l], conv_ln_w[l], conv_ln_b[l], w_conv_out,
                                               (w_exp_gate, w_exp_up, w_exp_down))
        rg = _retention(p3, ret_tables)
        x1, hx, ri, cnt = _mix(rg.reshape(NTOK, HEADS * DV), ya.reshape(NTOK, D), p, xf, gt1[l], sc2[l], sh2[l],
                               w_ret_out, w_out, l, norm2_w[l], rhi, rlo, rb, tri)
        dest, elo, ehi, nval = _slot_plan(ri, cnt)
        xs = _dispatch(dest, nval, hx)
        ys = _moe(elo, ehi, nval, xs, wg_bf, wu_bf, wd_bf)
        if l + 1 < DEPTH:
            xf, h = _combine(dest, ys, x1, gt2[l], norm1_w[l + 1], sc1[l + 1], sh1[l + 1], last=False)
        else:
            out = _combine(dest, ys, x1, gt2[l], final_norm_w, sc1[l], sh1[l], last=True)
    return out.reshape(BATCH, SEQ, D)
```

```python
import functools

import numpy as np
import jax
import jax.numpy as jnp
from jax import lax
from jax.experimental import pallas as pl
from jax.experimental.pallas import tpu as pltpu

F32 = jnp.float32
BF16 = jnp.bfloat16

D = 1024
BATCH = 8
SEQ = 2048
DEPTH = 4
NTOK = BATCH * SEQ
N_MOD = 6
EPS = 1e-6

CONV_K = 31
HEADS = 4
DK = 256
DV = 512
CHUNK = 256
ROPE_BASE = 10000.0
HALF = DK // 2

N_EXPERTS = 16
N_GROUPS = 4
GROUP_SIZE = 4
PAIR_ORDER = ((0, 1), (0, 2), (0, 3), (1, 3), (1, 2), (3, 2))
PAIRS = len(PAIR_ORDER)
N_BUCKETS = N_GROUPS * PAIRS
BUCKET_ROWS = 32

P_COLS = 9 * D
PCOL_U, PCOL_Q, PCOL_K, PCOL_V, PCOL_G, PCOL_GA, PCOL_GB = 0, 1, 2, 3, 5, 7, 8

MOE_BLK = 256
N_BLOCKS = NTOK // MOE_BLK + N_BUCKETS
SLOT_ROWS = N_BLOCKS * MOE_BLK
TAIL = 128
XROW = D + TAIL

VMEM_LIMIT = 56 * 1024 * 1024

NT_DIMS = (((1,), (1,)), ((), ()))


def _cparams(*sem, row_dma=False):
    return pltpu.CompilerParams(dimension_semantics=sem, vmem_limit_bytes=VMEM_LIMIT,
                                disable_bounds_checks=row_dma)


def _sigmoid(x):
    return 0.5 * jnp.tanh(0.5 * x) + 0.5


def _rms(x):
    return x * lax.rsqrt(jnp.mean(x * x, axis=-1, keepdims=True) + EPS)


ADA_TN = 3072


def _ada_kernel(c_ref, w_ref, b_ref, o_ref):
    c = c_ref[...]
    sc = c * _sigmoid(c)
    o_ref[0] = jnp.dot(sc, w_ref[0], precision=lax.Precision.HIGHEST,
                       preferred_element_type=F32) + b_ref[0]


def _ada(c, w_ada, b_ada):
    nj = N_MOD * D // ADA_TN
    return pl.pallas_call(
        _ada_kernel,
        out_shape=jax.ShapeDtypeStruct((DEPTH, BATCH, N_MOD * D), F32),
        grid=(DEPTH, nj),
        in_specs=[pl.BlockSpec((BATCH, D), lambda l, j: (0, 0)),
                  pl.BlockSpec((1, D, ADA_TN), lambda l, j: (l, 0, j)),
                  pl.BlockSpec((1, 1, ADA_TN), lambda l, j: (l, 0, j))],
        out_specs=pl.BlockSpec((1, BATCH, ADA_TN), lambda l, j: (l, 0, j)),
        compiler_params=_cparams("parallel", "parallel"),
        name="ada",
    )(c, w_ada, b_ada.reshape(DEPTH, 1, N_MOD * D))


NORM_TS = 512


def _modnorm_kernel(x_ref, w_ref, sc_ref, sh_ref, pos_ref, inv_ref, o_ref, cos_ref, sin_ref):
    y = _rms(x_ref[0]) * w_ref[...]
    o_ref[0] = (y * (1.0 + sc_ref[0]) + sh_ref[0]).astype(o_ref.dtype)
    ang = pos_ref[0] * inv_ref[...]
    cos_ref[0] = jnp.cos(ang)
    sin_ref[0] = jnp.sin(ang)


def _modnorm(x, w, sc, sh, positions):
    inv = ROPE_BASE ** (-jnp.arange(HALF, dtype=F32) / HALF)
    pos = positions.astype(F32).reshape(BATCH, SEQ, 1)
    tile = lambda width: pl.BlockSpec((1, NORM_TS, width), lambda b, i: (b, i, 0))
    table = jax.ShapeDtypeStruct((BATCH, SEQ, HALF), F32)
    h, cos, sin = pl.pallas_call(
        _modnorm_kernel,
        out_shape=(jax.ShapeDtypeStruct((BATCH, SEQ, D), BF16), table, table),
        grid=(BATCH, SEQ // NORM_TS),
        in_specs=[tile(D),
                  pl.BlockSpec((1, D), lambda b, i: (0, 0)),
                  pl.BlockSpec((1, 1, D), lambda b, i: (b, 0, 0)),
                  pl.BlockSpec((1, 1, D), lambda b, i: (b, 0, 0)),
                  tile(1),
                  pl.BlockSpec((1, HALF), lambda b, i: (0, 0))],
        out_specs=(tile(D), tile(HALF), tile(HALF)),
        compiler_params=_cparams("parallel", "parallel"),
        name="modnorm",
    )(x, w.reshape(1, D), sc, sh, pos, inv.reshape(1, HALF))
    return h.reshape(NTOK, D), cos.reshape(NTOK, HALF), sin.reshape(NTOK, HALF)


INP_TM = 1024
INP_GROUPS = 9


def _inproj_kernel(h_ref, w1_ref, w2_ref, cos_ref, sin_ref, o_ref, wb1_ref, wb2_ref):
    j = pl.program_id(0)
    m = pl.program_id(1)
    h = h_ref[...]

    @pl.when(m == 0)
    def _():
        wb1_ref[...] = w1_ref[0].astype(BF16)

    @pl.when((m == 0) & (j == 0))
    def _():
        wb2_ref[...] = w2_ref[0].astype(BF16)

    def proj():
        return jnp.dot(h, wb1_ref[...], preferred_element_type=F32)

    @pl.when(j == 0)
    def _():
        a = proj()
        b = jnp.dot(h, wb2_ref[...], preferred_element_type=F32)
        o_ref[...] = (a * _sigmoid(b)).astype(BF16)

    @pl.when((j == 1) | (j == 2))
    def _():
        t = proj()
        scale = jnp.where(j == 2, DK ** -0.5, 1.0).astype(F32)
        cos = cos_ref[...] * scale
        sin = sin_ref[...] * scale
        for hd in range(HEADS):
            c0 = hd * DK
            t1 = t[:, c0:c0 + HALF]
            t2 = t[:, c0 + HALF:c0 + DK]
            o_ref[:, c0:c0 + HALF] = (t1 * cos - t2 * sin).astype(BF16)
            o_ref[:, c0 + HALF:c0 + DK] = (t1 * sin + t2 * cos).astype(BF16)

    @pl.when((j == 3) | (j == 4))
    def _():
        o_ref[...] = proj().astype(BF16)

    @pl.when((j == 5) | (j == 6))
    def _():
        g = proj()
        o_ref[...] = (g * _sigmoid(g)).astype(BF16)

    @pl.when(j >= 7)
    def _():
        o_ref[...] = _sigmoid(proj()).astype(BF16)


def _inproj(h, w_in, l, cos, sin):
    def rope_idx(j, m):
        return (jnp.where((j == 1) | (j == 2), m, 0), 0)

    return pl.pallas_call(
        _inproj_kernel,
        out_shape=jax.ShapeDtypeStruct((NTOK, P_COLS), BF16),
        grid=(INP_GROUPS, NTOK // INP_TM),
        in_specs=[pl.BlockSpec((INP_TM, D), lambda j, m: (m, 0)),
                  pl.BlockSpec((1, D, D), lambda j, m: (l, 0, jnp.where(j == 0, 0, j + 1))),
                  pl.BlockSpec((1, D, D), lambda j, m: (l, 0, 1), pipeline_mode=pl.Buffered(1)),
                  pl.BlockSpec((INP_TM, HALF), rope_idx),
                  pl.BlockSpec((INP_TM, HALF), rope_idx)],
        out_specs=pl.BlockSpec((INP_TM, D), lambda j, m: (m, j)),
        scratch_shapes=[pltpu.VMEM((D, D), BF16), pltpu.VMEM((D, D), BF16)],
        compiler_params=_cparams("arbitrary", "arbitrary"),
        name="inproj",
    )(h, w_in, w_in, cos, sin)


CONV_TS = 512
CONV_HALO = 32
CONV_RC = 32
CONV_CW = 128
CONV_SH = CONV_TS + 24
SUBLANES = 8


def _conv_kernel(u_ref, halo_ref, sga_ref, cw_ref, cb_ref, lnw_ref, lnb_ref, wo_ref, eg_ref, eu_ref, ed_ref,
                 o_ref, og_ref, ou_ref, od_ref, buf_ref, sh_ref, acc_ref, wbf_ref):
    i = pl.program_id(1)

    @pl.when((pl.program_id(0) == 0) & (i == 0))
    def _():
        wbf_ref[...] = wo_ref[0].astype(BF16)

    for src, dst in ((eg_ref, og_ref), (eu_ref, ou_ref), (ed_ref, od_ref)):
        dst[0] = src[0, 0].astype(BF16)

    halo = halo_ref[0].astype(F32)
    buf_ref[0:CONV_HALO, :] = jnp.where(i > 0, halo, 0.0)
    buf_ref[CONV_HALO:CONV_HALO + CONV_TS, :] = u_ref[0].astype(F32)
    for r in range(1, SUBLANES):
        sh_ref[r - 1] = buf_ref[r:r + CONV_SH, :]

    groups = CONV_RC // SUBLANES

    def body(ci, carry):
        r0 = pl.multiple_of(ci * CONV_RC, CONV_RC)
        for cc in range(D // CONV_CW):
            cols = slice(cc * CONV_CW, (cc + 1) * CONV_CW)
            accs = [cb_ref[:, cols]] * groups
            for off in range(2, CONV_K + 2):
                q, r = divmod(off, SUBLANES)
                w8 = cw_ref[off - 2, :, cols]
                for g in range(groups):
                    rows = pl.ds(r0 + SUBLANES * (q + g), SUBLANES)
                    win = buf_ref[rows, cols] if r == 0 else sh_ref[r - 1, rows, cols]
                    accs[g] = accs[g] + win * w8
            for g in range(groups):
                acc_ref[pl.ds(r0 + SUBLANES * g, SUBLANES), cols] = accs[g]
        return carry

    lax.fori_loop(0, CONV_TS // CONV_RC, body, 0)

    c = acc_ref[...]
    mu = jnp.mean(c, axis=-1, keepdims=True)
    cen = c - mu
    var = jnp.mean(cen * cen, axis=-1, keepdims=True)
    y = cen * lax.rsqrt(var + EPS) * lnw_ref[...] + lnb_ref[...]
    y = y * _sigmoid(y)
    out = jnp.dot(y.astype(BF16), wbf_ref[...], preferred_element_type=F32)
    o_ref[0] = (out * sga_ref[0].astype(F32)).astype(BF16)


def _conv_branch(p3, l, conv_w, conv_b, ln_w, ln_b, w_conv_out, w_exp):
    hb = CONV_TS // CONV_HALO
    n_i = SEQ // CONV_TS
    cw8 = jnp.broadcast_to(conv_w[:, None, :], (CONV_K, SUBLANES, D))
    cb8 = jnp.broadcast_to(conv_b[None, :], (SUBLANES, D))
    slab_rows = N_EXPERTS * D // (BATCH * n_i)
    per_mat = D // slab_rows
    assert per_mat * slab_rows == D and slab_rows % 16 == 0

    def slab(b, i):
        s = b * n_i + i
        return s // per_mat, s % per_mat

    exp_in = pl.BlockSpec((1, 1, slab_rows, D), lambda b, i: (l, *slab(b, i), 0))
    exp_out = pl.BlockSpec((1, slab_rows, D), lambda b, i: (*slab(b, i), 0))
    exp_shape = jax.ShapeDtypeStruct((N_EXPERTS, D, D), BF16)
    return pl.pallas_call(
        _conv_kernel,
        out_shape=(jax.ShapeDtypeStruct((BATCH, SEQ, D), BF16), exp_shape, exp_shape, exp_shape),
        grid=(BATCH, n_i),
        in_specs=[pl.BlockSpec((1, CONV_TS, D), lambda b, i: (b, i, PCOL_U)),
                  pl.BlockSpec((1, CONV_HALO, D), lambda b, i: (b, jnp.maximum(i * hb - 1, 0), PCOL_U)),
                  pl.BlockSpec((1, CONV_TS, D), lambda b, i: (b, i, PCOL_GA)),
                  pl.BlockSpec((CONV_K, SUBLANES, D), lambda b, i: (0, 0, 0)),
                  pl.BlockSpec((SUBLANES, D), lambda b, i: (0, 0)),
                  pl.BlockSpec((1, D), lambda b, i: (0, 0)),
                  pl.BlockSpec((1, D), lambda b, i: (0, 0)),
                  pl.BlockSpec((1, D, D), lambda b, i: (l, 0, 0), pipeline_mode=pl.Buffered(1)),
                  exp_in, exp_in, exp_in],
        out_specs=(pl.BlockSpec((1, CONV_TS, D), lambda b, i: (b, i, 0)), exp_out, exp_out, exp_out),
        scratch_shapes=[pltpu.VMEM((CONV_HALO + CONV_TS, D), F32),
                        pltpu.VMEM((SUBLANES - 1, CONV_SH, D), F32),
                        pltpu.VMEM((CONV_TS, D), F32),
                        pltpu.VMEM((D, D), BF16)],
        compiler_params=_cparams("arbitrary", "arbitrary"),
        name="conv_branch",
    )(p3, p3, p3, cw8, cb8, ln_w.reshape(1, D), ln_b.reshape(1, D), w_conv_out, *w_exp)


def _ret_tables():
    hh = jnp.arange(HEADS, dtype=F32)
    log_g = jnp.log1p(-jnp.exp2(-5.0 - hh))
    idx = jnp.arange(CHUNK, dtype=F32)
    rel = idx[:, None] - idx[None, :]
    dmask = jnp.where(rel[None] >= 0, jnp.exp(jnp.maximum(rel, 0.0)[None] * log_g[:, None, None]), 0.0)
    xi = jnp.exp((idx + 1.0)[None, :] * log_g[:, None])[..., None]
    zeta = jnp.exp((CHUNK - 1.0 - idx)[None, :] * log_g[:, None])[..., None]
    g_chunk = jnp.exp(CHUNK * log_g)[:, None, None]
    return (dmask,
            jnp.broadcast_to(xi, (HEADS, CHUNK, DK)),
            jnp.broadcast_to(zeta, (HEADS, CHUNK, DK)),
            jnp.broadcast_to(g_chunk, (HEADS, 1, DV)))


RET_TS = 512


def _ret_kernel(q_ref, k_ref, v01_ref, v23_ref, g01_ref, g23_ref, dm_ref, xi_ref, zt_ref, gc_ref, o_ref, st_ref):
    @pl.when(pl.program_id(1) == 0)
    def _():
        st_ref[...] = jnp.zeros_like(st_ref)

    def body(c, carry):
        r0 = pl.multiple_of(c * CHUNK, CHUNK)
        rows = pl.ds(r0, CHUNK)
        for hd in range(HEADS):
            v_ref, g_ref = (v01_ref, g01_ref) if hd < 2 else (v23_ref, g23_ref)
            vcols = slice((hd % 2) * DV, (hd % 2 + 1) * DV)
            qc = q_ref[0, rows, hd * DK:(hd + 1) * DK]
            kc = k_ref[0, rows, hd * DK:(hd + 1) * DK]
            vc = v_ref[0, rows, vcols]
            scores = lax.dot_general(qc, kc, NT_DIMS, preferred_element_type=F32) * dm_ref[hd]
            inner = jnp.dot(scores.astype(BF16), vc, preferred_element_type=F32)
            st = st_ref[hd]
            qx = (qc.astype(F32) * xi_ref[hd]).astype(BF16)
            cross = jnp.dot(qx, st.astype(BF16), preferred_element_type=F32)
            kzt = (kc.astype(F32) * zt_ref[hd]).T.astype(BF16)
            st_ref[hd] = st * gc_ref[hd] + jnp.dot(kzt, vc, preferred_element_type=F32)
            o = inner + cross
            mu = jnp.mean(o, axis=-1, keepdims=True)
            cen = o - mu
            var = jnp.mean(cen * cen, axis=-1, keepdims=True)
            r = cen * lax.rsqrt(var + EPS)
            o_ref[0, rows, hd * DV:(hd + 1) * DV] = (r * g_ref[0, rows, vcols].astype(F32)).astype(BF16)
        return carry

    lax.fori_loop(0, RET_TS // CHUNK, body, 0)


def _retention(p3, tables):
    dmask, xi, zeta, gch = tables
    tile = lambda col: pl.BlockSpec((1, RET_TS, D), lambda b, i: (b, i, col))
    full = lambda a: pl.BlockSpec(a.shape, lambda b, i: (0, 0, 0))
    return pl.pallas_call(
        _ret_kernel,
        out_shape=jax.ShapeDtypeStruct((BATCH, SEQ, HEADS * DV), BF16),
        grid=(BATCH, SEQ // RET_TS),
        in_specs=[tile(PCOL_Q), tile(PCOL_K), tile(PCOL_V), tile(PCOL_V + 1), tile(PCOL_G), tile(PCOL_G + 1),
                  full(dmask), full(xi), full(zeta), full(gch)],
        out_specs=pl.BlockSpec((1, RET_TS, HEADS * DV), lambda b, i: (b, i, 0)),
        scratch_shapes=[pltpu.VMEM((HEADS, DK, DV), F32)],
        compiler_params=_cparams("parallel", "arbitrary"),
        name="retention",
    )(p3, p3, p3, p3, p3, p3, dmask, xi, zeta, gch)


MIX_TM = 512


def _route_rows(s, sb):
    row = lambda a, e: a[e:e + 1, :]
    best = None
    gidx = None
    for g in range(N_GROUPS):
        v = [row(sb, GROUP_SIZE * g + i) for i in range(GROUP_SIZE)]
        pair_sums = [v[a] + v[b] for a in range(GROUP_SIZE) for b in range(a + 1, GROUP_SIZE)]
        gs = functools.reduce(jnp.maximum, pair_sums)
        if g == 0:
            best, gidx = gs, jnp.zeros(gs.shape, jnp.int32)
        else:
            upd = gs > best
            gidx = jnp.where(upd, g, gidx)
            best = jnp.where(upd, gs, best)

    def pick(a, i):
        out = row(a, i)
        for g in range(1, N_GROUPS):
            out = jnp.where(gidx == g, row(a, GROUP_SIZE * g + i), out)
        return out

    vb = [pick(sb, i) for i in range(GROUP_SIZE)]
    vs = [pick(s, i) for i in range(GROUP_SIZE)]
    m1, i1, s1 = vb[0], jnp.zeros(gidx.shape, jnp.int32), vs[0]
    for i in range(1, GROUP_SIZE):
        upd = vb[i] > m1
        m1 = jnp.where(upd, vb[i], m1)
        i1 = jnp.where(upd, i, i1)
        s1 = jnp.where(upd, vs[i], s1)
    m2 = i2 = s2 = None
    for i in range(GROUP_SIZE):
        cand = jnp.where(i1 == i, -jnp.inf, vb[i])
        if m2 is None:
            m2, i2, s2 = cand, jnp.zeros(gidx.shape, jnp.int32), vs[0]
        else:
            upd = cand > m2
            m2 = jnp.where(upd, cand, m2)
            i2 = jnp.where(upd, i, i2)
            s2 = jnp.where(upd, vs[i], s2)
    den = s1 + s2
    w1 = s1 / den
    w2 = s2 / den
    lo = jnp.minimum(i1, i2)
    hi = jnp.maximum(i1, i2)
    pair = jnp.zeros(gidx.shape, jnp.int32)
    first = jnp.zeros(gidx.shape, jnp.int32)
    for p, (fa, fb) in enumerate(PAIR_ORDER):
        hit = (lo == min(fa, fb)) & (hi == max(fa, fb))
        pair = jnp.where(hit, p, pair)
        first = jnp.where(hit, fa, first)
    bucket = gidx * PAIRS + pair
    first_is_top1 = i1 == first
    return bucket, jnp.where(first_is_top1, w1, w2), jnp.where(first_is_top1, w2, w1)


def _mix_kernel(rg_ref, ya_ref, sgb_ref, x_ref, gt_ref, sc_ref, sh_ref, wr_ref, wo_ref, n2_ref,
                rhi_ref, rlo_ref, rb_ref, tri_ref,
                x1_ref, hx_ref, ri_ref, cnt_ref, carry_ref, wrb_ref, wob_ref):
    m = pl.program_id(0)

    @pl.when(m == 0)
    def _():
        carry_ref[...] = jnp.zeros_like(carry_ref)
        wrb_ref[...] = wr_ref[0].astype(BF16)
        wob_ref[...] = wo_ref[0].astype(BF16)

    yb = jnp.dot(rg_ref[...], wrb_ref[...], preferred_element_type=F32)
    y = ya_ref[...].astype(F32) + sgb_ref[...].astype(F32) * yb
    o = jnp.dot(y.astype(BF16), wob_ref[...], preferred_element_type=F32)
    x1 = x_ref[...] + gt_ref[0] * o
    x1_ref[...] = x1
    h2 = _rms(x1) * n2_ref[...] * (1.0 + sc_ref[0]) + sh_ref[0]
    hx_ref[:, 0:D] = h2

    hi = h2.astype(BF16)
    lo = (h2 - hi.astype(F32)).astype(BF16)
    rhi = rhi_ref[...]
    logits = (lax.dot_general(rhi, hi, NT_DIMS, preferred_element_type=F32)
              + lax.dot_general(rhi, lo, NT_DIMS, preferred_element_type=F32)
              + lax.dot_general(rlo_ref[...], hi, NT_DIMS, preferred_element_type=F32))
    s = _sigmoid(logits)
    bucket, w_lo, w_hi = _route_rows(s, s + rb_ref[...])

    onehot = (lax.broadcasted_iota(jnp.int32, (BUCKET_ROWS, MIX_TM), 0) == bucket).astype(F32)
    prefix = jnp.dot(onehot.astype(BF16), tri_ref[...], preferred_element_type=F32)
    carry = carry_ref[:, 0:1]
    rank = jnp.sum(onehot * (prefix + carry), axis=0, keepdims=True)
    new_carry = carry + jnp.sum(onehot, axis=1, keepdims=True)
    carry_ref[...] = jnp.broadcast_to(new_carry, carry_ref.shape)
    cnt_ref[...] = jnp.broadcast_to(new_carry, cnt_ref.shape)

    rid = lax.broadcasted_iota(jnp.int32, (8, MIX_TM), 0)
    ri_ref[...] = jnp.where(rid == 0, bucket, jnp.where(rid == 1, rank.astype(jnp.int32), 0))
    wid = lax.broadcasted_iota(jnp.int32, (TAIL, MIX_TM), 0)
    wrows = jnp.where(wid == 0, w_lo, jnp.where(wid == 1, w_hi, 0.0))
    hx_ref[:, D:XROW] = wrows.T


def _mix(rg, ya, p, x, gt1, sc2, sh2, w_ret_out, w_out, l, norm2_w, rhi, rlo, rb, tri):
    tpb = SEQ // MIX_TM
    bidx = lambda m: (m // tpb, 0, 0)
    return pl.pallas_call(
        _mix_kernel,
        out_shape=(jax.ShapeDtypeStruct((NTOK, D), F32),
                   jax.ShapeDtypeStruct((NTOK, XROW), F32),
                   jax.ShapeDtypeStruct((8, NTOK), jnp.int32),
                   jax.ShapeDtypeStruct((BUCKET_ROWS, 128), F32)),
        grid=(NTOK // MIX_TM,),
        in_specs=[pl.BlockSpec((MIX_TM, HEADS * DV), lambda m: (m, 0)),
                  pl.BlockSpec((MIX_TM, D), lambda m: (m, 0)),
                  pl.BlockSpec((MIX_TM, D), lambda m: (m, PCOL_GB)),
                  pl.BlockSpec((MIX_TM, D), lambda m: (m, 0)),
                  pl.BlockSpec((1, 1, D), bidx),
                  pl.BlockSpec((1, 1, D), bidx),
                  pl.BlockSpec((1, 1, D), bidx),
                  pl.BlockSpec((1, HEADS * DV, D), lambda m: (l, 0, 0), pipeline_mode=pl.Buffered(1)),
                  pl.BlockSpec((1, D, D), lambda m: (l, 0, 0), pipeline_mode=pl.Buffered(1)),
                  pl.BlockSpec((1, D), lambda m: (0, 0)),
                  pl.BlockSpec((N_EXPERTS, D), lambda m: (0, 0)),
                  pl.BlockSpec((N_EXPERTS, D), lambda m: (0, 0)),
                  pl.BlockSpec((N_EXPERTS, MIX_TM), lambda m: (0, 0)),
                  pl.BlockSpec((MIX_TM, MIX_TM), lambda m: (0, 0))],
        out_specs=(pl.BlockSpec((MIX_TM, D), lambda m: (m, 0)),
                   pl.BlockSpec((MIX_TM, XROW), lambda m: (m, 0)),
                   pl.BlockSpec((8, MIX_TM), lambda m: (0, m)),
                   pl.BlockSpec((BUCKET_ROWS, 128), lambda m: (0, 0))),
        scratch_shapes=[pltpu.VMEM((BUCKET_ROWS, 128), F32),
                        pltpu.VMEM((HEADS * DV, D), BF16), pltpu.VMEM((D, D), BF16)],
        compiler_params=_cparams("arbitrary"),
        name="mix_route",
    )(rg, ya, p, x, gt1, sc2, sh2, w_ret_out, w_out, norm2_w.reshape(1, D), rhi, rlo, rb, tri)


DISP_TG = 1024


def _dispatch_kernel(dest_ref, nval_ref, src_ref, xs_ref, zero_ref, sem, zsem):
    step = pl.program_id(0)
    base = step * DISP_TG

    @pl.when(step == 0)
    def _():
        zero_ref[...] = jnp.zeros_like(zero_ref)

        def zero_copy(b):
            r0 = pl.multiple_of(b * MOE_BLK, MOE_BLK)
            return pltpu.make_async_copy(zero_ref, xs_ref.at[pl.ds(r0, MOE_BLK), :], zsem)

        def zissue(b, carry):
            @pl.when(nval_ref[b] < MOE_BLK)
            def _():
                zero_copy(b).start()
            return carry

        lax.fori_loop(0, N_BLOCKS, zissue, 0)

        def zdrain(b, carry):
            @pl.when(nval_ref[b] < MOE_BLK)
            def _():
                zero_copy(b).wait()
            return carry

        lax.fori_loop(0, N_BLOCKS, zdrain, 0)

    def row_copy(t, d):
        return pltpu.make_async_copy(src_ref.at[pl.ds(t, 1), :], xs_ref.at[pl.ds(d, 1), :], sem)

    for t in range(DISP_TG):
        row_copy(t, dest_ref[base + t]).start(priority=t % 2)
    pltpu.make_async_copy(src_ref, xs_ref.at[pl.ds(0, DISP_TG), :], sem).wait()


def _dispatch(dest, nval, hx):
    return pl.pallas_call(
        _dispatch_kernel,
        out_shape=jax.ShapeDtypeStruct((SLOT_ROWS, XROW), F32),
        grid_spec=pltpu.PrefetchScalarGridSpec(
            num_scalar_prefetch=2,
            grid=(NTOK // DISP_TG,),
            in_specs=[pl.BlockSpec((DISP_TG, XROW), lambda i, d, nv: (i, 0))],
            out_specs=pl.BlockSpec(memory_space=pl.ANY),
            scratch_shapes=[pltpu.VMEM((MOE_BLK, XROW), F32),
                            pltpu.SemaphoreType.DMA(()), pltpu.SemaphoreType.DMA(())]),
        compiler_params=_cparams("arbitrary", row_dma=True),
        name="dispatch",
    )(dest, nval, hx)


def _moe_kernel(elo_ref, ehi_ref, nval_ref, xs_ref, wg1, wu1, wd1, wg2, wu2, wd2, o_ref):
    i = pl.program_id(0)
    nval = nval_ref[i]

    def run(rows):
        x = xs_ref[0:rows, 0:D].astype(BF16)
        w_lo = xs_ref[0:rows, D:D + 1]
        w_hi = xs_ref[0:rows, D + 1:D + 2]

        def expert(wg, wu, wd):
            g = jnp.dot(x, wg[0], preferred_element_type=F32)
            u = jnp.dot(x, wu[0], preferred_element_type=F32)
            a = (g * _sigmoid(g) * u).astype(BF16)
            return jnp.dot(a, wd[0], preferred_element_type=F32)

        o_ref[0:rows, :] = w_lo * expert(wg1, wu1, wd1) + w_hi * expert(wg2, wu2, wd2)

    @pl.when(nval > MOE_BLK // 2)
    def _():
        run(MOE_BLK)

    @pl.when((nval > 0) & (nval <= MOE_BLK // 2))
    def _():
        run(MOE_BLK // 2)
        o_ref[MOE_BLK // 2:MOE_BLK, :] = jnp.zeros((MOE_BLK // 2, D), F32)

    @pl.when(nval == 0)
    def _():
        o_ref[...] = jnp.zeros_like(o_ref)


def _moe(elo, ehi, nval, xs, wg, wu, wd):
    wspec = lambda tab: pl.BlockSpec((1, D, D), lambda i, elo, ehi, nv: ((elo, ehi)[tab][i], 0, 0))
    return pl.pallas_call(
        _moe_kernel,
        out_shape=jax.ShapeDtypeStruct((SLOT_ROWS, D), F32),
        grid_spec=pltpu.PrefetchScalarGridSpec(
            num_scalar_prefetch=3,
            grid=(N_BLOCKS,),
            in_specs=[pl.BlockSpec((MOE_BLK, XROW), lambda i, elo, ehi, nv: (i, 0)),
                      wspec(0), wspec(0), wspec(0), wspec(1), wspec(1), wspec(1)],
            out_specs=pl.BlockSpec((MOE_BLK, D), lambda i, elo, ehi, nv: (i, 0))),
        compiler_params=_cparams("arbitrary"),
        name="experts",
    )(elo, ehi, nval, xs, wg, wu, wd, wg, wu, wd)


COMB_TG = 256


def _combine_kernel(dest_ref, ys_ref, x1_ref, gt_ref, nw_ref, sc_ref, sh_ref, *rest, last):
    if last:
        hn_ref, ybuf, sem = rest
    else:
        x2_ref, hn_ref, ybuf, sem = rest
    step = pl.program_id(0)
    slot = step % 2

    def gather_tile(tile, into):
        base = tile * COMB_TG

        for t in range(COMB_TG):
            src = ys_ref.at[pl.ds(dest_ref[base + t], 1), :]
            pltpu.make_async_copy(src, ybuf.at[into, pl.ds(t, 1), :], sem.at[into]).start(priority=t % 2)

    @pl.when(step == 0)
    def _():
        gather_tile(0, 0)

    @pl.when(step + 1 < pl.num_programs(0))
    def _():
        gather_tile(step + 1, 1 - slot)

    pltpu.make_async_copy(ys_ref.at[pl.ds(0, COMB_TG), :], ybuf.at[slot], sem.at[slot]).wait()

    x2 = x1_ref[...] + gt_ref[0] * ybuf[slot]
    y = _rms(x2) * nw_ref[...]
    if last:
        hn_ref[...] = y
    else:
        x2_ref[...] = x2
        hn_ref[...] = (y * (1.0 + sc_ref[0]) + sh_ref[0]).astype(hn_ref.dtype)


def _combine(dest, ys, x1, gt2, nw, sc, sh, last):
    tpb = SEQ // COMB_TG
    bidx = lambda i, d: (i // tpb, 0, 0)
    tile = pl.BlockSpec((COMB_TG, D), lambda i, d: (i, 0))
    if last:
        out_shape = jax.ShapeDtypeStruct((NTOK, D), F32)
        out_specs = tile
    else:
        out_shape = (jax.ShapeDtypeStruct((NTOK, D), F32), jax.ShapeDtypeStruct((NTOK, D), BF16))
        out_specs = (tile, tile)
    return pl.pallas_call(
        functools.partial(_combine_kernel, last=last),
        out_shape=out_shape,
        grid_spec=pltpu.PrefetchScalarGridSpec(
            num_scalar_prefetch=1,
            grid=(NTOK // COMB_TG,),
            in_specs=[pl.BlockSpec(memory_space=pl.ANY),
                      tile,
                      pl.BlockSpec((1, 1, D), bidx),
                      pl.BlockSpec((1, D), lambda i, d: (0, 0)),
                      pl.BlockSpec((1, 1, D), bidx),
                      pl.BlockSpec((1, 1, D), bidx)],
            out_specs=out_specs,
            scratch_shapes=[pltpu.VMEM((2, COMB_TG, D), F32), pltpu.SemaphoreType.DMA((2,))]),
        compiler_params=_cparams("arbitrary", row_dma=True),
        name="combine",
    )(dest, ys, x1, gt2, nw.reshape(1, D), sc, sh)


def _pair_tables():
    first = [GROUP_SIZE * g + fa for g in range(N_GROUPS) for fa, _ in PAIR_ORDER]
    second = [GROUP_SIZE * g + fb for g in range(N_GROUPS) for _, fb in PAIR_ORDER]
    return np.asarray(first, np.int32), np.asarray(second, np.int32)


def _plan_kernel(cnt_ref, first_ref, second_ref, ri_ref, dest_ref, elo_ref, ehi_ref, nval_ref, start_ref):
    def per_bucket(b, carry):
        start, blk = carry
        count = cnt_ref[b]
        n_blk = (count + MOE_BLK - 1) // MOE_BLK
        start_ref[b] = start

        def per_block(k, c):
            elo_ref[blk + k] = first_ref[b]
            ehi_ref[blk + k] = second_ref[b]
            nval_ref[blk + k] = jnp.minimum(count - k * MOE_BLK, MOE_BLK)
            return c

        lax.fori_loop(0, n_blk, per_block, 0)
        return start + n_blk * MOE_BLK, blk + n_blk

    _, used = lax.fori_loop(0, N_BUCKETS, per_bucket, (jnp.int32(0), jnp.int32(0)))

    def unused_block(k, c):
        elo_ref[k] = first_ref[N_BUCKETS - 1]
        ehi_ref[k] = second_ref[N_BUCKETS - 1]
        nval_ref[k] = 0
        return c

    lax.fori_loop(used, N_BLOCKS, unused_block, 0)

    bucket = ri_ref[0:1, :]
    rank = ri_ref[1:2, :]
    dest = jnp.zeros_like(rank)
    for b in range(N_BUCKETS):
        dest = jnp.where(bucket == b, start_ref[b] + rank, dest)
    dest_ref[...] = dest


def _slot_plan(ri, cnt):
    counts = cnt[:, 0].astype(jnp.int32)
    first_tab, second_tab = _pair_tables()
    smem_out = pl.BlockSpec(memory_space=pltpu.SMEM)
    blocks = jax.ShapeDtypeStruct((N_BLOCKS,), jnp.int32)
    dest, elo, ehi, nval = pl.pallas_call(
        _plan_kernel,
        out_shape=(jax.ShapeDtypeStruct((1, NTOK), jnp.int32), blocks, blocks, blocks),
        grid_spec=pltpu.PrefetchScalarGridSpec(
            num_scalar_prefetch=3,
            grid=(1,),
            in_specs=[pl.BlockSpec((8, NTOK), lambda i, c, f, s: (0, 0))],
            out_specs=(pl.BlockSpec((1, NTOK), lambda i, c, f, s: (0, 0)), smem_out, smem_out, smem_out),
            scratch_shapes=[pltpu.SMEM((BUCKET_ROWS,), jnp.int32)]),
        compiler_params=_cparams("arbitrary"),
        name="slot_plan",
    )(counts, jnp.asarray(first_tab), jnp.asarray(second_tab), ri)
    return dest.reshape(NTOK), elo, ehi, nval


def kernel(x, c, positions, w_ada, b_ada, norm1_w, w_in, conv_w, conv_b, conv_ln_w, conv_ln_b, w_conv_out,
           w_ret_out, w_out, norm2_w, w_router, router_bias, w_exp_gate, w_exp_up, w_exp_down, final_norm_w):
    mod = _ada(c, w_ada, b_ada).reshape(DEPTH, BATCH, N_MOD, 1, D)
    sh1, sc1, gt1, sh2, sc2, gt2 = (mod[:, :, i] for i in range(N_MOD))
    ret_tables = _ret_tables()

    wr_t = w_router.T
    rhi = wr_t.astype(BF16)
    rlo = (wr_t - rhi.astype(F32)).astype(BF16)
    rb = jnp.broadcast_to(router_bias.astype(F32)[:, None], (N_EXPERTS, MIX_TM))
    tri = (jnp.arange(MIX_TM)[:, None] < jnp.arange(MIX_TM)[None, :]).astype(BF16)

    xf = x.reshape(NTOK, D)
    h, cos, sin = _modnorm(x, norm1_w[0], sc1[0], sh1[0], positions)
    out = None
    for l in range(DEPTH):
        p = _inproj(h, w_in, l, cos, sin)
        p3 = p.reshape(BATCH, SEQ, P_COLS)
        ya, wg_bf, wu_bf, wd_bf = _conv_branch(p3, l, conv_w[l], conv_b[l], conv_ln_w[l], conv_ln_b[l], w_conv_out,
                                               (w_exp_gate, w_exp_up, w_exp_down))
        rg = _retention(p3, ret_tables)
        x1, hx, ri, cnt = _mix(rg.reshape(NTOK, HEADS * DV), ya.reshape(NTOK, D), p, xf, gt1[l], sc2[l], sh2[l],
                               w_ret_out, w_out, l, norm2_w[l], rhi, rlo, rb, tri)
        dest, elo, ehi, nval = _slot_plan(ri, cnt)
        xs = _dispatch(dest, nval, hx)
        ys = _moe(elo, ehi, nval, xs, wg_bf, wu_bf, wd_bf)
        if l + 1 < DEPTH:
            xf, h = _combine(dest, ys, x1, gt2[l], norm1_w[l + 1], sc1[l + 1], sh1[l + 1], last=False)
        else:
            out = _combine(dest, ys, x1, gt2[l], final_norm_w, sc1[l], sh1[l], last=True)
    return out.reshape(BATCH, SEQ, D)
```

```python
import functools

import numpy as np
import jax
import jax.numpy as jnp
from jax import lax
from jax.experimental import pallas as pl
from jax.experimental.pallas import tpu as pltpu

F32 = jnp.float32
BF16 = jnp.bfloat16

D = 1024
BATCH = 8
SEQ = 2048
DEPTH = 4
NTOK = BATCH * SEQ
N_MOD = 6
EPS = 1e-6

CONV_K = 31
HEADS = 4
DK = 256
DV = 512
CHUNK = 256
ROPE_BASE = 10000.0
HALF = DK // 2

N_EXPERTS = 16
N_GROUPS = 4
GROUP_SIZE = 4
PAIR_ORDER = ((0, 1), (0, 2), (0, 3), (1, 3), (1, 2), (3, 2))
PAIRS = len(PAIR_ORDER)
N_BUCKETS = N_GROUPS * PAIRS
BUCKET_ROWS = 32

P_COLS = 9 * D
PCOL_U, PCOL_Q, PCOL_K, PCOL_V, PCOL_G, PCOL_GA, PCOL_GB = 0, 1, 2, 3, 5, 7, 8

MOE_BLK = 256
N_BLOCKS = NTOK // MOE_BLK + N_BUCKETS
SLOT_ROWS = N_BLOCKS * MOE_BLK
TAIL = 128
XROW = D + TAIL

VMEM_LIMIT = 56 * 1024 * 1024

NT_DIMS = (((1,), (1,)), ((), ()))


def _cparams(*sem, row_dma=False):
    return pltpu.CompilerParams(dimension_semantics=sem, vmem_limit_bytes=VMEM_LIMIT,
                                disable_bounds_checks=row_dma)


def _sigmoid(x):
    return 0.5 * jnp.tanh(0.5 * x) + 0.5


def _rms(x):
    return x * lax.rsqrt(jnp.mean(x * x, axis=-1, keepdims=True) + EPS)


ADA_TN = 3072


def _ada_kernel(c_ref, w_ref, b_ref, o_ref):
    c = c_ref[...]
    sc = c * _sigmoid(c)
    o_ref[0] = jnp.dot(sc, w_ref[0], precision=lax.Precision.HIGHEST,
                       preferred_element_type=F32) + b_ref[0]


def _ada(c, w_ada, b_ada):
    nj = N_MOD * D // ADA_TN
    return pl.pallas_call(
        _ada_kernel,
        out_shape=jax.ShapeDtypeStruct((DEPTH, BATCH, N_MOD * D), F32),
        grid=(DEPTH, nj),
        in_specs=[pl.BlockSpec((BATCH, D), lambda l, j: (0, 0)),
                  pl.BlockSpec((1, D, ADA_TN), lambda l, j: (l, 0, j)),
                  pl.BlockSpec((1, 1, ADA_TN), lambda l, j: (l, 0, j))],
        out_specs=pl.BlockSpec((1, BATCH, ADA_TN), lambda l, j: (l, 0, j)),
        compiler_params=_cparams("parallel", "parallel"),
        name="ada",
    )(c, w_ada, b_ada.reshape(DEPTH, 1, N_MOD * D))


NORM_TS = 512


def _modnorm_kernel(x_ref, w_ref, sc_ref, sh_ref, pos_ref, inv_ref, o_ref, cos_ref, sin_ref):
    y = _rms(x_ref[0]) * w_ref[...]
    o_ref[0] = (y * (1.0 + sc_ref[0]) + sh_ref[0]).astype(o_ref.dtype)
    ang = pos_ref[0] * inv_ref[...]
    cos_ref[0] = jnp.cos(ang)
    sin_ref[0] = jnp.sin(ang)


def _modnorm(x, w, sc, sh, positions):
    inv = ROPE_BASE ** (-jnp.arange(HALF, dtype=F32) / HALF)
    pos = positions.astype(F32).reshape(BATCH, SEQ, 1)
    tile = lambda width: pl.BlockSpec((1, NORM_TS, width), lambda b, i: (b, i, 0))
    table = jax.ShapeDtypeStruct((BATCH, SEQ, HALF), F32)
    h, cos, sin = pl.pallas_call(
        _modnorm_kernel,
        out_shape=(jax.ShapeDtypeStruct((BATCH, SEQ, D), BF16), table, table),
        grid=(BATCH, SEQ // NORM_TS),
        in_specs=[tile(D),
                  pl.BlockSpec((1, D), lambda b, i: (0, 0)),
                  pl.BlockSpec((1, 1, D), lambda b, i: (b, 0, 0)),
                  pl.BlockSpec((1, 1, D), lambda b, i: (b, 0, 0)),
                  tile(1),
                  pl.BlockSpec((1, HALF), lambda b, i: (0, 0))],
        out_specs=(tile(D), tile(HALF), tile(HALF)),
        compiler_params=_cparams("parallel", "parallel"),
        name="modnorm",
    )(x, w.reshape(1, D), sc, sh, pos, inv.reshape(1, HALF))
    return h.reshape(NTOK, D), cos.reshape(NTOK, HALF), sin.reshape(NTOK, HALF)


INP_TM = 1024
INP_GROUPS = 9
INP_CHUNK = 256


def _inproj_kernel(h_ref, w1_ref, w2_ref, cos_ref, sin_ref, o_ref, wb1_ref, wb2_ref):
    j = pl.program_id(0)
    m = pl.program_id(1)
    h = h_ref[...]

    @pl.when(m == 0)
    def _():
        wb1_ref[...] = w1_ref[0].astype(BF16)

    @pl.when((m == 0) & (j == 0))
    def _():
        wb2_ref[...] = w2_ref[0].astype(BF16)

    chunks = [slice(c0, c0 + INP_CHUNK) for c0 in range(0, D, INP_CHUNK)]

    def proj(cols, w_ref=wb1_ref):
        return jnp.dot(h, w_ref[:, cols], preferred_element_type=F32)

    @pl.when(j == 0)
    def _():
        for cols in chunks:
            o_ref[:, cols] = (proj(cols) * _sigmoid(proj(cols, wb2_ref))).astype(BF16)

    @pl.when((j == 1) | (j == 2))
    def _():
        scale = jnp.where(j == 2, DK ** -0.5, 1.0).astype(F32)
        cos = cos_ref[...] * scale
        sin = sin_ref[...] * scale
        for hd in range(HEADS):
            c0 = hd * DK
            t = proj(slice(c0, c0 + DK))
            t1 = t[:, 0:HALF]
            t2 = t[:, HALF:DK]
            o_ref[:, c0:c0 + HALF] = (t1 * cos - t2 * sin).astype(BF16)
            o_ref[:, c0 + HALF:c0 + DK] = (t1 * sin + t2 * cos).astype(BF16)

    @pl.when((j == 3) | (j == 4))
    def _():
        for cols in chunks:
            o_ref[:, cols] = proj(cols).astype(BF16)

    @pl.when((j == 5) | (j == 6))
    def _():
        for cols in chunks:
            g = proj(cols)
            o_ref[:, cols] = (g * _sigmoid(g)).astype(BF16)

    @pl.when(j >= 7)
    def _():
        for cols in chunks:
            o_ref[:, cols] = _sigmoid(proj(cols)).astype(BF16)


def _inproj(h, w_in, l, cos, sin):
    def rope_idx(j, m):
        return (jnp.where((j == 1) | (j == 2), m, 0), 0)

    return pl.pallas_call(
        _inproj_kernel,
        out_shape=jax.ShapeDtypeStruct((NTOK, P_COLS), BF16),
        grid=(INP_GROUPS, NTOK // INP_TM),
        in_specs=[pl.BlockSpec((INP_TM, D), lambda j, m: (m, 0)),
                  pl.BlockSpec((1, D, D), lambda j, m: (l, 0, jnp.where(j == 0, 0, j + 1))),
                  pl.BlockSpec((1, D, D), lambda j, m: (l, 0, 1), pipeline_mode=pl.Buffered(1)),
                  pl.BlockSpec((INP_TM, HALF), rope_idx),
                  pl.BlockSpec((INP_TM, HALF), rope_idx)],
        out_specs=pl.BlockSpec((INP_TM, D), lambda j, m: (m, j)),
        scratch_shapes=[pltpu.VMEM((D, D), BF16), pltpu.VMEM((D, D), BF16)],
        compiler_params=_cparams("arbitrary", "arbitrary"),
        name="inproj",
    )(h, w_in, w_in, cos, sin)


CONV_TS = 512
CONV_HALO = 32
CONV_RC = 32
CONV_CW = 128
CONV_SH = CONV_TS + 24
SUBLANES = 8


def _conv_kernel(u_ref, halo_ref, sga_ref, cw_ref, cb_ref, lnw_ref, lnb_ref, wo_ref, eg_ref, eu_ref, ed_ref,
                 o_ref, og_ref, ou_ref, od_ref, buf_ref, sh_ref, acc_ref, wbf_ref):
    i = pl.program_id(1)

    @pl.when((pl.program_id(0) == 0) & (i == 0))
    def _():
        wbf_ref[...] = wo_ref[0].astype(BF16)

    for src, dst in ((eg_ref, og_ref), (eu_ref, ou_ref), (ed_ref, od_ref)):
        dst[0] = src[0, 0].astype(BF16)

    halo = halo_ref[0].astype(F32)
    buf_ref[0:CONV_HALO, :] = jnp.where(i > 0, halo, 0.0)
    buf_ref[CONV_HALO:CONV_HALO + CONV_TS, :] = u_ref[0].astype(F32)
    for r in range(1, SUBLANES):
        sh_ref[r - 1] = buf_ref[r:r + CONV_SH, :]

    groups = CONV_RC // SUBLANES

    def body(ci, carry):
        r0 = pl.multiple_of(ci * CONV_RC, CONV_RC)
        for cc in range(D // CONV_CW):
            cols = slice(cc * CONV_CW, (cc + 1) * CONV_CW)
            accs = [cb_ref[:, cols]] * groups
            for off in range(2, CONV_K + 2):
                q, r = divmod(off, SUBLANES)
                w8 = cw_ref[off - 2, :, cols]
                for g in range(groups):
                    rows = pl.ds(r0 + SUBLANES * (q + g), SUBLANES)
                    win = buf_ref[rows, cols] if r == 0 else sh_ref[r - 1, rows, cols]
                    accs[g] = accs[g] + win * w8
            for g in range(groups):
                acc_ref[pl.ds(r0 + SUBLANES * g, SUBLANES), cols] = accs[g]
        return carry

    lax.fori_loop(0, CONV_TS // CONV_RC, body, 0)

    c = acc_ref[...]
    mu = jnp.mean(c, axis=-1, keepdims=True)
    cen = c - mu
    var = jnp.mean(cen * cen, axis=-1, keepdims=True)
    y = cen * lax.rsqrt(var + EPS) * lnw_ref[...] + lnb_ref[...]
    y = y * _sigmoid(y)
    out = jnp.dot(y.astype(BF16), wbf_ref[...], preferred_element_type=F32)
    o_ref[0] = (out * sga_ref[0].astype(F32)).astype(BF16)


def _conv_branch(p3, l, conv_w, conv_b, ln_w, ln_b, w_conv_out, w_exp):
    hb = CONV_TS // CONV_HALO
    n_i = SEQ // CONV_TS
    cw8 = jnp.broadcast_to(conv_w[:, None, :], (CONV_K, SUBLANES, D))
    cb8 = jnp.broadcast_to(conv_b[None, :], (SUBLANES, D))
    slab_rows = N_EXPERTS * D // (BATCH * n_i)
    per_mat = D // slab_rows
    assert per_mat * slab_rows == D and slab_rows % 16 == 0

    def slab(b, i):
        s = b * n_i + i
        return s // per_mat, s % per_mat

    exp_in = pl.BlockSpec((1, 1, slab_rows, D), lambda b, i: (l, *slab(b, i), 0))
    exp_out = pl.BlockSpec((1, slab_rows, D), lambda b, i: (*slab(b, i), 0))
    exp_shape = jax.ShapeDtypeStruct((N_EXPERTS, D, D), BF16)
    return pl.pallas_call(
        _conv_kernel,
        out_shape=(jax.ShapeDtypeStruct((BATCH, SEQ, D), BF16), exp_shape, exp_shape, exp_shape),
        grid=(BATCH, n_i),
        in_specs=[pl.BlockSpec((1, CONV_TS, D), lambda b, i: (b, i, PCOL_U)),
                  pl.BlockSpec((1, CONV_HALO, D), lambda b, i: (b, jnp.maximum(i * hb - 1, 0), PCOL_U)),
                  pl.BlockSpec((1, CONV_TS, D), lambda b, i: (b, i, PCOL_GA)),
                  pl.BlockSpec((CONV_K, SUBLANES, D), lambda b, i: (0, 0, 0)),
                  pl.BlockSpec((SUBLANES, D), lambda b, i: (0, 0)),
                  pl.BlockSpec((1, D), lambda b, i: (0, 0)),
                  pl.BlockSpec((1, D), lambda b, i: (0, 0)),
                  pl.BlockSpec((1, D, D), lambda b, i: (l, 0, 0), pipeline_mode=pl.Buffered(1)),
                  exp_in, exp_in, exp_in],
        out_specs=(pl.BlockSpec((1, CONV_TS, D), lambda b, i: (b, i, 0)), exp_out, exp_out, exp_out),
        scratch_shapes=[pltpu.VMEM((CONV_HALO + CONV_TS, D), F32),
                        pltpu.VMEM((SUBLANES - 1, CONV_SH, D), F32),
                        pltpu.VMEM((CONV_TS, D), F32),
                        pltpu.VMEM((D, D), BF16)],
        compiler_params=_cparams("arbitrary", "arbitrary"),
        name="conv_branch",
    )(p3, p3, p3, cw8, cb8, ln_w.reshape(1, D), ln_b.reshape(1, D), w_conv_out, *w_exp)


def _ret_tables():
    hh = jnp.arange(HEADS, dtype=F32)
    log_g = jnp.log1p(-jnp.exp2(-5.0 - hh))
    idx = jnp.arange(CHUNK, dtype=F32)
    rel = idx[:, None] - idx[None, :]
    dmask = jnp.where(rel[None] >= 0, jnp.exp(jnp.maximum(rel, 0.0)[None] * log_g[:, None, None]), 0.0)
    xi = jnp.exp((idx + 1.0)[None, :] * log_g[:, None])[..., None]
    zeta = jnp.exp((CHUNK - 1.0 - idx)[None, :] * log_g[:, None])[..., None]
    g_chunk = jnp.exp(CHUNK * log_g)[:, None, None]
    return (dmask,
            jnp.broadcast_to(xi, (HEADS, CHUNK, DK)),
            jnp.broadcast_to(zeta, (HEADS, CHUNK, DK)),
            jnp.broadcast_to(g_chunk, (HEADS, 1, DV)))


RET_TS = 512


def _ret_kernel(q_ref, k_ref, v01_ref, v23_ref, g01_ref, g23_ref, dm_ref, xi_ref, zt_ref, gc_ref, o_ref, st_ref):
    @pl.when(pl.program_id(1) == 0)
    def _():
        st_ref[...] = jnp.zeros_like(st_ref)

    def body(c, carry):
        r0 = pl.multiple_of(c * CHUNK, CHUNK)
        rows = pl.ds(r0, CHUNK)
        for hd in range(HEADS):
            v_ref, g_ref = (v01_ref, g01_ref) if hd < 2 else (v23_ref, g23_ref)
            vcols = slice((hd % 2) * DV, (hd % 2 + 1) * DV)
            qc = q_ref[0, rows, hd * DK:(hd + 1) * DK]
            kc = k_ref[0, rows, hd * DK:(hd + 1) * DK]
            vc = v_ref[0, rows, vcols]
            scores = lax.dot_general(qc, kc, NT_DIMS, preferred_element_type=F32) * dm_ref[hd]
            inner = jnp.dot(scores.astype(BF16), vc, preferred_element_type=F32)
            st = st_ref[hd]
            qx = (qc.astype(F32) * xi_ref[hd]).astype(BF16)
            cross = jnp.dot(qx, st.astype(BF16), preferred_element_type=F32)
            kzt = (kc.astype(F32) * zt_ref[hd]).T.astype(BF16)
            st_ref[hd] = st * gc_ref[hd] + jnp.dot(kzt, vc, preferred_element_type=F32)
            o = inner + cross
            mu = jnp.mean(o, axis=-1, keepdims=True)
            cen = o - mu
            var = jnp.mean(cen * cen, axis=-1, keepdims=True)
            r = cen * lax.rsqrt(var + EPS)
            o_ref[0, rows, hd * DV:(hd + 1) * DV] = (r * g_ref[0, rows, vcols].astype(F32)).astype(BF16)
        return carry

    lax.fori_loop(0, RET_TS // CHUNK, body, 0)


def _retention(p3, tables):
    dmask, xi, zeta, gch = tables
    tile = lambda col: pl.BlockSpec((1, RET_TS, D), lambda b, i: (b, i, col))
    full = lambda a: pl.BlockSpec(a.shape, lambda b, i: (0, 0, 0))
    return pl.pallas_call(
        _ret_kernel,
        out_shape=jax.ShapeDtypeStruct((BATCH, SEQ, HEADS * DV), BF16),
        grid=(BATCH, SEQ // RET_TS),
        in_specs=[tile(PCOL_Q), tile(PCOL_K), tile(PCOL_V), tile(PCOL_V + 1), tile(PCOL_G), tile(PCOL_G + 1),
                  full(dmask), full(xi), full(zeta), full(gch)],
        out_specs=pl.BlockSpec((1, RET_TS, HEADS * DV), lambda b, i: (b, i, 0)),
        scratch_shapes=[pltpu.VMEM((HEADS, DK, DV), F32)],
        compiler_params=_cparams("parallel", "arbitrary"),
        name="retention",
    )(p3, p3, p3, p3, p3, p3, dmask, xi, zeta, gch)


MIX_TM = 512


def _route_rows(s, sb):
    row = lambda a, e: a[e:e + 1, :]
    best = None
    gidx = None
    for g in range(N_GROUPS):
        v = [row(sb, GROUP_SIZE * g + i) for i in range(GROUP_SIZE)]
        pair_sums = [v[a] + v[b] for a in range(GROUP_SIZE) for b in range(a + 1, GROUP_SIZE)]
        gs = functools.reduce(jnp.maximum, pair_sums)
        if g == 0:
            best, gidx = gs, jnp.zeros(gs.shape, jnp.int32)
        else:
            upd = gs > best
            gidx = jnp.where(upd, g, gidx)
            best = jnp.where(upd, gs, best)

    def pick(a, i):
        out = row(a, i)
        for g in range(1, N_GROUPS):
            out = jnp.where(gidx == g, row(a, GROUP_SIZE * g + i), out)
        return out

    vb = [pick(sb, i) for i in range(GROUP_SIZE)]
    vs = [pick(s, i) for i in range(GROUP_SIZE)]
    m1, i1, s1 = vb[0], jnp.zeros(gidx.shape, jnp.int32), vs[0]
    for i in range(1, GROUP_SIZE):
        upd = vb[i] > m1
        m1 = jnp.where(upd, vb[i], m1)
        i1 = jnp.where(upd, i, i1)
        s1 = jnp.where(upd, vs[i], s1)
    m2 = i2 = s2 = None
    for i in range(GROUP_SIZE):
        cand = jnp.where(i1 == i, -jnp.inf, vb[i])
        if m2 is None:
            m2, i2, s2 = cand, jnp.zeros(gidx.shape, jnp.int32), vs[0]
        else:
            upd = cand > m2
            m2 = jnp.where(upd, cand, m2)
            i2 = jnp.where(upd, i, i2)
            s2 = jnp.where(upd, vs[i], s2)
    den = s1 + s2
    w1 = s1 / den
    w2 = s2 / den
    lo = jnp.minimum(i1, i2)
    hi = jnp.maximum(i1, i2)
    pair = jnp.zeros(gidx.shape, jnp.int32)
    first = jnp.zeros(gidx.shape, jnp.int32)
    for p, (fa, fb) in enumerate(PAIR_ORDER):
        hit = (lo == min(fa, fb)) & (hi == max(fa, fb))
        pair = jnp.where(hit, p, pair)
        first = jnp.where(hit, fa, first)
    bucket = gidx * PAIRS + pair
    first_is_top1 = i1 == first
    return bucket, jnp.where(first_is_top1, w1, w2), jnp.where(first_is_top1, w2, w1)


def _mix_kernel(rg_ref, ya_ref, sgb_ref, x_ref, gt_ref, sc_ref, sh_ref, wr_ref, wo_ref, n2_ref,
                rhi_ref, rlo_ref, rb_ref, tri_ref,
                x1_ref, hx_ref, ri_ref, cnt_ref, carry_ref, wrb_ref, wob_ref):
    m = pl.program_id(0)

    @pl.when(m == 0)
    def _():
        carry_ref[...] = jnp.zeros_like(carry_ref)
        wrb_ref[...] = wr_ref[0].astype(BF16)
        wob_ref[...] = wo_ref[0].astype(BF16)

    yb = jnp.dot(rg_ref[...], wrb_ref[...], preferred_element_type=F32)
    y = ya_ref[...].astype(F32) + sgb_ref[...].astype(F32) * yb
    o = jnp.dot(y.astype(BF16), wob_ref[...], preferred_element_type=F32)
    x1 = x_ref[...] + gt_ref[0] * o
    x1_ref[...] = x1
    h2 = _rms(x1) * n2_ref[...] * (1.0 + sc_ref[0]) + sh_ref[0]
    hx_ref[:, 0:D] = h2

    hi = h2.astype(BF16)
    lo = (h2 - hi.astype(F32)).astype(BF16)
    rhi = rhi_ref[...]
    logits = (lax.dot_general(rhi, hi, NT_DIMS, preferred_element_type=F32)
              + lax.dot_general(rhi, lo, NT_DIMS, preferred_element_type=F32)
              + lax.dot_general(rlo_ref[...], hi, NT_DIMS, preferred_element_type=F32))
    s = _sigmoid(logits)
    bucket, w_lo, w_hi = _route_rows(s, s + rb_ref[...])

    onehot = (lax.broadcasted_iota(jnp.int32, (BUCKET_ROWS, MIX_TM), 0) == bucket).astype(F32)
    prefix = jnp.dot(onehot.astype(BF16), tri_ref[...], preferred_element_type=F32)
    carry = carry_ref[:, 0:1]
    rank = jnp.sum(onehot * (prefix + carry), axis=0, keepdims=True)
    new_carry = carry + jnp.sum(onehot, axis=1, keepdims=True)
    carry_ref[...] = jnp.broadcast_to(new_carry, carry_ref.shape)
    cnt_ref[...] = jnp.broadcast_to(new_carry, cnt_ref.shape)

    rid = lax.broadcasted_iota(jnp.int32, (8, MIX_TM), 0)
    ri_ref[...] = jnp.where(rid == 0, bucket, jnp.where(rid == 1, rank.astype(jnp.int32), 0))
    wid = lax.broadcasted_iota(jnp.int32, (TAIL, MIX_TM), 0)
    wrows = jnp.where(wid == 0, w_lo, jnp.where(wid == 1, w_hi, 0.0))
    hx_ref[:, D:XROW] = wrows.T


def _mix(rg, ya, p, x, gt1, sc2, sh2, w_ret_out, w_out, l, norm2_w, rhi, rlo, rb, tri):
    tpb = SEQ // MIX_TM
    bidx = lambda m: (m // tpb, 0, 0)
    return pl.pallas_call(
        _mix_kernel,
        out_shape=(jax.ShapeDtypeStruct((NTOK, D), F32),
                   jax.ShapeDtypeStruct((NTOK, XROW), F32),
                   jax.ShapeDtypeStruct((8, NTOK), jnp.int32),
                   jax.ShapeDtypeStruct((BUCKET_ROWS, 128), F32)),
        grid=(NTOK // MIX_TM,),
        in_specs=[pl.BlockSpec((MIX_TM, HEADS * DV), lambda m: (m, 0)),
                  pl.BlockSpec((MIX_TM, D), lambda m: (m, 0)),
                  pl.BlockSpec((MIX_TM, D), lambda m: (m, PCOL_GB)),
                  pl.BlockSpec((MIX_TM, D), lambda m: (m, 0)),
                  pl.BlockSpec((1, 1, D), bidx),
                  pl.BlockSpec((1, 1, D), bidx),
                  pl.BlockSpec((1, 1, D), bidx),
                  pl.BlockSpec((1, HEADS * DV, D), lambda m: (l, 0, 0), pipeline_mode=pl.Buffered(1)),
                  pl.BlockSpec((1, D, D), lambda m: (l, 0, 0), pipeline_mode=pl.Buffered(1)),
                  pl.BlockSpec((1, D), lambda m: (0, 0)),
                  pl.BlockSpec((N_EXPERTS, D), lambda m: (0, 0)),
                  pl.BlockSpec((N_EXPERTS, D), lambda m: (0, 0)),
                  pl.BlockSpec((N_EXPERTS, MIX_TM), lambda m: (0, 0)),
                  pl.BlockSpec((MIX_TM, MIX_TM), lambda m: (0, 0))],
        out_specs=(pl.BlockSpec((MIX_TM, D), lambda m: (m, 0)),
                   pl.BlockSpec((MIX_TM, XROW), lambda m: (m, 0)),
                   pl.BlockSpec((8, MIX_TM), lambda m: (0, m)),
                   pl.BlockSpec((BUCKET_ROWS, 128), lambda m: (0, 0))),
        scratch_shapes=[pltpu.VMEM((BUCKET_ROWS, 128), F32),
                        pltpu.VMEM((HEADS * DV, D), BF16), pltpu.VMEM((D, D), BF16)],
        compiler_params=_cparams("arbitrary"),
        name="mix_route",
    )(rg, ya, p, x, gt1, sc2, sh2, w_ret_out, w_out, norm2_w.reshape(1, D), rhi, rlo, rb, tri)


DISP_TG = 1024


def _dispatch_kernel(dest_ref, nval_ref, src_ref, xs_ref, zero_ref, sem, zsem):
    step = pl.program_id(0)
    base = step * DISP_TG

    @pl.when(step == 0)
    def _():
        zero_ref[...] = jnp.zeros_like(zero_ref)

        def zero_copy(b):
            r0 = pl.multiple_of(b * MOE_BLK, MOE_BLK)
            return pltpu.make_async_copy(zero_ref, xs_ref.at[pl.ds(r0, MOE_BLK), :], zsem)

        def zissue(b, carry):
            @pl.when(nval_ref[b] < MOE_BLK)
            def _():
                zero_copy(b).start()
            return carry

        lax.fori_loop(0, N_BLOCKS, zissue, 0)

        def zdrain(b, carry):
            @pl.when(nval_ref[b] < MOE_BLK)
            def _():
                zero_copy(b).wait()
            return carry

        lax.fori_loop(0, N_BLOCKS, zdrain, 0)

    def row_copy(t, d):
        return pltpu.make_async_copy(src_ref.at[pl.ds(t, 1), :], xs_ref.at[pl.ds(d, 1), :], sem)

    for t in range(DISP_TG):
        row_copy(t, dest_ref[base + t]).start(priority=t % 2)
    pltpu.make_async_copy(src_ref, xs_ref.at[pl.ds(0, DISP_TG), :], sem).wait()


def _dispatch(dest, nval, hx):
    return pl.pallas_call(
        _dispatch_kernel,
        out_shape=jax.ShapeDtypeStruct((SLOT_ROWS, XROW), F32),
        grid_spec=pltpu.PrefetchScalarGridSpec(
            num_scalar_prefetch=2,
            grid=(NTOK // DISP_TG,),
            in_specs=[pl.BlockSpec((DISP_TG, XROW), lambda i, d, nv: (i, 0))],
            out_specs=pl.BlockSpec(memory_space=pl.ANY),
            scratch_shapes=[pltpu.VMEM((MOE_BLK, XROW), F32),
                            pltpu.SemaphoreType.DMA(()), pltpu.SemaphoreType.DMA(())]),
        compiler_params=_cparams("arbitrary", row_dma=True),
        name="dispatch",
    )(dest, nval, hx)


def _moe_kernel(elo_ref, ehi_ref, nval_ref, xs_ref, wg1, wu1, wd1, wg2, wu2, wd2, o_ref):
    i = pl.program_id(0)
    nval = nval_ref[i]

    def run(rows):
        x = xs_ref[0:rows, 0:D].astype(BF16)
        w_lo = xs_ref[0:rows, D:D + 1]
        w_hi = xs_ref[0:rows, D + 1:D + 2]

        def expert(wg, wu, wd):
            g = jnp.dot(x, wg[0], preferred_element_type=F32)
            u = jnp.dot(x, wu[0], preferred_element_type=F32)
            a = (g * _sigmoid(g) * u).astype(BF16)
            return jnp.dot(a, wd[0], preferred_element_type=F32)

        o_ref[0:rows, :] = w_lo * expert(wg1, wu1, wd1) + w_hi * expert(wg2, wu2, wd2)

    @pl.when(nval > MOE_BLK // 2)
    def _():
        run(MOE_BLK)

    @pl.when((nval > 0) & (nval <= MOE_BLK // 2))
    def _():
        run(MOE_BLK // 2)
        o_ref[MOE_BLK // 2:MOE_BLK, :] = jnp.zeros((MOE_BLK // 2, D), F32)

    @pl.when(nval == 0)
    def _():
        o_ref[...] = jnp.zeros_like(o_ref)


def _moe(elo, ehi, nval, xs, wg, wu, wd):
    wspec = lambda tab: pl.BlockSpec((1, D, D), lambda i, elo, ehi, nv: ((elo, ehi)[tab][i], 0, 0))
    return pl.pallas_call(
        _moe_kernel,
        out_shape=jax.ShapeDtypeStruct((SLOT_ROWS, D), F32),
        grid_spec=pltpu.PrefetchScalarGridSpec(
            num_scalar_prefetch=3,
            grid=(N_BLOCKS,),
            in_specs=[pl.BlockSpec((MOE_BLK, XROW), lambda i, elo, ehi, nv: (i, 0)),
                      wspec(0), wspec(0), wspec(0), wspec(1), wspec(1), wspec(1)],
            out_specs=pl.BlockSpec((MOE_BLK, D), lambda i, elo, ehi, nv: (i, 0))),
        compiler_params=_cparams("arbitrary"),
        name="experts",
    )(elo, ehi, nval, xs, wg, wu, wd, wg, wu, wd)


COMB_TG = 512


def _combine_kernel(dest_ref, ys_ref, x1_ref, gt_ref, nw_ref, sc_ref, sh_ref, *rest, last):
    if last:
        hn_ref, ybuf, sem = rest
    else:
        x2_ref, hn_ref, ybuf, sem = rest
    step = pl.program_id(0)
    slot = step % 2

    def gather_tile(tile, into):
        base = tile * COMB_TG

        for t in range(COMB_TG):
            src = ys_ref.at[pl.ds(dest_ref[base + t], 1), :]
            pltpu.make_async_copy(src, ybuf.at[into, pl.ds(t, 1), :], sem.at[into]).start(priority=t % 2)

    @pl.when(step == 0)
    def _():
        gather_tile(0, 0)

    @pl.when(step + 1 < pl.num_programs(0))
    def _():
        gather_tile(step + 1, 1 - slot)

    pltpu.make_async_copy(ys_ref.at[pl.ds(0, COMB_TG), :], ybuf.at[slot], sem.at[slot]).wait()

    x2 = x1_ref[...] + gt_ref[0] * ybuf[slot]
    y = _rms(x2) * nw_ref[...]
    if last:
        hn_ref[...] = y
    else:
        x2_ref[...] = x2
        hn_ref[...] = (y * (1.0 + sc_ref[0]) + sh_ref[0]).astype(hn_ref.dtype)


def _combine(dest, ys, x1, gt2, nw, sc, sh, last):
    tpb = SEQ // COMB_TG
    bidx = lambda i, d: (i // tpb, 0, 0)
    tile = pl.BlockSpec((COMB_TG, D), lambda i, d: (i, 0))
    if last:
        out_shape = jax.ShapeDtypeStruct((NTOK, D), F32)
        out_specs = tile
    else:
        out_shape = (jax.ShapeDtypeStruct((NTOK, D), F32), jax.ShapeDtypeStruct((NTOK, D), BF16))
        out_specs = (tile, tile)
    return pl.pallas_call(
        functools.partial(_combine_kernel, last=last),
        out_shape=out_shape,
        grid_spec=pltpu.PrefetchScalarGridSpec(
            num_scalar_prefetch=1,
            grid=(NTOK // COMB_TG,),
            in_specs=[pl.BlockSpec(memory_space=pl.ANY),
                      tile,
                      pl.BlockSpec((1, 1, D), bidx),
                      pl.BlockSpec((1, D), lambda i, d: (0, 0)),
                      pl.BlockSpec((1, 1, D), bidx),
                      pl.BlockSpec((1, 1, D), bidx)],
            out_specs=out_specs,
            scratch_shapes=[pltpu.VMEM((2, COMB_TG, D), F32), pltpu.SemaphoreType.DMA((2,))]),
        compiler_params=_cparams("arbitrary", row_dma=True),
        name="combine",
    )(dest, ys, x1, gt2, nw.reshape(1, D), sc, sh)


def _pair_tables():
    first = [GROUP_SIZE * g + fa for g in range(N_GROUPS) for fa, _ in PAIR_ORDER]
    second = [GROUP_SIZE * g + fb for g in range(N_GROUPS) for _, fb in PAIR_ORDER]
    return np.asarray(first, np.int32), np.asarray(second, np.int32)


def _plan_kernel(cnt_ref, first_ref, second_ref, ri_ref, dest_ref, elo_ref, ehi_ref, nval_ref, start_ref):
    def per_bucket(b, carry):
        start, blk = carry
        count = cnt_ref[b]
        n_blk = (count + MOE_BLK - 1) // MOE_BLK
        start_ref[b] = start

        def per_block(k, c):
            elo_ref[blk + k] = first_ref[b]
            ehi_ref[blk + k] = second_ref[b]
            nval_ref[blk + k] = jnp.minimum(count - k * MOE_BLK, MOE_BLK)
            return c

        lax.fori_loop(0, n_blk, per_block, 0)
        return start + n_blk * MOE_BLK, blk + n_blk

    _, used = lax.fori_loop(0, N_BUCKETS, per_bucket, (jnp.int32(0), jnp.int32(0)))

    def unused_block(k, c):
        elo_ref[k] = first_ref[N_BUCKETS - 1]
        ehi_ref[k] = second_ref[N_BUCKETS - 1]
        nval_ref[k] = 0
        return c

    lax.fori_loop(used, N_BLOCKS, unused_block, 0)

    bucket = ri_ref[0:1, :]
    rank = ri_ref[1:2, :]
    dest = jnp.zeros_like(rank)
    for b in range(N_BUCKETS):
        dest = jnp.where(bucket == b, start_ref[b] + rank, dest)
    dest_ref[...] = dest


def _slot_plan(ri, cnt):
    counts = cnt[:, 0].astype(jnp.int32)
    first_tab, second_tab = _pair_tables()
    smem_out = pl.BlockSpec(memory_space=pltpu.SMEM)
    blocks = jax.ShapeDtypeStruct((N_BLOCKS,), jnp.int32)
    dest, elo, ehi, nval = pl.pallas_call(
        _plan_kernel,
        out_shape=(jax.ShapeDtypeStruct((1, NTOK), jnp.int32), blocks, blocks, blocks),
        grid_spec=pltpu.PrefetchScalarGridSpec(
            num_scalar_prefetch=3,
            grid=(1,),
            in_specs=[pl.BlockSpec((8, NTOK), lambda i, c, f, s: (0, 0))],
            out_specs=(pl.BlockSpec((1, NTOK), lambda i, c, f, s: (0, 0)), smem_out, smem_out, smem_out),
            scratch_shapes=[pltpu.SMEM((BUCKET_ROWS,), jnp.int32)]),
        compiler_params=_cparams("arbitrary"),
        name="slot_plan",
    )(counts, jnp.asarray(first_tab), jnp.asarray(second_tab), ri)
    return dest.reshape(NTOK), elo, ehi, nval


def kernel(x, c, positions, w_ada, b_ada, norm1_w, w_in, conv_w, conv_b, conv_ln_w, conv_ln_b, w_conv_out,
           w_ret_out, w_out, norm2_w, w_router, router_bias, w_exp_gate, w_exp_up, w_exp_down, final_norm_w):
    mod = _ada(c, w_ada, b_ada).reshape(DEPTH, BATCH, N_MOD, 1, D)
    sh1, sc1, gt1, sh2, sc2, gt2 = (mod[:, :, i] for i in range(N_MOD))
    ret_tables = _ret_tables()

    wr_t = w_router.T
    rhi = wr_t.astype(BF16)
    rlo = (wr_t - rhi.astype(F32)).astype(BF16)
    rb = jnp.broadcast_to(router_bias.astype(F32)[:, None], (N_EXPERTS, MIX_TM))
    tri = (jnp.arange(MIX_TM)[:, None] < jnp.arange(MIX_TM)[None, :]).astype(BF16)

    xf = x.reshape(NTOK, D)
    h, cos, sin = _modnorm(x, norm1_w[0], sc1[0], sh1[0], positions)
    out = None
    for l in range(DEPTH):
        p = _inproj(h, w_in, l, cos, sin)
        p3 = p.reshape(BATCH, SEQ, P_COLS)
        ya, wg_bf, wu_bf, wd_bf = _conv_branch(p3, l, conv_w[l], conv_b[l], conv_ln_w[l], conv_ln_b[l], w_conv_out,
                                               (w_exp_gate, w_exp_up, w_exp_down))
        rg = _retention(p3, ret_tables)
        x1, hx, ri, cnt = _mix(rg.reshape(NTOK, HEADS * DV), ya.reshape(NTOK, D), p, xf, gt1[l], sc2[l], sh2[l],
                               w_ret_out, w_out, l, norm2_w[l], rhi, rlo, rb, tri)
        dest, elo, ehi, nval = _slot_plan(ri, cnt)
        xs = _dispatch(dest, nval, hx)
        ys = _moe(elo, ehi, nval, xs, wg_bf, wu_bf, wd_bf)
        if l + 1 < DEPTH:
            xf, h = _combine(dest, ys, x1, gt2[l], norm1_w[l + 1], sc1[l + 1], sh1[l + 1], last=False)
        else:
            out = _combine(dest, ys, x1, gt2[l], final_norm_w, sc1[l], sh1[l], last=True)
    return out.reshape(BATCH, SEQ, D)
```
